```python
import jax, jax.numpy as jnp
from jax import lax
import numpy as np

D_MODEL = 1024
BATCH = 8
SEQ = 8192
DEPTH = 2

CHUNK = 64
BAND_CHUNKS = 9
ATTN_HEADS = 8
HEAD_DIM = 64
ATTN_WIDTH = ATTN_HEADS * HEAD_DIM
POOL_WINDOWS = (2, 4, 8, 16)
POOL_GROUPS = len(POOL_WINDOWS)
POOL_WIDTH = D_MODEL // 2
POOL_GROUP_DIM = POOL_WIDTH // POOL_GROUPS
MAX_REL_DIST = 256
N_REL = 2 * MAX_REL_DIST + 1
D_FF = 2816
CONV_WIDTH = 3
N_BRANCH = 2
IN_WIDTH = 3 * ATTN_WIDTH + POOL_WIDTH + N_BRANCH * D_MODEL
EPS = 1e-6

kernel_name = "hybrid_chunk_attn_pool_sandwich"


def rms_norm(x, g):
    xf = x.astype(jnp.float32)
    y = xf * lax.rsqrt(jnp.mean(xf * xf, axis=-1, keepdims=True) + EPS)
    return (y * g.astype(jnp.float32)).astype(x.dtype)


def chunk_band_attention(q, k, v, rel_bias):
    b, s, h, dh = q.shape
    n_chunks = s // CHUNK
    band = BAND_CHUNKS * CHUNK
    lead = (BAND_CHUNKS - 1) * CHUNK
    pad = ((0, 0), (lead, 0), (0, 0), (0, 0))
    k_pad = jnp.pad(k, pad)
    v_pad = jnp.pad(v, pad)
    dist = jnp.arange(CHUNK)[:, None] + lead - jnp.arange(band)[None, :]
    idx = jnp.clip(dist, -MAX_REL_DIST, MAX_REL_DIST) + MAX_REL_DIST
    bias = rel_bias.astype(jnp.float32)[:, idx]
    scale = HEAD_DIM ** -0.5
    key_offsets = jnp.arange(band)

    def one_chunk(c):
        start = c * CHUNK
        q_c = lax.dynamic_slice_in_dim(q, start, CHUNK, axis=1)
        k_c = lax.dynamic_slice_in_dim(k_pad, start, band, axis=1)
        v_c = lax.dynamic_slice_in_dim(v_pad, start, band, axis=1)
        sc = jnp.einsum('bqhd,bkhd->bhqk', q_c, k_c,
                        preferred_element_type=jnp.float32) * scale + bias
        valid = (start - lead + key_offsets) >= 0
        sc = jnp.where(valid[None, None, None, :], sc, -1e30)
        p = jax.nn.softmax(sc, axis=-1).astype(v.dtype)
        return jnp.einsum('bhqk,bkhd->bqhd', p, v_c)

    out = lax.map(one_chunk, jnp.arange(n_chunks))
    return jnp.moveaxis(out, 0, 1).reshape(b, s, h * dh)


def multiscale_pool(u, w_group, scale):
    b, s, c = u.shape
    uf = u.astype(jnp.float32)
    max_w = max(POOL_WINDOWS)
    cs = jnp.pad(jnp.cumsum(uf, axis=1), ((0, 0), (max_w, 0), (0, 0)))
    t = jnp.arange(s)
    outs = []
    for g, w in enumerate(POOL_WINDOWS):
        sl = slice(g * POOL_GROUP_DIM, (g + 1) * POOL_GROUP_DIM)
        win = cs[:, max_w:, sl] - cs[:, max_w - w:max_w - w + s, sl]
        cnt = jnp.minimum(t + 1, w).astype(jnp.float32)[None, :, None]
        outs.append(win / cnt - uf[:, :, sl])
    pooled = jnp.stack(outs, axis=2).astype(u.dtype)
    mixed = jnp.einsum('bsgc,gcd->bsgd', pooled, w_group).reshape(b, s, c)
    return mixed * scale


def conv_gated_ffn(x, w_up, conv_w, conv_b, w_down):
    hu = x @ w_up
    s = hu.shape[1]
    hp = jnp.pad(hu, ((0, 0), (CONV_WIDTH - 1, 0), (0, 0)))
    hc = conv_b + conv_w[CONV_WIDTH - 1] * hu
    for i in range(CONV_WIDTH - 1):
        hc = hc + conv_w[i] * hp[:, i:i + s]
    val, gate = jnp.split(hc, 2, axis=-1)
    return (jax.nn.gelu(gate, approximate=True) * val) @ w_down


def _fwd_setup_inputs(seed: int = 0) -> dict:
    key = jax.random.key(seed)
    ks = jax.random.split(key, 20)
    f32 = jnp.float32

    def nrm(k, shape, s):
        return jax.random.normal(k, shape, f32) * s

    return {
        "x": jax.random.normal(ks[0], (BATCH, SEQ, D_MODEL), f32),
        "norm_mix_pre": 1.0 + nrm(ks[1], (DEPTH, D_MODEL), 0.05),
        "w_in": nrm(ks[2], (DEPTH, D_MODEL, IN_WIDTH), D_MODEL ** -0.5),
        "b_gate": nrm(ks[3], (DEPTH, N_BRANCH * D_MODEL), 0.01),
        "rel_bias": nrm(ks[4], (DEPTH, ATTN_HEADS, N_REL), 0.1),
        "w_attn_out": nrm(ks[5], (DEPTH, ATTN_WIDTH, D_MODEL), ATTN_WIDTH ** -0.5),
        "w_pool_group": nrm(ks[6], (DEPTH, POOL_GROUPS, POOL_GROUP_DIM, POOL_GROUP_DIM), POOL_GROUP_DIM ** -0.5),
        "pool_scale": 1.0 + nrm(ks[7], (DEPTH, POOL_WIDTH), 0.1),
        "w_pool_out": nrm(ks[8], (DEPTH, POOL_WIDTH, D_MODEL), POOL_WIDTH ** -0.5),
        "w_o": nrm(ks[9], (DEPTH, D_MODEL, D_MODEL), D_MODEL ** -0.5),
        "norm_mix_post": 1.0 + nrm(ks[10], (DEPTH, D_MODEL), 0.05),
        "norm_ffn_pre": 1.0 + nrm(ks[11], (DEPTH, D_MODEL), 0.05),
        "w_up": nrm(ks[12], (DEPTH, D_MODEL, 2 * D_FF), D_MODEL ** -0.5),
        "conv_w": nrm(ks[13], (DEPTH, CONV_WIDTH, 2 * D_FF), CONV_WIDTH ** -0.5),
        "conv_b": nrm(ks[14], (DEPTH, 2 * D_FF), 0.01),
        "w_down": nrm(ks[15], (DEPTH, D_FF, D_MODEL), D_FF ** -0.5),
        "norm_ffn_post": 1.0 + nrm(ks[16], (DEPTH, D_MODEL), 0.05),
    }


def _fwd_reference(x, norm_mix_pre, w_in, b_gate, rel_bias, w_attn_out, w_pool_group, pool_scale,
              w_pool_out, w_o, norm_mix_post, norm_ffn_pre, w_up, conv_w, conv_b, w_down,
              norm_ffn_post):
    b, s, _ = x.shape
    splits = [ATTN_WIDTH, 2 * ATTN_WIDTH, 3 * ATTN_WIDTH, 3 * ATTN_WIDTH + POOL_WIDTH]
    for l in range(DEPTH):
        h = rms_norm(x, norm_mix_pre[l])
        proj = h @ w_in[l]
        q, k, v, u, gates = jnp.split(proj, splits, axis=-1)
        q = q.reshape(b, s, ATTN_HEADS, HEAD_DIM)
        k = k.reshape(b, s, ATTN_HEADS, HEAD_DIM)
        v = v.reshape(b, s, ATTN_HEADS, HEAD_DIM)
        y_a = chunk_band_attention(q, k, v, rel_bias[l]) @ w_attn_out[l]
        y_b = multiscale_pool(u, w_pool_group[l], pool_scale[l]) @ w_pool_out[l]
        g_a, g_b = jnp.split(jax.nn.sigmoid(gates + b_gate[l]), N_BRANCH, axis=-1)
        mix = (g_a * y_a + g_b * y_b) @ w_o[l]
        x = x + rms_norm(mix, norm_mix_post[l])
        f = conv_gated_ffn(rms_norm(x, norm_ffn_pre[l]), w_up[l], conv_w[l], conv_b[l], w_down[l])
        x = x + rms_norm(f, norm_ffn_post[l])
    return x


import jax as _jax
import jax.numpy as _jnp

TWIN_FORMAT = 'train_step'
FWD_PARAMS = ['x', 'norm_mix_pre', 'w_in', 'b_gate', 'rel_bias', 'w_attn_out', 'w_pool_group', 'pool_scale', 'w_pool_out', 'w_o', 'norm_mix_post', 'norm_ffn_pre', 'w_up', 'conv_w', 'conv_b', 'w_down', 'norm_ffn_post']
TWIN_WEIGHTS = ['norm_mix_pre', 'w_in', 'b_gate', 'rel_bias', 'w_attn_out', 'w_pool_group', 'pool_scale', 'w_pool_out', 'w_o', 'norm_mix_post', 'norm_ffn_pre', 'w_up', 'conv_w', 'conv_b', 'w_down', 'norm_ffn_post']
TWIN_DIFF_INPUT = 'x'
TWIN_INPUTS = ['x', 'norm_mix_pre', 'w_in', 'b_gate', 'rel_bias', 'w_attn_out', 'w_pool_group', 'pool_scale', 'w_pool_out', 'w_o', 'norm_mix_post', 'norm_ffn_pre', 'w_up', 'conv_w', 'conv_b', 'w_down', 'norm_ffn_post', 'loss_target', 'm_norm_mix_pre', 'm_w_in', 'm_b_gate', 'm_rel_bias', 'm_w_attn_out', 'm_w_pool_group', 'm_pool_scale', 'm_w_pool_out', 'm_w_o', 'm_norm_mix_post', 'm_norm_ffn_pre', 'm_w_up', 'm_conv_w', 'm_conv_b', 'm_w_down', 'm_norm_ffn_post', 'v_norm_mix_pre', 'v_w_in', 'v_b_gate', 'v_rel_bias', 'v_w_attn_out', 'v_w_pool_group', 'v_pool_scale', 'v_w_pool_out', 'v_w_o', 'v_norm_mix_post', 'v_norm_ffn_pre', 'v_w_up', 'v_conv_w', 'v_conv_b', 'v_w_down', 'v_norm_ffn_post']
TWIN_OUTPUTS = ['loss', 'grad_x', 'grad_norm_mix_pre', 'grad_w_in', 'grad_b_gate', 'grad_rel_bias', 'grad_w_attn_out', 'grad_w_pool_group', 'grad_pool_scale', 'grad_w_pool_out', 'grad_w_o', 'grad_norm_mix_post', 'grad_norm_ffn_pre', 'grad_w_up', 'grad_conv_w', 'grad_conv_b', 'grad_w_down', 'grad_norm_ffn_post', 'delta_norm_mix_pre', 'delta_w_in', 'delta_b_gate', 'delta_rel_bias', 'delta_w_attn_out', 'delta_w_pool_group', 'delta_pool_scale', 'delta_w_pool_out', 'delta_w_o', 'delta_norm_mix_post', 'delta_norm_ffn_pre', 'delta_w_up', 'delta_conv_w', 'delta_conv_b', 'delta_w_down', 'delta_norm_ffn_post', 'new_m_norm_mix_pre', 'new_m_w_in', 'new_m_b_gate', 'new_m_rel_bias', 'new_m_w_attn_out', 'new_m_w_pool_group', 'new_m_pool_scale', 'new_m_w_pool_out', 'new_m_w_o', 'new_m_norm_mix_post', 'new_m_norm_ffn_pre', 'new_m_w_up', 'new_m_conv_w', 'new_m_conv_b', 'new_m_w_down', 'new_m_norm_ffn_post', 'new_v_norm_mix_pre', 'new_v_w_in', 'new_v_b_gate', 'new_v_rel_bias', 'new_v_w_attn_out', 'new_v_w_pool_group', 'new_v_pool_scale', 'new_v_w_pool_out', 'new_v_w_o', 'new_v_norm_mix_post', 'new_v_norm_ffn_pre', 'new_v_w_up', 'new_v_conv_w', 'new_v_conv_b', 'new_v_w_down', 'new_v_norm_ffn_post']
TWIN_LEAF_KINDS = {'loss': 'loss', 'grad_x': 'grad_x', 'grad_norm_mix_pre': 'grad_w', 'grad_w_in': 'grad_w', 'grad_b_gate': 'grad_w', 'grad_rel_bias': 'grad_w', 'grad_w_attn_out': 'grad_w', 'grad_w_pool_group': 'grad_w', 'grad_pool_scale': 'grad_w', 'grad_w_pool_out': 'grad_w', 'grad_w_o': 'grad_w', 'grad_norm_mix_post': 'grad_w', 'grad_norm_ffn_pre': 'grad_w', 'grad_w_up': 'grad_w', 'grad_conv_w': 'grad_w', 'grad_conv_b': 'grad_w', 'grad_w_down': 'grad_w', 'grad_norm_ffn_post': 'grad_w', 'delta_norm_mix_pre': 'delta_w', 'delta_w_in': 'delta_w', 'delta_b_gate': 'delta_w', 'delta_rel_bias': 'delta_w', 'delta_w_attn_out': 'delta_w', 'delta_w_pool_group': 'delta_w', 'delta_pool_scale': 'delta_w', 'delta_w_pool_out': 'delta_w', 'delta_w_o': 'delta_w', 'delta_norm_mix_post': 'delta_w', 'delta_norm_ffn_pre': 'delta_w', 'delta_w_up': 'delta_w', 'delta_conv_w': 'delta_w', 'delta_conv_b': 'delta_w', 'delta_w_down': 'delta_w', 'delta_norm_ffn_post': 'delta_w', 'new_m_norm_mix_pre': 'new_m', 'new_m_w_in': 'new_m', 'new_m_b_gate': 'new_m', 'new_m_rel_bias': 'new_m', 'new_m_w_attn_out': 'new_m', 'new_m_w_pool_group': 'new_m', 'new_m_pool_scale': 'new_m', 'new_m_w_pool_out': 'new_m', 'new_m_w_o': 'new_m', 'new_m_norm_mix_post': 'new_m', 'new_m_norm_ffn_pre': 'new_m', 'new_m_w_up': 'new_m', 'new_m_conv_w': 'new_m', 'new_m_conv_b': 'new_m', 'new_m_w_down': 'new_m', 'new_m_norm_ffn_post': 'new_m', 'new_v_norm_mix_pre': 'new_v', 'new_v_w_in': 'new_v', 'new_v_b_gate': 'new_v', 'new_v_rel_bias': 'new_v', 'new_v_w_attn_out': 'new_v', 'new_v_w_pool_group': 'new_v', 'new_v_pool_scale': 'new_v', 'new_v_w_pool_out': 'new_v', 'new_v_w_o': 'new_v', 'new_v_norm_mix_post': 'new_v', 'new_v_norm_ffn_pre': 'new_v', 'new_v_w_up': 'new_v', 'new_v_conv_w': 'new_v', 'new_v_conv_b': 'new_v', 'new_v_w_down': 'new_v', 'new_v_norm_ffn_post': 'new_v'}


def _forward(args):
    return _fwd_reference(*[args[k] for k in FWD_PARAMS])


def _output_shape():
    out = _jax.eval_shape(lambda: _forward(_fwd_setup_inputs(0)))
    return out.shape, out.dtype

N_MICROBATCH = 1
ADAM_LR = 0.001
ADAM_B1 = 0.9
ADAM_B2 = 0.999
ADAM_EPS = 1e-08
ADAM_WD = 0.01
ADAM_STEP = 10
PER_EXAMPLE_BATCH_AXIS = {'x': 0, 'loss_target': 0}
SHARED_INPUTS = []
_WEIGHT_DTYPES = {'norm_mix_pre': _jnp.float32, 'w_in': _jnp.float32, 'b_gate': _jnp.float32, 'rel_bias': _jnp.float32, 'w_attn_out': _jnp.float32, 'w_pool_group': _jnp.float32, 'pool_scale': _jnp.float32, 'w_pool_out': _jnp.float32, 'w_o': _jnp.float32, 'norm_mix_post': _jnp.float32, 'norm_ffn_pre': _jnp.float32, 'w_up': _jnp.float32, 'conv_w': _jnp.float32, 'conv_b': _jnp.float32, 'w_down': _jnp.float32, 'norm_ffn_post': _jnp.float32}
MOMENT_SCALE = {'norm_mix_pre': 1.855259e+00, 'w_in': 9.302312e-01, 'b_gate': 6.444342e-01, 'rel_bias': 8.174016e-02, 'w_attn_out': 1.893020e-01, 'w_pool_group': 2.990107e+00, 'pool_scale': 3.251725e+00, 'w_pool_out': 2.279445e+00, 'w_o': 2.379087e+00, 'norm_mix_post': 6.431781e+01, 'norm_ffn_pre': 1.089556e+00, 'w_up': 5.156188e-01, 'conv_w': 5.576941e-01, 'conv_b': 9.962363e-01, 'w_down': 1.122427e+00, 'norm_ffn_post': 6.364437e+01}


def _to_microbatches(a, axis):
    t = _jnp.moveaxis(a, axis, 0)
    t = t.reshape((N_MICROBATCH, t.shape[0] // N_MICROBATCH) + t.shape[1:])
    return _jnp.moveaxis(t, 1, axis + 1)


def setup_inputs(seed: int = 0) -> dict:
    inp = _fwd_setup_inputs(seed)
    key = _jax.random.fold_in(_jax.random.key(seed), 7919)
    shape, _ = _output_shape()
    out = dict(inp)
    out["loss_target"] = _jax.random.normal(_jax.random.fold_in(key, 0), shape, _jnp.float32)
    for i, name in enumerate(TWIN_WEIGHTS):
        w = inp[name].astype(_jnp.float32)
        if MOMENT_SCALE is None:
            s = _jnp.sqrt(_jnp.mean(_jnp.square(w)) + 1e-30)
        else:
            s = MOMENT_SCALE[name]
        km, kv = _jax.random.split(_jax.random.fold_in(key, i + 1))
        out[name] = w
        out["m_" + name] = s * _jax.random.normal(km, w.shape, _jnp.float32)
        out["v_" + name] = (s * s) * _jax.random.uniform(kv, w.shape, _jnp.float32, 0.5, 1.5)
    if N_MICROBATCH > 1:
        for name, axis in PER_EXAMPLE_BATCH_AXIS.items():
            out[name] = _to_microbatches(out[name], axis)
    return {'x': out['x'], 'norm_mix_pre': out['norm_mix_pre'], 'w_in': out['w_in'], 'b_gate': out['b_gate'], 'rel_bias': out['rel_bias'], 'w_attn_out': out['w_attn_out'], 'w_pool_group': out['w_pool_group'], 'pool_scale': out['pool_scale'], 'w_pool_out': out['w_pool_out'], 'w_o': out['w_o'], 'norm_mix_post': out['norm_mix_post'], 'norm_ffn_pre': out['norm_ffn_pre'], 'w_up': out['w_up'], 'conv_w': out['conv_w'], 'conv_b': out['conv_b'], 'w_down': out['w_down'], 'norm_ffn_post': out['norm_ffn_post'], 'loss_target': out['loss_target'], 'm_norm_mix_pre': out['m_norm_mix_pre'], 'm_w_in': out['m_w_in'], 'm_b_gate': out['m_b_gate'], 'm_rel_bias': out['m_rel_bias'], 'm_w_attn_out': out['m_w_attn_out'], 'm_w_pool_group': out['m_w_pool_group'], 'm_pool_scale': out['m_pool_scale'], 'm_w_pool_out': out['m_w_pool_out'], 'm_w_o': out['m_w_o'], 'm_norm_mix_post': out['m_norm_mix_post'], 'm_norm_ffn_pre': out['m_norm_ffn_pre'], 'm_w_up': out['m_w_up'], 'm_conv_w': out['m_conv_w'], 'm_conv_b': out['m_conv_b'], 'm_w_down': out['m_w_down'], 'm_norm_ffn_post': out['m_norm_ffn_post'], 'v_norm_mix_pre': out['v_norm_mix_pre'], 'v_w_in': out['v_w_in'], 'v_b_gate': out['v_b_gate'], 'v_rel_bias': out['v_rel_bias'], 'v_w_attn_out': out['v_w_attn_out'], 'v_w_pool_group': out['v_w_pool_group'], 'v_pool_scale': out['v_pool_scale'], 'v_w_pool_out': out['v_w_pool_out'], 'v_w_o': out['v_w_o'], 'v_norm_mix_post': out['v_norm_mix_post'], 'v_norm_ffn_pre': out['v_norm_ffn_pre'], 'v_w_up': out['v_w_up'], 'v_conv_w': out['v_conv_w'], 'v_conv_b': out['v_conv_b'], 'v_w_down': out['v_w_down'], 'v_norm_ffn_post': out['v_norm_ffn_post']}


def _loss(weights, diff, rest, loss_target):
    with _jax.named_scope("forward"):
        args = {**rest, TWIN_DIFF_INPUT: diff, **{k: w.astype(_WEIGHT_DTYPES[k]) for k, w in weights.items()}}
        y = _forward(args)
    with _jax.named_scope("loss_head"):
        err = _jnp.square(y.astype(_jnp.float32) - loss_target)
        return 0.5 * _jnp.sum(_jnp.mean(err, axis=-1)) if err.ndim else 0.5 * err


def _adamw(w, g, m, v):
    m = ADAM_B1 * m + (1.0 - ADAM_B1) * g
    v = ADAM_B2 * v + (1.0 - ADAM_B2) * _jnp.square(g)
    m_hat = m / (1.0 - ADAM_B1 ** ADAM_STEP)
    v_hat = v / (1.0 - ADAM_B2 ** ADAM_STEP)
    delta = -ADAM_LR * (m_hat / (_jnp.sqrt(v_hat) + ADAM_EPS) + ADAM_WD * w)
    return delta, m, v


def reference(x, norm_mix_pre, w_in, b_gate, rel_bias, w_attn_out, w_pool_group, pool_scale, w_pool_out, w_o, norm_mix_post, norm_ffn_pre, w_up, conv_w, conv_b, w_down, norm_ffn_post, loss_target, m_norm_mix_pre, m_w_in, m_b_gate, m_rel_bias, m_w_attn_out, m_w_pool_group, m_pool_scale, m_w_pool_out, m_w_o, m_norm_mix_post, m_norm_ffn_pre, m_w_up, m_conv_w, m_conv_b, m_w_down, m_norm_ffn_post, v_norm_mix_pre, v_w_in, v_b_gate, v_rel_bias, v_w_attn_out, v_w_pool_group, v_pool_scale, v_w_pool_out, v_w_o, v_norm_mix_post, v_norm_ffn_pre, v_w_up, v_conv_w, v_conv_b, v_w_down, v_norm_ffn_post):
    given = dict(x=x, norm_mix_pre=norm_mix_pre, w_in=w_in, b_gate=b_gate, rel_bias=rel_bias, w_attn_out=w_attn_out, w_pool_group=w_pool_group, pool_scale=pool_scale, w_pool_out=w_pool_out, w_o=w_o, norm_mix_post=norm_mix_post, norm_ffn_pre=norm_ffn_pre, w_up=w_up, conv_w=conv_w, conv_b=conv_b, w_down=w_down, norm_ffn_post=norm_ffn_post, loss_target=loss_target, m_norm_mix_pre=m_norm_mix_pre, m_w_in=m_w_in, m_b_gate=m_b_gate, m_rel_bias=m_rel_bias, m_w_attn_out=m_w_attn_out, m_w_pool_group=m_w_pool_group, m_pool_scale=m_pool_scale, m_w_pool_out=m_w_pool_out, m_w_o=m_w_o, m_norm_mix_post=m_norm_mix_post, m_norm_ffn_pre=m_norm_ffn_pre, m_w_up=m_w_up, m_conv_w=m_conv_w, m_conv_b=m_conv_b, m_w_down=m_w_down, m_norm_ffn_post=m_norm_ffn_post, v_norm_mix_pre=v_norm_mix_pre, v_w_in=v_w_in, v_b_gate=v_b_gate, v_rel_bias=v_rel_bias, v_w_attn_out=v_w_attn_out, v_w_pool_group=v_w_pool_group, v_pool_scale=v_pool_scale, v_w_pool_out=v_w_pool_out, v_w_o=v_w_o, v_norm_mix_post=v_norm_mix_post, v_norm_ffn_pre=v_norm_ffn_pre, v_w_up=v_w_up, v_conv_w=v_conv_w, v_conv_b=v_conv_b, v_w_down=v_w_down, v_norm_ffn_post=v_norm_ffn_post)
    weights = {n: given[n] for n in TWIN_WEIGHTS}
    shared = {n: given[n] for n in SHARED_INPUTS}
    per_example = {n: given[n] for n in ['x']}
    grad_fn = _jax.value_and_grad(_loss, argnums=(0, 1))

    def one_microbatch(ex, loss_target):
        ex = dict(ex)
        diff = ex.pop(TWIN_DIFF_INPUT)
        return grad_fn(weights, diff, {**shared, **ex}, loss_target)

    if N_MICROBATCH == 1:
        loss, (grad_w, grad_x) = one_microbatch(per_example, given["loss_target"])
    else:
        def body(carry, xs):
            loss_sum, grad_sum = carry
            l_k, (gw_k, gx_k) = one_microbatch(xs[0], xs[1])
            with _jax.named_scope("update"):
                return (loss_sum + l_k, _jax.tree.map(_jnp.add, grad_sum, gw_k)), gx_k

        init = (_jnp.zeros((), _jnp.float32), _jax.tree.map(_jnp.zeros_like, weights))
        (loss, grad_w), grad_x = _jax.lax.scan(body, init, (per_example, given["loss_target"]))
    with _jax.named_scope("update"):
        delta_w, new_m, new_v = {}, {}, {}
        for n in TWIN_WEIGHTS:
            delta_w[n], new_m[n], new_v[n] = _adamw(weights[n], grad_w[n], given["m_" + n], given["v_" + n])
    return (loss, grad_x, *[grad_w[n] for n in TWIN_WEIGHTS], *[delta_w[n] for n in TWIN_WEIGHTS],
            *[new_m[n] for n in TWIN_WEIGHTS], *[new_v[n] for n in TWIN_WEIGHTS])
```

```python
import numpy as np
import jax
import jax.numpy as jnp
from jax import lax
from jax.experimental import pallas as pl
from jax.experimental.pallas import tpu as pltpu

F32, BF16 = jnp.float32, jnp.bfloat16

D = 1024
AW = 512
PW = 512
PG = 128
INW = 4096
FS = 704
NFF = 4
NDEV = 8
DEPTH = 2
HEADS = 8
NREL = 513
MAXREL = 256
POOL_WINDOWS = (2, 4, 8, 16)
EPS = 1e-6
SCALE = 0.125
NEG = -1e30
QB = 256
KW = 3 * QB
BAND = 576
ADAM_LR, ADAM_B1, ADAM_B2, ADAM_EPS, ADAM_WD, ADAM_STEP = 0.001, 0.9, 0.999, 1e-08, 0.01, 10
VMEM_LIMIT_V7X = 56 * 1024 * 1024
MESH = pl.DeviceIdType.MESH
GELU_C = 0.7978845608028654
GELU_A = 0.044715


def _call(body, name, grid, in_specs, out_specs, out_shape, scratch=()):
    return pl.pallas_call(
        body, name=name, grid=grid, in_specs=in_specs, out_specs=out_specs, out_shape=out_shape,
        scratch_shapes=list(scratch),
        compiler_params=pltpu.CompilerParams(vmem_limit_bytes=VMEM_LIMIT_V7X))


def _dot(a, b):
    return jnp.dot(a, b, preferred_element_type=F32)


def _dot_nt(a, b):
    return lax.dot_general(a, b, (((1,), (1,)), ((), ())), preferred_element_type=F32)


def _dot_tn(a, b):
    return lax.dot_general(a, b, (((0,), (0,)), ((), ())), preferred_element_type=F32)


def _rms(x, g):
    r = lax.rsqrt(jnp.mean(x * x, axis=-1, keepdims=True) + EPS)
    return x * r * g


def _rms_bwd(x, g, dy):
    r = lax.rsqrt(jnp.mean(x * x, axis=-1, keepdims=True) + EPS)
    xh = x * r
    dg = jnp.sum(dy * xh, axis=0, keepdims=True)
    dxh = dy * g
    dx = r * (dxh - xh * jnp.mean(dxh * xh, axis=-1, keepdims=True))
    return dx, dg


def _sigmoid(x):
    return 1.0 / (1.0 + jnp.exp(-x))


def _gelu_parts(x):
    x2 = x * x
    th = jnp.tanh(GELU_C * (x + GELU_A * x * x2))
    g = 0.5 * x * (1.0 + th)
    dg = 0.5 * (1.0 + th) + 0.5 * x * (1.0 - th * th) * (GELU_C * (1.0 + 3.0 * GELU_A * x2))
    return g, dg


def _row(i):
    return (i, 0)


def _fixed2(*_):
    return (0, 0)


def _rmsnorm_call(x, g, name):
    s = x.shape[0]
    tm = 512

    def body(x_ref, g_ref, o_ref):
        o_ref[...] = _rms(x_ref[...], g_ref[...]).astype(BF16)

    return _call(body, name, (s // tm,),
                 [pl.BlockSpec((tm, D), _row), pl.BlockSpec((1, D), _fixed2)],
                 pl.BlockSpec((tm, D), _row), jax.ShapeDtypeStruct((s, D), BF16))(x, g)


def _mm_in(h, win, name):
    s = h.shape[0]
    tm = min(1024, s)

    def body(h_ref, w_ref, o_ref):
        o_ref[...] = _dot(h_ref[...], w_ref[...])

    return _call(body, name, (NDEV, s // tm),
                 [pl.BlockSpec((tm, D), lambda j, i: (i, 0)), pl.BlockSpec((None, D, 512), lambda j, i: (j, 0, 0))],
                 pl.BlockSpec((tm, 512), lambda j, i: (i, j)), jax.ShapeDtypeStruct((s, INW), F32))(h, win)


def _bias_table(rel_bias):
    a = np.arange(QB)[:, None]
    b = np.arange(KW)[None, :]
    idx = np.clip(a + 512 - b, -MAXREL, MAXREL) + MAXREL
    qc, kc = a // 64, b // 64
    allowed = (kc >= qc) & (kc <= qc + 8)
    return jnp.where(jnp.asarray(allowed)[None], rel_bias[:, jnp.asarray(idx)], NEG)


def _attn_probs(q_ref, k_refs, b_ref, i):
    lane = lax.broadcasted_iota(jnp.int32, (QB, 128), 1)
    q = q_ref[...] * SCALE
    qs = [jnp.where(lane < 64, q, 0.0).astype(BF16), jnp.where(lane >= 64, q, 0.0).astype(BF16)]
    k = jnp.concatenate([r[...] for r in k_refs], axis=0).astype(BF16)
    colb = lax.broadcasted_iota(jnp.int32, (QB, KW), 1) // QB
    valid = colb + i >= 2
    ps = []
    for hh in (0, 1):
        sc = _dot_nt(qs[hh], k) + b_ref[hh]
        sc = jnp.where(valid, sc, NEG)
        m = jnp.max(sc, axis=1, keepdims=True)
        p = jnp.exp(sc - m)
        ps.append(p / jnp.sum(p, axis=1, keepdims=True))
    return qs, ps, k, lane


def _attn_fwd(proj, tab, name):
    s = proj.shape[0]
    nq = s // QB

    def body(q_ref, k0, k1, k2, v0, v1, v2, b_ref, o_ref):
        i = pl.program_id(1)
        _, ps, _, lane = _attn_probs(q_ref, (k0, k1, k2), b_ref, i)
        v = jnp.concatenate([v0[...], v1[...], v2[...]], axis=0).astype(BF16)
        o = [_dot(ps[hh].astype(BF16), v) for hh in (0, 1)]
        o_ref[...] = jnp.where(lane < 64, o[0], o[1]).astype(BF16)

    def kv(col, d):
        return pl.BlockSpec((QB, 128), lambda p, i: (jnp.maximum(i - 2 + d, 0), col + p))

    in_specs = [pl.BlockSpec((QB, 128), lambda p, i: (i, p))]
    in_specs += [kv(4, d) for d in range(3)] + [kv(8, d) for d in range(3)]
    in_specs += [pl.BlockSpec((2, QB, KW), lambda p, i: (p, 0, 0))]
    return _call(body, name, (4, nq), in_specs, pl.BlockSpec((QB, 128), lambda p, i: (i, p)),
                 jax.ShapeDtypeStruct((s, AW), BF16))(proj, proj, proj, proj, proj, proj, proj, tab)


def _pool_fwd(proj, wg, scale, name):
    s = proj.shape[0]
    tb = 512
    e = tb + 16

    def body(u_ref, halo_ref, wg_ref, sc_ref, pooled_ref, mixed_ref):
        i = pl.program_id(0)
        cur = u_ref[...]
        prev = jnp.where(i > 0, halo_ref[...], 0.0)
        xs = jnp.concatenate([prev, cur], axis=0)
        t = i * tb + lax.broadcasted_iota(jnp.int32, (tb, 1), 0)
        for g, w in enumerate(POOL_WINDOWS):
            sl = slice(g * PG, (g + 1) * PG)
            a = xs[:, sl]
            sh = 1
            while sh < w:
                a = a + pltpu.roll(a, sh, 0)
                sh *= 2
            cnt = jnp.minimum(t + 1, w).astype(F32)
            pooled = (a[16:] / cnt - cur[:, sl]).astype(BF16)
            pooled_ref[:, sl] = pooled
            mixed_ref[:, sl] = (_dot(pooled, wg_ref[g].astype(BF16)) * sc_ref[:, sl]).astype(BF16)

    assert e % 8 == 0
    return _call(body, name, (s // tb,),
                 [pl.BlockSpec((tb, PW), lambda i: (i, 3)),
                  pl.BlockSpec((16, PW), lambda i: (jnp.maximum(i * (tb // 16) - 1, 0), 3)),
                  pl.BlockSpec((4, PG, PG), lambda i: (0, 0, 0)), pl.BlockSpec((1, PW), _fixed2)],
                 [pl.BlockSpec((tb, PW), _row), pl.BlockSpec((tb, PW), _row)],
                 [jax.ShapeDtypeStruct((s, PW), BF16), jax.ShapeDtypeStruct((s, PW), BF16)])(proj, proj, wg, scale)


def _branch_fwd(att, mixed, wao, wpo, proj, bgate, name):
    s = att.shape[0]
    tm = 512

    def body(att_ref, mx_ref, wao_ref, wpo_ref, ga_ref, gb_ref, ba_ref, bb_ref, z_ref, ya_ref, yb_ref):
        ya = _dot(att_ref[...], wao_ref[...])
        yb = _dot(mx_ref[...], wpo_ref[...])
        ga = _sigmoid(ga_ref[...] + ba_ref[...])
        gb = _sigmoid(gb_ref[...] + bb_ref[...])
        ya_ref[...] = ya
        yb_ref[...] = yb
        z_ref[...] = (ga * ya + gb * yb).astype(BF16)

    return _call(body, name, (s // tm,),
                 [pl.BlockSpec((tm, AW), _row), pl.BlockSpec((tm, PW), _row),
                  pl.BlockSpec((AW, D), _fixed2), pl.BlockSpec((PW, D), _fixed2),
                  pl.BlockSpec((tm, D), lambda i: (i, 2)), pl.BlockSpec((tm, D), lambda i: (i, 3)),
                  pl.BlockSpec((1, D), lambda i: (0, 0)), pl.BlockSpec((1, D), lambda i: (0, 1))],
                 [pl.BlockSpec((tm, D), _row)] * 3,
                 [jax.ShapeDtypeStruct((s, D), BF16), jax.ShapeDtypeStruct((s, D), F32),
                  jax.ShapeDtypeStruct((s, D), F32)])(att, mixed, wao, wpo, proj, proj, bgate, bgate)


def _mm_o_fwd(z, wo, x, g2, g3, name):
    s = z.shape[0]
    tm = 512

    def body(z_ref, w_ref, x_ref, g2_ref, g3_ref, mix_ref, x1_ref, h2_ref):
        mix = _dot(z_ref[...], w_ref[...])
        x1 = x_ref[...] + _rms(mix, g2_ref[...])
        mix_ref[...] = mix
        x1_ref[...] = x1
        h2_ref[...] = _rms(x1, g3_ref[...]).astype(BF16)

    return _call(body, name, (s // tm,),
                 [pl.BlockSpec((tm, D), _row), pl.BlockSpec((D, D), _fixed2), pl.BlockSpec((tm, D), _row),
                  pl.BlockSpec((1, D), _fixed2), pl.BlockSpec((1, D), _fixed2)],
                 [pl.BlockSpec((tm, D), _row)] * 3,
                 [jax.ShapeDtypeStruct((s, D), F32), jax.ShapeDtypeStruct((s, D), F32),
                  jax.ShapeDtypeStruct((s, D), BF16)])(z, wo, x, g2, g3)


def _mm_up(h2, wup, name):
    s = h2.shape[0]
    tm = min(1024, s)

    def body(h_ref, w_ref, o_ref):
        o_ref[...] = _dot(h_ref[...], w_ref[...])

    return _call(body, name, (NDEV, s // tm),
                 [pl.BlockSpec((tm, D), lambda j, i: (i, 0)), pl.BlockSpec((None, D, FS), lambda j, i: (j, 0, 0))],
                 pl.BlockSpec((None, tm, FS), lambda j, i: (j, i, 0)),
                 jax.ShapeDtypeStruct((NDEV, s, FS), F32))(h2, wup)


def _conv_rows(xs, cw, cb):
    m1 = pltpu.roll(xs, 1, 0)
    m2 = pltpu.roll(xs, 2, 0)
    return cb + cw[2:3] * xs + cw[0:1] * m2 + cw[1:2] * m1, m1, m2


def _conv_gate_fwd(hu, cw, cb, name):
    s = hu.shape[2]
    tb = 512

    def body(hu_ref, halo_ref, cw_ref, cb_ref, a_ref):
        i = pl.program_id(1)
        hc = []
        for sd in (0, 1):
            prev = jnp.where(i > 0, halo_ref[sd], 0.0)
            xs = jnp.concatenate([prev, hu_ref[sd]], axis=0)
            hc.append(_conv_rows(xs, cw_ref[sd], cb_ref[sd])[0][8:])
        a_ref[...] = (_gelu_parts(hc[1])[0] * hc[0]).astype(BF16)

    return _call(body, name, (NFF, s // tb),
                 [pl.BlockSpec((2, None, tb, FS), lambda j, i: (0, j, i, 0)),
                  pl.BlockSpec((2, None, 8, FS), lambda j, i: (0, j, jnp.maximum(i * (tb // 8) - 1, 0), 0)),
                  pl.BlockSpec((2, None, 3, FS), lambda j, i: (0, j, 0, 0)),
                  pl.BlockSpec((2, None, 1, FS), lambda j, i: (0, j, 0, 0))],
                 pl.BlockSpec((None, tb, FS), lambda j, i: (j, i, 0)),
                 jax.ShapeDtypeStruct((NFF, s, FS), BF16))(hu, hu, cw, cb)


def _mm_down_fwd(a, wd, x1, g4, gnext, name):
    s = a.shape[1]
    tm = 512

    def body(a_ref, w_ref, x1_ref, g4_ref, gn_ref, f_ref, x2_ref, hn_ref):
        f = _dot(a_ref[0], w_ref[0:FS, :])
        for j in range(1, NFF):
            f = f + _dot(a_ref[j], w_ref[j * FS:(j + 1) * FS, :])
        x2 = x1_ref[...] + _rms(f, g4_ref[...])
        f_ref[...] = f
        x2_ref[...] = x2
        hn_ref[...] = _rms(x2, gn_ref[...]).astype(BF16)

    return _call(body, name, (s // tm,),
                 [pl.BlockSpec((NFF, tm, FS), lambda i: (0, i, 0)), pl.BlockSpec((NFF * FS, D), _fixed2),
                  pl.BlockSpec((tm, D), _row), pl.BlockSpec((1, D), _fixed2), pl.BlockSpec((1, D), _fixed2)],
                 [pl.BlockSpec((tm, D), _row)] * 3,
                 [jax.ShapeDtypeStruct((s, D), F32), jax.ShapeDtypeStruct((s, D), F32),
                  jax.ShapeDtypeStruct((s, D), BF16)])(a, wd, x1, g4, gnext)


def _mm_down_loss(a, wd, x1, g4, target, name):
    s = a.shape[1]
    tm = 512

    def body(a_ref, w_ref, x1_ref, g4_ref, t_ref, f_ref, dy_ref, loss_ref):
        i = pl.program_id(0)
        f = _dot(a_ref[0], w_ref[0:FS, :])
        for j in range(1, NFF):
            f = f + _dot(a_ref[j], w_ref[j * FS:(j + 1) * FS, :])
        err = x1_ref[...] + _rms(f, g4_ref[...]) - t_ref[...]
        f_ref[...] = f
        dy_ref[...] = err * (1.0 / D)
        part = 0.5 * jnp.sum(jnp.mean(err * err, axis=-1, keepdims=True), axis=0, keepdims=True)

        @pl.when(i == 0)
        def _():
            loss_ref[...] = jnp.zeros_like(loss_ref)

        loss_ref[...] += jnp.broadcast_to(part, loss_ref.shape)

    return _call(body, name, (s // tm,),
                 [pl.BlockSpec((NFF, tm, FS), lambda i: (0, i, 0)), pl.BlockSpec((NFF * FS, D), _fixed2),
                  pl.BlockSpec((tm, D), _row), pl.BlockSpec((1, D), _fixed2), pl.BlockSpec((tm, D), _row)],
                 [pl.BlockSpec((tm, D), _row), pl.BlockSpec((tm, D), _row), pl.BlockSpec((8, 128), _fixed2)],
                 [jax.ShapeDtypeStruct((s, D), F32), jax.ShapeDtypeStruct((s, D), F32),
                  jax.ShapeDtypeStruct((8, 128), F32)])(a, wd, x1, g4, target)


def _norm_bwd(f, g, dy, name):
    s = f.shape[0]
    tm = 512

    def body(f_ref, g_ref, dy_ref, df_ref, dg_ref):
        i = pl.program_id(0)
        df, dg = _rms_bwd(f_ref[...], g_ref[...], dy_ref[...])
        df_ref[...] = df.astype(BF16)

        @pl.when(i == 0)
        def _():
            dg_ref[...] = jnp.zeros_like(dg_ref)

        dg_ref[...] += dg

    return _call(body, name, (s // tm,),
                 [pl.BlockSpec((tm, D), _row), pl.BlockSpec((1, D), _fixed2), pl.BlockSpec((tm, D), _row)],
                 [pl.BlockSpec((tm, D), _row), pl.BlockSpec((1, D), _fixed2)],
                 [jax.ShapeDtypeStruct((s, D), BF16), jax.ShapeDtypeStruct((1, D), F32)])(f, g, dy)


def _mm_da(df, wd, name):
    s = df.shape[0]
    tm = 512

    def body(df_ref, w_ref, da_ref):
        dfv = df_ref[...]
        for j in range(NFF):
            da_ref[j] = _dot_nt(dfv, w_ref[j * FS:(j + 1) * FS, :])

    return _call(body, name, (s // tm,),
                 [pl.BlockSpec((tm, D), _row), pl.BlockSpec((NFF * FS, D), _fixed2)],
                 pl.BlockSpec((NFF, tm, FS), lambda i: (0, i, 0)),
                 jax.ShapeDtypeStruct((NFF, s, FS), F32))(df, wd)


def _conv_gate_bwd(hu, da, cw, cb, name):
    s = hu.shape[2]
    tb = 512
    nt = s // tb
    e = tb + 16

    def body(hu_ref, prev_ref, next_ref, da_ref, dan_ref, cw_ref, cb_ref, dhu_ref, dcw_ref, dcb_ref):
        i = pl.program_id(1)
        first, last = i == 0, i == nt - 1
        hc, taps = [], []
        for sd in (0, 1):
            xs = jnp.concatenate([jnp.where(first, 0.0, prev_ref[sd]), hu_ref[sd],
                                  jnp.where(last, 0.0, next_ref[sd])], axis=0)
            h, m1, m2 = _conv_rows(xs, cw_ref[sd], cb_ref[sd])
            hc.append(h)
            taps.append((m2, m1, xs))
        da = jnp.concatenate([jnp.zeros((8, FS), F32), da_ref[...], jnp.where(last, 0.0, dan_ref[...])], axis=0)
        gl, dgl = _gelu_parts(hc[1])
        dhc = (da * gl, da * hc[0] * dgl)

        @pl.when(i == 0)
        def _():
            dcw_ref[...] = jnp.zeros_like(dcw_ref)
            dcb_ref[...] = jnp.zeros_like(dcb_ref)

        for sd in (0, 1):
            cw = cw_ref[sd]
            dh = dhc[sd]
            dhu = cw[2:3] * dh + cw[1:2] * pltpu.roll(dh, e - 1, 0) + cw[0:1] * pltpu.roll(dh, e - 2, 0)
            dhu_ref[sd] = dhu[8:8 + tb].astype(BF16)
            dc = dh[8:8 + tb]
            dcw_ref[sd] += jnp.concatenate(
                [jnp.sum(dc * tap[8:8 + tb], axis=0, keepdims=True) for tap in taps[sd]], axis=0)
            dcb_ref[sd] += jnp.sum(dc, axis=0, keepdims=True)

    nb8 = s // 8
    return _call(body, name, (NFF, nt),
                 [pl.BlockSpec((2, None, tb, FS), lambda j, i: (0, j, i, 0)),
                  pl.BlockSpec((2, None, 8, FS), lambda j, i: (0, j, jnp.maximum(i * (tb // 8) - 1, 0), 0)),
                  pl.BlockSpec((2, None, 8, FS), lambda j, i: (0, j, jnp.minimum((i + 1) * (tb // 8), nb8 - 1), 0)),
                  pl.BlockSpec((None, tb, FS), lambda j, i: (j, i, 0)),
                  pl.BlockSpec((None, 8, FS), lambda j, i: (j, jnp.minimum((i + 1) * (tb // 8), nb8 - 1), 0)),
                  pl.BlockSpec((2, None, 3, FS), lambda j, i: (0, j, 0, 0)),
                  pl.BlockSpec((2, None, 1, FS), lambda j, i: (0, j, 0, 0))],
                 [pl.BlockSpec((2, None, tb, FS), lambda j, i: (0, j, i, 0)),
                  pl.BlockSpec((2, None, 3, FS), lambda j, i: (0, j, 0, 0)),
                  pl.BlockSpec((2, None, 1, FS), lambda j, i: (0, j, 0, 0))],
                 [jax.ShapeDtypeStruct((2, NFF, s, FS), BF16), jax.ShapeDtypeStruct((2, NFF, 3, FS), F32),
                  jax.ShapeDtypeStruct((2, NFF, 1, FS), F32)])(hu, hu, hu, da, da, cw, cb)


def _dw_down(a, df, name):
    s = a.shape[1]
    tk = 512
    nk = s // tk

    def body(a_ref, df_ref, o_ref, acc):
        k = pl.program_id(1)

        @pl.when(k == 0)
        def _():
            acc[...] = jnp.zeros_like(acc)

        acc[...] += _dot_tn(a_ref[...], df_ref[...])

        @pl.when(k == nk - 1)
        def _():
            o_ref[0] = acc[0:FS // 2, :].astype(BF16)
            o_ref[1] = acc[FS // 2:FS, :].astype(BF16)

    return _call(body, name, (NFF, nk),
                 [pl.BlockSpec((None, tk, FS), lambda j, k: (j, k, 0)), pl.BlockSpec((tk, D), lambda j, k: (k, 0))],
                 pl.BlockSpec((2, FS // 2, D), lambda j, k: (j, 0, 0)),
                 jax.ShapeDtypeStruct((NDEV, FS // 2, D), BF16), [pltpu.VMEM((FS, D), F32)])(a, df)


def _mm_dh2(dhu, wup, x1, g3, dx2, mix, g2, name):
    s = x1.shape[0]
    tm = 512

    def body(dhu_ref, w_ref, x1_ref, g3_ref, dx2_ref, mix_ref, g2_ref, dx1_ref, dmix_ref, dg3_ref, dg2_ref, acc):
        i, j = pl.program_id(0), pl.program_id(1)

        @pl.when((i == 0) & (j == 0))
        def _():
            dg3_ref[...] = jnp.zeros_like(dg3_ref)
            dg2_ref[...] = jnp.zeros_like(dg2_ref)

        @pl.when(j == 0)
        def _():
            acc[...] = jnp.zeros_like(acc)

        acc[...] += _dot_nt(dhu_ref[...], w_ref[...])

        @pl.when(j == NDEV - 1)
        def _():
            dn, dg3 = _rms_bwd(x1_ref[...], g3_ref[...], acc[...])
            dx1 = dx2_ref[...] + dn
            dmix, dg2 = _rms_bwd(mix_ref[...], g2_ref[...], dx1)
            dx1_ref[...] = dx1
            dmix_ref[...] = dmix.astype(BF16)
            dg3_ref[...] += dg3
            dg2_ref[...] += dg2

    rowi = lambda i, j: (i, 0)
    fix = lambda i, j: (0, 0)
    return _call(body, name, (s // tm, NDEV),
                 [pl.BlockSpec((None, None, tm, FS), lambda i, j: (j // NFF, j % NFF, i, 0)),
                  pl.BlockSpec((None, D, FS), lambda i, j: (j, 0, 0)),
                  pl.BlockSpec((tm, D), rowi), pl.BlockSpec((1, D), fix), pl.BlockSpec((tm, D), rowi),
                  pl.BlockSpec((tm, D), rowi), pl.BlockSpec((1, D), fix)],
                 [pl.BlockSpec((tm, D), rowi), pl.BlockSpec((tm, D), rowi), pl.BlockSpec((1, D), fix),
                  pl.BlockSpec((1, D), fix)],
                 [jax.ShapeDtypeStruct((s, D), F32), jax.ShapeDtypeStruct((s, D), BF16),
                  jax.ShapeDtypeStruct((1, D), F32), jax.ShapeDtypeStruct((1, D), F32)],
                 [pltpu.VMEM((tm, D), F32)])(dhu, wup, x1, g3, dx2, mix, g2)


def _dw_up(h2, dhu, name):
    s = h2.shape[0]
    tk = 512
    nk = s // tk

    def body(h_ref, d_ref, o_ref, acc):
        k = pl.program_id(1)

        @pl.when(k == 0)
        def _():
            acc[...] = jnp.zeros_like(acc)

        acc[...] += _dot_tn(h_ref[...], d_ref[...])

        @pl.when(k == nk - 1)
        def _():
            o_ref[...] = acc[...].astype(BF16)

    return _call(body, name, (NDEV, nk),
                 [pl.BlockSpec((tk, D), lambda j, k: (k, 0)),
                  pl.BlockSpec((None, None, tk, FS), lambda j, k: (j // NFF, j % NFF, k, 0))],
                 pl.BlockSpec((None, D, FS), lambda j, k: (j, 0, 0)),
                 jax.ShapeDtypeStruct((NDEV, D, FS), BF16), [pltpu.VMEM((D, FS), F32)])(h2, dhu)


def _mm_dz(dmix, wo, proj, bgate, ya, yb, name):
    s = dmix.shape[0]
    tm = 512

    def body(dm_ref, w_ref, ga_ref, gb_ref, ba_ref, bb_ref, ya_ref, yb_ref, dya_ref, dyb_ref, dg_ref, dbg_ref):
        i = pl.program_id(0)
        dz = _dot_nt(dm_ref[...], w_ref[...])
        ga = _sigmoid(ga_ref[...] + ba_ref[...])
        gb = _sigmoid(gb_ref[...] + bb_ref[...])
        dya_ref[...] = (dz * ga).astype(BF16)
        dyb_ref[...] = (dz * gb).astype(BF16)
        dga = dz * ya_ref[...] * ga * (1.0 - ga)
        dgb = dz * yb_ref[...] * gb * (1.0 - gb)
        dg_ref[:, 0:D] = dga.astype(BF16)
        dg_ref[:, D:2 * D] = dgb.astype(BF16)

        @pl.when(i == 0)
        def _():
            dbg_ref[...] = jnp.zeros_like(dbg_ref)

        dbg_ref[:, 0:D] += jnp.sum(dga, axis=0, keepdims=True)
        dbg_ref[:, D:2 * D] += jnp.sum(dgb, axis=0, keepdims=True)

    return _call(body, name, (s // tm,),
                 [pl.BlockSpec((tm, D), _row), pl.BlockSpec((D, D), _fixed2),
                  pl.BlockSpec((tm, D), lambda i: (i, 2)), pl.BlockSpec((tm, D), lambda i: (i, 3)),
                  pl.BlockSpec((1, D), lambda i: (0, 0)), pl.BlockSpec((1, D), lambda i: (0, 1)),
                  pl.BlockSpec((tm, D), _row), pl.BlockSpec((tm, D), _row)],
                 [pl.BlockSpec((tm, D), _row), pl.BlockSpec((tm, D), _row), pl.BlockSpec((tm, 2 * D), _row),
                  pl.BlockSpec((1, 2 * D), _fixed2)],
                 [jax.ShapeDtypeStruct((s, D), BF16), jax.ShapeDtypeStruct((s, D), BF16),
                  jax.ShapeDtypeStruct((s, 2 * D), BF16), jax.ShapeDtypeStruct((1, 2 * D), F32)])(
                     dmix, wo, proj, proj, bgate, bgate, ya, yb)


def _dw_o(z, dmix, name):
    s = z.shape[0]
    tk = 512
    nk = s // tk

    def body(z_ref, d_ref, o_ref, acc):
        k = pl.program_id(0)

        @pl.when(k == 0)
        def _():
            acc[...] = jnp.zeros_like(acc)

        acc[...] += _dot_tn(z_ref[...], d_ref[...])

        @pl.when(k == nk - 1)
        def _():
            for j in range(NDEV):
                o_ref[j] = acc[j * 128:(j + 1) * 128, :].astype(BF16)

    return _call(body, name, (nk,),
                 [pl.BlockSpec((tk, D), _row), pl.BlockSpec((tk, D), _row)],
                 pl.BlockSpec((NDEV, 128, D), lambda k: (0, 0, 0)),
                 jax.ShapeDtypeStruct((NDEV, 128, D), BF16), [pltpu.VMEM((D, D), F32)])(z, dmix)


def _branch_bwd(dya, dyb, wao, wpo, name):
    s = dya.shape[0]
    tm = 512

    def body(dya_ref, dyb_ref, wao_ref, wpo_ref, datt_ref, dmx_ref):
        datt_ref[...] = _dot_nt(dya_ref[...], wao_ref[...]).astype(BF16)
        dmx_ref[...] = _dot_nt(dyb_ref[...], wpo_ref[...])

    return _call(body, name, (s // tm,),
                 [pl.BlockSpec((tm, D), _row), pl.BlockSpec((tm, D), _row),
                  pl.BlockSpec((AW, D), _fixed2), pl.BlockSpec((PW, D), _fixed2)],
                 [pl.BlockSpec((tm, AW), _row), pl.BlockSpec((tm, PW), _row)],
                 [jax.ShapeDtypeStruct((s, AW), BF16), jax.ShapeDtypeStruct((s, PW), F32)])(dya, dyb, wao, wpo)


def _dw_branch(att, mixed, dya, dyb, name):
    s = att.shape[0]
    tk = 512
    nk = s // tk

    def body(att_ref, mx_ref, dya_ref, dyb_ref, oa_ref, ob_ref, acca, accb):
        k = pl.program_id(0)

        @pl.when(k == 0)
        def _():
            acca[...] = jnp.zeros_like(acca)
            accb[...] = jnp.zeros_like(accb)

        acca[...] += _dot_tn(att_ref[...], dya_ref[...])
        accb[...] += _dot_tn(mx_ref[...], dyb_ref[...])

        @pl.when(k == nk - 1)
        def _():
            for j in range(NDEV):
                oa_ref[j] = acca[:, j * 128:(j + 1) * 128].astype(BF16)
                ob_ref[j] = accb[:, j * 128:(j + 1) * 128].astype(BF16)

    out = jax.ShapeDtypeStruct((NDEV, AW, 128), BF16)
    return _call(body, name, (nk,),
                 [pl.BlockSpec((tk, AW), _row), pl.BlockSpec((tk, PW), _row),
                  pl.BlockSpec((tk, D), _row), pl.BlockSpec((tk, D), _row)],
                 [pl.BlockSpec((NDEV, AW, 128), lambda k: (0, 0, 0))] * 2, [out, out],
                 [pltpu.VMEM((AW, D), F32), pltpu.VMEM((PW, D), F32)])(att, mixed, dya, dyb)


def _attn_bwd(proj, datt, tab, name):
    s = proj.shape[0]
    nq = s // QB

    def body(q_ref, k0, k1, k2, v0, v1, v2, do_ref, b_ref, dq_ref, dk_ref, dv_ref, db_ref, dka, dkb, dva, dvb):
        i = pl.program_id(1)

        @pl.when(i == 0)
        def _():
            for r in (dka, dkb, dva, dvb):
                r[...] = jnp.zeros_like(r)
            db_ref[...] = jnp.zeros_like(db_ref)

        @pl.when(i < nq)
        def _():
            qs, ps, k, lane = _attn_probs(q_ref, (k0, k1, k2), b_ref, i)
            v = jnp.concatenate([v0[...], v1[...], v2[...]], axis=0).astype(BF16)
            do = do_ref[...]
            dq = jnp.zeros((QB, 128), F32)
            dkw = jnp.zeros((KW, 128), F32)
            dvw = jnp.zeros((KW, 128), F32)
            for hh in (0, 1):
                mine = (lane < 64) if hh == 0 else (lane >= 64)
                doh = jnp.where(mine, do, jnp.zeros_like(do))
                kmask = lax.broadcasted_iota(jnp.int32, (KW, 128), 1)
                kh = jnp.where((kmask < 64) if hh == 0 else (kmask >= 64), k, jnp.zeros_like(k))
                p = ps[hh]
                dp = _dot_nt(doh, v)
                ds = p * (dp - jnp.sum(dp * p, axis=1, keepdims=True))
                db_ref[hh] += ds
                dsb = ds.astype(BF16)
                dq = dq + _dot(dsb, kh)
                dkw = dkw + _dot_tn(dsb, qs[hh])
                dvw = dvw + _dot_tn(p.astype(BF16), doh)
            dq_ref[...] = (dq * SCALE).astype(BF16)
            dk_ref[...] = (dka[...] + dkw[0:QB]).astype(BF16)
            dka[...] = dkb[...] + dkw[QB:2 * QB]
            dkb[...] = dkw[2 * QB:3 * QB]
            dv_ref[...] = (dva[...] + dvw[0:QB]).astype(BF16)
            dva[...] = dvb[...] + dvw[QB:2 * QB]
            dvb[...] = dvw[2 * QB:3 * QB]

        @pl.when(i >= nq)
        def _():
            dk_ref[...] = dka[...].astype(BF16)
            dka[...] = dkb[...]
            dkb[...] = jnp.zeros_like(dkb)
            dv_ref[...] = dva[...].astype(BF16)
            dva[...] = dvb[...]
            dvb[...] = jnp.zeros_like(dvb)

    def kv(col, d):
        return pl.BlockSpec((QB, 128), lambda p, i: (jnp.clip(i - 2 + d, 0, nq - 1), col + p))

    cur = lambda p, i: (jnp.minimum(i, nq - 1), p)
    done = lambda p, i: (jnp.maximum(i - 2, 0), p)
    in_specs = [pl.BlockSpec((QB, 128), cur)]
    in_specs += [kv(4, d) for d in range(3)] + [kv(8, d) for d in range(3)]
    in_specs += [pl.BlockSpec((QB, 128), cur), pl.BlockSpec((2, QB, KW), lambda p, i: (p, 0, 0))]
    o = jax.ShapeDtypeStruct((s, AW), BF16)
    return _call(body, name, (4, nq + 2), in_specs,
                 [pl.BlockSpec((QB, 128), cur), pl.BlockSpec((QB, 128), done), pl.BlockSpec((QB, 128), done),
                  pl.BlockSpec((2, QB, KW), lambda p, i: (p, 0, 0))],
                 [o, o, o, jax.ShapeDtypeStruct((HEADS, QB, KW), F32)],
                 [pltpu.VMEM((QB, 128), F32)] * 4)(proj, proj, proj, proj, proj, proj, proj, datt, tab)


def _rel_bias_grad(dtab, name):
    wdt = 640

    def body(x_ref, o_ref):
        x = x_ref[...]
        xc = x[0:64, 0:wdt]
        for qc in range(1, QB // 64):
            xc = xc + pltpu.roll(x[qc * 64:(qc + 1) * 64, :], KW - qc * 64, 1)[:, 0:wdt]
        r = lax.broadcasted_iota(jnp.int32, (64, 64), 0)
        c = lax.broadcasted_iota(jnp.int32, (64, 64), 1)
        flip = (r + c == 63).astype(F32)
        y = jnp.dot(flip, xc, preferred_element_type=F32, precision=lax.Precision.HIGHEST)
        z = pltpu.roll(y, 0, 1, stride=1, stride_axis=0)
        t = jnp.broadcast_to(jnp.sum(z, axis=0, keepdims=True), (8, wdt))
        e = lax.broadcasted_iota(jnp.int32, (wdt, wdt), 0)
        rr = lax.broadcasted_iota(jnp.int32, (wdt, wdt), 1)
        onehot = (jnp.clip(BAND - 1 - e, -MAXREL, MAXREL) + MAXREL == rr).astype(F32)
        o_ref[...] = jnp.dot(t, onehot, preferred_element_type=F32, precision=lax.Precision.HIGHEST)

    return _call(body, name, (HEADS,), [pl.BlockSpec((None, QB, KW), lambda h: (h, 0, 0))],
                 pl.BlockSpec((None, 8, wdt), lambda h: (h, 0, 0)),
                 jax.ShapeDtypeStruct((HEADS, 8, wdt), F32))(dtab)


def _pool_bwd(dmixed, pooled, wg, scale, name):
    s = dmixed.shape[0]
    tb = 512
    nt = s // tb
    e = tb + 16

    def body(dm_ref, dmn_ref, pl_ref, wg_ref, sc_ref, du_ref, dwg_ref, dsc_ref):
        i = pl.program_id(0)
        dm = jnp.concatenate([dm_ref[...], jnp.where(i == nt - 1, 0.0, dmn_ref[...])], axis=0)
        t = i * tb + lax.broadcasted_iota(jnp.int32, (e, 1), 0)

        @pl.when(i == 0)
        def _():
            dwg_ref[...] = jnp.zeros_like(dwg_ref)
            dsc_ref[...] = jnp.zeros_like(dsc_ref)

        for g, w in enumerate(POOL_WINDOWS):
            sl = slice(g * PG, (g + 1) * PG)
            wgb = wg_ref[g].astype(BF16)
            pb = pl_ref[:, sl]
            dsc_ref[:, sl] += jnp.sum(dm[0:tb, sl] * _dot(pb, wgb), axis=0, keepdims=True)
            dpre = (dm[:, sl] * sc_ref[:, sl]).astype(BF16)
            dwg_ref[g] += _dot_tn(pb, dpre[0:tb])
            dpool = _dot_nt(dpre, wgb)
            a = dpool / jnp.minimum(t + 1, w).astype(F32)
            sh = 1
            while sh < w:
                a = a + pltpu.roll(a, e - sh, 0)
                sh *= 2
            du_ref[:, sl] = (a[0:tb] - dpool[0:tb]).astype(BF16)

    nb16 = s // 16
    return _call(body, name, (nt,),
                 [pl.BlockSpec((tb, PW), _row),
                  pl.BlockSpec((16, PW), lambda i: (jnp.minimum((i + 1) * (tb // 16), nb16 - 1), 0)),
                  pl.BlockSpec((tb, PW), _row), pl.BlockSpec((4, PG, PG), lambda i: (0, 0, 0)),
                  pl.BlockSpec((1, PW), _fixed2)],
                 [pl.BlockSpec((tb, PW), _row), pl.BlockSpec((4, PG, PG), lambda i: (0, 0, 0)),
                  pl.BlockSpec((1, PW), _fixed2)],
                 [jax.ShapeDtypeStruct((s, PW), BF16), jax.ShapeDtypeStruct((4, PG, PG), F32),
                  jax.ShapeDtypeStruct((1, PW), F32)])(dmixed, dmixed, pooled, wg, scale)


def _dproj_specs(tm, rows, seg_of, freeze):
    def piece(n):
        if not freeze:
            return pl.BlockSpec((tm, 512), lambda a, b: (rows(a, b), 0))
        return pl.BlockSpec((tm, 512), lambda a, b: (jnp.where(seg_of(a, b) == n, rows(a, b), 0), 0))

    def gates_map(a, b):
        row = jnp.where(seg_of(a, b) >= 4, rows(a, b), 0) if freeze else rows(a, b)
        return (row, jnp.maximum(seg_of(a, b) - 4, 0))

    return [piece(0), piece(1), piece(2), piece(3), pl.BlockSpec((tm, 512), gates_map)]


def _mm_dh(segs, win, x, g1, dx1, name):
    s = x.shape[0]
    tm = 512

    def body(dq_ref, dk_ref, dv_ref, du_ref, dg_ref, w_ref, x_ref, g1_ref, dx1_ref, dx_ref, dg1_ref, acc):
        i, j = pl.program_id(0), pl.program_id(1)

        @pl.when((i == 0) & (j == 0))
        def _():
            dg1_ref[...] = jnp.zeros_like(dg1_ref)

        @pl.when(j == 0)
        def _():
            acc[...] = jnp.zeros_like(acc)

        for n, ref in enumerate((dq_ref, dk_ref, dv_ref, du_ref)):
            @pl.when(j == n)
            def _(ref=ref):
                acc[...] += _dot_nt(ref[...], w_ref[...])

        @pl.when(j >= 4)
        def _():
            acc[...] += _dot_nt(dg_ref[...], w_ref[...])

        @pl.when(j == NDEV - 1)
        def _():
            dn, dg1 = _rms_bwd(x_ref[...], g1_ref[...], acc[...])
            dx_ref[...] = dx1_ref[...] + dn
            dg1_ref[...] += dg1

    rowi = lambda i, j: (i, 0)
    fix = lambda i, j: (0, 0)
    return _call(body, name, (s // tm, NDEV),
                 _dproj_specs(tm, lambda i, j: i, lambda i, j: j, False)
                 + [pl.BlockSpec((None, D, 512), lambda i, j: (j, 0, 0)), pl.BlockSpec((tm, D), rowi),
                    pl.BlockSpec((1, D), fix), pl.BlockSpec((tm, D), rowi)],
                 [pl.BlockSpec((tm, D), rowi), pl.BlockSpec((1, D), fix)],
                 [jax.ShapeDtypeStruct((s, D), F32), jax.ShapeDtypeStruct((1, D), F32)],
                 [pltpu.VMEM((tm, D), F32)])(*segs, win, x, g1, dx1)


def _dw_in(h, segs, name):
    s = h.shape[0]
    tk = 512
    nk = s // tk

    def body(h_ref, dq_ref, dk_ref, dv_ref, du_ref, dg_ref, o_ref, acc):
        j, k = pl.program_id(0), pl.program_id(1)

        @pl.when(k == 0)
        def _():
            acc[...] = jnp.zeros_like(acc)

        for n, ref in enumerate((dq_ref, dk_ref, dv_ref, du_ref)):
            @pl.when(j == n)
            def _(ref=ref):
                acc[...] += _dot_tn(h_ref[...], ref[...])

        @pl.when(j >= 4)
        def _():
            acc[...] += _dot_tn(h_ref[...], dg_ref[...])

        @pl.when(k == nk - 1)
        def _():
            o_ref[...] = acc[...].astype(BF16)

    return _call(body, name, (NDEV, nk),
                 [pl.BlockSpec((tk, D), lambda j, k: (k, 0))] + _dproj_specs(tk, lambda j, k: k, lambda j, k: j, True),
                 pl.BlockSpec((None, D, 512), lambda j, k: (j, 0, 0)),
                 jax.ShapeDtypeStruct((NDEV, D, 512), BF16), [pltpu.VMEM((D, 512), F32)])(h, *segs)


HBM = pl.BlockSpec(memory_space=pltpu.HBM)


def _slot(px, py, pc):
    return 4 * px + 2 * py + pc


def _all_gather(shards, name):
    n = len(shards)

    def body(*refs):
        ins, outs = refs[:n], refs[n:2 * n]
        send_sems, recv_sems, local_sems = refs[2 * n:]
        x, y, c = lax.axis_index("x"), lax.axis_index("y"), lax.axis_index("c")
        me, sibling = (x, y, c), (x, y, 1 - c)
        chips = [(1 - x, y), (x, 1 - y), (1 - x, 1 - y)]

        def copy(w, k, block, to, src=None):
            dst = outs[w].at[_slot(*block)]
            return pltpu.make_async_remote_copy(
                src_ref=dst if src is None else src, dst_ref=dst, send_sem=send_sems.at[7 * w + k],
                recv_sem=recv_sems.at[7 * w + k], device_id=to, device_id_type=MESH)

        mine, first, passed = [], [], []
        for w in range(n):
            mine.append(pltpu.make_async_copy(ins[w], outs[w].at[_slot(*me)], local_sems.at[w]))
            mine[-1].start()
            first.append(copy(w, 0, me, sibling, src=ins[w]))
            first += [copy(w, 1 + j, me, (*chip, c), src=ins[w]) for j, chip in enumerate(chips)]
        for cp in first:
            cp.start()
        for w in range(n):
            for j, chip in enumerate(chips):
                copy(w, 1 + j, (*chip, c), me).wait_recv()
                passed.append(copy(w, 4 + j, (*chip, c), sibling))
                passed[-1].start()
        for w in range(n):
            copy(w, 0, sibling, me).wait_recv()
            for j, chip in enumerate(chips):
                copy(w, 4 + j, (*chip, 1 - c), me).wait_recv()
        for cp in first + passed:
            cp.wait_send()
        for cp in mine:
            cp.wait()

    return pl.pallas_call(
        body, name=name, in_specs=[HBM] * n, out_specs=[HBM] * n,
        out_shape=[jax.ShapeDtypeStruct((NDEV,) + a.shape, a.dtype) for a in shards],
        scratch_shapes=[pltpu.SemaphoreType.DMA((7 * n,)), pltpu.SemaphoreType.DMA((7 * n,)),
                        pltpu.SemaphoreType.DMA((n,))])(*shards)


def _all_to_all(parts, name):
    n = len(parts)

    def body(*refs):
        ins, outs = refs[:n], refs[n:2 * n]
        send_sems, recv_sems, local_sems = refs[2 * n:]
        x, y, c = lax.axis_index("x"), lax.axis_index("y"), lax.axis_index("c")
        me = _slot(x, y, c)
        peers = []
        for r in range(1, NDEV):
            peers.append(((1 - x) if r & 4 else x, (1 - y) if r & 2 else y, (1 - c) if r & 1 else c))
        copies, mine = [], []
        for w in range(n):
            mine.append(pltpu.make_async_copy(ins[w].at[me], outs[w].at[me], local_sems.at[w]))
            mine[-1].start()
            for r, peer in enumerate(peers):
                copies.append(pltpu.make_async_remote_copy(
                    src_ref=ins[w].at[_slot(*peer)], dst_ref=outs[w].at[me], send_sem=send_sems.at[7 * w + r],
                    recv_sem=recv_sems.at[7 * w + r], device_id=peer, device_id_type=MESH))
                copies[-1].start()
        for w in range(n):
            for r, peer in enumerate(peers):
                pltpu.make_async_remote_copy(
                    src_ref=ins[w].at[me], dst_ref=outs[w].at[_slot(*peer)], send_sem=send_sems.at[7 * w + r],
                    recv_sem=recv_sems.at[7 * w + r], device_id=peer, device_id_type=MESH).wait_recv()
        for cp in copies:
            cp.wait_send()
        for cp in mine:
            cp.wait()

    return pl.pallas_call(
        body, name=name, in_specs=[HBM] * n, out_specs=[HBM] * n,
        out_shape=[jax.ShapeDtypeStruct(a.shape, a.dtype) for a in parts],
        scratch_shapes=[pltpu.SemaphoreType.DMA((7 * n,)), pltpu.SemaphoreType.DMA((7 * n,)),
                        pltpu.SemaphoreType.DMA((n,))])(*parts)


def _adamw(w, g, m, v):
    m = ADAM_B1 * m + (1.0 - ADAM_B1) * g
    v = ADAM_B2 * v + (1.0 - ADAM_B2) * (g * g)
    m_hat = m / (1.0 - ADAM_B1 ** ADAM_STEP)
    v_hat = v / (1.0 - ADAM_B2 ** ADAM_STEP)
    delta = -ADAM_LR * (m_hat / (jnp.sqrt(v_hat) + ADAM_EPS) + ADAM_WD * w)
    return delta, m, v


def _adamw_sharded(recvs, w, m, v, tr, name):
    _, r, cdim = w.shape
    nr = r // tr

    def body(r0_ref, r1_ref, w_ref, m_ref, v_ref, g_ref, d_ref, nm_ref, nv_ref):
        l = pl.program_id(0)

        def total(ref):
            acc = ref[0].astype(F32)
            for k in range(1, NDEV):
                acc = acc + ref[k].astype(F32)
            return acc

        g = jnp.where(l == 0, total(r0_ref), total(r1_ref))
        d, nm, nv = _adamw(w_ref[...], g, m_ref[...], v_ref[...])
        g_ref[...] = g
        d_ref[...] = d
        nm_ref[...] = nm
        nv_ref[...] = nv

    mine = pl.BlockSpec((None, tr, cdim), lambda l, i: (l, i, 0))
    out = jax.ShapeDtypeStruct(w.shape, F32)
    return _call(body, name, (DEPTH, nr),
                 [pl.BlockSpec((NDEV, tr, cdim), lambda l, i: (0, jnp.where(l == 0, i, nr - 1), 0)),
                  pl.BlockSpec((NDEV, tr, cdim), lambda l, i: (0, jnp.where(l == 1, i, 0), 0)),
                  mine, mine, mine],
                 [mine] * 4, [out] * 4)(recvs[0], recvs[1], w, m, v)


def _adamw_small(parts, w, m, v, name):
    r = w.shape[0]

    def body(p_ref, w_ref, m_ref, v_ref, g_ref, d_ref, nm_ref, nv_ref):
        g = p_ref[0]
        for k in range(1, NDEV):
            g = g + p_ref[k]
        d, nm, nv = _adamw(w_ref[...], g, m_ref[...], v_ref[...])
        g_ref[...] = g
        d_ref[...] = d
        nm_ref[...] = nm
        nv_ref[...] = nv

    whole = pl.BlockSpec((r, 1024), _fixed2)
    out = jax.ShapeDtypeStruct((r, 1024), F32)
    return _call(body, name, (1,), [pl.BlockSpec((NDEV, r, 1024), lambda i: (0, 0, 0)), whole, whole, whole],
                 [whole] * 4, [out] * 4)(parts, w, m, v)


SMALL = (("norm_mix_pre", (D,)), ("b_gate", (2 * D,)), ("rel_bias", (HEADS, NREL)), ("w_pool_group", (4, PG, PG)),
         ("pool_scale", (PW,)), ("norm_mix_post", (D,)), ("norm_ffn_pre", (D,)), ("conv_b", (NFF * 2 * FS,)),
         ("norm_ffn_post", (D,)))
SHARDED = (("w_in", 256), ("w_attn_out", 512), ("w_pool_out", 512), ("w_o", 128), ("w_up", 256), ("conv_w", 3),
           ("w_down", 176))
SMALL_ROWS = 168


def _pack_small(tree):
    flat = jnp.concatenate([tree[name].reshape(-1) for name, _ in SMALL])
    return jnp.pad(flat, (0, SMALL_ROWS * 1024 - flat.shape[0])).reshape(SMALL_ROWS, 1024)


def _unpack_small(packed):
    flat = packed.reshape(-1)
    out, at = {}, 0
    for name, shape in SMALL:
        size = DEPTH * int(np.prod(shape))
        out[name] = flat[at:at + size].reshape((DEPTH,) + shape)
        at += size
    return out


def _layer_fwd(l, x, h, p, gath, target, gnext):
    n = f"l{l}"
    g = lambda name: p[name][l].reshape(1, -1)
    res = {"x": x, "h": h}
    res["tab"] = _bias_table(p["rel_bias"][l])
    proj = _mm_in(h, gath["w_in"], f"mm_in_{n}")
    att = _attn_fwd(proj, res["tab"], f"attn_fwd_{n}")
    pooled, mixed = _pool_fwd(proj, p["w_pool_group"][l], g("pool_scale"), f"pool_fwd_{n}")
    bgate = g("b_gate")
    z, ya, yb = _branch_fwd(att, mixed, gath["w_attn_out"], gath["w_pool_out"], proj, bgate, f"branch_fwd_{n}")
    mix, x1, h2 = _mm_o_fwd(z, gath["w_o"], x, g("norm_mix_post"), g("norm_ffn_pre"), f"mm_o_fwd_{n}")
    hu = _mm_up(h2, gath["w_up"], f"mm_up_{n}").reshape(2, NFF, -1, FS)
    cw = gath["conv_w"].reshape(2, NFF, 3, FS)
    cb = p["conv_b"][l].reshape(2, NFF, 1, FS)
    a = _conv_gate_fwd(hu, cw, cb, f"conv_gate_fwd_{n}")
    res.update(proj=proj, att=att, pooled=pooled, mixed=mixed, z=z, ya=ya, yb=yb, mix=mix, x1=x1, h2=h2, hu=hu,
               a=a, cw=cw, cb=cb, bgate=bgate)
    if target is None:
        f, x2, hn = _mm_down_fwd(a, gath["w_down"], x1, g("norm_ffn_post"), gnext, f"mm_down_fwd_{n}")
        res["f"] = f
        return res, x2, hn
    f, dy, loss = _mm_down_loss(a, gath["w_down"], x1, g("norm_ffn_post"), target, f"mm_down_loss_{n}")
    res["f"] = f
    return res, dy, loss


def _layer_bwd(l, dx2, p, gath, res):
    n = f"l{l}"
    g = lambda name: p[name][l].reshape(1, -1)
    big, small = {}, {}
    df, small["norm_ffn_post"] = _norm_bwd(res["f"], g("norm_ffn_post"), dx2, f"norm_bwd_{n}")
    da = _mm_da(df, gath["w_down"], f"mm_da_{n}")
    dhu, dcw, dcb = _conv_gate_bwd(res["hu"], da, res["cw"], res["cb"], f"conv_gate_bwd_{n}")
    big["conv_w"] = dcw.reshape(NDEV, 3, FS)
    small["conv_b"] = dcb
    big["w_down"] = _dw_down(res["a"], df, f"dw_down_{n}")
    dx1, dmix, small["norm_ffn_pre"], small["norm_mix_post"] = _mm_dh2(
        dhu, gath["w_up"], res["x1"], g("norm_ffn_pre"), dx2, res["mix"], g("norm_mix_post"), f"mm_dh2_{n}")
    big["w_up"] = _dw_up(res["h2"], dhu, f"dw_up_{n}")
    dya, dyb, dgates, small["b_gate"] = _mm_dz(dmix, gath["w_o"], res["proj"], res["bgate"], res["ya"], res["yb"],
                                                f"mm_dz_{n}")
    big["w_o"] = _dw_o(res["z"], dmix, f"dw_o_{n}")
    datt, dmixed = _branch_bwd(dya, dyb, gath["w_attn_out"], gath["w_pool_out"], f"branch_bwd_{n}")
    big["w_attn_out"], big["w_pool_out"] = _dw_branch(res["att"], res["mixed"], dya, dyb, f"dw_branch_{n}")
    dq, dk, dv, dtab = _attn_bwd(res["proj"], datt, res["tab"], f"attn_bwd_{n}")
    small["rel_bias"] = _rel_bias_grad(dtab, f"rel_bias_grad_{n}")[:, 0, :NREL]
    du, small["w_pool_group"], small["pool_scale"] = _pool_bwd(
        dmixed, res["pooled"], p["w_pool_group"][l], g("pool_scale"), f"pool_bwd_{n}")
    segs = (dq, dk, dv, du, dgates)
    dx, small["norm_mix_pre"] = _mm_dh(segs, gath["w_in"], res["x"], g("norm_mix_pre"), dx1, f"mm_dh_{n}")
    big["w_in"] = _dw_in(res["h"], segs, f"dw_in_{n}")
    return dx, big, small


def _gather_layer(l, p):
    names = ("w_in", "w_attn_out", "w_pool_out", "w_o", "w_up", "w_down")
    shards = [p[k][l].astype(BF16) for k in names] + [p["conv_w"][l]]
    got = dict(zip(names + ("conv_w",), _all_gather(shards, f"all_gather_l{l}")))
    for k in ("w_attn_out", "w_pool_out"):
        got[k] = jnp.transpose(got[k], (1, 0, 2)).reshape(AW, D)
    got["w_o"] = got["w_o"].reshape(D, D)
    got["w_down"] = got["w_down"].reshape(NFF * FS, D)
    return got


def kernel(x, norm_mix_pre, w_in, b_gate, rel_bias, w_attn_out, w_pool_group, pool_scale, w_pool_out, w_o, norm_mix_post, norm_ffn_pre, w_up, conv_w, conv_b, w_down, norm_ffn_post, loss_target, m_norm_mix_pre, m_w_in, m_b_gate, m_rel_bias, m_w_attn_out, m_w_pool_group, m_pool_scale, m_w_pool_out, m_w_o, m_norm_mix_post, m_norm_ffn_pre, m_w_up, m_conv_w, m_conv_b, m_w_down, m_norm_ffn_post, v_norm_mix_pre, v_w_in, v_b_gate, v_rel_bias, v_w_attn_out, v_w_pool_group, v_pool_scale, v_w_pool_out, v_w_o, v_norm_mix_post, v_norm_ffn_pre, v_w_up, v_conv_w, v_conv_b, v_w_down, v_norm_ffn_post):
    names = ("norm_mix_pre", "w_in", "b_gate", "rel_bias", "w_attn_out", "w_pool_group", "pool_scale", "w_pool_out",
             "w_o", "norm_mix_post", "norm_ffn_pre", "w_up", "conv_w", "conv_b", "w_down", "norm_ffn_post")
    p = dict(zip(names, (norm_mix_pre, w_in, b_gate, rel_bias, w_attn_out, w_pool_group, pool_scale, w_pool_out, w_o,
                         norm_mix_post, norm_ffn_pre, w_up, conv_w, conv_b, w_down, norm_ffn_post)))
    mom = dict(zip(names, (m_norm_mix_pre, m_w_in, m_b_gate, m_rel_bias, m_w_attn_out, m_w_pool_group, m_pool_scale,
                           m_w_pool_out, m_w_o, m_norm_mix_post, m_norm_ffn_pre, m_w_up, m_conv_w, m_conv_b, m_w_down,
                           m_norm_ffn_post)))
    var = dict(zip(names, (v_norm_mix_pre, v_w_in, v_b_gate, v_rel_bias, v_w_attn_out, v_w_pool_group, v_pool_scale,
                           v_w_pool_out, v_w_o, v_norm_mix_post, v_norm_ffn_pre, v_w_up, v_conv_w, v_conv_b, v_w_down,
                           v_norm_ffn_post)))
    s = x.shape[1]
    xs = x.reshape(s, D)
    target = loss_target.reshape(s, D)

    gath = [_gather_layer(l, p) for l in range(DEPTH)]
    h0 = _rmsnorm_call(xs, p["norm_mix_pre"][0].reshape(1, D), "rmsnorm_l0")
    res0, x2, h1 = _layer_fwd(0, xs, h0, p, gath[0], None, p["norm_mix_pre"][1].reshape(1, D))
    res1, dy, loss_part = _layer_fwd(1, x2, h1, p, gath[1], target, None)
    dx2, big1, small1 = _layer_bwd(1, dy, p, gath[1], res1)
    grad_x, big0, small0 = _layer_bwd(0, dx2, p, gath[0], res0)

    loss = lax.psum(loss_part[0, 0], ("x", "y", "c"))

    order = [k for k, _ in SHARDED]
    recv = [dict(zip(order, _all_to_all([big[k] for k in order], f"all_to_all_l{l}")))
            for l, big in enumerate((big0, big1))]
    out = {}
    for k, tr in SHARDED:
        out[k] = _adamw_sharded((recv[0][k], recv[1][k]), p[k], mom[k], var[k], tr, f"adamw_{k}")

    small = {k: jnp.stack([small0[k].reshape(shape), small1[k].reshape(shape)]) for k, shape in SMALL}
    parts = _all_gather([_pack_small(small)], "all_gather_small")[0]
    packed = _adamw_small(parts, _pack_small(p), _pack_small(mom), _pack_small(var), "adamw_small")
    unpacked = [_unpack_small(a) for a in packed]
    for k, _ in SMALL:
        out[k] = tuple(u[k] for u in unpacked)

    return (loss, grad_x.reshape(x.shape), *[out[k][0] for k in names], *[out[k][1] for k in names],
            *[out[k][2] for k in names], *[out[k][3] for k in names])
```

```python
import numpy as np
import jax
import jax.numpy as jnp
from jax import lax
from jax.experimental import pallas as pl
from jax.experimental.pallas import tpu as pltpu

F32, BF16 = jnp.float32, jnp.bfloat16

D = 1024
AW = 512
PW = 512
PG = 128
INW = 4096
FS = 704
NFF = 4
NDEV = 8
DEPTH = 2
HEADS = 8
NREL = 513
MAXREL = 256
POOL_WINDOWS = (2, 4, 8, 16)
EPS = 1e-6
SCALE = 0.125
NEG = -1e30
QB = 256
KW = 3 * QB
BAND = 576
ADAM_LR, ADAM_B1, ADAM_B2, ADAM_EPS, ADAM_WD, ADAM_STEP = 0.001, 0.9, 0.999, 1e-08, 0.01, 10
VMEM_LIMIT_V7X = 56 * 1024 * 1024
MESH = pl.DeviceIdType.MESH
GELU_C = 0.7978845608028654
GELU_A = 0.044715


HBM = pl.BlockSpec(memory_space=pltpu.HBM)


class _Exchange:
    def __init__(self, operands, out_shape, scratch, phases, deliver):
        self.operands, self.out_shape, self.scratch = list(operands), list(out_shape), list(scratch)
        self.phases, self.deliver = phases, deliver


def _call(body, name, grid, in_specs, out_specs, out_shape, scratch=(), exchange=None):
    params = pltpu.CompilerParams(vmem_limit_bytes=VMEM_LIMIT_V7X)
    if exchange is None:
        return pl.pallas_call(body, name=name, grid=grid, in_specs=in_specs, out_specs=out_specs, out_shape=out_shape,
                              scratch_shapes=list(scratch), compiler_params=params)
    single = not isinstance(out_shape, (list, tuple))
    outs, ospecs = ([out_shape], [out_specs]) if single else (list(out_shape), list(out_specs))
    n_in, n_out, n_scr = len(in_specs), len(outs), len(scratch)
    ne_in, ne_out = len(exchange.operands), len(exchange.out_shape)
    nsteps = int(np.prod(grid))

    def carried(*refs):
        cut = np.cumsum([0, n_in, ne_in, n_out, ne_out, n_scr])
        base_in, ex_in, base_out, ex_out, base_scr = (refs[cut[k]:cut[k + 1]] for k in range(5))
        step = pl.program_id(0)
        for axis in range(1, len(grid)):
            step = step * grid[axis] + pl.program_id(axis)
        phases = exchange.phases(ex_in, ex_out, refs[cut[5]:])
        pl.when(step == 0)(phases[0])
        body(*base_in, *base_out, *base_scr)
        at = {2: [nsteps - 1], 3: [(7 * nsteps) // 10, nsteps - 1]}[len(phases)]
        for phase, when in zip(phases[1:], at):
            pl.when(step == when)(phase)

    call = pl.pallas_call(
        carried, name=name, grid=grid, in_specs=list(in_specs) + [HBM] * ne_in, out_specs=ospecs + [HBM] * ne_out,
        out_shape=outs + exchange.out_shape, scratch_shapes=list(scratch) + exchange.scratch, compiler_params=params)

    def run(*args):
        res = call(*args, *exchange.operands)
        exchange.deliver(res[n_out:])
        return res[0] if single else res[:n_out]

    return run


def _dot(a, b):
    return jnp.dot(a, b, preferred_element_type=F32)


def _dot_nt(a, b):
    return lax.dot_general(a, b, (((1,), (1,)), ((), ())), preferred_element_type=F32)


def _dot_tn(a, b):
    return lax.dot_general(a, b, (((0,), (0,)), ((), ())), preferred_element_type=F32)


def _rms(x, g):
    r = lax.rsqrt(jnp.mean(x * x, axis=-1, keepdims=True) + EPS)
    return x * r * g


def _rms_bwd(x, g, dy):
    r = lax.rsqrt(jnp.mean(x * x, axis=-1, keepdims=True) + EPS)
    xh = x * r
    dg = jnp.sum(dy * xh, axis=0, keepdims=True)
    dxh = dy * g
    dx = r * (dxh - xh * jnp.mean(dxh * xh, axis=-1, keepdims=True))
    return dx, dg


def _sigmoid(x):
    return 1.0 / (1.0 + jnp.exp(-x))


def _gelu_parts(x):
    x2 = x * x
    th = jnp.tanh(GELU_C * (x + GELU_A * x * x2))
    g = 0.5 * x * (1.0 + th)
    dg = 0.5 * (1.0 + th) + 0.5 * x * (1.0 - th * th) * (GELU_C * (1.0 + 3.0 * GELU_A * x2))
    return g, dg


def _row(i):
    return (i, 0)


def _fixed2(*_):
    return (0, 0)


def _rmsnorm_call(x, g, name):
    s = x.shape[0]
    tm = 512

    def body(x_ref, g_ref, o_ref):
        o_ref[...] = _rms(x_ref[...], g_ref[...]).astype(BF16)

    return _call(body, name, (s // tm,),
                 [pl.BlockSpec((tm, D), _row), pl.BlockSpec((1, D), _fixed2)],
                 pl.BlockSpec((tm, D), _row), jax.ShapeDtypeStruct((s, D), BF16))(x, g)


def _mm_in(h, win, name, exchange=None):
    s = h.shape[0]
    tm = min(1024, s)

    def body(h_ref, w_ref, o_ref):
        o_ref[...] = _dot(h_ref[...], w_ref[...])

    return _call(body, name, (NDEV, s // tm),
                 [pl.BlockSpec((tm, D), lambda j, i: (i, 0)), pl.BlockSpec((None, D, 512), lambda j, i: (j, 0, 0))],
                 pl.BlockSpec((tm, 512), lambda j, i: (i, j)), jax.ShapeDtypeStruct((s, INW), F32),
                 exchange=exchange)(h, win)


def _bias_table(rel_bias, name):
    wdt = 1024
    rel = jnp.pad(rel_bias, ((0, 0), (0, 640 - NREL))).reshape(HEADS, 1, 640)

    def body(r_ref, o_ref):
        rr = lax.broadcasted_iota(jnp.int32, (640, wdt), 0)
        m = lax.broadcasted_iota(jnp.int32, (640, wdt), 1)
        d = jnp.where(m < KW, m, m - wdt)
        onehot = (jnp.clip(512 - d, -MAXREL, MAXREL) + MAXREL == rr).astype(F32)
        row = jnp.dot(jnp.broadcast_to(r_ref[...], (8, 640)), onehot, preferred_element_type=F32,
                      precision=lax.Precision.HIGHEST)[0:1]
        t = pltpu.roll(jnp.broadcast_to(row, (QB, wdt)), 0, 1, stride=1, stride_axis=0)[:, 0:KW]
        qc = lax.broadcasted_iota(jnp.int32, (QB, KW), 0) // 64
        kc = lax.broadcasted_iota(jnp.int32, (QB, KW), 1) // 64
        o_ref[...] = jnp.where((kc >= qc) & (kc <= qc + 8), t, NEG)

    return _call(body, name, (HEADS,), [pl.BlockSpec((None, 1, 640), lambda h: (h, 0, 0))],
                 pl.BlockSpec((None, QB, KW), lambda h: (h, 0, 0)),
                 jax.ShapeDtypeStruct((HEADS, QB, KW), F32))(rel)


def _attn_probs(q_ref, k_refs, b_ref, i):
    lane = lax.broadcasted_iota(jnp.int32, (QB, 128), 1)
    q = q_ref[...] * SCALE
    qs = [jnp.where(lane < 64, q, 0.0).astype(BF16), jnp.where(lane >= 64, q, 0.0).astype(BF16)]
    k = jnp.concatenate([r[...] for r in k_refs], axis=0).astype(BF16)
    colb = lax.broadcasted_iota(jnp.int32, (QB, KW), 1) // QB
    valid = colb + i >= 2
    ps = []
    for hh in (0, 1):
        sc = _dot_nt(qs[hh], k) + b_ref[hh]
        sc = jnp.where(valid, sc, NEG)
        m = jnp.max(sc, axis=1, keepdims=True)
        p = jnp.exp(sc - m)
        ps.append(p / jnp.sum(p, axis=1, keepdims=True))
    return qs, ps, k, lane


def _attn_fwd(proj, tab, name, exchange=None):
    s = proj.shape[0]
    nq = s // QB

    def body(q_ref, k0, k1, k2, v0, v1, v2, b_ref, o_ref):
        i = pl.program_id(1)
        _, ps, _, lane = _attn_probs(q_ref, (k0, k1, k2), b_ref, i)
        v = jnp.concatenate([v0[...], v1[...], v2[...]], axis=0).astype(BF16)
        o = [_dot(ps[hh].astype(BF16), v) for hh in (0, 1)]
        o_ref[...] = jnp.where(lane < 64, o[0], o[1]).astype(BF16)

    def kv(col, d):
        return pl.BlockSpec((QB, 128), lambda p, i: (jnp.maximum(i - 2 + d, 0), col + p))

    in_specs = [pl.BlockSpec((QB, 128), lambda p, i: (i, p))]
    in_specs += [kv(4, d) for d in range(3)] + [kv(8, d) for d in range(3)]
    in_specs += [pl.BlockSpec((2, QB, KW), lambda p, i: (p, 0, 0))]
    return _call(body, name, (4, nq), in_specs, pl.BlockSpec((QB, 128), lambda p, i: (i, p)),
                 jax.ShapeDtypeStruct((s, AW), BF16), exchange=exchange)(proj, proj, proj, proj, proj, proj, proj, tab)


def _pool_fwd(proj, wg, scale, name):
    s = proj.shape[0]
    tb = 512
    e = tb + 16

    def body(u_ref, halo_ref, wg_ref, sc_ref, pooled_ref, mixed_ref):
        i = pl.program_id(0)
        cur = u_ref[...]
        prev = jnp.where(i > 0, halo_ref[...], 0.0)
        xs = jnp.concatenate([prev, cur], axis=0)
        t = i * tb + lax.broadcasted_iota(jnp.int32, (tb, 1), 0)
        for g, w in enumerate(POOL_WINDOWS):
            sl = slice(g * PG, (g + 1) * PG)
            a = xs[:, sl]
            sh = 1
            while sh < w:
                a = a + pltpu.roll(a, sh, 0)
                sh *= 2
            cnt = jnp.minimum(t + 1, w).astype(F32)
            pooled = (a[16:] / cnt - cur[:, sl]).astype(BF16)
            pooled_ref[:, sl] = pooled
            mixed_ref[:, sl] = (_dot(pooled, wg_ref[g].astype(BF16)) * sc_ref[:, sl]).astype(BF16)

    assert e % 8 == 0
    return _call(body, name, (s // tb,),
                 [pl.BlockSpec((tb, PW), lambda i: (i, 3)),
                  pl.BlockSpec((16, PW), lambda i: (jnp.maximum(i * (tb // 16) - 1, 0), 3)),
                  pl.BlockSpec((4, PG, PG), lambda i: (0, 0, 0)), pl.BlockSpec((1, PW), _fixed2)],
                 [pl.BlockSpec((tb, PW), _row), pl.BlockSpec((tb, PW), _row)],
                 [jax.ShapeDtypeStruct((s, PW), BF16), jax.ShapeDtypeStruct((s, PW), BF16)])(proj, proj, wg, scale)


def _branch_fwd(att, mixed, wao, wpo, proj, bgate, name):
    s = att.shape[0]
    tm = 512

    def body(att_ref, mx_ref, wao_ref, wpo_ref, ga_ref, gb_ref, ba_ref, bb_ref, z_ref, ya_ref, yb_ref):
        ya = _dot(att_ref[...], wao_ref[...])
        yb = _dot(mx_ref[...], wpo_ref[...])
        ga = _sigmoid(ga_ref[...] + ba_ref[...])
        gb = _sigmoid(gb_ref[...] + bb_ref[...])
        ya_ref[...] = ya
        yb_ref[...] = yb
        z_ref[...] = (ga * ya + gb * yb).astype(BF16)

    return _call(body, name, (s // tm,),
                 [pl.BlockSpec((tm, AW), _row), pl.BlockSpec((tm, PW), _row),
                  pl.BlockSpec((AW, D), _fixed2), pl.BlockSpec((PW, D), _fixed2),
                  pl.BlockSpec((tm, D), lambda i: (i, 2)), pl.BlockSpec((tm, D), lambda i: (i, 3)),
                  pl.BlockSpec((1, D), lambda i: (0, 0)), pl.BlockSpec((1, D), lambda i: (0, 1))],
                 [pl.BlockSpec((tm, D), _row)] * 3,
                 [jax.ShapeDtypeStruct((s, D), BF16), jax.ShapeDtypeStruct((s, D), F32),
                  jax.ShapeDtypeStruct((s, D), F32)])(att, mixed, wao, wpo, proj, proj, bgate, bgate)


def _mm_o_fwd(z, wo, x, g2, g3, name):
    s = z.shape[0]
    tm = 512

    def body(z_ref, w_ref, x_ref, g2_ref, g3_ref, mix_ref, x1_ref, h2_ref):
        mix = _dot(z_ref[...], w_ref[...])
        x1 = x_ref[...] + _rms(mix, g2_ref[...])
        mix_ref[...] = mix
        x1_ref[...] = x1
        h2_ref[...] = _rms(x1, g3_ref[...]).astype(BF16)

    return _call(body, name, (s // tm,),
                 [pl.BlockSpec((tm, D), _row), pl.BlockSpec((D, D), _fixed2), pl.BlockSpec((tm, D), _row),
                  pl.BlockSpec((1, D), _fixed2), pl.BlockSpec((1, D), _fixed2)],
                 [pl.BlockSpec((tm, D), _row)] * 3,
                 [jax.ShapeDtypeStruct((s, D), F32), jax.ShapeDtypeStruct((s, D), F32),
                  jax.ShapeDtypeStruct((s, D), BF16)])(z, wo, x, g2, g3)


def _mm_up(h2, wup, name, exchange=None):
    s = h2.shape[0]
    tm = min(1024, s)

    def body(h_ref, w_ref, o_ref):
        o_ref[...] = _dot(h_ref[...], w_ref[...])

    return _call(body, name, (NDEV, s // tm),
                 [pl.BlockSpec((tm, D), lambda j, i: (i, 0)), pl.BlockSpec((None, D, FS), lambda j, i: (j, 0, 0))],
                 pl.BlockSpec((None, tm, FS), lambda j, i: (j, i, 0)),
                 jax.ShapeDtypeStruct((NDEV, s, FS), F32), exchange=exchange)(h2, wup)


def _conv_rows(xs, cw, cb):
    m1 = pltpu.roll(xs, 1, 0)
    m2 = pltpu.roll(xs, 2, 0)
    return cb + cw[2:3] * xs + cw[0:1] * m2 + cw[1:2] * m1, m1, m2


def _conv_gate_fwd(hu, cw, cb, name, exchange=None):
    s = hu.shape[2]
    tb = 512

    def body(hu_ref, halo_ref, cw_ref, cb_ref, a_ref):
        i = pl.program_id(1)
        hc = []
        for sd in (0, 1):
            prev = jnp.where(i > 0, halo_ref[sd], 0.0)
            xs = jnp.concatenate([prev, hu_ref[sd]], axis=0)
            hc.append(_conv_rows(xs, cw_ref[sd], cb_ref[sd])[0][8:])
        a_ref[...] = (_gelu_parts(hc[1])[0] * hc[0]).astype(BF16)

    return _call(body, name, (NFF, s // tb),
                 [pl.BlockSpec((2, None, tb, FS), lambda j, i: (0, j, i, 0)),
                  pl.BlockSpec((2, None, 8, FS), lambda j, i: (0, j, jnp.maximum(i * (tb // 8) - 1, 0), 0)),
                  pl.BlockSpec((2, None, 3, FS), lambda j, i: (0, j, 0, 0)),
                  pl.BlockSpec((2, None, 1, FS), lambda j, i: (0, j, 0, 0))],
                 pl.BlockSpec((None, tb, FS), lambda j, i: (j, i, 0)),
                 jax.ShapeDtypeStruct((NFF, s, FS), BF16), exchange=exchange)(hu, hu, cw, cb)


def _mm_down_fwd(a, wd, x1, g4, gnext, name):
    s = a.shape[1]
    tm = 512

    def body(a_ref, w_ref, x1_ref, g4_ref, gn_ref, f_ref, x2_ref, hn_ref):
        f = _dot(a_ref[0], w_ref[0:FS, :])
        for j in range(1, NFF):
            f = f + _dot(a_ref[j], w_ref[j * FS:(j + 1) * FS, :])
        x2 = x1_ref[...] + _rms(f, g4_ref[...])
        f_ref[...] = f
        x2_ref[...] = x2
        hn_ref[...] = _rms(x2, gn_ref[...]).astype(BF16)

    return _call(body, name, (s // tm,),
                 [pl.BlockSpec((NFF, tm, FS), lambda i: (0, i, 0)), pl.BlockSpec((NFF * FS, D), _fixed2),
                  pl.BlockSpec((tm, D), _row), pl.BlockSpec((1, D), _fixed2), pl.BlockSpec((1, D), _fixed2)],
                 [pl.BlockSpec((tm, D), _row)] * 3,
                 [jax.ShapeDtypeStruct((s, D), F32), jax.ShapeDtypeStruct((s, D), F32),
                  jax.ShapeDtypeStruct((s, D), BF16)])(a, wd, x1, g4, gnext)


def _mm_down_loss(a, wd, x1, g4, target, name):
    s = a.shape[1]
    tm = 512

    def body(a_ref, w_ref, x1_ref, g4_ref, t_ref, f_ref, dy_ref, loss_ref):
        i = pl.program_id(0)
        f = _dot(a_ref[0], w_ref[0:FS, :])
        for j in range(1, NFF):
            f = f + _dot(a_ref[j], w_ref[j * FS:(j + 1) * FS, :])
        err = x1_ref[...] + _rms(f, g4_ref[...]) - t_ref[...]
        f_ref[...] = f
        dy_ref[...] = err * (1.0 / D)
        part = 0.5 * jnp.sum(jnp.mean(err * err, axis=-1, keepdims=True), axis=0, keepdims=True)

        @pl.when(i == 0)
        def _():
            loss_ref[...] = jnp.zeros_like(loss_ref)

        loss_ref[...] += jnp.broadcast_to(part, loss_ref.shape)

    return _call(body, name, (s // tm,),
                 [pl.BlockSpec((NFF, tm, FS), lambda i: (0, i, 0)), pl.BlockSpec((NFF * FS, D), _fixed2),
                  pl.BlockSpec((tm, D), _row), pl.BlockSpec((1, D), _fixed2), pl.BlockSpec((tm, D), _row)],
                 [pl.BlockSpec((tm, D), _row), pl.BlockSpec((tm, D), _row), pl.BlockSpec((8, 128), _fixed2)],
                 [jax.ShapeDtypeStruct((s, D), F32), jax.ShapeDtypeStruct((s, D), F32),
                  jax.ShapeDtypeStruct((8, 128), F32)])(a, wd, x1, g4, target)


def _norm_bwd(f, g, dy, name):
    s = f.shape[0]
    tm = 512

    def body(f_ref, g_ref, dy_ref, df_ref, dg_ref):
        i = pl.program_id(0)
        df, dg = _rms_bwd(f_ref[...], g_ref[...], dy_ref[...])
        df_ref[...] = df.astype(BF16)

        @pl.when(i == 0)
        def _():
            dg_ref[...] = jnp.zeros_like(dg_ref)

        dg_ref[...] += dg

    return _call(body, name, (s // tm,),
                 [pl.BlockSpec((tm, D), _row), pl.BlockSpec((1, D), _fixed2), pl.BlockSpec((tm, D), _row)],
                 [pl.BlockSpec((tm, D), _row), pl.BlockSpec((1, D), _fixed2)],
                 [jax.ShapeDtypeStruct((s, D), BF16), jax.ShapeDtypeStruct((1, D), F32)])(f, g, dy)


def _mm_da(df, wd, name):
    s = df.shape[0]
    tm = 512

    def body(df_ref, w_ref, da_ref):
        dfv = df_ref[...]
        for j in range(NFF):
            da_ref[j] = _dot_nt(dfv, w_ref[j * FS:(j + 1) * FS, :])

    return _call(body, name, (s // tm,),
                 [pl.BlockSpec((tm, D), _row), pl.BlockSpec((NFF * FS, D), _fixed2)],
                 pl.BlockSpec((NFF, tm, FS), lambda i: (0, i, 0)),
                 jax.ShapeDtypeStruct((NFF, s, FS), F32))(df, wd)


def _conv_gate_bwd(hu, da, cw, cb, name, exchange=None):
    s = hu.shape[2]
    tb = 512
    nt = s // tb
    e = tb + 16

    def body(hu_ref, prev_ref, next_ref, da_ref, dan_ref, cw_ref, cb_ref, dhu_ref, dcw_ref, dcb_ref):
        i = pl.program_id(1)
        first, last = i == 0, i == nt - 1
        hc, taps = [], []
        for sd in (0, 1):
            xs = jnp.concatenate([jnp.where(first, 0.0, prev_ref[sd]), hu_ref[sd],
                                  jnp.where(last, 0.0, next_ref[sd])], axis=0)
            h, m1, m2 = _conv_rows(xs, cw_ref[sd], cb_ref[sd])
            hc.append(h)
            taps.append((m2, m1, xs))
        da = jnp.concatenate([jnp.zeros((8, FS), F32), da_ref[...], jnp.where(last, 0.0, dan_ref[...])], axis=0)
        gl, dgl = _gelu_parts(hc[1])
        dhc = (da * gl, da * hc[0] * dgl)

        @pl.when(i == 0)
        def _():
            dcw_ref[...] = jnp.zeros_like(dcw_ref)
            dcb_ref[...] = jnp.zeros_like(dcb_ref)

        for sd in (0, 1):
            cw = cw_ref[sd]
            dh = dhc[sd]
            dhu = cw[2:3] * dh + cw[1:2] * pltpu.roll(dh, e - 1, 0) + cw[0:1] * pltpu.roll(dh, e - 2, 0)
            dhu_ref[sd] = dhu[8:8 + tb].astype(BF16)
            dc = dh[8:8 + tb]
            dcw_ref[sd] += jnp.concatenate(
                [jnp.sum(dc * tap[8:8 + tb], axis=0, keepdims=True) for tap in taps[sd]], axis=0)
            dcb_ref[sd] += jnp.sum(dc, axis=0, keepdims=True)

    nb8 = s // 8
    return _call(body, name, (NFF, nt),
                 [pl.BlockSpec((2, None, tb, FS), lambda j, i: (0, j, i, 0)),
                  pl.BlockSpec((2, None, 8, FS), lambda j, i: (0, j, jnp.maximum(i * (tb // 8) - 1, 0), 0)),
                  pl.BlockSpec((2, None, 8, FS), lambda j, i: (0, j, jnp.minimum((i + 1) * (tb // 8), nb8 - 1), 0)),
                  pl.BlockSpec((None, tb, FS), lambda j, i: (j, i, 0)),
                  pl.BlockSpec((None, 8, FS), lambda j, i: (j, jnp.minimum((i + 1) * (tb // 8), nb8 - 1), 0)),
                  pl.BlockSpec((2, None, 3, FS), lambda j, i: (0, j, 0, 0)),
                  pl.BlockSpec((2, None, 1, FS), lambda j, i: (0, j, 0, 0))],
                 [pl.BlockSpec((2, None, tb, FS), lambda j, i: (0, j, i, 0)),
                  pl.BlockSpec((2, None, 3, FS), lambda j, i: (0, j, 0, 0)),
                  pl.BlockSpec((2, None, 1, FS), lambda j, i: (0, j, 0, 0))],
                 [jax.ShapeDtypeStruct((2, NFF, s, FS), BF16), jax.ShapeDtypeStruct((2, NFF, 3, FS), F32),
                  jax.ShapeDtypeStruct((2, NFF, 1, FS), F32)], exchange=exchange)(hu, hu, hu, da, da, cw, cb)


def _dw_down(a, df, name):
    s = a.shape[1]
    tk = 512
    nk = s // tk

    def body(a_ref, df_ref, o_ref, acc):
        k = pl.program_id(1)

        @pl.when(k == 0)
        def _():
            acc[...] = jnp.zeros_like(acc)

        acc[...] += _dot_tn(a_ref[...], df_ref[...])

        @pl.when(k == nk - 1)
        def _():
            o_ref[0] = acc[0:FS // 2, :].astype(BF16)
            o_ref[1] = acc[FS // 2:FS, :].astype(BF16)

    return _call(body, name, (NFF, nk),
                 [pl.BlockSpec((None, tk, FS), lambda j, k: (j, k, 0)), pl.BlockSpec((tk, D), lambda j, k: (k, 0))],
                 pl.BlockSpec((2, FS // 2, D), lambda j, k: (j, 0, 0)),
                 jax.ShapeDtypeStruct((NDEV, FS // 2, D), BF16), [pltpu.VMEM((FS, D), F32)])(a, df)


def _mm_dh2(dhu, wup, x1, g3, dx2, mix, g2, name):
    s = x1.shape[0]
    tm = 512

    def body(dhu_ref, w_ref, x1_ref, g3_ref, dx2_ref, mix_ref, g2_ref, dx1_ref, dmix_ref, dg3_ref, dg2_ref, acc):
        i, j = pl.program_id(0), pl.program_id(1)

        @pl.when((i == 0) & (j == 0))
        def _():
            dg3_ref[...] = jnp.zeros_like(dg3_ref)
            dg2_ref[...] = jnp.zeros_like(dg2_ref)

        @pl.when(j == 0)
        def _():
            acc[...] = jnp.zeros_like(acc)

        acc[...] += _dot_nt(dhu_ref[...], w_ref[...])

        @pl.when(j == NDEV - 1)
        def _():
            dn, dg3 = _rms_bwd(x1_ref[...], g3_ref[...], acc[...])
            dx1 = dx2_ref[...] + dn
            dmix, dg2 = _rms_bwd(mix_ref[...], g2_ref[...], dx1)
            dx1_ref[...] = dx1
            dmix_ref[...] = dmix.astype(BF16)
            dg3_ref[...] += dg3
            dg2_ref[...] += dg2

    rowi = lambda i, j: (i, 0)
    fix = lambda i, j: (0, 0)
    return _call(body, name, (s // tm, NDEV),
                 [pl.BlockSpec((None, None, tm, FS), lambda i, j: (j // NFF, j % NFF, i, 0)),
                  pl.BlockSpec((None, D, FS), lambda i, j: (j, 0, 0)),
                  pl.BlockSpec((tm, D), rowi), pl.BlockSpec((1, D), fix), pl.BlockSpec((tm, D), rowi),
                  pl.BlockSpec((tm, D), rowi), pl.BlockSpec((1, D), fix)],
                 [pl.BlockSpec((tm, D), rowi), pl.BlockSpec((tm, D), rowi), pl.BlockSpec((1, D), fix),
                  pl.BlockSpec((1, D), fix)],
                 [jax.ShapeDtypeStruct((s, D), F32), jax.ShapeDtypeStruct((s, D), BF16),
                  jax.ShapeDtypeStruct((1, D), F32), jax.ShapeDtypeStruct((1, D), F32)],
                 [pltpu.VMEM((tm, D), F32)])(dhu, wup, x1, g3, dx2, mix, g2)


def _dw_up(h2, dhu, name):
    s = h2.shape[0]
    tk = 512
    nk = s // tk

    def body(h_ref, d_ref, o_ref, acc):
        k = pl.program_id(1)

        @pl.when(k == 0)
        def _():
            acc[...] = jnp.zeros_like(acc)

        acc[...] += _dot_tn(h_ref[...], d_ref[...])

        @pl.when(k == nk - 1)
        def _():
            o_ref[...] = acc[...].astype(BF16)

    return _call(body, name, (NDEV, nk),
                 [pl.BlockSpec((tk, D), lambda j, k: (k, 0)),
                  pl.BlockSpec((None, None, tk, FS), lambda j, k: (j // NFF, j % NFF, k, 0))],
                 pl.BlockSpec((None, D, FS), lambda j, k: (j, 0, 0)),
                 jax.ShapeDtypeStruct((NDEV, D, FS), BF16), [pltpu.VMEM((D, FS), F32)])(h2, dhu)


def _mm_dz(dmix, wo, proj, bgate, ya, yb, name):
    s = dmix.shape[0]
    tm = 512

    def body(dm_ref, w_ref, ga_ref, gb_ref, ba_ref, bb_ref, ya_ref, yb_ref, dya_ref, dyb_ref, dg_ref, dbg_ref):
        i = pl.program_id(0)
        dz = _dot_nt(dm_ref[...], w_ref[...])
        ga = _sigmoid(ga_ref[...] + ba_ref[...])
        gb = _sigmoid(gb_ref[...] + bb_ref[...])
        dya_ref[...] = (dz * ga).astype(BF16)
        dyb_ref[...] = (dz * gb).astype(BF16)
        dga = dz * ya_ref[...] * ga * (1.0 - ga)
        dgb = dz * yb_ref[...] * gb * (1.0 - gb)
        dg_ref[:, 0:D] = dga.astype(BF16)
        dg_ref[:, D:2 * D] = dgb.astype(BF16)

        @pl.when(i == 0)
        def _():
            dbg_ref[...] = jnp.zeros_like(dbg_ref)

        dbg_ref[:, 0:D] += jnp.sum(dga, axis=0, keepdims=True)
        dbg_ref[:, D:2 * D] += jnp.sum(dgb, axis=0, keepdims=True)

    return _call(body, name, (s // tm,),
                 [pl.BlockSpec((tm, D), _row), pl.BlockSpec((D, D), _fixed2),
                  pl.BlockSpec((tm, D), lambda i: (i, 2)), pl.BlockSpec((tm, D), lambda i: (i, 3)),
                  pl.BlockSpec((1, D), lambda i: (0, 0)), pl.BlockSpec((1, D), lambda i: (0, 1)),
                  pl.BlockSpec((tm, D), _row), pl.BlockSpec((tm, D), _row)],
                 [pl.BlockSpec((tm, D), _row), pl.BlockSpec((tm, D), _row), pl.BlockSpec((tm, 2 * D), _row),
                  pl.BlockSpec((1, 2 * D), _fixed2)],
                 [jax.ShapeDtypeStruct((s, D), BF16), jax.ShapeDtypeStruct((s, D), BF16),
                  jax.ShapeDtypeStruct((s, 2 * D), BF16), jax.ShapeDtypeStruct((1, 2 * D), F32)])(
                     dmix, wo, proj, proj, bgate, bgate, ya, yb)


def _dw_o(z, dmix, name):
    s = z.shape[0]
    tk = 512
    nk = s // tk

    def body(z_ref, d_ref, o_ref, acc):
        k = pl.program_id(0)

        @pl.when(k == 0)
        def _():
            acc[...] = jnp.zeros_like(acc)

        acc[...] += _dot_tn(z_ref[...], d_ref[...])

        @pl.when(k == nk - 1)
        def _():
            for j in range(NDEV):
                o_ref[j] = acc[j * 128:(j + 1) * 128, :].astype(BF16)

    return _call(body, name, (nk,),
                 [pl.BlockSpec((tk, D), _row), pl.BlockSpec((tk, D), _row)],
                 pl.BlockSpec((NDEV, 128, D), lambda k: (0, 0, 0)),
                 jax.ShapeDtypeStruct((NDEV, 128, D), BF16), [pltpu.VMEM((D, D), F32)])(z, dmix)


def _branch_bwd(dya, dyb, wao, wpo, name):
    s = dya.shape[0]
    tm = 512

    def body(dya_ref, dyb_ref, wao_ref, wpo_ref, datt_ref, dmx_ref):
        datt_ref[...] = _dot_nt(dya_ref[...], wao_ref[...]).astype(BF16)
        dmx_ref[...] = _dot_nt(dyb_ref[...], wpo_ref[...])

    return _call(body, name, (s // tm,),
                 [pl.BlockSpec((tm, D), _row), pl.BlockSpec((tm, D), _row),
                  pl.BlockSpec((AW, D), _fixed2), pl.BlockSpec((PW, D), _fixed2)],
                 [pl.BlockSpec((tm, AW), _row), pl.BlockSpec((tm, PW), _row)],
                 [jax.ShapeDtypeStruct((s, AW), BF16), jax.ShapeDtypeStruct((s, PW), F32)])(dya, dyb, wao, wpo)


def _dw_branch(att, mixed, dya, dyb, name):
    s = att.shape[0]
    tk = 512
    nk = s // tk

    def body(att_ref, mx_ref, dya_ref, dyb_ref, oa_ref, ob_ref, acca, accb):
        k = pl.program_id(0)

        @pl.when(k == 0)
        def _():
            acca[...] = jnp.zeros_like(acca)
            accb[...] = jnp.zeros_like(accb)

        acca[...] += _dot_tn(att_ref[...], dya_ref[...])
        accb[...] += _dot_tn(mx_ref[...], dyb_ref[...])

        @pl.when(k == nk - 1)
        def _():
            for j in range(NDEV):
                oa_ref[j] = acca[:, j * 128:(j + 1) * 128].astype(BF16)
                ob_ref[j] = accb[:, j * 128:(j + 1) * 128].astype(BF16)

    out = jax.ShapeDtypeStruct((NDEV, AW, 128), BF16)
    return _call(body, name, (nk,),
                 [pl.BlockSpec((tk, AW), _row), pl.BlockSpec((tk, PW), _row),
                  pl.BlockSpec((tk, D), _row), pl.BlockSpec((tk, D), _row)],
                 [pl.BlockSpec((NDEV, AW, 128), lambda k: (0, 0, 0))] * 2, [out, out],
                 [pltpu.VMEM((AW, D), F32), pltpu.VMEM((PW, D), F32)])(att, mixed, dya, dyb)


def _attn_bwd(proj, datt, tab, name, exchange=None):
    s = proj.shape[0]
    nq = s // QB

    def body(q_ref, k0, k1, k2, v0, v1, v2, do_ref, b_ref, dq_ref, dk_ref, dv_ref, db_ref, dka, dkb, dva, dvb):
        i = pl.program_id(1)

        @pl.when(i == 0)
        def _():
            for r in (dka, dkb, dva, dvb):
                r[...] = jnp.zeros_like(r)
            db_ref[...] = jnp.zeros_like(db_ref)

        @pl.when(i < nq)
        def _():
            qs, ps, k, lane = _attn_probs(q_ref, (k0, k1, k2), b_ref, i)
            v = jnp.concatenate([v0[...], v1[...], v2[...]], axis=0).astype(BF16)
            do = do_ref[...]
            dq = jnp.zeros((QB, 128), F32)
            dkw = jnp.zeros((KW, 128), F32)
            dvw = jnp.zeros((KW, 128), F32)
            for hh in (0, 1):
                mine = (lane < 64) if hh == 0 else (lane >= 64)
                doh = jnp.where(mine, do, jnp.zeros_like(do))
                kmask = lax.broadcasted_iota(jnp.int32, (KW, 128), 1)
                kh = jnp.where((kmask < 64) if hh == 0 else (kmask >= 64), k, jnp.zeros_like(k))
                p = ps[hh]
                dp = _dot_nt(doh, v)
                ds = p * (dp - jnp.sum(dp * p, axis=1, keepdims=True))
                db_ref[hh] += ds
                dsb = ds.astype(BF16)
                dq = dq + _dot(dsb, kh)
                dkw = dkw + _dot_tn(dsb, qs[hh])
                dvw = dvw + _dot_tn(p.astype(BF16), doh)
            dq_ref[...] = (dq * SCALE).astype(BF16)
            dk_ref[...] = (dka[...] + dkw[0:QB]).astype(BF16)
            dka[...] = dkb[...] + dkw[QB:2 * QB]
            dkb[...] = dkw[2 * QB:3 * QB]
            dv_ref[...] = (dva[...] + dvw[0:QB]).astype(BF16)
            dva[...] = dvb[...] + dvw[QB:2 * QB]
            dvb[...] = dvw[2 * QB:3 * QB]

        @pl.when(i >= nq)
        def _():
            dk_ref[...] = dka[...].astype(BF16)
            dka[...] = dkb[...]
            dkb[...] = jnp.zeros_like(dkb)
            dv_ref[...] = dva[...].astype(BF16)
            dva[...] = dvb[...]
            dvb[...] = jnp.zeros_like(dvb)

    def kv(col, d):
        return pl.BlockSpec((QB, 128), lambda p, i: (jnp.clip(i - 2 + d, 0, nq - 1), col + p))

    cur = lambda p, i: (jnp.minimum(i, nq - 1), p)
    done = lambda p, i: (jnp.maximum(i - 2, 0), p)
    in_specs = [pl.BlockSpec((QB, 128), cur)]
    in_specs += [kv(4, d) for d in range(3)] + [kv(8, d) for d in range(3)]
    in_specs += [pl.BlockSpec((QB, 128), cur), pl.BlockSpec((2, QB, KW), lambda p, i: (p, 0, 0))]
    o = jax.ShapeDtypeStruct((s, AW), BF16)
    return _call(body, name, (4, nq + 2), in_specs,
                 [pl.BlockSpec((QB, 128), cur), pl.BlockSpec((QB, 128), done), pl.BlockSpec((QB, 128), done),
                  pl.BlockSpec((2, QB, KW), lambda p, i: (p, 0, 0))],
                 [o, o, o, jax.ShapeDtypeStruct((HEADS, QB, KW), F32)],
                 [pltpu.VMEM((QB, 128), F32)] * 4, exchange=exchange)(
                     proj, proj, proj, proj, proj, proj, proj, datt, tab)


def _rel_bias_grad(dtab, name):
    wdt = 640

    def body(x_ref, o_ref):
        x = x_ref[...]
        xc = x[0:64, 0:wdt]
        for qc in range(1, QB // 64):
            xc = xc + pltpu.roll(x[qc * 64:(qc + 1) * 64, :], KW - qc * 64, 1)[:, 0:wdt]
        r = lax.broadcasted_iota(jnp.int32, (64, 64), 0)
        c = lax.broadcasted_iota(jnp.int32, (64, 64), 1)
        flip = (r + c == 63).astype(F32)
        y = jnp.dot(flip, xc, preferred_element_type=F32, precision=lax.Precision.HIGHEST)
        z = pltpu.roll(y, 0, 1, stride=1, stride_axis=0)
        t = jnp.broadcast_to(jnp.sum(z, axis=0, keepdims=True), (8, wdt))
        e = lax.broadcasted_iota(jnp.int32, (wdt, wdt), 0)
        rr = lax.broadcasted_iota(jnp.int32, (wdt, wdt), 1)
        onehot = (jnp.clip(BAND - 1 - e, -MAXREL, MAXREL) + MAXREL == rr).astype(F32)
        o_ref[...] = jnp.dot(t, onehot, preferred_element_type=F32, precision=lax.Precision.HIGHEST)

    return _call(body, name, (HEADS,), [pl.BlockSpec((None, QB, KW), lambda h: (h, 0, 0))],
                 pl.BlockSpec((None, 8, wdt), lambda h: (h, 0, 0)),
                 jax.ShapeDtypeStruct((HEADS, 8, wdt), F32))(dtab)


def _pool_bwd(dmixed, pooled, wg, scale, name):
    s = dmixed.shape[0]
    tb = 512
    nt = s // tb
    e = tb + 16

    def body(dm_ref, dmn_ref, pl_ref, wg_ref, sc_ref, du_ref, dwg_ref, dsc_ref):
        i = pl.program_id(0)
        dm = jnp.concatenate([dm_ref[...], jnp.where(i == nt - 1, 0.0, dmn_ref[...])], axis=0)
        t = i * tb + lax.broadcasted_iota(jnp.int32, (e, 1), 0)

        @pl.when(i == 0)
        def _():
            dwg_ref[...] = jnp.zeros_like(dwg_ref)
            dsc_ref[...] = jnp.zeros_like(dsc_ref)

        for g, w in enumerate(POOL_WINDOWS):
            sl = slice(g * PG, (g + 1) * PG)
            wgb = wg_ref[g].astype(BF16)
            pb = pl_ref[:, sl]
            dsc_ref[:, sl] += jnp.sum(dm[0:tb, sl] * _dot(pb, wgb), axis=0, keepdims=True)
            dpre = (dm[:, sl] * sc_ref[:, sl]).astype(BF16)
            dwg_ref[g] += _dot_tn(pb, dpre[0:tb])
            dpool = _dot_nt(dpre, wgb)
            a = dpool / jnp.minimum(t + 1, w).astype(F32)
            sh = 1
            while sh < w:
                a = a + pltpu.roll(a, e - sh, 0)
                sh *= 2
            du_ref[:, sl] = (a[0:tb] - dpool[0:tb]).astype(BF16)

    nb16 = s // 16
    return _call(body, name, (nt,),
                 [pl.BlockSpec((tb, PW), _row),
                  pl.BlockSpec((16, PW), lambda i: (jnp.minimum((i + 1) * (tb // 16), nb16 - 1), 0)),
                  pl.BlockSpec((tb, PW), _row), pl.BlockSpec((4, PG, PG), lambda i: (0, 0, 0)),
                  pl.BlockSpec((1, PW), _fixed2)],
                 [pl.BlockSpec((tb, PW), _row), pl.BlockSpec((4, PG, PG), lambda i: (0, 0, 0)),
                  pl.BlockSpec((1, PW), _fixed2)],
                 [jax.ShapeDtypeStruct((s, PW), BF16), jax.ShapeDtypeStruct((4, PG, PG), F32),
                  jax.ShapeDtypeStruct((1, PW), F32)])(dmixed, dmixed, pooled, wg, scale)


def _dproj_specs(tm, rows, seg_of, freeze):
    def piece(n):
        if not freeze:
            return pl.BlockSpec((tm, 512), lambda a, b: (rows(a, b), 0))
        return pl.BlockSpec((tm, 512), lambda a, b: (jnp.where(seg_of(a, b) == n, rows(a, b), 0), 0))

    def gates_map(a, b):
        row = jnp.where(seg_of(a, b) >= 4, rows(a, b), 0) if freeze else rows(a, b)
        return (row, jnp.maximum(seg_of(a, b) - 4, 0))

    return [piece(0), piece(1), piece(2), piece(3), pl.BlockSpec((tm, 512), gates_map)]


def _mm_dh(segs, win, x, g1, dx1, name):
    s = x.shape[0]
    tm = 512

    def body(dq_ref, dk_ref, dv_ref, du_ref, dg_ref, w_ref, x_ref, g1_ref, dx1_ref, dx_ref, dg1_ref, acc):
        i, j = pl.program_id(0), pl.program_id(1)

        @pl.when((i == 0) & (j == 0))
        def _():
            dg1_ref[...] = jnp.zeros_like(dg1_ref)

        @pl.when(j == 0)
        def _():
            acc[...] = jnp.zeros_like(acc)

        for n, ref in enumerate((dq_ref, dk_ref, dv_ref, du_ref)):
            @pl.when(j == n)
            def _(ref=ref):
                acc[...] += _dot_nt(ref[...], w_ref[...])

        @pl.when(j >= 4)
        def _():
            acc[...] += _dot_nt(dg_ref[...], w_ref[...])

        @pl.when(j == NDEV - 1)
        def _():
            dn, dg1 = _rms_bwd(x_ref[...], g1_ref[...], acc[...])
            dx_ref[...] = dx1_ref[...] + dn
            dg1_ref[...] += dg1

    rowi = lambda i, j: (i, 0)
    fix = lambda i, j: (0, 0)
    return _call(body, name, (s // tm, NDEV),
                 _dproj_specs(tm, lambda i, j: i, lambda i, j: j, False)
                 + [pl.BlockSpec((None, D, 512), lambda i, j: (j, 0, 0)), pl.BlockSpec((tm, D), rowi),
                    pl.BlockSpec((1, D), fix), pl.BlockSpec((tm, D), rowi)],
                 [pl.BlockSpec((tm, D), rowi), pl.BlockSpec((1, D), fix)],
                 [jax.ShapeDtypeStruct((s, D), F32), jax.ShapeDtypeStruct((1, D), F32)],
                 [pltpu.VMEM((tm, D), F32)])(*segs, win, x, g1, dx1)


def _dw_in(h, segs, name):
    s = h.shape[0]
    tk = 512
    nk = s // tk

    def body(h_ref, dq_ref, dk_ref, dv_ref, du_ref, dg_ref, o_ref, acc):
        j, k = pl.program_id(0), pl.program_id(1)

        @pl.when(k == 0)
        def _():
            acc[...] = jnp.zeros_like(acc)

        for n, ref in enumerate((dq_ref, dk_ref, dv_ref, du_ref)):
            @pl.when(j == n)
            def _(ref=ref):
                acc[...] += _dot_tn(h_ref[...], ref[...])

        @pl.when(j >= 4)
        def _():
            acc[...] += _dot_tn(h_ref[...], dg_ref[...])

        @pl.when(k == nk - 1)
        def _():
            o_ref[...] = acc[...].astype(BF16)

    return _call(body, name, (NDEV, nk),
                 [pl.BlockSpec((tk, D), lambda j, k: (k, 0))] + _dproj_specs(tk, lambda j, k: k, lambda j, k: j, True),
                 pl.BlockSpec((None, D, 512), lambda j, k: (j, 0, 0)),
                 jax.ShapeDtypeStruct((NDEV, D, 512), BF16), [pltpu.VMEM((D, 512), F32)])(h, *segs)


def _slot(px, py, pc):
    return 4 * px + 2 * py + pc


def _gather_exchange(shards, deliver):
    n = len(shards)

    def phases(ins, outs, sems):
        send_sems, recv_sems, local_sems = sems
        x, y, c = lax.axis_index("x"), lax.axis_index("y"), lax.axis_index("c")
        me, sibling = (x, y, c), (x, y, 1 - c)
        chips = [(1 - x, y), (x, 1 - y), (1 - x, 1 - y)]

        def copy(w, k, block, to, src=None):
            dst = outs[w].at[_slot(*block)]
            return pltpu.make_async_remote_copy(
                src_ref=dst if src is None else src, dst_ref=dst, send_sem=send_sems.at[7 * w + k],
                recv_sem=recv_sems.at[7 * w + k], device_id=to, device_id_type=MESH)

        def mine(w):
            return pltpu.make_async_copy(ins[w], outs[w].at[_slot(*me)], local_sems.at[w])

        def first(w):
            return [copy(w, 0, me, sibling, src=ins[w])] + [
                copy(w, 1 + j, me, (*chip, c), src=ins[w]) for j, chip in enumerate(chips)]

        def passed(w):
            return [copy(w, 4 + j, (*chip, c), sibling) for j, chip in enumerate(chips)]

        def send():
            for w in range(n):
                mine(w).start()
                for cp in first(w):
                    cp.start()

        def forward():
            for w in range(n):
                for j, chip in enumerate(chips):
                    copy(w, 1 + j, (*chip, c), me).wait_recv()
                    passed(w)[j].start()

        def finish():
            for w in range(n):
                copy(w, 0, sibling, me).wait_recv()
                for j, chip in enumerate(chips):
                    copy(w, 4 + j, (*chip, 1 - c), me).wait_recv()
            for w in range(n):
                for cp in first(w) + passed(w):
                    cp.wait_send()
                mine(w).wait()

        return [send, forward, finish]

    return _Exchange(shards, [jax.ShapeDtypeStruct((NDEV,) + a.shape, a.dtype) for a in shards],
                     [pltpu.SemaphoreType.DMA((7 * n,)), pltpu.SemaphoreType.DMA((7 * n,)),
                      pltpu.SemaphoreType.DMA((n,))], phases, deliver)


def _scatter_exchange(parts, deliver):
    n = len(parts)

    def phases(ins, outs, sems):
        send_sems, recv_sems, local_sems = sems
        x, y, c = lax.axis_index("x"), lax.axis_index("y"), lax.axis_index("c")
        me = _slot(x, y, c)
        peers = [((1 - x) if r & 4 else x, (1 - y) if r & 2 else y, (1 - c) if r & 1 else c) for r in range(1, NDEV)]

        def mine(w):
            return pltpu.make_async_copy(ins[w].at[me], outs[w].at[me], local_sems.at[w])

        def copy(w, r, block_here, block_there):
            return pltpu.make_async_remote_copy(
                src_ref=ins[w].at[block_here], dst_ref=outs[w].at[block_there], send_sem=send_sems.at[7 * w + r],
                recv_sem=recv_sems.at[7 * w + r], device_id=peers[r], device_id_type=MESH)

        def send():
            for w in range(n):
                mine(w).start()
                for r, peer in enumerate(peers):
                    copy(w, r, _slot(*peer), me).start()

        def finish():
            for w in range(n):
                for r, peer in enumerate(peers):
                    copy(w, r, me, _slot(*peer)).wait_recv()
            for w in range(n):
                for r, peer in enumerate(peers):
                    copy(w, r, _slot(*peer), me).wait_send()
                mine(w).wait()

        return [send, finish]

    return _Exchange(parts, [jax.ShapeDtypeStruct(a.shape, a.dtype) for a in parts],
                     [pltpu.SemaphoreType.DMA((7 * n,)), pltpu.SemaphoreType.DMA((7 * n,)),
                      pltpu.SemaphoreType.DMA((n,))], phases, deliver)


def _exchange_alone(exchange, name):
    n_in = len(exchange.operands)
    n_out = len(exchange.out_shape)

    def body(*refs):
        for phase in exchange.phases(refs[:n_in], refs[n_in:n_in + n_out], refs[n_in + n_out:]):
            phase()

    exchange.deliver(pl.pallas_call(
        body, name=name, in_specs=[HBM] * n_in, out_specs=[HBM] * n_out, out_shape=exchange.out_shape,
        scratch_shapes=exchange.scratch)(*exchange.operands))


def _adamw(w, g, m, v):
    m = ADAM_B1 * m + (1.0 - ADAM_B1) * g
    v = ADAM_B2 * v + (1.0 - ADAM_B2) * (g * g)
    m_hat = m / (1.0 - ADAM_B1 ** ADAM_STEP)
    v_hat = v / (1.0 - ADAM_B2 ** ADAM_STEP)
    delta = -ADAM_LR * (m_hat / (jnp.sqrt(v_hat) + ADAM_EPS) + ADAM_WD * w)
    return delta, m, v


def _adamw_sharded(recvs, w, m, v, tr, name):
    _, r, cdim = w.shape
    nr = r // tr

    def body(r0_ref, r1_ref, w_ref, m_ref, v_ref, g_ref, d_ref, nm_ref, nv_ref):
        l = pl.program_id(0)

        def total(ref):
            acc = ref[0].astype(F32)
            for k in range(1, NDEV):
                acc = acc + ref[k].astype(F32)
            return acc

        g = jnp.where(l == 0, total(r0_ref), total(r1_ref))
        d, nm, nv = _adamw(w_ref[...], g, m_ref[...], v_ref[...])
        g_ref[...] = g
        d_ref[...] = d
        nm_ref[...] = nm
        nv_ref[...] = nv

    mine = pl.BlockSpec((None, tr, cdim), lambda l, i: (l, i, 0))
    out = jax.ShapeDtypeStruct(w.shape, F32)
    return _call(body, name, (DEPTH, nr),
                 [pl.BlockSpec((NDEV, tr, cdim), lambda l, i: (0, jnp.where(l == 0, i, nr - 1), 0)),
                  pl.BlockSpec((NDEV, tr, cdim), lambda l, i: (0, jnp.where(l == 1, i, 0), 0)),
                  mine, mine, mine],
                 [mine] * 4, [out] * 4)(recvs[0], recvs[1], w, m, v)


def _adamw_small(parts, w, m, v, name):
    r = w.shape[0]

    def body(p_ref, w_ref, m_ref, v_ref, g_ref, d_ref, nm_ref, nv_ref):
        g = p_ref[0]
        for k in range(1, NDEV):
            g = g + p_ref[k]
        d, nm, nv = _adamw(w_ref[...], g, m_ref[...], v_ref[...])
        g_ref[...] = g
        d_ref[...] = d
        nm_ref[...] = nm
        nv_ref[...] = nv

    whole = pl.BlockSpec((r, 1024), _fixed2)
    out = jax.ShapeDtypeStruct((r, 1024), F32)
    return _call(body, name, (1,), [pl.BlockSpec((NDEV, r, 1024), lambda i: (0, 0, 0)), whole, whole, whole],
                 [whole] * 4, [out] * 4)(parts, w, m, v)


SMALL = (("norm_mix_pre", (D,)), ("b_gate", (2 * D,)), ("rel_bias", (HEADS, NREL)), ("w_pool_group", (4, PG, PG)),
         ("pool_scale", (PW,)), ("norm_mix_post", (D,)), ("norm_ffn_pre", (D,)), ("conv_b", (NFF * 2 * FS,)),
         ("norm_ffn_post", (D,)))
SHARDED = (("w_in", 256), ("w_attn_out", 512), ("w_pool_out", 512), ("w_o", 128), ("w_up", 256), ("conv_w", 3),
           ("w_down", 176))
SMALL_ROWS = 168


def _pack_small(tree):
    flat = jnp.concatenate([tree[name].reshape(-1) for name, _ in SMALL])
    return jnp.pad(flat, (0, SMALL_ROWS * 1024 - flat.shape[0])).reshape(SMALL_ROWS, 1024)


def _unpack_small(packed):
    flat = packed.reshape(-1)
    out, at = {}, 0
    for name, shape in SMALL:
        size = DEPTH * int(np.prod(shape))
        out[name] = flat[at:at + size].reshape((DEPTH,) + shape)
        at += size
    return out


def _layer_fwd(l, x, h, p, gath, target, gnext, carry):
    n = f"l{l}"
    g = lambda name: p[name][l].reshape(1, -1)
    res = {"x": x, "h": h}
    res["tab"] = _bias_table(p["rel_bias"][l], f"bias_table_{n}")
    proj = _mm_in(h, gath["w_in"], f"mm_in_{n}", carry.get("mm_in"))
    att = _attn_fwd(proj, res["tab"], f"attn_fwd_{n}", carry.get("attn_fwd"))
    pooled, mixed = _pool_fwd(proj, p["w_pool_group"][l], g("pool_scale"), f"pool_fwd_{n}")
    bgate = g("b_gate")
    z, ya, yb = _branch_fwd(att, mixed, gath["w_attn_out"], gath["w_pool_out"], proj, bgate, f"branch_fwd_{n}")
    mix, x1, h2 = _mm_o_fwd(z, gath["w_o"], x, g("norm_mix_post"), g("norm_ffn_pre"), f"mm_o_fwd_{n}")
    hu = _mm_up(h2, gath["w_up"], f"mm_up_{n}", carry.get("mm_up")).reshape(2, NFF, -1, FS)
    cw = gath["conv_w"].reshape(2, NFF, 3, FS)
    cb = p["conv_b"][l].reshape(2, NFF, 1, FS)
    a = _conv_gate_fwd(hu, cw, cb, f"conv_gate_fwd_{n}", carry.get("conv_gate_fwd"))
    res.update(proj=proj, att=att, pooled=pooled, mixed=mixed, z=z, ya=ya, yb=yb, mix=mix, x1=x1, h2=h2, hu=hu,
               a=a, cw=cw, cb=cb, bgate=bgate)
    if target is None:
        f, x2, hn = _mm_down_fwd(a, gath["w_down"], x1, g("norm_ffn_post"), gnext, f"mm_down_fwd_{n}")
        res["f"] = f
        return res, x2, hn
    f, dy, loss = _mm_down_loss(a, gath["w_down"], x1, g("norm_ffn_post"), target, f"mm_down_loss_{n}")
    res["f"] = f
    return res, dy, loss


def _layer_bwd(l, dx2, p, gath, res, carry):
    n = f"l{l}"
    g = lambda name: p[name][l].reshape(1, -1)
    big, small = {}, {}
    taken = lambda call: carry[call](big) if call in carry else None
    df, small["norm_ffn_post"] = _norm_bwd(res["f"], g("norm_ffn_post"), dx2, f"norm_bwd_{n}")
    da = _mm_da(df, gath["w_down"], f"mm_da_{n}")
    dhu, dcw, dcb = _conv_gate_bwd(res["hu"], da, res["cw"], res["cb"], f"conv_gate_bwd_{n}", taken("conv_gate_bwd"))
    big["conv_w"] = dcw.reshape(NDEV, 3, FS)
    small["conv_b"] = dcb
    big["w_down"] = _dw_down(res["a"], df, f"dw_down_{n}")
    dx1, dmix, small["norm_ffn_pre"], small["norm_mix_post"] = _mm_dh2(
        dhu, gath["w_up"], res["x1"], g("norm_ffn_pre"), dx2, res["mix"], g("norm_mix_post"), f"mm_dh2_{n}")
    big["w_up"] = _dw_up(res["h2"], dhu, f"dw_up_{n}")
    dya, dyb, dgates, small["b_gate"] = _mm_dz(dmix, gath["w_o"], res["proj"], res["bgate"], res["ya"], res["yb"],
                                                f"mm_dz_{n}")
    big["w_o"] = _dw_o(res["z"], dmix, f"dw_o_{n}")
    datt, dmixed = _branch_bwd(dya, dyb, gath["w_attn_out"], gath["w_pool_out"], f"branch_bwd_{n}")
    big["w_attn_out"], big["w_pool_out"] = _dw_branch(res["att"], res["mixed"], dya, dyb, f"dw_branch_{n}")
    dq, dk, dv, dtab = _attn_bwd(res["proj"], datt, res["tab"], f"attn_bwd_{n}", taken("attn_bwd"))
    small["rel_bias"] = _rel_bias_grad(dtab, f"rel_bias_grad_{n}")[:, 0, :NREL]
    du, small["w_pool_group"], small["pool_scale"] = _pool_bwd(
        dmixed, res["pooled"], p["w_pool_group"][l], g("pool_scale"), f"pool_bwd_{n}")
    segs = (dq, dk, dv, du, dgates)
    dx, small["norm_mix_pre"] = _mm_dh(segs, gath["w_in"], res["x"], g("norm_mix_pre"), dx1, f"mm_dh_{n}")
    big["w_in"] = _dw_in(res["h"], segs, f"dw_in_{n}")
    return dx, big, small


def _gather_weights(gath, l, p, names):
    shards = [p[k][l] if k == "conv_w" else p[k][l].astype(BF16) for k in names]

    def deliver(results):
        for k, a in zip(names, results):
            if k in ("w_attn_out", "w_pool_out"):
                a = jnp.transpose(a, (1, 0, 2)).reshape(AW, D)
            elif k == "w_o":
                a = a.reshape(D, D)
            elif k == "w_down":
                a = a.reshape(NFF * FS, D)
            gath[k] = a

    return _gather_exchange(shards, deliver)


def _scatter_grads(recv, big, names):
    return _scatter_exchange([big[k] for k in names], lambda results: recv.update(zip(names, results)))


def kernel(x, norm_mix_pre, w_in, b_gate, rel_bias, w_attn_out, w_pool_group, pool_scale, w_pool_out, w_o, norm_mix_post, norm_ffn_pre, w_up, conv_w, conv_b, w_down, norm_ffn_post, loss_target, m_norm_mix_pre, m_w_in, m_b_gate, m_rel_bias, m_w_attn_out, m_w_pool_group, m_pool_scale, m_w_pool_out, m_w_o, m_norm_mix_post, m_norm_ffn_pre, m_w_up, m_conv_w, m_conv_b, m_w_down, m_norm_ffn_post, v_norm_mix_pre, v_w_in, v_b_gate, v_rel_bias, v_w_attn_out, v_w_pool_group, v_pool_scale, v_w_pool_out, v_w_o, v_norm_mix_post, v_norm_ffn_pre, v_w_up, v_conv_w, v_conv_b, v_w_down, v_norm_ffn_post):
    names = ("norm_mix_pre", "w_in", "b_gate", "rel_bias", "w_attn_out", "w_pool_group", "pool_scale", "w_pool_out",
             "w_o", "norm_mix_post", "norm_ffn_pre", "w_up", "conv_w", "conv_b", "w_down", "norm_ffn_post")
    p = dict(zip(names, (norm_mix_pre, w_in, b_gate, rel_bias, w_attn_out, w_pool_group, pool_scale, w_pool_out, w_o,
                         norm_mix_post, norm_ffn_pre, w_up, conv_w, conv_b, w_down, norm_ffn_post)))
    mom = dict(zip(names, (m_norm_mix_pre, m_w_in, m_b_gate, m_rel_bias, m_w_attn_out, m_w_pool_group, m_pool_scale,
                           m_w_pool_out, m_w_o, m_norm_mix_post, m_norm_ffn_pre, m_w_up, m_conv_w, m_conv_b, m_w_down,
                           m_norm_ffn_post)))
    var = dict(zip(names, (v_norm_mix_pre, v_w_in, v_b_gate, v_rel_bias, v_w_attn_out, v_w_pool_group, v_pool_scale,
                           v_w_pool_out, v_w_o, v_norm_mix_post, v_norm_ffn_pre, v_w_up, v_conv_w, v_conv_b, v_w_down,
                           v_norm_ffn_post)))
    s = x.shape[1]
    xs = x.reshape(s, D)
    target = loss_target.reshape(s, D)

    gath = [{}, {}]
    rest = ("w_attn_out", "w_pool_out", "w_o", "w_down", "conv_w")
    _exchange_alone(_gather_weights(gath[0], 0, p, ("w_in",)), "all_gather_l0_w_in")
    h0 = _rmsnorm_call(xs, p["norm_mix_pre"][0].reshape(1, D), "rmsnorm_l0")
    res0, x2, h1 = _layer_fwd(0, xs, h0, p, gath[0], None, p["norm_mix_pre"][1].reshape(1, D), {
        "mm_in": _gather_weights(gath[0], 0, p, rest),
        "attn_fwd": _gather_weights(gath[0], 0, p, ("w_up",)),
        "mm_up": _gather_weights(gath[1], 1, p, ("w_in",) + rest),
        "conv_gate_fwd": _gather_weights(gath[1], 1, p, ("w_up",))})
    res1, dy, loss_part = _layer_fwd(1, x2, h1, p, gath[1], target, None, {})

    order = [k for k, _ in SHARDED]
    recv = [{}, {}]
    dx2, big1, small1 = _layer_bwd(1, dy, p, gath[1], res1, {})
    grad_x, big0, small0 = _layer_bwd(0, dx2, p, gath[0], res0, {
        "conv_gate_bwd": lambda big: _scatter_grads(recv[1], big1, order),
        "attn_bwd": lambda big: _scatter_grads(recv[0], big, [k for k in order if k != "w_in"])})
    _exchange_alone(_scatter_grads(recv[0], big0, ["w_in"]), "all_to_all_l0_w_in")

    loss = lax.psum(loss_part[0, 0], ("x", "y", "c"))

    out = {}
    for k, tr in SHARDED:
        out[k] = _adamw_sharded((recv[0][k], recv[1][k]), p[k], mom[k], var[k], tr, f"adamw_{k}")

    small = {k: jnp.stack([small0[k].reshape(shape), small1[k].reshape(shape)]) for k, shape in SMALL}
    parts = []
    _exchange_alone(_gather_exchange([_pack_small(small)], parts.extend), "all_gather_small")
    packed = _adamw_small(parts[0], _pack_small(p), _pack_small(mom), _pack_small(var), "adamw_small")
    unpacked = [_unpack_small(a) for a in packed]
    for k, _ in SMALL:
        out[k] = tuple(u[k] for u in unpacked)

    return (loss, grad_x.reshape(x.shape), *[out[k][0] for k in names], *[out[k][1] for k in names],
            *[out[k][2] for k in names], *[out[k][3] for k in names])
```

```python
import numpy as np
import jax
import jax.numpy as jnp
from jax import lax
from jax.experimental import pallas as pl
from jax.experimental.pallas import tpu as pltpu

F32, BF16 = jnp.float32, jnp.bfloat16

D = 1024
AW = 512
PW = 512
PG = 128
INW = 4096
FS = 704
NFF = 4
NDEV = 8
DEPTH = 2
HEADS = 8
NREL = 513
MAXREL = 256
POOL_WINDOWS = (2, 4, 8, 16)
EPS = 1e-6
SCALE = 0.125
NEG = -1e30
QB = 256
KW = 3 * QB
BAND = 576
ADAM_LR, ADAM_B1, ADAM_B2, ADAM_EPS, ADAM_WD, ADAM_STEP = 0.001, 0.9, 0.999, 1e-08, 0.01, 10
VMEM_LIMIT_V7X = 56 * 1024 * 1024
MESH = pl.DeviceIdType.MESH
GELU_C = 0.7978845608028654
GELU_A = 0.044715


HBM = pl.BlockSpec(memory_space=pltpu.HBM)


class _Exchange:
    def __init__(self, operands, out_shape, scratch, phases, deliver):
        self.operands, self.out_shape, self.scratch = list(operands), list(out_shape), list(scratch)
        self.phases, self.deliver = phases, deliver


def _call(body, name, grid, in_specs, out_specs, out_shape, scratch=(), exchange=None):
    params = pltpu.CompilerParams(vmem_limit_bytes=VMEM_LIMIT_V7X)
    if exchange is None:
        return pl.pallas_call(body, name=name, grid=grid, in_specs=in_specs, out_specs=out_specs, out_shape=out_shape,
                              scratch_shapes=list(scratch), compiler_params=params)
    single = not isinstance(out_shape, (list, tuple))
    outs, ospecs = ([out_shape], [out_specs]) if single else (list(out_shape), list(out_specs))
    n_in, n_out, n_scr = len(in_specs), len(outs), len(scratch)
    ne_in, ne_out = len(exchange.operands), len(exchange.out_shape)
    nsteps = int(np.prod(grid))

    def carried(*refs):
        cut = np.cumsum([0, n_in, ne_in, n_out, ne_out, n_scr])
        base_in, ex_in, base_out, ex_out, base_scr = (refs[cut[k]:cut[k + 1]] for k in range(5))
        step = pl.program_id(0)
        for axis in range(1, len(grid)):
            step = step * grid[axis] + pl.program_id(axis)
        phases = exchange.phases(ex_in, ex_out, refs[cut[5]:])
        pl.when(step == 0)(phases[0])
        body(*base_in, *base_out, *base_scr)
        at = {2: [nsteps - 1], 3: [(7 * nsteps) // 10, nsteps - 1]}[len(phases)]
        for phase, when in zip(phases[1:], at):
            pl.when(step == when)(phase)

    call = pl.pallas_call(
        carried, name=name, grid=grid, in_specs=list(in_specs) + [HBM] * ne_in, out_specs=ospecs + [HBM] * ne_out,
        out_shape=outs + exchange.out_shape, scratch_shapes=list(scratch) + exchange.scratch, compiler_params=params)

    def run(*args):
        res = call(*args, *exchange.operands)
        exchange.deliver(res[n_out:])
        return res[0] if single else res[:n_out]

    return run


def _dot(a, b):
    return jnp.dot(a, b, preferred_element_type=F32)


def _dot_nt(a, b):
    return lax.dot_general(a, b, (((1,), (1,)), ((), ())), preferred_element_type=F32)


def _dot_tn(a, b):
    return lax.dot_general(a, b, (((0,), (0,)), ((), ())), preferred_element_type=F32)


def _rms(x, g):
    r = lax.rsqrt(jnp.mean(x * x, axis=-1, keepdims=True) + EPS)
    return x * r * g


def _rms_bwd(x, g, dy):
    r = lax.rsqrt(jnp.mean(x * x, axis=-1, keepdims=True) + EPS)
    xh = x * r
    dg = jnp.sum(dy * xh, axis=0, keepdims=True)
    dxh = dy * g
    dx = r * (dxh - xh * jnp.mean(dxh * xh, axis=-1, keepdims=True))
    return dx, dg


def _sigmoid(x):
    return 1.0 / (1.0 + jnp.exp(-x))


def _gelu_parts(x):
    x2 = x * x
    th = jnp.tanh(x * (GELU_C + (GELU_C * GELU_A) * x2))
    s = 0.5 * th + 0.5
    dg = s * (1.0 + x * (1.0 - s) * (2.0 * GELU_C + (6.0 * GELU_C * GELU_A) * x2))
    return x * s, dg


def _row(i):
    return (i, 0)


def _fixed2(*_):
    return (0, 0)


def _rmsnorm_call(x, g, name):
    s = x.shape[0]
    tm = 512

    def body(x_ref, g_ref, o_ref):
        o_ref[...] = _rms(x_ref[...], g_ref[...]).astype(BF16)

    return _call(body, name, (s // tm,),
                 [pl.BlockSpec((tm, D), _row), pl.BlockSpec((1, D), _fixed2)],
                 pl.BlockSpec((tm, D), _row), jax.ShapeDtypeStruct((s, D), BF16))(x, g)


def _resident(shape):
    return pl.BlockSpec(shape, lambda *_: (0,) * len(shape), pipeline_mode=pl.Buffered(1))


def _mm_in(h, win, name, exchange=None):
    s = h.shape[0]
    tm = 512

    def body(h_ref, w_ref, o_ref):
        hv = h_ref[...]
        for j in range(NDEV):
            o_ref[:, j * 512:(j + 1) * 512] = _dot(hv, w_ref[j])

    return _call(body, name, (s // tm,),
                 [pl.BlockSpec((tm, D), _row), _resident((NDEV, D, 512))],
                 pl.BlockSpec((tm, INW), _row), jax.ShapeDtypeStruct((s, INW), F32), exchange=exchange)(h, win)


def _bias_table(rel_bias, name):
    wdt = 1024
    rel = jnp.pad(rel_bias, ((0, 0), (0, 640 - NREL))).reshape(HEADS, 1, 640)

    def body(r_ref, o_ref):
        rr = lax.broadcasted_iota(jnp.int32, (640, wdt), 0)
        m = lax.broadcasted_iota(jnp.int32, (640, wdt), 1)
        d = jnp.where(m < KW, m, m - wdt)
        onehot = (jnp.clip(512 - d, -MAXREL, MAXREL) + MAXREL == rr).astype(F32)
        row = jnp.dot(jnp.broadcast_to(r_ref[...], (8, 640)), onehot, preferred_element_type=F32,
                      precision=lax.Precision.HIGHEST)[0:1]
        t = pltpu.roll(jnp.broadcast_to(row, (QB, wdt)), 0, 1, stride=1, stride_axis=0)[:, 0:KW]
        qc = lax.broadcasted_iota(jnp.int32, (QB, KW), 0) // 64
        kc = lax.broadcasted_iota(jnp.int32, (QB, KW), 1) // 64
        o_ref[...] = jnp.where((kc >= qc) & (kc <= qc + 8), t, NEG)

    return _call(body, name, (HEADS,), [pl.BlockSpec((None, 1, 640), lambda h: (h, 0, 0))],
                 pl.BlockSpec((None, QB, KW), lambda h: (h, 0, 0)),
                 jax.ShapeDtypeStruct((HEADS, QB, KW), F32))(rel)


def _attn_probs(q_ref, k_refs, b_ref, i):
    lane = lax.broadcasted_iota(jnp.int32, (QB, 128), 1)
    q = q_ref[...] * SCALE
    qs = [jnp.where(lane < 64, q, 0.0).astype(BF16), jnp.where(lane >= 64, q, 0.0).astype(BF16)]
    k = jnp.concatenate([r[...] for r in k_refs], axis=0).astype(BF16)
    colb = lax.broadcasted_iota(jnp.int32, (QB, KW), 1) // QB
    valid = colb + i >= 2
    ps = []
    for hh in (0, 1):
        sc = _dot_nt(qs[hh], k) + b_ref[hh]
        sc = jnp.where(valid, sc, NEG)
        m = jnp.max(sc, axis=1, keepdims=True)
        p = jnp.exp(sc - m)
        ps.append(p / jnp.sum(p, axis=1, keepdims=True))
    return qs, ps, k, lane


def _attn_fwd(proj, tab, name, exchange=None):
    s = proj.shape[0]
    nq = s // QB

    def body(q_ref, k0, k1, k2, v0, v1, v2, b_ref, o_ref):
        i = pl.program_id(1)
        _, ps, _, lane = _attn_probs(q_ref, (k0, k1, k2), b_ref, i)
        v = jnp.concatenate([v0[...], v1[...], v2[...]], axis=0).astype(BF16)
        o = [_dot(ps[hh].astype(BF16), v) for hh in (0, 1)]
        o_ref[...] = jnp.where(lane < 64, o[0], o[1]).astype(BF16)

    def kv(col, d):
        return pl.BlockSpec((QB, 128), lambda p, i: (jnp.maximum(i - 2 + d, 0), col + p))

    in_specs = [pl.BlockSpec((QB, 128), lambda p, i: (i, p))]
    in_specs += [kv(4, d) for d in range(3)] + [kv(8, d) for d in range(3)]
    in_specs += [pl.BlockSpec((2, QB, KW), lambda p, i: (p, 0, 0))]
    return _call(body, name, (4, nq), in_specs, pl.BlockSpec((QB, 128), lambda p, i: (i, p)),
                 jax.ShapeDtypeStruct((s, AW), BF16), exchange=exchange)(proj, proj, proj, proj, proj, proj, proj, tab)


def _pool_fwd(proj, wg, scale, name):
    s = proj.shape[0]
    tb = 512
    e = tb + 16

    def body(u_ref, halo_ref, wg_ref, sc_ref, pooled_ref, mixed_ref):
        i = pl.program_id(0)
        cur = u_ref[...]
        prev = jnp.where(i > 0, halo_ref[...], 0.0)
        xs = jnp.concatenate([prev, cur], axis=0)
        t = i * tb + lax.broadcasted_iota(jnp.int32, (tb, 1), 0)
        for g, w in enumerate(POOL_WINDOWS):
            sl = slice(g * PG, (g + 1) * PG)
            a = xs[:, sl]
            sh = 1
            while sh < w:
                a = a + pltpu.roll(a, sh, 0)
                sh *= 2
            cnt = jnp.minimum(t + 1, w).astype(F32)
            pooled = (a[16:] / cnt - cur[:, sl]).astype(BF16)
            pooled_ref[:, sl] = pooled
            mixed_ref[:, sl] = (_dot(pooled, wg_ref[g].astype(BF16)) * sc_ref[:, sl]).astype(BF16)

    assert e % 8 == 0
    return _call(body, name, (s // tb,),
                 [pl.BlockSpec((tb, PW), lambda i: (i, 3)),
                  pl.BlockSpec((16, PW), lambda i: (jnp.maximum(i * (tb // 16) - 1, 0), 3)),
                  pl.BlockSpec((4, PG, PG), lambda i: (0, 0, 0)), pl.BlockSpec((1, PW), _fixed2)],
                 [pl.BlockSpec((tb, PW), _row), pl.BlockSpec((tb, PW), _row)],
                 [jax.ShapeDtypeStruct((s, PW), BF16), jax.ShapeDtypeStruct((s, PW), BF16)])(proj, proj, wg, scale)


def _branch_fwd(att, mixed, wao, wpo, proj, bgate, name):
    s = att.shape[0]
    tm = 512

    def body(att_ref, mx_ref, wao_ref, wpo_ref, ga_ref, gb_ref, ba_ref, bb_ref, z_ref, ya_ref, yb_ref):
        ya = _dot(att_ref[...], wao_ref[...])
        yb = _dot(mx_ref[...], wpo_ref[...])
        ga = _sigmoid(ga_ref[...] + ba_ref[...])
        gb = _sigmoid(gb_ref[...] + bb_ref[...])
        ya_ref[...] = ya
        yb_ref[...] = yb
        z_ref[...] = (ga * ya + gb * yb).astype(BF16)

    return _call(body, name, (s // tm,),
                 [pl.BlockSpec((tm, AW), _row), pl.BlockSpec((tm, PW), _row),
                  pl.BlockSpec((AW, D), _fixed2), pl.BlockSpec((PW, D), _fixed2),
                  pl.BlockSpec((tm, D), lambda i: (i, 2)), pl.BlockSpec((tm, D), lambda i: (i, 3)),
                  pl.BlockSpec((1, D), lambda i: (0, 0)), pl.BlockSpec((1, D), lambda i: (0, 1))],
                 [pl.BlockSpec((tm, D), _row)] * 3,
                 [jax.ShapeDtypeStruct((s, D), BF16), jax.ShapeDtypeStruct((s, D), F32),
                  jax.ShapeDtypeStruct((s, D), F32)])(att, mixed, wao, wpo, proj, proj, bgate, bgate)


def _mm_o_fwd(z, wo, x, g2, g3, name):
    s = z.shape[0]
    tm = 512

    def body(z_ref, w_ref, x_ref, g2_ref, g3_ref, mix_ref, x1_ref, h2_ref):
        mix = _dot(z_ref[...], w_ref[...])
        x1 = x_ref[...] + _rms(mix, g2_ref[...])
        mix_ref[...] = mix
        x1_ref[...] = x1
        h2_ref[...] = _rms(x1, g3_ref[...]).astype(BF16)

    return _call(body, name, (s // tm,),
                 [pl.BlockSpec((tm, D), _row), pl.BlockSpec((D, D), _fixed2), pl.BlockSpec((tm, D), _row),
                  pl.BlockSpec((1, D), _fixed2), pl.BlockSpec((1, D), _fixed2)],
                 [pl.BlockSpec((tm, D), _row)] * 3,
                 [jax.ShapeDtypeStruct((s, D), F32), jax.ShapeDtypeStruct((s, D), F32),
                  jax.ShapeDtypeStruct((s, D), BF16)])(z, wo, x, g2, g3)


def _mm_up(h2, wup, name, exchange=None):
    s = h2.shape[0]
    tm = 512

    def body(h_ref, w_ref, o_ref):
        hv = h_ref[...]
        for j in range(NDEV):
            o_ref[j] = _dot(hv, w_ref[j])

    return _call(body, name, (s // tm,),
                 [pl.BlockSpec((tm, D), _row), _resident((NDEV, D, FS))],
                 pl.BlockSpec((NDEV, tm, FS), lambda i: (0, i, 0)),
                 jax.ShapeDtypeStruct((NDEV, s, FS), F32), exchange=exchange)(h2, wup)


def _conv_rows(xs, cw, cb):
    m1 = pltpu.roll(xs, 1, 0)
    m2 = pltpu.roll(xs, 2, 0)
    return cb + cw[2:3] * xs + cw[0:1] * m2 + cw[1:2] * m1, m1, m2


def _conv_gate_fwd(hu, cw, cb, name, exchange=None):
    s = hu.shape[2]
    tb = 512

    def body(hu_ref, halo_ref, cw_ref, cb_ref, a_ref):
        i = pl.program_id(1)
        hc = []
        for sd in (0, 1):
            prev = jnp.where(i > 0, halo_ref[sd], 0.0)
            xs = jnp.concatenate([prev, hu_ref[sd]], axis=0)
            hc.append(_conv_rows(xs, cw_ref[sd], cb_ref[sd])[0][8:])
        a_ref[...] = (_gelu_parts(hc[1])[0] * hc[0]).astype(BF16)

    return _call(body, name, (NFF, s // tb),
                 [pl.BlockSpec((2, None, tb, FS), lambda j, i: (0, j, i, 0)),
                  pl.BlockSpec((2, None, 8, FS), lambda j, i: (0, j, jnp.maximum(i * (tb // 8) - 1, 0), 0)),
                  pl.BlockSpec((2, None, 3, FS), lambda j, i: (0, j, 0, 0)),
                  pl.BlockSpec((2, None, 1, FS), lambda j, i: (0, j, 0, 0))],
                 pl.BlockSpec((None, tb, FS), lambda j, i: (j, i, 0)),
                 jax.ShapeDtypeStruct((NFF, s, FS), BF16), exchange=exchange)(hu, hu, cw, cb)


def _mm_down_fwd(a, wd, x1, g4, gnext, name):
    s = a.shape[1]
    tm = 512

    def body(a_ref, w_ref, x1_ref, g4_ref, gn_ref, f_ref, x2_ref, hn_ref):
        f = _dot(a_ref[0], w_ref[0:FS, :])
        for j in range(1, NFF):
            f = f + _dot(a_ref[j], w_ref[j * FS:(j + 1) * FS, :])
        x2 = x1_ref[...] + _rms(f, g4_ref[...])
        f_ref[...] = f
        x2_ref[...] = x2
        hn_ref[...] = _rms(x2, gn_ref[...]).astype(BF16)

    return _call(body, name, (s // tm,),
                 [pl.BlockSpec((NFF, tm, FS), lambda i: (0, i, 0)), pl.BlockSpec((NFF * FS, D), _fixed2),
                  pl.BlockSpec((tm, D), _row), pl.BlockSpec((1, D), _fixed2), pl.BlockSpec((1, D), _fixed2)],
                 [pl.BlockSpec((tm, D), _row)] * 3,
                 [jax.ShapeDtypeStruct((s, D), F32), jax.ShapeDtypeStruct((s, D), F32),
                  jax.ShapeDtypeStruct((s, D), BF16)])(a, wd, x1, g4, gnext)


def _mm_down_loss(a, wd, x1, g4, target, name):
    s = a.shape[1]
    tm = 512

    def body(a_ref, w_ref, x1_ref, g4_ref, t_ref, f_ref, dy_ref, loss_ref):
        i = pl.program_id(0)
        f = _dot(a_ref[0], w_ref[0:FS, :])
        for j in range(1, NFF):
            f = f + _dot(a_ref[j], w_ref[j * FS:(j + 1) * FS, :])
        err = x1_ref[...] + _rms(f, g4_ref[...]) - t_ref[...]
        f_ref[...] = f
        dy_ref[...] = err * (1.0 / D)
        part = 0.5 * jnp.sum(jnp.mean(err * err, axis=-1, keepdims=True), axis=0, keepdims=True)

        @pl.when(i == 0)
        def _():
            loss_ref[...] = jnp.zeros_like(loss_ref)

        loss_ref[...] += jnp.broadcast_to(part, loss_ref.shape)

    return _call(body, name, (s // tm,),
                 [pl.BlockSpec((NFF, tm, FS), lambda i: (0, i, 0)), pl.BlockSpec((NFF * FS, D), _fixed2),
                  pl.BlockSpec((tm, D), _row), pl.BlockSpec((1, D), _fixed2), pl.BlockSpec((tm, D), _row)],
                 [pl.BlockSpec((tm, D), _row), pl.BlockSpec((tm, D), _row), pl.BlockSpec((8, 128), _fixed2)],
                 [jax.ShapeDtypeStruct((s, D), F32), jax.ShapeDtypeStruct((s, D), F32),
                  jax.ShapeDtypeStruct((8, 128), F32)])(a, wd, x1, g4, target)


def _norm_bwd(f, g, dy, name):
    s = f.shape[0]
    tm = 512

    def body(f_ref, g_ref, dy_ref, df_ref, dg_ref):
        i = pl.program_id(0)
        df, dg = _rms_bwd(f_ref[...], g_ref[...], dy_ref[...])
        df_ref[...] = df.astype(BF16)

        @pl.when(i == 0)
        def _():
            dg_ref[...] = jnp.zeros_like(dg_ref)

        dg_ref[...] += dg

    return _call(body, name, (s // tm,),
                 [pl.BlockSpec((tm, D), _row), pl.BlockSpec((1, D), _fixed2), pl.BlockSpec((tm, D), _row)],
                 [pl.BlockSpec((tm, D), _row), pl.BlockSpec((1, D), _fixed2)],
                 [jax.ShapeDtypeStruct((s, D), BF16), jax.ShapeDtypeStruct((1, D), F32)])(f, g, dy)


LANE_COLUMNS = [(c0, min(128, FS - c0)) for c0 in range(0, FS, 128)]


def _conv_gate_bwd(hu, df, wd, cw, cb, name, exchange=None):
    s = hu.shape[2]
    tb = 512
    nt = s // tb
    e = tb + 16

    def body(hu_ref, prev_ref, next_ref, df_ref, dfn_ref, wd_ref, cw_ref, cb_ref, dhu_ref, dcw_ref, dcb_ref,
             dabuf, dbuf):
        i = pl.program_id(1)
        first, last = i == 0, i == nt - 1
        dabuf[0:tb] = _dot_nt(df_ref[...], wd_ref[...])
        dabuf[tb:tb + 8] = jnp.where(last, 0.0, _dot_nt(dfn_ref[...], wd_ref[...])[0:8])

        @pl.when(i == 0)
        def _():
            dcw_ref[...] = jnp.zeros_like(dcw_ref)
            dcb_ref[...] = jnp.zeros_like(dcb_ref)

        for c0, w in LANE_COLUMNS:
            lanes = pl.ds(c0, w)
            cwb = [[jnp.broadcast_to(cw_ref[sd, t:t + 1, lanes], (8, w)) for t in range(3)] for sd in (0, 1)]
            cbb = [jnp.broadcast_to(cb_ref[sd, :, lanes], (8, w)) for sd in (0, 1)]

            row = lax.broadcasted_iota(jnp.int32, (8, w), 0)

            def tile(k, carry, summed=True):
                sums, rolled = carry[:8], carry[8:]
                r = pl.multiple_of(8 + 8 * k, 8)
                xs, keep = [], []
                for sd in (0, 1):
                    if summed:
                        cur = hu_ref[sd, pl.ds(r - 8, 8), lanes]
                    else:
                        cur = jnp.where(last, 0.0, next_ref[sd, :, lanes])
                    r1, r2 = pltpu.roll(cur, 1, 0), pltpu.roll(cur, 2, 0)
                    xs.append([jnp.where(row >= 2, r2, rolled[2 * sd + 1]), jnp.where(row >= 1, r1, rolled[2 * sd]),
                               cur])
                    keep += [r1, r2]
                hc = [cbb[sd] + cwb[sd][2] * xs[sd][2] + cwb[sd][0] * xs[sd][0] + cwb[sd][1] * xs[sd][1]
                      for sd in (0, 1)]
                da = dabuf[pl.ds(r - 8, 8), lanes]
                gl, dgl = _gelu_parts(hc[1])
                dhc = (da * gl, da * hc[0] * dgl)
                for sd in (0, 1):
                    dbuf[sd, pl.ds(r, 8), lanes] = dhc[sd]
                if not summed:
                    return carry
                new = []
                for sd in (0, 1):
                    new += [sums[4 * sd + t] + dhc[sd] * xs[sd][t] for t in range(3)] + [sums[4 * sd + 3] + dhc[sd]]
                return tuple(new + keep)

            start = [jnp.zeros((8, w), F32) for _ in range(8)]
            for sd in (0, 1):
                halo = jnp.where(first, 0.0, prev_ref[sd, :, lanes])
                start += [pltpu.roll(halo, 1, 0), pltpu.roll(halo, 2, 0)]
            def tiles(k4, carry):
                for u in range(4):
                    carry = tile(4 * k4 + u, carry)
                return carry

            carry = lax.fori_loop(0, tb // 32, tiles, tuple(start))
            tile(tb // 8, carry, summed=False)
            sums = carry[:8]

            def up(v):
                return pltpu.roll(v, 7, 0), pltpu.roll(v, 6, 0)

            def out_tile(k, carry):
                r = pl.multiple_of(8 + 16 * k, 8)
                new = []
                for sd in (0, 1):
                    va, va1, va2 = carry[3 * sd:3 * sd + 3]
                    vb, vc = dbuf[sd, pl.ds(r + 8, 8), lanes], dbuf[sd, pl.ds(r + 16, 8), lanes]
                    (vb1, vb2), (vc1, vc2) = up(vb), up(vc)
                    c0b, c1b, c2b = cwb[sd]
                    top = c2b * va + c1b * jnp.where(row <= 6, va1, vb1) + c0b * jnp.where(row <= 5, va2, vb2)
                    bot = c2b * vb + c1b * jnp.where(row <= 6, vb1, vc1) + c0b * jnp.where(row <= 5, vb2, vc2)
                    dhu_ref[sd, pl.ds(pl.multiple_of(16 * k, 16), 16), lanes] = jnp.concatenate(
                        [top, bot], axis=0).astype(BF16)
                    new += [vc, vc1, vc2]
                return tuple(new)

            begin = []
            for sd in (0, 1):
                va = dbuf[sd, 8:16, lanes]
                begin += [va, *up(va)]
            def out_tiles(k2, carry):
                return out_tile(2 * k2 + 1, out_tile(2 * k2, carry))

            lax.fori_loop(0, tb // 32, out_tiles, tuple(begin))
            for sd in (0, 1):
                for t in range(3):
                    dcw_ref[sd, t:t + 1, lanes] += jnp.sum(sums[4 * sd + t], axis=0, keepdims=True)
                dcb_ref[sd, :, lanes] += jnp.sum(sums[4 * sd + 3], axis=0, keepdims=True)

    nb8, nb16 = s // 8, s // 16
    return _call(body, name, (NFF, nt),
                 [pl.BlockSpec((2, None, tb, FS), lambda j, i: (0, j, i, 0)),
                  pl.BlockSpec((2, None, 8, FS), lambda j, i: (0, j, jnp.maximum(i * (tb // 8) - 1, 0), 0)),
                  pl.BlockSpec((2, None, 8, FS), lambda j, i: (0, j, jnp.minimum((i + 1) * (tb // 8), nb8 - 1), 0)),
                  pl.BlockSpec((tb, D), lambda j, i: (i, 0)),
                  pl.BlockSpec((16, D), lambda j, i: (jnp.minimum((i + 1) * (tb // 16), nb16 - 1), 0)),
                  pl.BlockSpec((FS, D), lambda j, i: (j, 0)),
                  pl.BlockSpec((2, None, 3, FS), lambda j, i: (0, j, 0, 0)),
                  pl.BlockSpec((2, None, 1, FS), lambda j, i: (0, j, 0, 0))],
                 [pl.BlockSpec((2, None, tb, FS), lambda j, i: (0, j, i, 0)),
                  pl.BlockSpec((2, None, 3, FS), lambda j, i: (0, j, 0, 0)),
                  pl.BlockSpec((2, None, 1, FS), lambda j, i: (0, j, 0, 0))],
                 [jax.ShapeDtypeStruct((2, NFF, s, FS), BF16), jax.ShapeDtypeStruct((2, NFF, 3, FS), F32),
                  jax.ShapeDtypeStruct((2, NFF, 1, FS), F32)],
                 [pltpu.VMEM((tb + 8, FS), F32), pltpu.VMEM((2, e, FS), F32)],
                 exchange=exchange)(hu, hu, hu, df, df, wd, cw, cb)


def _dw_down(a, df, name):
    s = a.shape[1]
    tk = 512
    nk = s // tk

    def body(a_ref, df_ref, o_ref, acc):
        k = pl.program_id(1)

        @pl.when(k == 0)
        def _():
            acc[...] = jnp.zeros_like(acc)

        acc[...] += _dot_tn(a_ref[...], df_ref[...])

        @pl.when(k == nk - 1)
        def _():
            o_ref[0] = acc[0:FS // 2, :].astype(BF16)
            o_ref[1] = acc[FS // 2:FS, :].astype(BF16)

    return _call(body, name, (NFF, nk),
                 [pl.BlockSpec((None, tk, FS), lambda j, k: (j, k, 0)), pl.BlockSpec((tk, D), lambda j, k: (k, 0))],
                 pl.BlockSpec((2, FS // 2, D), lambda j, k: (j, 0, 0)),
                 jax.ShapeDtypeStruct((NDEV, FS // 2, D), BF16), [pltpu.VMEM((FS, D), F32)])(a, df)


def _mm_dh2(dhu, wup, x1, g3, dx2, mix, g2, name):
    s = x1.shape[0]
    tm = 512

    def body(dhu_ref, w_ref, x1_ref, g3_ref, dx2_ref, mix_ref, g2_ref, dx1_ref, dmix_ref, dg3_ref, dg2_ref):
        i = pl.program_id(0)

        @pl.when(i == 0)
        def _():
            dg3_ref[...] = jnp.zeros_like(dg3_ref)
            dg2_ref[...] = jnp.zeros_like(dg2_ref)

        dh2 = _dot_nt(dhu_ref[0, 0], w_ref[0])
        for j in range(1, NDEV):
            dh2 = dh2 + _dot_nt(dhu_ref[j // NFF, j % NFF], w_ref[j])
        dn, dg3 = _rms_bwd(x1_ref[...], g3_ref[...], dh2)
        dx1 = dx2_ref[...] + dn
        dmix, dg2 = _rms_bwd(mix_ref[...], g2_ref[...], dx1)
        dx1_ref[...] = dx1
        dmix_ref[...] = dmix.astype(BF16)
        dg3_ref[...] += dg3
        dg2_ref[...] += dg2

    return _call(body, name, (s // tm,),
                 [pl.BlockSpec((2, NFF, tm, FS), lambda i: (0, 0, i, 0)), _resident((NDEV, D, FS)),
                  pl.BlockSpec((tm, D), _row), pl.BlockSpec((1, D), _fixed2), pl.BlockSpec((tm, D), _row),
                  pl.BlockSpec((tm, D), _row), pl.BlockSpec((1, D), _fixed2)],
                 [pl.BlockSpec((tm, D), _row), pl.BlockSpec((tm, D), _row), pl.BlockSpec((1, D), _fixed2),
                  pl.BlockSpec((1, D), _fixed2)],
                 [jax.ShapeDtypeStruct((s, D), F32), jax.ShapeDtypeStruct((s, D), BF16),
                  jax.ShapeDtypeStruct((1, D), F32), jax.ShapeDtypeStruct((1, D), F32)])(
                     dhu, wup, x1, g3, dx2, mix, g2)


def _dw_up(h2, dhu, name):
    s = h2.shape[0]
    tk = 512
    nk = s // tk

    def body(h_ref, d_ref, o_ref, acc):
        k = pl.program_id(1)

        @pl.when(k == 0)
        def _():
            acc[...] = jnp.zeros_like(acc)

        ht = h_ref[...].T
        for j in range(NFF):
            acc[j] += _dot(ht, d_ref[j])

        @pl.when(k == nk - 1)
        def _():
            o_ref[...] = acc[...].astype(BF16)

    return _call(body, name, (2, nk),
                 [pl.BlockSpec((tk, D), lambda hf, k: (k, 0)),
                  pl.BlockSpec((None, NFF, tk, FS), lambda hf, k: (hf, 0, k, 0))],
                 pl.BlockSpec((NFF, D, FS), lambda hf, k: (hf, 0, 0)),
                 jax.ShapeDtypeStruct((NDEV, D, FS), BF16), [pltpu.VMEM((NFF, D, FS), F32)])(h2, dhu)


def _mm_dz(dmix, wo, proj, bgate, ya, yb, name):
    s = dmix.shape[0]
    tm = 512

    def body(dm_ref, w_ref, ga_ref, gb_ref, ba_ref, bb_ref, ya_ref, yb_ref, dya_ref, dyb_ref, dg_ref, dbg_ref):
        i = pl.program_id(0)
        dz = _dot_nt(dm_ref[...], w_ref[...])
        ga = _sigmoid(ga_ref[...] + ba_ref[...])
        gb = _sigmoid(gb_ref[...] + bb_ref[...])
        dya_ref[...] = (dz * ga).astype(BF16)
        dyb_ref[...] = (dz * gb).astype(BF16)
        dga = dz * ya_ref[...] * ga * (1.0 - ga)
        dgb = dz * yb_ref[...] * gb * (1.0 - gb)
        dg_ref[:, 0:D] = dga.astype(BF16)
        dg_ref[:, D:2 * D] = dgb.astype(BF16)

        @pl.when(i == 0)
        def _():
            dbg_ref[...] = jnp.zeros_like(dbg_ref)

        dbg_ref[:, 0:D] += jnp.sum(dga, axis=0, keepdims=True)
        dbg_ref[:, D:2 * D] += jnp.sum(dgb, axis=0, keepdims=True)

    return _call(body, name, (s // tm,),
                 [pl.BlockSpec((tm, D), _row), pl.BlockSpec((D, D), _fixed2),
                  pl.BlockSpec((tm, D), lambda i: (i, 2)), pl.BlockSpec((tm, D), lambda i: (i, 3)),
                  pl.BlockSpec((1, D), lambda i: (0, 0)), pl.BlockSpec((1, D), lambda i: (0, 1)),
                  pl.BlockSpec((tm, D), _row), pl.BlockSpec((tm, D), _row)],
                 [pl.BlockSpec((tm, D), _row), pl.BlockSpec((tm, D), _row), pl.BlockSpec((tm, 2 * D), _row),
                  pl.BlockSpec((1, 2 * D), _fixed2)],
                 [jax.ShapeDtypeStruct((s, D), BF16), jax.ShapeDtypeStruct((s, D), BF16),
                  jax.ShapeDtypeStruct((s, 2 * D), BF16), jax.ShapeDtypeStruct((1, 2 * D), F32)])(
                     dmix, wo, proj, proj, bgate, bgate, ya, yb)


def _dw_o(z, dmix, name):
    s = z.shape[0]
    tk = 512
    nk = s // tk

    def body(z_ref, d_ref, o_ref, acc):
        k = pl.program_id(0)

        @pl.when(k == 0)
        def _():
            acc[...] = jnp.zeros_like(acc)

        acc[...] += _dot_tn(z_ref[...], d_ref[...])

        @pl.when(k == nk - 1)
        def _():
            for j in range(NDEV):
                o_ref[j] = acc[j * 128:(j + 1) * 128, :].astype(BF16)

    return _call(body, name, (nk,),
                 [pl.BlockSpec((tk, D), _row), pl.BlockSpec((tk, D), _row)],
                 pl.BlockSpec((NDEV, 128, D), lambda k: (0, 0, 0)),
                 jax.ShapeDtypeStruct((NDEV, 128, D), BF16), [pltpu.VMEM((D, D), F32)])(z, dmix)


def _branch_bwd(dya, dyb, wao, wpo, name):
    s = dya.shape[0]
    tm = 512

    def body(dya_ref, dyb_ref, wao_ref, wpo_ref, datt_ref, dmx_ref):
        datt_ref[...] = _dot_nt(dya_ref[...], wao_ref[...]).astype(BF16)
        dmx_ref[...] = _dot_nt(dyb_ref[...], wpo_ref[...])

    return _call(body, name, (s // tm,),
                 [pl.BlockSpec((tm, D), _row), pl.BlockSpec((tm, D), _row),
                  pl.BlockSpec((AW, D), _fixed2), pl.BlockSpec((PW, D), _fixed2)],
                 [pl.BlockSpec((tm, AW), _row), pl.BlockSpec((tm, PW), _row)],
                 [jax.ShapeDtypeStruct((s, AW), BF16), jax.ShapeDtypeStruct((s, PW), F32)])(dya, dyb, wao, wpo)


def _dw_branch(att, mixed, dya, dyb, name):
    s = att.shape[0]
    tk = 512
    nk = s // tk

    def body(att_ref, mx_ref, dya_ref, dyb_ref, oa_ref, ob_ref, acca, accb):
        k = pl.program_id(0)

        @pl.when(k == 0)
        def _():
            acca[...] = jnp.zeros_like(acca)
            accb[...] = jnp.zeros_like(accb)

        acca[...] += _dot_tn(att_ref[...], dya_ref[...])
        accb[...] += _dot_tn(mx_ref[...], dyb_ref[...])

        @pl.when(k == nk - 1)
        def _():
            for j in range(NDEV):
                oa_ref[j] = acca[:, j * 128:(j + 1) * 128].astype(BF16)
                ob_ref[j] = accb[:, j * 128:(j + 1) * 128].astype(BF16)

    out = jax.ShapeDtypeStruct((NDEV, AW, 128), BF16)
    return _call(body, name, (nk,),
                 [pl.BlockSpec((tk, AW), _row), pl.BlockSpec((tk, PW), _row),
                  pl.BlockSpec((tk, D), _row), pl.BlockSpec((tk, D), _row)],
                 [pl.BlockSpec((NDEV, AW, 128), lambda k: (0, 0, 0))] * 2, [out, out],
                 [pltpu.VMEM((AW, D), F32), pltpu.VMEM((PW, D), F32)])(att, mixed, dya, dyb)


def _attn_bwd(proj, datt, tab, name, exchange=None):
    s = proj.shape[0]
    nq = s // QB

    def body(q_ref, k0, k1, k2, v0, v1, v2, do_ref, b_ref, dq_ref, dk_ref, dv_ref, db_ref, dka, dkb, dva, dvb):
        i = pl.program_id(1)

        @pl.when(i == 0)
        def _():
            for r in (dka, dkb, dva, dvb):
                r[...] = jnp.zeros_like(r)
            db_ref[...] = jnp.zeros_like(db_ref)

        @pl.when(i < nq)
        def _():
            qs, ps, k, lane = _attn_probs(q_ref, (k0, k1, k2), b_ref, i)
            v = jnp.concatenate([v0[...], v1[...], v2[...]], axis=0).astype(BF16)
            do = do_ref[...]
            dq = jnp.zeros((QB, 128), F32)
            dkw = jnp.zeros((KW, 128), F32)
            dvw = jnp.zeros((KW, 128), F32)
            for hh in (0, 1):
                mine = (lane < 64) if hh == 0 else (lane >= 64)
                doh = jnp.where(mine, do, jnp.zeros_like(do))
                kmask = lax.broadcasted_iota(jnp.int32, (KW, 128), 1)
                kh = jnp.where((kmask < 64) if hh == 0 else (kmask >= 64), k, jnp.zeros_like(k))
                p = ps[hh]
                dp = _dot_nt(doh, v)
                ds = p * (dp - jnp.sum(dp * p, axis=1, keepdims=True))
                db_ref[hh] += ds
                dsb = ds.astype(BF16)
                dq = dq + _dot(dsb, kh)
                dkw = dkw + _dot_tn(dsb, qs[hh])
                dvw = dvw + _dot_tn(p.astype(BF16), doh)
            dq_ref[...] = (dq * SCALE).astype(BF16)
            dk_ref[...] = (dka[...] + dkw[0:QB]).astype(BF16)
            dka[...] = dkb[...] + dkw[QB:2 * QB]
            dkb[...] = dkw[2 * QB:3 * QB]
            dv_ref[...] = (dva[...] + dvw[0:QB]).astype(BF16)
            dva[...] = dvb[...] + dvw[QB:2 * QB]
            dvb[...] = dvw[2 * QB:3 * QB]

        @pl.when(i >= nq)
        def _():
            dk_ref[...] = dka[...].astype(BF16)
            dka[...] = dkb[...]
            dkb[...] = jnp.zeros_like(dkb)
            dv_ref[...] = dva[...].astype(BF16)
            dva[...] = dvb[...]
            dvb[...] = jnp.zeros_like(dvb)

    def kv(col, d):
        return pl.BlockSpec((QB, 128), lambda p, i: (jnp.clip(i - 2 + d, 0, nq - 1), col + p))

    cur = lambda p, i: (jnp.minimum(i, nq - 1), p)
    done = lambda p, i: (jnp.maximum(i - 2, 0), p)
    in_specs = [pl.BlockSpec((QB, 128), cur)]
    in_specs += [kv(4, d) for d in range(3)] + [kv(8, d) for d in range(3)]
    in_specs += [pl.BlockSpec((QB, 128), cur), pl.BlockSpec((2, QB, KW), lambda p, i: (p, 0, 0))]
    o = jax.ShapeDtypeStruct((s, AW), BF16)
    return _call(body, name, (4, nq + 2), in_specs,
                 [pl.BlockSpec((QB, 128), cur), pl.BlockSpec((QB, 128), done), pl.BlockSpec((QB, 128), done),
                  pl.BlockSpec((2, QB, KW), lambda p, i: (p, 0, 0))],
                 [o, o, o, jax.ShapeDtypeStruct((HEADS, QB, KW), F32)],
                 [pltpu.VMEM((QB, 128), F32)] * 4, exchange=exchange)(
                     proj, proj, proj, proj, proj, proj, proj, datt, tab)


def _rel_bias_grad(dtab, name):
    wdt = 640

    def body(x_ref, o_ref):
        x = x_ref[...]
        xc = x[0:64, 0:wdt]
        for qc in range(1, QB // 64):
            xc = xc + pltpu.roll(x[qc * 64:(qc + 1) * 64, :], KW - qc * 64, 1)[:, 0:wdt]
        r = lax.broadcasted_iota(jnp.int32, (64, 64), 0)
        c = lax.broadcasted_iota(jnp.int32, (64, 64), 1)
        flip = (r + c == 63).astype(F32)
        y = jnp.dot(flip, xc, preferred_element_type=F32, precision=lax.Precision.HIGHEST)
        z = pltpu.roll(y, 0, 1, stride=1, stride_axis=0)
        t = jnp.broadcast_to(jnp.sum(z, axis=0, keepdims=True), (8, wdt))
        e = lax.broadcasted_iota(jnp.int32, (wdt, wdt), 0)
        rr = lax.broadcasted_iota(jnp.int32, (wdt, wdt), 1)
        onehot = (jnp.clip(BAND - 1 - e, -MAXREL, MAXREL) + MAXREL == rr).astype(F32)
        o_ref[...] = jnp.dot(t, onehot, preferred_element_type=F32, precision=lax.Precision.HIGHEST)

    return _call(body, name, (HEADS,), [pl.BlockSpec((None, QB, KW), lambda h: (h, 0, 0))],
                 pl.BlockSpec((None, 8, wdt), lambda h: (h, 0, 0)),
                 jax.ShapeDtypeStruct((HEADS, 8, wdt), F32))(dtab)


def _pool_bwd(dmixed, pooled, wg, scale, name):
    s = dmixed.shape[0]
    tb = 512
    nt = s // tb
    e = tb + 16

    def body(dm_ref, dmn_ref, pl_ref, wg_ref, sc_ref, du_ref, dwg_ref, dsc_ref):
        i = pl.program_id(0)
        dm = jnp.concatenate([dm_ref[...], jnp.where(i == nt - 1, 0.0, dmn_ref[...])], axis=0)
        t = i * tb + lax.broadcasted_iota(jnp.int32, (e, 1), 0)

        @pl.when(i == 0)
        def _():
            dwg_ref[...] = jnp.zeros_like(dwg_ref)
            dsc_ref[...] = jnp.zeros_like(dsc_ref)

        for g, w in enumerate(POOL_WINDOWS):
            sl = slice(g * PG, (g + 1) * PG)
            wgb = wg_ref[g].astype(BF16)
            pb = pl_ref[:, sl]
            dsc_ref[:, sl] += jnp.sum(dm[0:tb, sl] * _dot(pb, wgb), axis=0, keepdims=True)
            dpre = (dm[:, sl] * sc_ref[:, sl]).astype(BF16)
            dwg_ref[g] += _dot_tn(pb, dpre[0:tb])
            dpool = _dot_nt(dpre, wgb)
            a = dpool / jnp.minimum(t + 1, w).astype(F32)
            sh = 1
            while sh < w:
                a = a + pltpu.roll(a, e - sh, 0)
                sh *= 2
            du_ref[:, sl] = (a[0:tb] - dpool[0:tb]).astype(BF16)

    nb16 = s // 16
    return _call(body, name, (nt,),
                 [pl.BlockSpec((tb, PW), _row),
                  pl.BlockSpec((16, PW), lambda i: (jnp.minimum((i + 1) * (tb // 16), nb16 - 1), 0)),
                  pl.BlockSpec((tb, PW), _row), pl.BlockSpec((4, PG, PG), lambda i: (0, 0, 0)),
                  pl.BlockSpec((1, PW), _fixed2)],
                 [pl.BlockSpec((tb, PW), _row), pl.BlockSpec((4, PG, PG), lambda i: (0, 0, 0)),
                  pl.BlockSpec((1, PW), _fixed2)],
                 [jax.ShapeDtypeStruct((s, PW), BF16), jax.ShapeDtypeStruct((4, PG, PG), F32),
                  jax.ShapeDtypeStruct((1, PW), F32)])(dmixed, dmixed, pooled, wg, scale)


def _dproj_specs(t):
    return [pl.BlockSpec((t, 512), _row)] * 4 + [pl.BlockSpec((t, 2 * D), _row)]


def _dproj_segment(refs, n):
    return refs[n][...] if n < 4 else refs[4][:, (n - 4) * 512:(n - 3) * 512]


def _mm_dh(segs, win, x, g1, dx1, name):
    s = x.shape[0]
    tm = 512

    def body(dq_ref, dk_ref, dv_ref, du_ref, dg_ref, w_ref, x_ref, g1_ref, dx1_ref, dx_ref, dg1_ref):
        i = pl.program_id(0)

        @pl.when(i == 0)
        def _():
            dg1_ref[...] = jnp.zeros_like(dg1_ref)

        pieces = (dq_ref, dk_ref, dv_ref, du_ref, dg_ref)
        dh = _dot_nt(_dproj_segment(pieces, 0), w_ref[0])
        for n in range(1, NDEV):
            dh = dh + _dot_nt(_dproj_segment(pieces, n), w_ref[n])
        dn, dg1 = _rms_bwd(x_ref[...], g1_ref[...], dh)
        dx_ref[...] = dx1_ref[...] + dn
        dg1_ref[...] += dg1

    return _call(body, name, (s // tm,),
                 _dproj_specs(tm) + [_resident((NDEV, D, 512)), pl.BlockSpec((tm, D), _row),
                                     pl.BlockSpec((1, D), _fixed2), pl.BlockSpec((tm, D), _row)],
                 [pl.BlockSpec((tm, D), _row), pl.BlockSpec((1, D), _fixed2)],
                 [jax.ShapeDtypeStruct((s, D), F32), jax.ShapeDtypeStruct((1, D), F32)])(*segs, win, x, g1, dx1)


def _dw_in(h, segs, name):
    s = h.shape[0]
    tk = 512
    nk = s // tk

    def body(h_ref, dq_ref, dk_ref, dv_ref, du_ref, dg_ref, o_ref, acc):
        k = pl.program_id(0)

        @pl.when(k == 0)
        def _():
            acc[...] = jnp.zeros_like(acc)

        ht = h_ref[...].T
        pieces = (dq_ref, dk_ref, dv_ref, du_ref, dg_ref)
        for n in range(NDEV):
            acc[n] += _dot(ht, _dproj_segment(pieces, n))

        @pl.when(k == nk - 1)
        def _():
            o_ref[...] = acc[...].astype(BF16)

    return _call(body, name, (nk,), [pl.BlockSpec((tk, D), _row)] + _dproj_specs(tk),
                 pl.BlockSpec((NDEV, D, 512), lambda k: (0, 0, 0)),
                 jax.ShapeDtypeStruct((NDEV, D, 512), BF16), [pltpu.VMEM((NDEV, D, 512), F32)])(h, *segs)


def _slot(px, py, pc):
    return 4 * px + 2 * py + pc


def _gather_exchange(shards, deliver):
    n = len(shards)

    def phases(ins, outs, sems):
        send_sems, recv_sems, local_sems = sems
        x, y, c = lax.axis_index("x"), lax.axis_index("y"), lax.axis_index("c")
        me, sibling = (x, y, c), (x, y, 1 - c)
        chips = [(1 - x, y), (x, 1 - y), (1 - x, 1 - y)]

        def copy(w, k, block, to, src=None):
            dst = outs[w].at[_slot(*block)]
            return pltpu.make_async_remote_copy(
                src_ref=dst if src is None else src, dst_ref=dst, send_sem=send_sems.at[7 * w + k],
                recv_sem=recv_sems.at[7 * w + k], device_id=to, device_id_type=MESH)

        def mine(w):
            return pltpu.make_async_copy(ins[w], outs[w].at[_slot(*me)], local_sems.at[w])

        def first(w):
            return [copy(w, 0, me, sibling, src=ins[w])] + [
                copy(w, 1 + j, me, (*chip, c), src=ins[w]) for j, chip in enumerate(chips)]

        def passed(w):
            return [copy(w, 4 + j, (*chip, c), sibling) for j, chip in enumerate(chips)]

        def send():
            for w in range(n):
                mine(w).start()
                for cp in first(w):
                    cp.start()

        def forward():
            for w in range(n):
                for j, chip in enumerate(chips):
                    copy(w, 1 + j, (*chip, c), me).wait_recv()
                    passed(w)[j].start()

        def finish():
            for w in range(n):
                copy(w, 0, sibling, me).wait_recv()
                for j, chip in enumerate(chips):
                    copy(w, 4 + j, (*chip, 1 - c), me).wait_recv()
            for w in range(n):
                for cp in first(w) + passed(w):
                    cp.wait_send()
                mine(w).wait()

        return [send, forward, finish]

    return _Exchange(shards, [jax.ShapeDtypeStruct((NDEV,) + a.shape, a.dtype) for a in shards],
                     [pltpu.SemaphoreType.DMA((7 * n,)), pltpu.SemaphoreType.DMA((7 * n,)),
                      pltpu.SemaphoreType.DMA((n,))], phases, deliver)


def _scatter_exchange(parts, deliver):
    n = len(parts)

    def phases(ins, outs, sems):
        send_sems, recv_sems, local_sems = sems
        x, y, c = lax.axis_index("x"), lax.axis_index("y"), lax.axis_index("c")
        me = _slot(x, y, c)
        peers = [((1 - x) if r & 4 else x, (1 - y) if r & 2 else y, (1 - c) if r & 1 else c) for r in range(1, NDEV)]

        def mine(w):
            return pltpu.make_async_copy(ins[w].at[me], outs[w].at[me], local_sems.at[w])

        def copy(w, r, block_here, block_there):
            return pltpu.make_async_remote_copy(
                src_ref=ins[w].at[block_here], dst_ref=outs[w].at[block_there], send_sem=send_sems.at[7 * w + r],
                recv_sem=recv_sems.at[7 * w + r], device_id=peers[r], device_id_type=MESH)

        def send():
            for w in range(n):
                mine(w).start()
                for r, peer in enumerate(peers):
                    copy(w, r, _slot(*peer), me).start()

        def finish():
            for w in range(n):
                for r, peer in enumerate(peers):
                    copy(w, r, me, _slot(*peer)).wait_recv()
            for w in range(n):
                for r, peer in enumerate(peers):
                    copy(w, r, _slot(*peer), me).wait_send()
                mine(w).wait()

        return [send, finish]

    return _Exchange(parts, [jax.ShapeDtypeStruct(a.shape, a.dtype) for a in parts],
                     [pltpu.SemaphoreType.DMA((7 * n,)), pltpu.SemaphoreType.DMA((7 * n,)),
                      pltpu.SemaphoreType.DMA((n,))], phases, deliver)


def _exchange_alone(exchange, name):
    n_in = len(exchange.operands)
    n_out = len(exchange.out_shape)

    def body(*refs):
        for phase in exchange.phases(refs[:n_in], refs[n_in:n_in + n_out], refs[n_in + n_out:]):
            phase()

    exchange.deliver(pl.pallas_call(
        body, name=name, in_specs=[HBM] * n_in, out_specs=[HBM] * n_out, out_shape=exchange.out_shape,
        scratch_shapes=exchange.scratch)(*exchange.operands))


def _adamw(w, g, m, v):
    m = ADAM_B1 * m + (1.0 - ADAM_B1) * g
    v = ADAM_B2 * v + (1.0 - ADAM_B2) * (g * g)
    m_hat = m / (1.0 - ADAM_B1 ** ADAM_STEP)
    v_hat = v / (1.0 - ADAM_B2 ** ADAM_STEP)
    delta = -ADAM_LR * (m_hat / (jnp.sqrt(v_hat) + ADAM_EPS) + ADAM_WD * w)
    return delta, m, v


def _adamw_sharded(recvs, w, m, v, tr, name):
    _, r, cdim = w.shape
    nr = r // tr

    def body(r0_ref, r1_ref, w_ref, m_ref, v_ref, g_ref, d_ref, nm_ref, nv_ref):
        l = pl.program_id(0)

        def total(ref):
            acc = ref[0].astype(F32)
            for k in range(1, NDEV):
                acc = acc + ref[k].astype(F32)
            return acc

        g = jnp.where(l == 0, total(r0_ref), total(r1_ref))
        d, nm, nv = _adamw(w_ref[...], g, m_ref[...], v_ref[...])
        g_ref[...] = g
        d_ref[...] = d
        nm_ref[...] = nm
        nv_ref[...] = nv

    mine = pl.BlockSpec((None, tr, cdim), lambda l, i: (l, i, 0))
    out = jax.ShapeDtypeStruct(w.shape, F32)
    return _call(body, name, (DEPTH, nr),
                 [pl.BlockSpec((NDEV, tr, cdim), lambda l, i: (0, jnp.where(l == 0, i, nr - 1), 0)),
                  pl.BlockSpec((NDEV, tr, cdim), lambda l, i: (0, jnp.where(l == 1, i, 0), 0)),
                  mine, mine, mine],
                 [mine] * 4, [out] * 4)(recvs[0], recvs[1], w, m, v)


def _adamw_small(parts, w, m, v, name):
    r = w.shape[0]

    def body(p_ref, w_ref, m_ref, v_ref, g_ref, d_ref, nm_ref, nv_ref):
        g = p_ref[0]
        for k in range(1, NDEV):
            g = g + p_ref[k]
        d, nm, nv = _adamw(w_ref[...], g, m_ref[...], v_ref[...])
        g_ref[...] = g
        d_ref[...] = d
        nm_ref[...] = nm
        nv_ref[...] = nv

    whole = pl.BlockSpec((r, 1024), _fixed2)
    out = jax.ShapeDtypeStruct((r, 1024), F32)
    return _call(body, name, (1,), [pl.BlockSpec((NDEV, r, 1024), lambda i: (0, 0, 0)), whole, whole, whole],
                 [whole] * 4, [out] * 4)(parts, w, m, v)


SMALL = (("norm_mix_pre", (D,)), ("b_gate", (2 * D,)), ("rel_bias", (HEADS, NREL)), ("w_pool_group", (4, PG, PG)),
         ("pool_scale", (PW,)), ("norm_mix_post", (D,)), ("norm_ffn_pre", (D,)), ("conv_b", (NFF * 2 * FS,)),
         ("norm_ffn_post", (D,)))
SHARDED = (("w_in", 256), ("w_attn_out", 512), ("w_pool_out", 512), ("w_o", 128), ("w_up", 256), ("conv_w", 3),
           ("w_down", 176))
SMALL_ROWS = 168


def _pack_small(tree):
    flat = jnp.concatenate([tree[name].reshape(-1) for name, _ in SMALL])
    return jnp.pad(flat, (0, SMALL_ROWS * 1024 - flat.shape[0])).reshape(SMALL_ROWS, 1024)


def _unpack_small(packed):
    flat = packed.reshape(-1)
    out, at = {}, 0
    for name, shape in SMALL:
        size = DEPTH * int(np.prod(shape))
        out[name] = flat[at:at + size].reshape((DEPTH,) + shape)
        at += size
    return out


def _layer_fwd(l, x, h, p, gath, target, gnext, carry):
    n = f"l{l}"
    g = lambda name: p[name][l].reshape(1, -1)
    res = {"x": x, "h": h}
    res["tab"] = _bias_table(p["rel_bias"][l], f"bias_table_{n}")
    proj = _mm_in(h, gath["w_in"], f"mm_in_{n}", carry.get("mm_in"))
    att = _attn_fwd(proj, res["tab"], f"attn_fwd_{n}", carry.get("attn_fwd"))
    pooled, mixed = _pool_fwd(proj, p["w_pool_group"][l], g("pool_scale"), f"pool_fwd_{n}")
    bgate = g("b_gate")
    z, ya, yb = _branch_fwd(att, mixed, gath["w_attn_out"], gath["w_pool_out"], proj, bgate, f"branch_fwd_{n}")
    mix, x1, h2 = _mm_o_fwd(z, gath["w_o"], x, g("norm_mix_post"), g("norm_ffn_pre"), f"mm_o_fwd_{n}")
    hu = _mm_up(h2, gath["w_up"], f"mm_up_{n}", carry.get("mm_up")).reshape(2, NFF, -1, FS)
    cw = gath["conv_w"].reshape(2, NFF, 3, FS)
    cb = p["conv_b"][l].reshape(2, NFF, 1, FS)
    a = _conv_gate_fwd(hu, cw, cb, f"conv_gate_fwd_{n}", carry.get("conv_gate_fwd"))
    res.update(proj=proj, att=att, pooled=pooled, mixed=mixed, z=z, ya=ya, yb=yb, mix=mix, x1=x1, h2=h2, hu=hu,
               a=a, cw=cw, cb=cb, bgate=bgate)
    if target is None:
        f, x2, hn = _mm_down_fwd(a, gath["w_down"], x1, g("norm_ffn_post"), gnext, f"mm_down_fwd_{n}")
        res["f"] = f
        return res, x2, hn
    f, dy, loss = _mm_down_loss(a, gath["w_down"], x1, g("norm_ffn_post"), target, f"mm_down_loss_{n}")
    res["f"] = f
    return res, dy, loss


def _layer_bwd(l, dx2, p, gath, res, carry):
    n = f"l{l}"
    g = lambda name: p[name][l].reshape(1, -1)
    big, small = {}, {}
    taken = lambda call: carry[call](big) if call in carry else None
    df, small["norm_ffn_post"] = _norm_bwd(res["f"], g("norm_ffn_post"), dx2, f"norm_bwd_{n}")
    dhu, dcw, dcb = _conv_gate_bwd(res["hu"], df, gath["w_down"], res["cw"], res["cb"], f"conv_gate_bwd_{n}",
                                   taken("conv_gate_bwd"))
    big["conv_w"] = dcw.reshape(NDEV, 3, FS)
    small["conv_b"] = dcb
    big["w_down"] = _dw_down(res["a"], df, f"dw_down_{n}")
    dx1, dmix, small["norm_ffn_pre"], small["norm_mix_post"] = _mm_dh2(
        dhu, gath["w_up"], res["x1"], g("norm_ffn_pre"), dx2, res["mix"], g("norm_mix_post"), f"mm_dh2_{n}")
    big["w_up"] = _dw_up(res["h2"], dhu, f"dw_up_{n}")
    dya, dyb, dgates, small["b_gate"] = _mm_dz(dmix, gath["w_o"], res["proj"], res["bgate"], res["ya"], res["yb"],
                                                f"mm_dz_{n}")
    big["w_o"] = _dw_o(res["z"], dmix, f"dw_o_{n}")
    datt, dmixed = _branch_bwd(dya, dyb, gath["w_attn_out"], gath["w_pool_out"], f"branch_bwd_{n}")
    big["w_attn_out"], big["w_pool_out"] = _dw_branch(res["att"], res["mixed"], dya, dyb, f"dw_branch_{n}")
    dq, dk, dv, dtab = _attn_bwd(res["proj"], datt, res["tab"], f"attn_bwd_{n}", taken("attn_bwd"))
    small["rel_bias"] = _rel_bias_grad(dtab, f"rel_bias_grad_{n}")[:, 0, :NREL]
    du, small["w_pool_group"], small["pool_scale"] = _pool_bwd(
        dmixed, res["pooled"], p["w_pool_group"][l], g("pool_scale"), f"pool_bwd_{n}")
    segs = (dq, dk, dv, du, dgates)
    dx, small["norm_mix_pre"] = _mm_dh(segs, gath["w_in"], res["x"], g("norm_mix_pre"), dx1, f"mm_dh_{n}")
    big["w_in"] = _dw_in(res["h"], segs, f"dw_in_{n}")
    return dx, big, small


def _gather_weights(gath, l, p, names):
    shards = [p[k][l] if k == "conv_w" else p[k][l].astype(BF16) for k in names]

    def deliver(results):
        for k, a in zip(names, results):
            if k in ("w_attn_out", "w_pool_out"):
                a = jnp.transpose(a, (1, 0, 2)).reshape(AW, D)
            elif k == "w_o":
                a = a.reshape(D, D)
            elif k == "w_down":
                a = a.reshape(NFF * FS, D)
            gath[k] = a

    return _gather_exchange(shards, deliver)


def _scatter_grads(recv, big, names):
    return _scatter_exchange([big[k] for k in names], lambda results: recv.update(zip(names, results)))


def kernel(x, norm_mix_pre, w_in, b_gate, rel_bias, w_attn_out, w_pool_group, pool_scale, w_pool_out, w_o, norm_mix_post, norm_ffn_pre, w_up, conv_w, conv_b, w_down, norm_ffn_post, loss_target, m_norm_mix_pre, m_w_in, m_b_gate, m_rel_bias, m_w_attn_out, m_w_pool_group, m_pool_scale, m_w_pool_out, m_w_o, m_norm_mix_post, m_norm_ffn_pre, m_w_up, m_conv_w, m_conv_b, m_w_down, m_norm_ffn_post, v_norm_mix_pre, v_w_in, v_b_gate, v_rel_bias, v_w_attn_out, v_w_pool_group, v_pool_scale, v_w_pool_out, v_w_o, v_norm_mix_post, v_norm_ffn_pre, v_w_up, v_conv_w, v_conv_b, v_w_down, v_norm_ffn_post):
    names = ("norm_mix_pre", "w_in", "b_gate", "rel_bias", "w_attn_out", "w_pool_group", "pool_scale", "w_pool_out",
             "w_o", "norm_mix_post", "norm_ffn_pre", "w_up", "conv_w", "conv_b", "w_down", "norm_ffn_post")
    p = dict(zip(names, (norm_mix_pre, w_in, b_gate, rel_bias, w_attn_out, w_pool_group, pool_scale, w_pool_out, w_o,
                         norm_mix_post, norm_ffn_pre, w_up, conv_w, conv_b, w_down, norm_ffn_post)))
    mom = dict(zip(names, (m_norm_mix_pre, m_w_in, m_b_gate, m_rel_bias, m_w_attn_out, m_w_pool_group, m_pool_scale,
                           m_w_pool_out, m_w_o, m_norm_mix_post, m_norm_ffn_pre, m_w_up, m_conv_w, m_conv_b, m_w_down,
                           m_norm_ffn_post)))
    var = dict(zip(names, (v_norm_mix_pre, v_w_in, v_b_gate, v_rel_bias, v_w_attn_out, v_w_pool_group, v_pool_scale,
                           v_w_pool_out, v_w_o, v_norm_mix_post, v_norm_ffn_pre, v_w_up, v_conv_w, v_conv_b, v_w_down,
                           v_norm_ffn_post)))
    s = x.shape[1]
    xs = x.reshape(s, D)
    target = loss_target.reshape(s, D)

    gath = [{}, {}]
    rest = ("w_attn_out", "w_pool_out", "w_o", "w_down", "conv_w")
    _exchange_alone(_gather_weights(gath[0], 0, p, ("w_in",)), "all_gather_l0_w_in")
    h0 = _rmsnorm_call(xs, p["norm_mix_pre"][0].reshape(1, D), "rmsnorm_l0")
    res0, x2, h1 = _layer_fwd(0, xs, h0, p, gath[0], None, p["norm_mix_pre"][1].reshape(1, D), {
        "mm_in": _gather_weights(gath[0], 0, p, rest),
        "attn_fwd": _gather_weights(gath[0], 0, p, ("w_up",)),
        "mm_up": _gather_weights(gath[1], 1, p, ("w_in",) + rest),
        "conv_gate_fwd": _gather_weights(gath[1], 1, p, ("w_up",))})
    res1, dy, loss_part = _layer_fwd(1, x2, h1, p, gath[1], target, None, {})

    order = [k for k, _ in SHARDED]
    recv = [{}, {}]
    dx2, big1, small1 = _layer_bwd(1, dy, p, gath[1], res1, {})
    grad_x, big0, small0 = _layer_bwd(0, dx2, p, gath[0], res0, {
        "conv_gate_bwd": lambda big: _scatter_grads(recv[1], big1, order),
        "attn_bwd": lambda big: _scatter_grads(recv[0], big, [k for k in order if k != "w_in"])})
    _exchange_alone(_scatter_grads(recv[0], big0, ["w_in"]), "all_to_all_l0_w_in")

    loss = lax.psum(loss_part[0, 0], ("x", "y", "c"))

    out = {}
    for k, tr in SHARDED:
        out[k] = _adamw_sharded((recv[0][k], recv[1][k]), p[k], mom[k], var[k], tr, f"adamw_{k}")

    small = {k: jnp.stack([small0[k].reshape(shape), small1[k].reshape(shape)]) for k, shape in SMALL}
    parts = []
    _exchange_alone(_gather_exchange([_pack_small(small)], parts.extend), "all_gather_small")
    packed = _adamw_small(parts[0], _pack_small(p), _pack_small(mom), _pack_small(var), "adamw_small")
    unpacked = [_unpack_small(a) for a in packed]
    for k, _ in SMALL:
        out[k] = tuple(u[k] for u in unpacked)

    return (loss, grad_x.reshape(x.shape), *[out[k][0] for k in names], *[out[k][1] for k in names],
            *[out[k][2] for k in names], *[out[k][3] for k in names])
```

```python
import numpy as np
import jax
import jax.numpy as jnp
from jax import lax
from jax.experimental import pallas as pl
from jax.experimental.pallas import tpu as pltpu

F32, BF16 = jnp.float32, jnp.bfloat16

D = 1024
AW = 512
PW = 512
PG = 128
INW = 4096
FS = 704
NFF = 4
NDEV = 8
DEPTH = 2
HEADS = 8
NREL = 513
MAXREL = 256
POOL_WINDOWS = (2, 4, 8, 16)
EPS = 1e-6
SCALE = 0.125
NEG = -1e30
QB = 256
KW = 3 * QB
BAND = 576
ADAM_LR, ADAM_B1, ADAM_B2, ADAM_EPS, ADAM_WD, ADAM_STEP = 0.001, 0.9, 0.999, 1e-08, 0.01, 10
VMEM_LIMIT_V7X = 56 * 1024 * 1024
MESH = pl.DeviceIdType.MESH
GELU_C = 0.7978845608028654
GELU_A = 0.044715


HBM = pl.BlockSpec(memory_space=pltpu.HBM)


class _Exchange:
    def __init__(self, operands, out_shape, scratch, phases, deliver):
        self.operands, self.out_shape, self.scratch = list(operands), list(out_shape), list(scratch)
        self.phases, self.deliver = phases, deliver


def _call(body, name, grid, in_specs, out_specs, out_shape, scratch=(), exchange=None):
    params = pltpu.CompilerParams(vmem_limit_bytes=VMEM_LIMIT_V7X)
    if exchange is None:
        return pl.pallas_call(body, name=name, grid=grid, in_specs=in_specs, out_specs=out_specs, out_shape=out_shape,
                              scratch_shapes=list(scratch), compiler_params=params)
    single = not isinstance(out_shape, (list, tuple))
    outs, ospecs = ([out_shape], [out_specs]) if single else (list(out_shape), list(out_specs))
    n_in, n_out, n_scr = len(in_specs), len(outs), len(scratch)
    ne_in, ne_out = len(exchange.operands), len(exchange.out_shape)
    nsteps = int(np.prod(grid))

    def carried(*refs):
        cut = np.cumsum([0, n_in, ne_in, n_out, ne_out, n_scr])
        base_in, ex_in, base_out, ex_out, base_scr = (refs[cut[k]:cut[k + 1]] for k in range(5))
        step = pl.program_id(0)
        for axis in range(1, len(grid)):
            step = step * grid[axis] + pl.program_id(axis)
        phases = exchange.phases(ex_in, ex_out, refs[cut[5]:])
        pl.when(step == 0)(phases[0])
        body(*base_in, *base_out, *base_scr)
        at = {2: [nsteps - 1], 3: [(7 * nsteps) // 10, nsteps - 1]}[len(phases)]
        for phase, when in zip(phases[1:], at):
            pl.when(step == when)(phase)

    call = pl.pallas_call(
        carried, name=name, grid=grid, in_specs=list(in_specs) + [HBM] * ne_in, out_specs=ospecs + [HBM] * ne_out,
        out_shape=outs + exchange.out_shape, scratch_shapes=list(scratch) + exchange.scratch, compiler_params=params)

    def run(*args):
        res = call(*args, *exchange.operands)
        exchange.deliver(res[n_out:])
        return res[0] if single else res[:n_out]

    return run


def _dot(a, b):
    return jnp.dot(a, b, preferred_element_type=F32)


def _dot_nt(a, b):
    return lax.dot_general(a, b, (((1,), (1,)), ((), ())), preferred_element_type=F32)


def _dot_tn(a, b):
    return lax.dot_general(a, b, (((0,), (0,)), ((), ())), preferred_element_type=F32)


def _rms(x, g):
    r = lax.rsqrt(jnp.mean(x * x, axis=-1, keepdims=True) + EPS)
    return x * r * g


def _rms_bwd(x, g, dy):
    r = lax.rsqrt(jnp.mean(x * x, axis=-1, keepdims=True) + EPS)
    xh = x * r
    dg = jnp.sum(dy * xh, axis=0, keepdims=True)
    dxh = dy * g
    dx = r * (dxh - xh * jnp.mean(dxh * xh, axis=-1, keepdims=True))
    return dx, dg


def _sigmoid(x):
    return 1.0 / (1.0 + jnp.exp(-x))


def _gelu_parts(x):
    x2 = x * x
    th = jnp.tanh(x * (GELU_C + (GELU_C * GELU_A) * x2))
    s = 0.5 * th + 0.5
    dg = s * (1.0 + x * (1.0 - s) * (2.0 * GELU_C + (6.0 * GELU_C * GELU_A) * x2))
    return x * s, dg


def _row(i):
    return (i, 0)


def _fixed2(*_):
    return (0, 0)


def _rmsnorm_call(x, g, name, exchange=None):
    s = x.shape[0]
    tm = 512

    def body(x_ref, g_ref, o_ref):
        o_ref[...] = _rms(x_ref[...], g_ref[...]).astype(BF16)

    return _call(body, name, (s // tm,),
                 [pl.BlockSpec((tm, D), _row), pl.BlockSpec((1, D), _fixed2)],
                 pl.BlockSpec((tm, D), _row), jax.ShapeDtypeStruct((s, D), BF16), exchange=exchange)(x, g)


def _resident(shape):
    return pl.BlockSpec(shape, lambda *_: (0,) * len(shape), pipeline_mode=pl.Buffered(1))


def _mm_in(h, win, name, exchange=None):
    s = h.shape[0]
    tm = 512

    def body(h_ref, w_ref, o_ref):
        hv = h_ref[...]
        for j in range(NDEV):
            o_ref[:, j * 512:(j + 1) * 512] = _dot(hv, w_ref[j])

    return _call(body, name, (s // tm,),
                 [pl.BlockSpec((tm, D), _row), _resident((NDEV, D, 512))],
                 pl.BlockSpec((tm, INW), _row), jax.ShapeDtypeStruct((s, INW), F32), exchange=exchange)(h, win)


def _bias_table(rel_bias, name):
    wdt = 1024
    rel = jnp.pad(rel_bias, ((0, 0), (0, 640 - NREL))).reshape(HEADS, 1, 640)

    def body(r_ref, o_ref):
        rr = lax.broadcasted_iota(jnp.int32, (640, wdt), 0)
        m = lax.broadcasted_iota(jnp.int32, (640, wdt), 1)
        d = jnp.where(m < KW, m, m - wdt)
        onehot = (jnp.clip(512 - d, -MAXREL, MAXREL) + MAXREL == rr).astype(F32)
        row = jnp.dot(jnp.broadcast_to(r_ref[...], (8, 640)), onehot, preferred_element_type=F32,
                      precision=lax.Precision.HIGHEST)[0:1]
        t = pltpu.roll(jnp.broadcast_to(row, (QB, wdt)), 0, 1, stride=1, stride_axis=0)[:, 0:KW]
        qc = lax.broadcasted_iota(jnp.int32, (QB, KW), 0) // 64
        kc = lax.broadcasted_iota(jnp.int32, (QB, KW), 1) // 64
        o_ref[...] = jnp.where((kc >= qc) & (kc <= qc + 8), t, NEG)

    return _call(body, name, (HEADS,), [pl.BlockSpec((None, 1, 640), lambda h: (h, 0, 0))],
                 pl.BlockSpec((None, QB, KW), lambda h: (h, 0, 0)),
                 jax.ShapeDtypeStruct((HEADS, QB, KW), F32))(rel)


def _attn_probs(q_ref, k_refs, b_ref, i):
    lane = lax.broadcasted_iota(jnp.int32, (QB, 128), 1)
    q = q_ref[...] * SCALE
    qs = [jnp.where(lane < 64, q, 0.0).astype(BF16), jnp.where(lane >= 64, q, 0.0).astype(BF16)]
    k = jnp.concatenate([r[...] for r in k_refs], axis=0).astype(BF16)
    colb = lax.broadcasted_iota(jnp.int32, (QB, KW), 1) // QB
    valid = colb + i >= 2
    ps = []
    for hh in (0, 1):
        sc = _dot_nt(qs[hh], k) + b_ref[hh]
        sc = jnp.where(valid, sc, NEG)
        m = jnp.max(sc, axis=1, keepdims=True)
        p = jnp.exp(sc - m)
        ps.append(p / jnp.sum(p, axis=1, keepdims=True))
    return qs, ps, k, lane


def _attn_fwd(proj, tab, name, exchange=None):
    s = proj.shape[0]
    nq = s // QB

    def body(q_ref, k0, k1, k2, v0, v1, v2, b_ref, o_ref):
        i = pl.program_id(1)
        _, ps, _, lane = _attn_probs(q_ref, (k0, k1, k2), b_ref, i)
        v = jnp.concatenate([v0[...], v1[...], v2[...]], axis=0).astype(BF16)
        o = [_dot(ps[hh].astype(BF16), v) for hh in (0, 1)]
        o_ref[...] = jnp.where(lane < 64, o[0], o[1]).astype(BF16)

    def kv(col, d):
        return pl.BlockSpec((QB, 128), lambda p, i: (jnp.maximum(i - 2 + d, 0), col + p))

    in_specs = [pl.BlockSpec((QB, 128), lambda p, i: (i, p))]
    in_specs += [kv(4, d) for d in range(3)] + [kv(8, d) for d in range(3)]
    in_specs += [pl.BlockSpec((2, QB, KW), lambda p, i: (p, 0, 0))]
    return _call(body, name, (4, nq), in_specs, pl.BlockSpec((QB, 128), lambda p, i: (i, p)),
                 jax.ShapeDtypeStruct((s, AW), BF16), exchange=exchange)(proj, proj, proj, proj, proj, proj, proj, tab)


def _pool_fwd(proj, wg, scale, name):
    s = proj.shape[0]
    tb = 512
    e = tb + 16

    def body(u_ref, halo_ref, wg_ref, sc_ref, pooled_ref, mixed_ref):
        i = pl.program_id(0)
        cur = u_ref[...]
        prev = jnp.where(i > 0, halo_ref[...], 0.0)
        xs = jnp.concatenate([prev, cur], axis=0)
        t = i * tb + lax.broadcasted_iota(jnp.int32, (tb, 1), 0)
        for g, w in enumerate(POOL_WINDOWS):
            sl = slice(g * PG, (g + 1) * PG)
            a = xs[:, sl]
            sh = 1
            while sh < w:
                a = a + pltpu.roll(a, sh, 0)
                sh *= 2
            cnt = jnp.minimum(t + 1, w).astype(F32)
            pooled = (a[16:] / cnt - cur[:, sl]).astype(BF16)
            pooled_ref[:, sl] = pooled
            mixed_ref[:, sl] = (_dot(pooled, wg_ref[g].astype(BF16)) * sc_ref[:, sl]).astype(BF16)

    assert e % 8 == 0
    return _call(body, name, (s // tb,),
                 [pl.BlockSpec((tb, PW), lambda i: (i, 3)),
                  pl.BlockSpec((16, PW), lambda i: (jnp.maximum(i * (tb // 16) - 1, 0), 3)),
                  pl.BlockSpec((4, PG, PG), lambda i: (0, 0, 0)), pl.BlockSpec((1, PW), _fixed2)],
                 [pl.BlockSpec((tb, PW), _row), pl.BlockSpec((tb, PW), _row)],
                 [jax.ShapeDtypeStruct((s, PW), BF16), jax.ShapeDtypeStruct((s, PW), BF16)])(proj, proj, wg, scale)


def _branch_fwd(att, mixed, wao, wpo, proj, bgate, name):
    s = att.shape[0]
    tm = 512

    def body(att_ref, mx_ref, wao_ref, wpo_ref, ga_ref, gb_ref, ba_ref, bb_ref, z_ref, ya_ref, yb_ref):
        ya = _dot(att_ref[...], wao_ref[...])
        yb = _dot(mx_ref[...], wpo_ref[...])
        ga = _sigmoid(ga_ref[...] + ba_ref[...])
        gb = _sigmoid(gb_ref[...] + bb_ref[...])
        ya_ref[...] = ya
        yb_ref[...] = yb
        z_ref[...] = (ga * ya + gb * yb).astype(BF16)

    return _call(body, name, (s // tm,),
                 [pl.BlockSpec((tm, AW), _row), pl.BlockSpec((tm, PW), _row),
                  pl.BlockSpec((AW, D), _fixed2), pl.BlockSpec((PW, D), _fixed2),
                  pl.BlockSpec((tm, D), lambda i: (i, 2)), pl.BlockSpec((tm, D), lambda i: (i, 3)),
                  pl.BlockSpec((1, D), lambda i: (0, 0)), pl.BlockSpec((1, D), lambda i: (0, 1))],
                 [pl.BlockSpec((tm, D), _row)] * 3,
                 [jax.ShapeDtypeStruct((s, D), BF16), jax.ShapeDtypeStruct((s, D), F32),
                  jax.ShapeDtypeStruct((s, D), F32)])(att, mixed, wao, wpo, proj, proj, bgate, bgate)


def _mm_o_fwd(z, wo, x, g2, g3, name):
    s = z.shape[0]
    tm = 512

    def body(z_ref, w_ref, x_ref, g2_ref, g3_ref, mix_ref, x1_ref, h2_ref):
        mix = _dot(z_ref[...], w_ref[...])
        x1 = x_ref[...] + _rms(mix, g2_ref[...])
        mix_ref[...] = mix
        x1_ref[...] = x1
        h2_ref[...] = _rms(x1, g3_ref[...]).astype(BF16)

    return _call(body, name, (s // tm,),
                 [pl.BlockSpec((tm, D), _row), pl.BlockSpec((D, D), _fixed2), pl.BlockSpec((tm, D), _row),
                  pl.BlockSpec((1, D), _fixed2), pl.BlockSpec((1, D), _fixed2)],
                 [pl.BlockSpec((tm, D), _row)] * 3,
                 [jax.ShapeDtypeStruct((s, D), F32), jax.ShapeDtypeStruct((s, D), F32),
                  jax.ShapeDtypeStruct((s, D), BF16)])(z, wo, x, g2, g3)


def _mm_up(h2, wup, name, exchange=None):
    s = h2.shape[0]
    tm = 512

    def body(h_ref, w_ref, o_ref):
        hv = h_ref[...]
        for j in range(NDEV):
            o_ref[j] = _dot(hv, w_ref[j])

    return _call(body, name, (s // tm,),
                 [pl.BlockSpec((tm, D), _row), _resident((NDEV, D, FS))],
                 pl.BlockSpec((NDEV, tm, FS), lambda i: (0, i, 0)),
                 jax.ShapeDtypeStruct((NDEV, s, FS), F32), exchange=exchange)(h2, wup)


LANE_COLUMNS = [(c0, min(128, FS - c0)) for c0 in range(0, FS, 128)]


def _conv_gate_fwd(hu, cw, cb, name, exchange=None):
    s = hu.shape[2]
    tb = 512

    def body(hu_ref, halo_ref, cw_ref, cb_ref, a_ref):
        first = pl.program_id(1) == 0
        for c0, w in LANE_COLUMNS:
            lanes = pl.ds(c0, w)
            cwb = [[jnp.broadcast_to(cw_ref[sd, t:t + 1, lanes], (8, w)) for t in range(3)] for sd in (0, 1)]
            cbb = [jnp.broadcast_to(cb_ref[sd, :, lanes], (8, w)) for sd in (0, 1)]
            row = lax.broadcasted_iota(jnp.int32, (8, w), 0)
            rolled = []
            for sd in (0, 1):
                halo = jnp.where(first, 0.0, halo_ref[sd, :, lanes])
                rolled += [pltpu.roll(halo, 1, 0), pltpu.roll(halo, 2, 0)]
            for k in range(tb // 16):
                outs = []
                for r in (16 * k, 16 * k + 8):
                    hc, keep = [], []
                    for sd in (0, 1):
                        cur = hu_ref[sd, pl.ds(r, 8), lanes]
                        r1, r2 = pltpu.roll(cur, 1, 0), pltpu.roll(cur, 2, 0)
                        hc.append(cbb[sd] + cwb[sd][2] * cur + cwb[sd][0] * jnp.where(row >= 2, r2, rolled[2 * sd + 1])
                                  + cwb[sd][1] * jnp.where(row >= 1, r1, rolled[2 * sd]))
                        keep += [r1, r2]
                    rolled = keep
                    outs.append(_gelu_parts(hc[1])[0] * hc[0])
                a_ref[pl.ds(16 * k, 16), lanes] = jnp.concatenate(outs, axis=0).astype(BF16)

    return _call(body, name, (NFF, s // tb),
                 [pl.BlockSpec((2, None, tb, FS), lambda j, i: (0, j, i, 0)),
                  pl.BlockSpec((2, None, 8, FS), lambda j, i: (0, j, jnp.maximum(i * (tb // 8) - 1, 0), 0)),
                  pl.BlockSpec((2, None, 3, FS), lambda j, i: (0, j, 0, 0)),
                  pl.BlockSpec((2, None, 1, FS), lambda j, i: (0, j, 0, 0))],
                 pl.BlockSpec((None, tb, FS), lambda j, i: (j, i, 0)),
                 jax.ShapeDtypeStruct((NFF, s, FS), BF16), exchange=exchange)(hu, hu, cw, cb)


def _mm_down_fwd(a, wd, x1, g4, gnext, name):
    s = a.shape[1]
    tm = 512

    def body(a_ref, w_ref, x1_ref, g4_ref, gn_ref, f_ref, x2_ref, hn_ref):
        f = _dot(a_ref[0], w_ref[0:FS, :])
        for j in range(1, NFF):
            f = f + _dot(a_ref[j], w_ref[j * FS:(j + 1) * FS, :])
        x2 = x1_ref[...] + _rms(f, g4_ref[...])
        f_ref[...] = f
        x2_ref[...] = x2
        hn_ref[...] = _rms(x2, gn_ref[...]).astype(BF16)

    return _call(body, name, (s // tm,),
                 [pl.BlockSpec((NFF, tm, FS), lambda i: (0, i, 0)), pl.BlockSpec((NFF * FS, D), _fixed2),
                  pl.BlockSpec((tm, D), _row), pl.BlockSpec((1, D), _fixed2), pl.BlockSpec((1, D), _fixed2)],
                 [pl.BlockSpec((tm, D), _row)] * 3,
                 [jax.ShapeDtypeStruct((s, D), F32), jax.ShapeDtypeStruct((s, D), F32),
                  jax.ShapeDtypeStruct((s, D), BF16)])(a, wd, x1, g4, gnext)


def _mm_down_loss(a, wd, x1, g4, target, name):
    s = a.shape[1]
    tm = 512

    def body(a_ref, w_ref, x1_ref, g4_ref, t_ref, f_ref, dy_ref, loss_ref):
        i = pl.program_id(0)
        f = _dot(a_ref[0], w_ref[0:FS, :])
        for j in range(1, NFF):
            f = f + _dot(a_ref[j], w_ref[j * FS:(j + 1) * FS, :])
        err = x1_ref[...] + _rms(f, g4_ref[...]) - t_ref[...]
        f_ref[...] = f
        dy_ref[...] = err * (1.0 / D)
        part = 0.5 * jnp.sum(jnp.mean(err * err, axis=-1, keepdims=True), axis=0, keepdims=True)

        @pl.when(i == 0)
        def _():
            loss_ref[...] = jnp.zeros_like(loss_ref)

        loss_ref[...] += jnp.broadcast_to(part, loss_ref.shape)

    return _call(body, name, (s // tm,),
                 [pl.BlockSpec((NFF, tm, FS), lambda i: (0, i, 0)), pl.BlockSpec((NFF * FS, D), _fixed2),
                  pl.BlockSpec((tm, D), _row), pl.BlockSpec((1, D), _fixed2), pl.BlockSpec((tm, D), _row)],
                 [pl.BlockSpec((tm, D), _row), pl.BlockSpec((tm, D), _row), pl.BlockSpec((8, 128), _fixed2)],
                 [jax.ShapeDtypeStruct((s, D), F32), jax.ShapeDtypeStruct((s, D), F32),
                  jax.ShapeDtypeStruct((8, 128), F32)])(a, wd, x1, g4, target)


def _norm_bwd(f, g, dy, name):
    s = f.shape[0]
    tm = 512

    def body(f_ref, g_ref, dy_ref, df_ref, dg_ref):
        i = pl.program_id(0)
        df, dg = _rms_bwd(f_ref[...], g_ref[...], dy_ref[...])
        df_ref[...] = df.astype(BF16)

        @pl.when(i == 0)
        def _():
            dg_ref[...] = jnp.zeros_like(dg_ref)

        dg_ref[...] += dg

    return _call(body, name, (s // tm,),
                 [pl.BlockSpec((tm, D), _row), pl.BlockSpec((1, D), _fixed2), pl.BlockSpec((tm, D), _row)],
                 [pl.BlockSpec((tm, D), _row), pl.BlockSpec((1, D), _fixed2)],
                 [jax.ShapeDtypeStruct((s, D), BF16), jax.ShapeDtypeStruct((1, D), F32)])(f, g, dy)


def _conv_gate_bwd(hu, df, wd, cw, cb, name, exchange=None):
    s = hu.shape[2]
    tb = 512
    nt = s // tb
    e = tb + 16

    def body(hu_ref, prev_ref, next_ref, df_ref, dfn_ref, wd_ref, cw_ref, cb_ref, dhu_ref, dcw_ref, dcb_ref,
             dabuf, dbuf):
        i = pl.program_id(1)
        first, last = i == 0, i == nt - 1
        dabuf[0:tb] = _dot_nt(df_ref[...], wd_ref[...])
        dabuf[tb:tb + 8] = jnp.where(last, 0.0, _dot_nt(dfn_ref[...], wd_ref[...])[0:8])

        @pl.when(i == 0)
        def _():
            dcw_ref[...] = jnp.zeros_like(dcw_ref)
            dcb_ref[...] = jnp.zeros_like(dcb_ref)

        for c0, w in LANE_COLUMNS:
            lanes = pl.ds(c0, w)
            cwb = [[jnp.broadcast_to(cw_ref[sd, t:t + 1, lanes], (8, w)) for t in range(3)] for sd in (0, 1)]
            cbb = [jnp.broadcast_to(cb_ref[sd, :, lanes], (8, w)) for sd in (0, 1)]

            row = lax.broadcasted_iota(jnp.int32, (8, w), 0)

            def tile(k, carry, summed=True):
                sums, rolled = carry[:8], carry[8:]
                r = 8 + 8 * k
                xs, keep = [], []
                for sd in (0, 1):
                    if summed:
                        cur = hu_ref[sd, pl.ds(r - 8, 8), lanes]
                    else:
                        cur = jnp.where(last, 0.0, next_ref[sd, :, lanes])
                    r1, r2 = pltpu.roll(cur, 1, 0), pltpu.roll(cur, 2, 0)
                    xs.append([jnp.where(row >= 2, r2, rolled[2 * sd + 1]), jnp.where(row >= 1, r1, rolled[2 * sd]),
                               cur])
                    keep += [r1, r2]
                hc = [cbb[sd] + cwb[sd][2] * xs[sd][2] + cwb[sd][0] * xs[sd][0] + cwb[sd][1] * xs[sd][1]
                      for sd in (0, 1)]
                da = dabuf[pl.ds(r - 8, 8), lanes]
                gl, dgl = _gelu_parts(hc[1])
                dhc = (da * gl, da * hc[0] * dgl)
                for sd in (0, 1):
                    dbuf[sd, pl.ds(r, 8), lanes] = dhc[sd]
                if not summed:
                    return carry
                new = []
                for sd in (0, 1):
                    new += [sums[4 * sd + t] + dhc[sd] * xs[sd][t] for t in range(3)] + [sums[4 * sd + 3] + dhc[sd]]
                return tuple(new + keep)

            start = [jnp.zeros((8, w), F32) for _ in range(8)]
            for sd in (0, 1):
                halo = jnp.where(first, 0.0, prev_ref[sd, :, lanes])
                start += [pltpu.roll(halo, 1, 0), pltpu.roll(halo, 2, 0)]
            def tiles(k4, carry):
                for u in range(4):
                    carry = tile(4 * k4 + u, carry)
                return carry

            carry = tuple(start)
            for k4 in range(tb // 32):
                carry = tiles(k4, carry)
            tile(tb // 8, carry, summed=False)
            sums = carry[:8]

            def up(v):
                return pltpu.roll(v, 7, 0), pltpu.roll(v, 6, 0)

            def out_tile(k, carry):
                r = 8 + 16 * k
                new = []
                for sd in (0, 1):
                    va, va1, va2 = carry[3 * sd:3 * sd + 3]
                    vb, vc = dbuf[sd, pl.ds(r + 8, 8), lanes], dbuf[sd, pl.ds(r + 16, 8), lanes]
                    (vb1, vb2), (vc1, vc2) = up(vb), up(vc)
                    c0b, c1b, c2b = cwb[sd]
                    top = c2b * va + c1b * jnp.where(row <= 6, va1, vb1) + c0b * jnp.where(row <= 5, va2, vb2)
                    bot = c2b * vb + c1b * jnp.where(row <= 6, vb1, vc1) + c0b * jnp.where(row <= 5, vb2, vc2)
                    dhu_ref[sd, pl.ds(16 * k, 16), lanes] = jnp.concatenate(
                        [top, bot], axis=0).astype(BF16)
                    new += [vc, vc1, vc2]
                return tuple(new)

            begin = []
            for sd in (0, 1):
                va = dbuf[sd, 8:16, lanes]
                begin += [va, *up(va)]
            def out_tiles(k2, carry):
                return out_tile(2 * k2 + 1, out_tile(2 * k2, carry))

            carry = tuple(begin)
            for k2 in range(tb // 32):
                carry = out_tiles(k2, carry)
            for sd in (0, 1):
                for t in range(3):
                    dcw_ref[sd, t:t + 1, lanes] += jnp.sum(sums[4 * sd + t], axis=0, keepdims=True)
                dcb_ref[sd, :, lanes] += jnp.sum(sums[4 * sd + 3], axis=0, keepdims=True)

    nb8, nb16 = s // 8, s // 16
    return _call(body, name, (NFF, nt),
                 [pl.BlockSpec((2, None, tb, FS), lambda j, i: (0, j, i, 0)),
                  pl.BlockSpec((2, None, 8, FS), lambda j, i: (0, j, jnp.maximum(i * (tb // 8) - 1, 0), 0)),
                  pl.BlockSpec((2, None, 8, FS), lambda j, i: (0, j, jnp.minimum((i + 1) * (tb // 8), nb8 - 1), 0)),
                  pl.BlockSpec((tb, D), lambda j, i: (i, 0)),
                  pl.BlockSpec((16, D), lambda j, i: (jnp.minimum((i + 1) * (tb // 16), nb16 - 1), 0)),
                  pl.BlockSpec((FS, D), lambda j, i: (j, 0)),
                  pl.BlockSpec((2, None, 3, FS), lambda j, i: (0, j, 0, 0)),
                  pl.BlockSpec((2, None, 1, FS), lambda j, i: (0, j, 0, 0))],
                 [pl.BlockSpec((2, None, tb, FS), lambda j, i: (0, j, i, 0)),
                  pl.BlockSpec((2, None, 3, FS), lambda j, i: (0, j, 0, 0)),
                  pl.BlockSpec((2, None, 1, FS), lambda j, i: (0, j, 0, 0))],
                 [jax.ShapeDtypeStruct((2, NFF, s, FS), BF16), jax.ShapeDtypeStruct((2, NFF, 3, FS), F32),
                  jax.ShapeDtypeStruct((2, NFF, 1, FS), F32)],
                 [pltpu.VMEM((tb + 8, FS), F32), pltpu.VMEM((2, e, FS), F32)],
                 exchange=exchange)(hu, hu, hu, df, df, wd, cw, cb)


def _dw_down(a, df, name):
    s = a.shape[1]
    tk = 512
    nk = s // tk

    def body(a_ref, df_ref, o_ref, acc):
        k = pl.program_id(1)

        @pl.when(k == 0)
        def _():
            acc[...] = jnp.zeros_like(acc)

        acc[...] += _dot_tn(a_ref[...], df_ref[...])

        @pl.when(k == nk - 1)
        def _():
            o_ref[0] = acc[0:FS // 2, :].astype(BF16)
            o_ref[1] = acc[FS // 2:FS, :].astype(BF16)

    return _call(body, name, (NFF, nk),
                 [pl.BlockSpec((None, tk, FS), lambda j, k: (j, k, 0)), pl.BlockSpec((tk, D), lambda j, k: (k, 0))],
                 pl.BlockSpec((2, FS // 2, D), lambda j, k: (j, 0, 0)),
                 jax.ShapeDtypeStruct((NDEV, FS // 2, D), BF16), [pltpu.VMEM((FS, D), F32)])(a, df)


def _mm_dh2(dhu, wup, x1, g3, dx2, mix, g2, name):
    s = x1.shape[0]
    tm = 512

    def body(dhu_ref, w_ref, x1_ref, g3_ref, dx2_ref, mix_ref, g2_ref, dx1_ref, dmix_ref, dg3_ref, dg2_ref):
        i = pl.program_id(0)

        @pl.when(i == 0)
        def _():
            dg3_ref[...] = jnp.zeros_like(dg3_ref)
            dg2_ref[...] = jnp.zeros_like(dg2_ref)

        dh2 = _dot_nt(dhu_ref[0, 0], w_ref[0])
        for j in range(1, NDEV):
            dh2 = dh2 + _dot_nt(dhu_ref[j // NFF, j % NFF], w_ref[j])
        dn, dg3 = _rms_bwd(x1_ref[...], g3_ref[...], dh2)
        dx1 = dx2_ref[...] + dn
        dmix, dg2 = _rms_bwd(mix_ref[...], g2_ref[...], dx1)
        dx1_ref[...] = dx1
        dmix_ref[...] = dmix.astype(BF16)
        dg3_ref[...] += dg3
        dg2_ref[...] += dg2

    return _call(body, name, (s // tm,),
                 [pl.BlockSpec((2, NFF, tm, FS), lambda i: (0, 0, i, 0)), _resident((NDEV, D, FS)),
                  pl.BlockSpec((tm, D), _row), pl.BlockSpec((1, D), _fixed2), pl.BlockSpec((tm, D), _row),
                  pl.BlockSpec((tm, D), _row), pl.BlockSpec((1, D), _fixed2)],
                 [pl.BlockSpec((tm, D), _row), pl.BlockSpec((tm, D), _row), pl.BlockSpec((1, D), _fixed2),
                  pl.BlockSpec((1, D), _fixed2)],
                 [jax.ShapeDtypeStruct((s, D), F32), jax.ShapeDtypeStruct((s, D), BF16),
                  jax.ShapeDtypeStruct((1, D), F32), jax.ShapeDtypeStruct((1, D), F32)])(
                     dhu, wup, x1, g3, dx2, mix, g2)


def _dw_up(h2, dhu, name):
    s = h2.shape[0]
    tk = 512
    nk = s // tk

    def body(h_ref, d_ref, o_ref, acc):
        k = pl.program_id(1)

        @pl.when(k == 0)
        def _():
            acc[...] = jnp.zeros_like(acc)

        ht = h_ref[...].T
        for j in range(NFF):
            acc[j] += _dot(ht, d_ref[j])

        @pl.when(k == nk - 1)
        def _():
            o_ref[...] = acc[...].astype(BF16)

    return _call(body, name, (2, nk),
                 [pl.BlockSpec((tk, D), lambda hf, k: (k, 0)),
                  pl.BlockSpec((None, NFF, tk, FS), lambda hf, k: (hf, 0, k, 0))],
                 pl.BlockSpec((NFF, D, FS), lambda hf, k: (hf, 0, 0)),
                 jax.ShapeDtypeStruct((NDEV, D, FS), BF16), [pltpu.VMEM((NFF, D, FS), F32)])(h2, dhu)


def _mm_dz(dmix, wo, proj, bgate, ya, yb, name):
    s = dmix.shape[0]
    tm = 512

    def body(dm_ref, w_ref, ga_ref, gb_ref, ba_ref, bb_ref, ya_ref, yb_ref, dya_ref, dyb_ref, dg_ref, dbg_ref):
        i = pl.program_id(0)
        dz = _dot_nt(dm_ref[...], w_ref[...])
        ga = _sigmoid(ga_ref[...] + ba_ref[...])
        gb = _sigmoid(gb_ref[...] + bb_ref[...])
        dya_ref[...] = (dz * ga).astype(BF16)
        dyb_ref[...] = (dz * gb).astype(BF16)
        dga = dz * ya_ref[...] * ga * (1.0 - ga)
        dgb = dz * yb_ref[...] * gb * (1.0 - gb)
        dg_ref[:, 0:D] = dga.astype(BF16)
        dg_ref[:, D:2 * D] = dgb.astype(BF16)

        @pl.when(i == 0)
        def _():
            dbg_ref[...] = jnp.zeros_like(dbg_ref)

        dbg_ref[:, 0:D] += jnp.sum(dga, axis=0, keepdims=True)
        dbg_ref[:, D:2 * D] += jnp.sum(dgb, axis=0, keepdims=True)

    return _call(body, name, (s // tm,),
                 [pl.BlockSpec((tm, D), _row), pl.BlockSpec((D, D), _fixed2),
                  pl.BlockSpec((tm, D), lambda i: (i, 2)), pl.BlockSpec((tm, D), lambda i: (i, 3)),
                  pl.BlockSpec((1, D), lambda i: (0, 0)), pl.BlockSpec((1, D), lambda i: (0, 1)),
                  pl.BlockSpec((tm, D), _row), pl.BlockSpec((tm, D), _row)],
                 [pl.BlockSpec((tm, D), _row), pl.BlockSpec((tm, D), _row), pl.BlockSpec((tm, 2 * D), _row),
                  pl.BlockSpec((1, 2 * D), _fixed2)],
                 [jax.ShapeDtypeStruct((s, D), BF16), jax.ShapeDtypeStruct((s, D), BF16),
                  jax.ShapeDtypeStruct((s, 2 * D), BF16), jax.ShapeDtypeStruct((1, 2 * D), F32)])(
                     dmix, wo, proj, proj, bgate, bgate, ya, yb)


def _dw_o(z, dmix, name):
    s = z.shape[0]
    tk = 512
    nk = s // tk

    def body(z_ref, d_ref, o_ref, acc):
        k = pl.program_id(0)

        @pl.when(k == 0)
        def _():
            acc[...] = jnp.zeros_like(acc)

        acc[...] += _dot_tn(z_ref[...], d_ref[...])

        @pl.when(k == nk - 1)
        def _():
            for j in range(NDEV):
                o_ref[j] = acc[j * 128:(j + 1) * 128, :].astype(BF16)

    return _call(body, name, (nk,),
                 [pl.BlockSpec((tk, D), _row), pl.BlockSpec((tk, D), _row)],
                 pl.BlockSpec((NDEV, 128, D), lambda k: (0, 0, 0)),
                 jax.ShapeDtypeStruct((NDEV, 128, D), BF16), [pltpu.VMEM((D, D), F32)])(z, dmix)


def _branch_bwd(dya, dyb, wao, wpo, name):
    s = dya.shape[0]
    tm = 512

    def body(dya_ref, dyb_ref, wao_ref, wpo_ref, datt_ref, dmx_ref):
        datt_ref[...] = _dot_nt(dya_ref[...], wao_ref[...]).astype(BF16)
        dmx_ref[...] = _dot_nt(dyb_ref[...], wpo_ref[...])

    return _call(body, name, (s // tm,),
                 [pl.BlockSpec((tm, D), _row), pl.BlockSpec((tm, D), _row),
                  pl.BlockSpec((AW, D), _fixed2), pl.BlockSpec((PW, D), _fixed2)],
                 [pl.BlockSpec((tm, AW), _row), pl.BlockSpec((tm, PW), _row)],
                 [jax.ShapeDtypeStruct((s, AW), BF16), jax.ShapeDtypeStruct((s, PW), F32)])(dya, dyb, wao, wpo)


def _dw_branch(att, mixed, dya, dyb, name):
    s = att.shape[0]
    tk = 512
    nk = s // tk

    def body(att_ref, mx_ref, dya_ref, dyb_ref, oa_ref, ob_ref, acca, accb):
        k = pl.program_id(0)

        @pl.when(k == 0)
        def _():
            acca[...] = jnp.zeros_like(acca)
            accb[...] = jnp.zeros_like(accb)

        acca[...] += _dot_tn(att_ref[...], dya_ref[...])
        accb[...] += _dot_tn(mx_ref[...], dyb_ref[...])

        @pl.when(k == nk - 1)
        def _():
            for j in range(NDEV):
                oa_ref[j] = acca[:, j * 128:(j + 1) * 128].astype(BF16)
                ob_ref[j] = accb[:, j * 128:(j + 1) * 128].astype(BF16)

    out = jax.ShapeDtypeStruct((NDEV, AW, 128), BF16)
    return _call(body, name, (nk,),
                 [pl.BlockSpec((tk, AW), _row), pl.BlockSpec((tk, PW), _row),
                  pl.BlockSpec((tk, D), _row), pl.BlockSpec((tk, D), _row)],
                 [pl.BlockSpec((NDEV, AW, 128), lambda k: (0, 0, 0))] * 2, [out, out],
                 [pltpu.VMEM((AW, D), F32), pltpu.VMEM((PW, D), F32)])(att, mixed, dya, dyb)


def _attn_bwd(proj, datt, tab, name, exchange=None):
    s = proj.shape[0]
    nq = s // QB

    def body(q_ref, k0, k1, k2, v0, v1, v2, do_ref, b_ref, dq_ref, dk_ref, dv_ref, db_ref, dka, dkb, dva, dvb):
        i = pl.program_id(1)

        @pl.when(i == 0)
        def _():
            for r in (dka, dkb, dva, dvb):
                r[...] = jnp.zeros_like(r)
            db_ref[...] = jnp.zeros_like(db_ref)

        @pl.when(i < nq)
        def _():
            qs, ps, k, lane = _attn_probs(q_ref, (k0, k1, k2), b_ref, i)
            v = jnp.concatenate([v0[...], v1[...], v2[...]], axis=0).astype(BF16)
            do = do_ref[...]
            dq = jnp.zeros((QB, 128), F32)
            dkw = jnp.zeros((KW, 128), F32)
            dvw = jnp.zeros((KW, 128), F32)
            for hh in (0, 1):
                mine = (lane < 64) if hh == 0 else (lane >= 64)
                doh = jnp.where(mine, do, jnp.zeros_like(do))
                kmask = lax.broadcasted_iota(jnp.int32, (KW, 128), 1)
                kh = jnp.where((kmask < 64) if hh == 0 else (kmask >= 64), k, jnp.zeros_like(k))
                p = ps[hh]
                dp = _dot_nt(doh, v)
                ds = p * (dp - jnp.sum(dp * p, axis=1, keepdims=True))
                db_ref[hh] += ds
                dsb = ds.astype(BF16)
                dq = dq + _dot(dsb, kh)
                dkw = dkw + _dot_tn(dsb, qs[hh])
                dvw = dvw + _dot_tn(p.astype(BF16), doh)
            dq_ref[...] = (dq * SCALE).astype(BF16)
            dk_ref[...] = (dka[...] + dkw[0:QB]).astype(BF16)
            dka[...] = dkb[...] + dkw[QB:2 * QB]
            dkb[...] = dkw[2 * QB:3 * QB]
            dv_ref[...] = (dva[...] + dvw[0:QB]).astype(BF16)
            dva[...] = dvb[...] + dvw[QB:2 * QB]
            dvb[...] = dvw[2 * QB:3 * QB]

        @pl.when(i >= nq)
        def _():
            dk_ref[...] = dka[...].astype(BF16)
            dka[...] = dkb[...]
            dkb[...] = jnp.zeros_like(dkb)
            dv_ref[...] = dva[...].astype(BF16)
            dva[...] = dvb[...]
            dvb[...] = jnp.zeros_like(dvb)

    def kv(col, d):
        return pl.BlockSpec((QB, 128), lambda p, i: (jnp.clip(i - 2 + d, 0, nq - 1), col + p))

    cur = lambda p, i: (jnp.minimum(i, nq - 1), p)
    done = lambda p, i: (jnp.maximum(i - 2, 0), p)
    in_specs = [pl.BlockSpec((QB, 128), cur)]
    in_specs += [kv(4, d) for d in range(3)] + [kv(8, d) for d in range(3)]
    in_specs += [pl.BlockSpec((QB, 128), cur), pl.BlockSpec((2, QB, KW), lambda p, i: (p, 0, 0))]
    o = jax.ShapeDtypeStruct((s, AW), BF16)
    return _call(body, name, (4, nq + 2), in_specs,
                 [pl.BlockSpec((QB, 128), cur), pl.BlockSpec((QB, 128), done), pl.BlockSpec((QB, 128), done),
                  pl.BlockSpec((2, QB, KW), lambda p, i: (p, 0, 0))],
                 [o, o, o, jax.ShapeDtypeStruct((HEADS, QB, KW), F32)],
                 [pltpu.VMEM((QB, 128), F32)] * 4, exchange=exchange)(
                     proj, proj, proj, proj, proj, proj, proj, datt, tab)


def _rel_bias_grad(dtab, name):
    wdt = 640

    def body(x_ref, o_ref):
        x = x_ref[...]
        xc = x[0:64, 0:wdt]
        for qc in range(1, QB // 64):
            xc = xc + pltpu.roll(x[qc * 64:(qc + 1) * 64, :], KW - qc * 64, 1)[:, 0:wdt]
        r = lax.broadcasted_iota(jnp.int32, (64, 64), 0)
        c = lax.broadcasted_iota(jnp.int32, (64, 64), 1)
        flip = (r + c == 63).astype(F32)
        y = jnp.dot(flip, xc, preferred_element_type=F32, precision=lax.Precision.HIGHEST)
        z = pltpu.roll(y, 0, 1, stride=1, stride_axis=0)
        t = jnp.broadcast_to(jnp.sum(z, axis=0, keepdims=True), (8, wdt))
        e = lax.broadcasted_iota(jnp.int32, (wdt, wdt), 0)
        rr = lax.broadcasted_iota(jnp.int32, (wdt, wdt), 1)
        onehot = (jnp.clip(BAND - 1 - e, -MAXREL, MAXREL) + MAXREL == rr).astype(F32)
        o_ref[...] = jnp.dot(t, onehot, preferred_element_type=F32, precision=lax.Precision.HIGHEST)

    return _call(body, name, (HEADS,), [pl.BlockSpec((None, QB, KW), lambda h: (h, 0, 0))],
                 pl.BlockSpec((None, 8, wdt), lambda h: (h, 0, 0)),
                 jax.ShapeDtypeStruct((HEADS, 8, wdt), F32))(dtab)


def _pool_bwd(dmixed, pooled, wg, scale, name):
    s = dmixed.shape[0]
    tb = 512
    nt = s // tb
    e = tb + 16

    def body(dm_ref, dmn_ref, pl_ref, wg_ref, sc_ref, du_ref, dwg_ref, dsc_ref):
        i = pl.program_id(0)
        dm = jnp.concatenate([dm_ref[...], jnp.where(i == nt - 1, 0.0, dmn_ref[...])], axis=0)
        t = i * tb + lax.broadcasted_iota(jnp.int32, (e, 1), 0)

        @pl.when(i == 0)
        def _():
            dwg_ref[...] = jnp.zeros_like(dwg_ref)
            dsc_ref[...] = jnp.zeros_like(dsc_ref)

        for g, w in enumerate(POOL_WINDOWS):
            sl = slice(g * PG, (g + 1) * PG)
            wgb = wg_ref[g].astype(BF16)
            pb = pl_ref[:, sl]
            dsc_ref[:, sl] += jnp.sum(dm[0:tb, sl] * _dot(pb, wgb), axis=0, keepdims=True)
            dpre = (dm[:, sl] * sc_ref[:, sl]).astype(BF16)
            dwg_ref[g] += _dot_tn(pb, dpre[0:tb])
            dpool = _dot_nt(dpre, wgb)
            a = dpool / jnp.minimum(t + 1, w).astype(F32)
            sh = 1
            while sh < w:
                a = a + pltpu.roll(a, e - sh, 0)
                sh *= 2
            du_ref[:, sl] = (a[0:tb] - dpool[0:tb]).astype(BF16)

    nb16 = s // 16
    return _call(body, name, (nt,),
                 [pl.BlockSpec((tb, PW), _row),
                  pl.BlockSpec((16, PW), lambda i: (jnp.minimum((i + 1) * (tb // 16), nb16 - 1), 0)),
                  pl.BlockSpec((tb, PW), _row), pl.BlockSpec((4, PG, PG), lambda i: (0, 0, 0)),
                  pl.BlockSpec((1, PW), _fixed2)],
                 [pl.BlockSpec((tb, PW), _row), pl.BlockSpec((4, PG, PG), lambda i: (0, 0, 0)),
                  pl.BlockSpec((1, PW), _fixed2)],
                 [jax.ShapeDtypeStruct((s, PW), BF16), jax.ShapeDtypeStruct((4, PG, PG), F32),
                  jax.ShapeDtypeStruct((1, PW), F32)])(dmixed, dmixed, pooled, wg, scale)


def _dproj_specs(t):
    return [pl.BlockSpec((t, 512), _row)] * 4 + [pl.BlockSpec((t, 2 * D), _row)]


def _dproj_segment(refs, n):
    return refs[n][...] if n < 4 else refs[4][:, (n - 4) * 512:(n - 3) * 512]


def _mm_dh(segs, win, x, g1, dx1, name, exchange=None):
    s = x.shape[0]
    tm = 512

    def body(dq_ref, dk_ref, dv_ref, du_ref, dg_ref, w_ref, x_ref, g1_ref, dx1_ref, dx_ref, dg1_ref):
        i = pl.program_id(0)

        @pl.when(i == 0)
        def _():
            dg1_ref[...] = jnp.zeros_like(dg1_ref)

        pieces = (dq_ref, dk_ref, dv_ref, du_ref, dg_ref)
        dh = _dot_nt(_dproj_segment(pieces, 0), w_ref[0])
        for n in range(1, NDEV):
            dh = dh + _dot_nt(_dproj_segment(pieces, n), w_ref[n])
        dn, dg1 = _rms_bwd(x_ref[...], g1_ref[...], dh)
        dx_ref[...] = dx1_ref[...] + dn
        dg1_ref[...] += dg1

    return _call(body, name, (s // tm,),
                 _dproj_specs(tm) + [_resident((NDEV, D, 512)), pl.BlockSpec((tm, D), _row),
                                     pl.BlockSpec((1, D), _fixed2), pl.BlockSpec((tm, D), _row)],
                 [pl.BlockSpec((tm, D), _row), pl.BlockSpec((1, D), _fixed2)],
                 [jax.ShapeDtypeStruct((s, D), F32), jax.ShapeDtypeStruct((1, D), F32)], exchange=exchange)(
                     *segs, win, x, g1, dx1)


def _dw_in(h, segs, name):
    s = h.shape[0]
    tk = 512
    nk = s // tk

    def body(h_ref, dq_ref, dk_ref, dv_ref, du_ref, dg_ref, o_ref, acc):
        k = pl.program_id(0)

        @pl.when(k == 0)
        def _():
            acc[...] = jnp.zeros_like(acc)

        ht = h_ref[...].T
        pieces = (dq_ref, dk_ref, dv_ref, du_ref, dg_ref)
        for n in range(NDEV):
            acc[n] += _dot(ht, _dproj_segment(pieces, n))

        @pl.when(k == nk - 1)
        def _():
            o_ref[...] = acc[...].astype(BF16)

    return _call(body, name, (nk,), [pl.BlockSpec((tk, D), _row)] + _dproj_specs(tk),
                 pl.BlockSpec((NDEV, D, 512), lambda k: (0, 0, 0)),
                 jax.ShapeDtypeStruct((NDEV, D, 512), BF16), [pltpu.VMEM((NDEV, D, 512), F32)])(h, *segs)


def _slot(px, py, pc):
    return 4 * px + 2 * py + pc


def _gather_exchange(shards, deliver):
    n = len(shards)

    def phases(ins, outs, sems):
        send_sems, recv_sems, local_sems = sems
        x, y, c = lax.axis_index("x"), lax.axis_index("y"), lax.axis_index("c")
        me, sibling = (x, y, c), (x, y, 1 - c)
        chips = [(1 - x, y), (x, 1 - y), (1 - x, 1 - y)]

        def copy(w, k, block, to, src=None):
            dst = outs[w].at[_slot(*block)]
            return pltpu.make_async_remote_copy(
                src_ref=dst if src is None else src, dst_ref=dst, send_sem=send_sems.at[7 * w + k],
                recv_sem=recv_sems.at[7 * w + k], device_id=to, device_id_type=MESH)

        def mine(w):
            return pltpu.make_async_copy(ins[w], outs[w].at[_slot(*me)], local_sems.at[w])

        def first(w):
            return [copy(w, 0, me, sibling, src=ins[w])] + [
                copy(w, 1 + j, me, (*chip, c), src=ins[w]) for j, chip in enumerate(chips)]

        def passed(w):
            return [copy(w, 4 + j, (*chip, c), sibling) for j, chip in enumerate(chips)]

        def send():
            for w in range(n):
                mine(w).start()
                for cp in first(w):
                    cp.start()

        def forward():
            for w in range(n):
                for j, chip in enumerate(chips):
                    copy(w, 1 + j, (*chip, c), me).wait_recv()
                    passed(w)[j].start()

        def finish():
            for w in range(n):
                copy(w, 0, sibling, me).wait_recv()
                for j, chip in enumerate(chips):
                    copy(w, 4 + j, (*chip, 1 - c), me).wait_recv()
            for w in range(n):
                for cp in first(w) + passed(w):
                    cp.wait_send()
                mine(w).wait()

        return [send, forward, finish]

    return _Exchange(shards, [jax.ShapeDtypeStruct((NDEV,) + a.shape, a.dtype) for a in shards],
                     [pltpu.SemaphoreType.DMA((7 * n,)), pltpu.SemaphoreType.DMA((7 * n,)),
                      pltpu.SemaphoreType.DMA((n,))], phases, deliver)


def _scatter_exchange(parts, deliver):
    n = len(parts)

    def phases(ins, outs, sems):
        send_sems, recv_sems, local_sems = sems
        x, y, c = lax.axis_index("x"), lax.axis_index("y"), lax.axis_index("c")
        me = _slot(x, y, c)
        peers = [((1 - x) if r & 4 else x, (1 - y) if r & 2 else y, (1 - c) if r & 1 else c) for r in range(1, NDEV)]

        def mine(w):
            return pltpu.make_async_copy(ins[w].at[me], outs[w].at[me], local_sems.at[w])

        def copy(w, r, block_here, block_there):
            return pltpu.make_async_remote_copy(
                src_ref=ins[w].at[block_here], dst_ref=outs[w].at[block_there], send_sem=send_sems.at[7 * w + r],
                recv_sem=recv_sems.at[7 * w + r], device_id=peers[r], device_id_type=MESH)

        def send():
            for w in range(n):
                mine(w).start()
                for r, peer in enumerate(peers):
                    copy(w, r, _slot(*peer), me).start()

        def finish():
            for w in range(n):
                for r, peer in enumerate(peers):
                    copy(w, r, me, _slot(*peer)).wait_recv()
            for w in range(n):
                for r, peer in enumerate(peers):
                    copy(w, r, _slot(*peer), me).wait_send()
                mine(w).wait()

        return [send, finish]

    return _Exchange(parts, [jax.ShapeDtypeStruct(a.shape, a.dtype) for a in parts],
                     [pltpu.SemaphoreType.DMA((7 * n,)), pltpu.SemaphoreType.DMA((7 * n,)),
                      pltpu.SemaphoreType.DMA((n,))], phases, deliver)


def _adamw(w, g, m, v):
    m = ADAM_B1 * m + (1.0 - ADAM_B1) * g
    v = ADAM_B2 * v + (1.0 - ADAM_B2) * (g * g)
    m_hat = m / (1.0 - ADAM_B1 ** ADAM_STEP)
    v_hat = v / (1.0 - ADAM_B2 ** ADAM_STEP)
    delta = -ADAM_LR * (m_hat / (jnp.sqrt(v_hat) + ADAM_EPS) + ADAM_WD * w)
    return delta, m, v


def _adamw_sharded(recvs, w, m, v, tr, name, exchange=None):
    _, r, cdim = w.shape
    nr = r // tr

    def body(r0_ref, r1_ref, w_ref, m_ref, v_ref, g_ref, d_ref, nm_ref, nv_ref):
        l = pl.program_id(0)

        def total(ref):
            acc = ref[0].astype(F32)
            for k in range(1, NDEV):
                acc = acc + ref[k].astype(F32)
            return acc

        g = jnp.where(l == 0, total(r0_ref), total(r1_ref))
        d, nm, nv = _adamw(w_ref[...], g, m_ref[...], v_ref[...])
        g_ref[...] = g
        d_ref[...] = d
        nm_ref[...] = nm
        nv_ref[...] = nv

    mine = pl.BlockSpec((None, tr, cdim), lambda l, i: (l, i, 0))
    out = jax.ShapeDtypeStruct(w.shape, F32)
    return _call(body, name, (DEPTH, nr),
                 [pl.BlockSpec((NDEV, tr, cdim), lambda l, i: (0, jnp.where(l == 0, i, nr - 1), 0)),
                  pl.BlockSpec((NDEV, tr, cdim), lambda l, i: (0, jnp.where(l == 1, i, 0), 0)),
                  mine, mine, mine],
                 [mine] * 4, [out] * 4, exchange=exchange)(recvs[0], recvs[1], w, m, v)


def _adamw_small(parts, w, m, v, name):
    r = w.shape[0]

    def body(p_ref, w_ref, m_ref, v_ref, g_ref, d_ref, nm_ref, nv_ref):
        g = p_ref[0]
        for k in range(1, NDEV):
            g = g + p_ref[k]
        d, nm, nv = _adamw(w_ref[...], g, m_ref[...], v_ref[...])
        g_ref[...] = g
        d_ref[...] = d
        nm_ref[...] = nm
        nv_ref[...] = nv

    whole = pl.BlockSpec((r, 1024), _fixed2)
    out = jax.ShapeDtypeStruct((r, 1024), F32)
    return _call(body, name, (1,), [pl.BlockSpec((NDEV, r, 1024), lambda i: (0, 0, 0)), whole, whole, whole],
                 [whole] * 4, [out] * 4)(parts, w, m, v)


SMALL = (("norm_mix_pre", (D,)), ("b_gate", (2 * D,)), ("rel_bias", (HEADS, NREL)), ("w_pool_group", (4, PG, PG)),
         ("pool_scale", (PW,)), ("norm_mix_post", (D,)), ("norm_ffn_pre", (D,)), ("conv_b", (NFF * 2 * FS,)),
         ("norm_ffn_post", (D,)))
SHARDED = (("w_in", 256), ("w_attn_out", 512), ("w_pool_out", 512), ("w_o", 128), ("w_up", 256), ("conv_w", 3),
           ("w_down", 176))
SMALL_ROWS = 168


def _pack_small(tree):
    flat = jnp.concatenate([tree[name].reshape(-1) for name, _ in SMALL])
    return jnp.pad(flat, (0, SMALL_ROWS * 1024 - flat.shape[0])).reshape(SMALL_ROWS, 1024)


def _unpack_small(packed):
    flat = packed.reshape(-1)
    out, at = {}, 0
    for name, shape in SMALL:
        size = DEPTH * int(np.prod(shape))
        out[name] = flat[at:at + size].reshape((DEPTH,) + shape)
        at += size
    return out


def _layer_fwd(l, x, h, p, gath, target, gnext, carry):
    n = f"l{l}"
    g = lambda name: p[name][l].reshape(1, -1)
    res = {"x": x, "h": h}
    res["tab"] = _bias_table(p["rel_bias"][l], f"bias_table_{n}")
    proj = _mm_in(h, gath["w_in"], f"mm_in_{n}", carry.get("mm_in"))
    att = _attn_fwd(proj, res["tab"], f"attn_fwd_{n}", carry.get("attn_fwd"))
    pooled, mixed = _pool_fwd(proj, p["w_pool_group"][l], g("pool_scale"), f"pool_fwd_{n}")
    bgate = g("b_gate")
    z, ya, yb = _branch_fwd(att, mixed, gath["w_attn_out"], gath["w_pool_out"], proj, bgate, f"branch_fwd_{n}")
    mix, x1, h2 = _mm_o_fwd(z, gath["w_o"], x, g("norm_mix_post"), g("norm_ffn_pre"), f"mm_o_fwd_{n}")
    hu = _mm_up(h2, gath["w_up"], f"mm_up_{n}", carry.get("mm_up")).reshape(2, NFF, -1, FS)
    cw = gath["conv_w"].reshape(2, NFF, 3, FS)
    cb = p["conv_b"][l].reshape(2, NFF, 1, FS)
    a = _conv_gate_fwd(hu, cw, cb, f"conv_gate_fwd_{n}", carry.get("conv_gate_fwd"))
    res.update(proj=proj, att=att, pooled=pooled, mixed=mixed, z=z, ya=ya, yb=yb, mix=mix, x1=x1, h2=h2, hu=hu,
               a=a, cw=cw, cb=cb, bgate=bgate)
    if target is None:
        f, x2, hn = _mm_down_fwd(a, gath["w_down"], x1, g("norm_ffn_post"), gnext, f"mm_down_fwd_{n}")
        res["f"] = f
        return res, x2, hn
    f, dy, loss = _mm_down_loss(a, gath["w_down"], x1, g("norm_ffn_post"), target, f"mm_down_loss_{n}")
    res["f"] = f
    return res, dy, loss


def _layer_bwd(l, dx2, p, gath, res, carry):
    n = f"l{l}"
    g = lambda name: p[name][l].reshape(1, -1)
    big, small = {}, {}
    taken = lambda call: carry[call](big) if call in carry else None
    df, small["norm_ffn_post"] = _norm_bwd(res["f"], g("norm_ffn_post"), dx2, f"norm_bwd_{n}")
    dhu, dcw, dcb = _conv_gate_bwd(res["hu"], df, gath["w_down"], res["cw"], res["cb"], f"conv_gate_bwd_{n}",
                                   taken("conv_gate_bwd"))
    big["conv_w"] = dcw.reshape(NDEV, 3, FS)
    small["conv_b"] = dcb
    big["w_down"] = _dw_down(res["a"], df, f"dw_down_{n}")
    dx1, dmix, small["norm_ffn_pre"], small["norm_mix_post"] = _mm_dh2(
        dhu, gath["w_up"], res["x1"], g("norm_ffn_pre"), dx2, res["mix"], g("norm_mix_post"), f"mm_dh2_{n}")
    big["w_up"] = _dw_up(res["h2"], dhu, f"dw_up_{n}")
    dya, dyb, dgates, small["b_gate"] = _mm_dz(dmix, gath["w_o"], res["proj"], res["bgate"], res["ya"], res["yb"],
                                                f"mm_dz_{n}")
    big["w_o"] = _dw_o(res["z"], dmix, f"dw_o_{n}")
    datt, dmixed = _branch_bwd(dya, dyb, gath["w_attn_out"], gath["w_pool_out"], f"branch_bwd_{n}")
    big["w_attn_out"], big["w_pool_out"] = _dw_branch(res["att"], res["mixed"], dya, dyb, f"dw_branch_{n}")
    dq, dk, dv, dtab = _attn_bwd(res["proj"], datt, res["tab"], f"attn_bwd_{n}", taken("attn_bwd"))
    small["rel_bias"] = _rel_bias_grad(dtab, f"rel_bias_grad_{n}")[:, 0, :NREL]
    du, small["w_pool_group"], small["pool_scale"] = _pool_bwd(
        dmixed, res["pooled"], p["w_pool_group"][l], g("pool_scale"), f"pool_bwd_{n}")
    segs = (dq, dk, dv, du, dgates)
    big["w_in"] = _dw_in(res["h"], segs, f"dw_in_{n}")
    dx, small["norm_mix_pre"] = _mm_dh(segs, gath["w_in"], res["x"], g("norm_mix_pre"), dx1, f"mm_dh_{n}",
                                       taken("mm_dh"))
    return dx, big, small


def _gather_weights(gath, l, p, names):
    shards = [p[k][l] if k == "conv_w" else p[k][l].astype(BF16) for k in names]

    def deliver(results):
        for k, a in zip(names, results):
            if k in ("w_attn_out", "w_pool_out"):
                a = jnp.transpose(a, (1, 0, 2)).reshape(AW, D)
            elif k == "w_o":
                a = a.reshape(D, D)
            elif k == "w_down":
                a = a.reshape(NFF * FS, D)
            gath[k] = a

    return _gather_exchange(shards, deliver)


def _scatter_grads(recv, big, names):
    return _scatter_exchange([big[k] for k in names], lambda results: recv.update(zip(names, results)))


def kernel(x, norm_mix_pre, w_in, b_gate, rel_bias, w_attn_out, w_pool_group, pool_scale, w_pool_out, w_o, norm_mix_post, norm_ffn_pre, w_up, conv_w, conv_b, w_down, norm_ffn_post, loss_target, m_norm_mix_pre, m_w_in, m_b_gate, m_rel_bias, m_w_attn_out, m_w_pool_group, m_pool_scale, m_w_pool_out, m_w_o, m_norm_mix_post, m_norm_ffn_pre, m_w_up, m_conv_w, m_conv_b, m_w_down, m_norm_ffn_post, v_norm_mix_pre, v_w_in, v_b_gate, v_rel_bias, v_w_attn_out, v_w_pool_group, v_pool_scale, v_w_pool_out, v_w_o, v_norm_mix_post, v_norm_ffn_pre, v_w_up, v_conv_w, v_conv_b, v_w_down, v_norm_ffn_post):
    names = ("norm_mix_pre", "w_in", "b_gate", "rel_bias", "w_attn_out", "w_pool_group", "pool_scale", "w_pool_out",
             "w_o", "norm_mix_post", "norm_ffn_pre", "w_up", "conv_w", "conv_b", "w_down", "norm_ffn_post")
    p = dict(zip(names, (norm_mix_pre, w_in, b_gate, rel_bias, w_attn_out, w_pool_group, pool_scale, w_pool_out, w_o,
                         norm_mix_post, norm_ffn_pre, w_up, conv_w, conv_b, w_down, norm_ffn_post)))
    mom = dict(zip(names, (m_norm_mix_pre, m_w_in, m_b_gate, m_rel_bias, m_w_attn_out, m_w_pool_group, m_pool_scale,
                           m_w_pool_out, m_w_o, m_norm_mix_post, m_norm_ffn_pre, m_w_up, m_conv_w, m_conv_b, m_w_down,
                           m_norm_ffn_post)))
    var = dict(zip(names, (v_norm_mix_pre, v_w_in, v_b_gate, v_rel_bias, v_w_attn_out, v_w_pool_group, v_pool_scale,
                           v_w_pool_out, v_w_o, v_norm_mix_post, v_norm_ffn_pre, v_w_up, v_conv_w, v_conv_b, v_w_down,
                           v_norm_ffn_post)))
    s = x.shape[1]
    xs = x.reshape(s, D)
    target = loss_target.reshape(s, D)

    gath = [{}, {}]
    rest = ("w_attn_out", "w_pool_out", "w_o", "w_down", "conv_w")
    h0 = _rmsnorm_call(xs, p["norm_mix_pre"][0].reshape(1, D), "rmsnorm_l0",
                       _gather_weights(gath[0], 0, p, ("w_in",)))
    res0, x2, h1 = _layer_fwd(0, xs, h0, p, gath[0], None, p["norm_mix_pre"][1].reshape(1, D), {
        "mm_in": _gather_weights(gath[0], 0, p, rest),
        "attn_fwd": _gather_weights(gath[0], 0, p, ("w_up",)),
        "mm_up": _gather_weights(gath[1], 1, p, ("w_in",) + rest),
        "conv_gate_fwd": _gather_weights(gath[1], 1, p, ("w_up",))})
    res1, dy, loss_part = _layer_fwd(1, x2, h1, p, gath[1], target, None, {})

    order = [k for k, _ in SHARDED]
    recv = [{}, {}]
    dx2, big1, small1 = _layer_bwd(1, dy, p, gath[1], res1, {})
    grad_x, big0, small0 = _layer_bwd(0, dx2, p, gath[0], res0, {
        "conv_gate_bwd": lambda big: _scatter_grads(recv[1], big1, order),
        "attn_bwd": lambda big: _scatter_grads(recv[0], big, [k for k in order if k != "w_in"]),
        "mm_dh": lambda big: _scatter_grads(recv[0], big, ["w_in"])})

    loss = lax.psum(loss_part[0, 0], ("x", "y", "c"))

    small = {k: jnp.stack([small0[k].reshape(shape), small1[k].reshape(shape)]) for k, shape in SMALL}
    parts = []
    carried = {"w_up": _gather_exchange([_pack_small(small)], parts.extend)}
    out = {}
    for k, tr in sorted(SHARDED, key=lambda kt: kt[0] not in carried):
        out[k] = _adamw_sharded((recv[0][k], recv[1][k]), p[k], mom[k], var[k], tr, f"adamw_{k}", carried.get(k))
    packed = _adamw_small(parts[0], _pack_small(p), _pack_small(mom), _pack_small(var), "adamw_small")
    unpacked = [_unpack_small(a) for a in packed]
    for k, _ in SMALL:
        out[k] = tuple(u[k] for u in unpacked)

    return (loss, grad_x.reshape(x.shape), *[out[k][0] for k in names], *[out[k][1] for k in names],
            *[out[k][2] for k in names], *[out[k][3] for k in names])
```

```python
import numpy as np
import jax
import jax.numpy as jnp
from jax import lax
from jax.experimental import pallas as pl
from jax.experimental.pallas import tpu as pltpu

F32, BF16 = jnp.float32, jnp.bfloat16

D = 1024
AW = 512
PW = 512
PG = 128
INW = 4096
FS = 704
NFF = 4
NDEV = 8
DEPTH = 2
HEADS = 8
NREL = 513
MAXREL = 256
POOL_WINDOWS = (2, 4, 8, 16)
EPS = 1e-6
SCALE = 0.125
NEG = -1e30
QB = 256
KW = 3 * QB
BAND = 576
ADAM_LR, ADAM_B1, ADAM_B2, ADAM_EPS, ADAM_WD, ADAM_STEP = 0.001, 0.9, 0.999, 1e-08, 0.01, 10
VMEM_LIMIT_V7X = 56 * 1024 * 1024
MESH = pl.DeviceIdType.MESH
GELU_C = 0.7978845608028654
GELU_A = 0.044715


HBM = pl.BlockSpec(memory_space=pltpu.HBM)


class _Exchange:
    def __init__(self, operands, out_shape, scratch, phases, deliver):
        self.operands, self.out_shape, self.scratch = list(operands), list(out_shape), list(scratch)
        self.phases, self.deliver = phases, deliver


def _call(body, name, grid, in_specs, out_specs, out_shape, scratch=(), exchange=None):
    params = pltpu.CompilerParams(vmem_limit_bytes=VMEM_LIMIT_V7X)
    if exchange is None:
        return pl.pallas_call(body, name=name, grid=grid, in_specs=in_specs, out_specs=out_specs, out_shape=out_shape,
                              scratch_shapes=list(scratch), compiler_params=params)
    single = not isinstance(out_shape, (list, tuple))
    outs, ospecs = ([out_shape], [out_specs]) if single else (list(out_shape), list(out_specs))
    n_in, n_out, n_scr = len(in_specs), len(outs), len(scratch)
    ne_in, ne_out = len(exchange.operands), len(exchange.out_shape)
    nsteps = int(np.prod(grid))

    def carried(*refs):
        cut = np.cumsum([0, n_in, ne_in, n_out, ne_out, n_scr])
        base_in, ex_in, base_out, ex_out, base_scr = (refs[cut[k]:cut[k + 1]] for k in range(5))
        step = pl.program_id(0)
        for axis in range(1, len(grid)):
            step = step * grid[axis] + pl.program_id(axis)
        phases = exchange.phases(ex_in, ex_out, refs[cut[5]:])
        pl.when(step == 0)(phases[0])
        body(*base_in, *base_out, *base_scr)
        at = {2: [nsteps - 1], 3: [(7 * nsteps) // 10, nsteps - 1]}[len(phases)]
        for phase, when in zip(phases[1:], at):
            pl.when(step == when)(phase)

    call = pl.pallas_call(
        carried, name=name, grid=grid, in_specs=list(in_specs) + [HBM] * ne_in, out_specs=ospecs + [HBM] * ne_out,
        out_shape=outs + exchange.out_shape, scratch_shapes=list(scratch) + exchange.scratch, compiler_params=params)

    def run(*args):
        res = call(*args, *exchange.operands)
        exchange.deliver(res[n_out:])
        return res[0] if single else res[:n_out]

    return run


def _dot(a, b):
    return jnp.dot(a, b, preferred_element_type=F32)


def _dot_nt(a, b):
    return lax.dot_general(a, b, (((1,), (1,)), ((), ())), preferred_element_type=F32)


def _dot_tn(a, b):
    return lax.dot_general(a, b, (((0,), (0,)), ((), ())), preferred_element_type=F32)


def _rms(x, g):
    r = lax.rsqrt(jnp.mean(x * x, axis=-1, keepdims=True) + EPS)
    return x * r * g


def _rms_bwd(x, g, dy):
    r = lax.rsqrt(jnp.mean(x * x, axis=-1, keepdims=True) + EPS)
    xh = x * r
    dg = jnp.sum(dy * xh, axis=0, keepdims=True)
    dxh = dy * g
    dx = r * (dxh - xh * jnp.mean(dxh * xh, axis=-1, keepdims=True))
    return dx, dg


def _sigmoid(x):
    return 1.0 / (1.0 + jnp.exp(-x))


def _gelu_parts(x):
    x2 = x * x
    th = jnp.tanh(x * (GELU_C + (GELU_C * GELU_A) * x2))
    s = 0.5 * th + 0.5
    dg = s * (1.0 + x * (1.0 - s) * (2.0 * GELU_C + (6.0 * GELU_C * GELU_A) * x2))
    return x * s, dg


def _row(i):
    return (i, 0)


def _fixed2(*_):
    return (0, 0)


def _rmsnorm_call(x, g, name, exchange=None):
    s = x.shape[0]
    tm = 512

    def body(x_ref, g_ref, o_ref):
        o_ref[...] = _rms(x_ref[...], g_ref[...]).astype(BF16)

    return _call(body, name, (s // tm,),
                 [pl.BlockSpec((tm, D), _row), pl.BlockSpec((1, D), _fixed2)],
                 pl.BlockSpec((tm, D), _row), jax.ShapeDtypeStruct((s, D), BF16), exchange=exchange)(x, g)


def _resident(shape):
    return pl.BlockSpec(shape, lambda *_: (0,) * len(shape), pipeline_mode=pl.Buffered(1))


def _mm_in(h, win, name, exchange=None):
    s = h.shape[0]
    tm = 512

    def body(h_ref, w_ref, o_ref):
        hv = h_ref[...]
        for j in range(NDEV):
            o_ref[:, j * 512:(j + 1) * 512] = _dot(hv, w_ref[j])

    return _call(body, name, (s // tm,),
                 [pl.BlockSpec((tm, D), _row), _resident((NDEV, D, 512))],
                 pl.BlockSpec((tm, INW), _row), jax.ShapeDtypeStruct((s, INW), F32), exchange=exchange)(h, win)


def _bias_table(rel_bias, name):
    wdt = 1024
    rel = jnp.pad(rel_bias, ((0, 0), (0, 640 - NREL))).reshape(HEADS, 1, 640)

    def body(r_ref, o_ref):
        rr = lax.broadcasted_iota(jnp.int32, (640, wdt), 0)
        m = lax.broadcasted_iota(jnp.int32, (640, wdt), 1)
        d = jnp.where(m < KW, m, m - wdt)
        onehot = (jnp.clip(512 - d, -MAXREL, MAXREL) + MAXREL == rr).astype(F32)
        row = jnp.dot(jnp.broadcast_to(r_ref[...], (8, 640)), onehot, preferred_element_type=F32,
                      precision=lax.Precision.HIGHEST)[0:1]
        t = pltpu.roll(jnp.broadcast_to(row, (QB, wdt)), 0, 1, stride=1, stride_axis=0)[:, 0:KW]
        qc = lax.broadcasted_iota(jnp.int32, (QB, KW), 0) // 64
        kc = lax.broadcasted_iota(jnp.int32, (QB, KW), 1) // 64
        o_ref[...] = jnp.where((kc >= qc) & (kc <= qc + 8), t, NEG)

    return _call(body, name, (HEADS,), [pl.BlockSpec((None, 1, 640), lambda h: (h, 0, 0))],
                 pl.BlockSpec((None, QB, KW), lambda h: (h, 0, 0)),
                 jax.ShapeDtypeStruct((HEADS, QB, KW), F32))(rel)


def _attn_probs(q_ref, k_refs, b_ref, i):
    lane = lax.broadcasted_iota(jnp.int32, (QB, 128), 1)
    q = q_ref[...] * SCALE
    qs = [jnp.where(lane < 64, q, 0.0).astype(BF16), jnp.where(lane >= 64, q, 0.0).astype(BF16)]
    k = jnp.concatenate([r[...] for r in k_refs], axis=0).astype(BF16)
    colb = lax.broadcasted_iota(jnp.int32, (QB, KW), 1) // QB
    valid = colb + i >= 2
    ps = []
    for hh in (0, 1):
        sc = _dot_nt(qs[hh], k) + b_ref[hh]
        sc = jnp.where(valid, sc, NEG)
        m = jnp.max(sc, axis=1, keepdims=True)
        p = jnp.exp(sc - m)
        ps.append(p / jnp.sum(p, axis=1, keepdims=True))
    return qs, ps, k, lane


def _attn_fwd(proj, tab, name, exchange=None):
    s = proj.shape[0]
    nq = s // QB

    def body(q_ref, k0, k1, k2, v0, v1, v2, b_ref, o_ref):
        i = pl.program_id(1)
        _, ps, _, lane = _attn_probs(q_ref, (k0, k1, k2), b_ref, i)
        v = jnp.concatenate([v0[...], v1[...], v2[...]], axis=0).astype(BF16)
        o = [_dot(ps[hh].astype(BF16), v) for hh in (0, 1)]
        o_ref[...] = jnp.where(lane < 64, o[0], o[1]).astype(BF16)

    def kv(col, d):
        return pl.BlockSpec((QB, 128), lambda p, i: (jnp.maximum(i - 2 + d, 0), col + p))

    in_specs = [pl.BlockSpec((QB, 128), lambda p, i: (i, p))]
    in_specs += [kv(4, d) for d in range(3)] + [kv(8, d) for d in range(3)]
    in_specs += [pl.BlockSpec((2, QB, KW), lambda p, i: (p, 0, 0))]
    return _call(body, name, (4, nq), in_specs, pl.BlockSpec((QB, 128), lambda p, i: (i, p)),
                 jax.ShapeDtypeStruct((s, AW), BF16), exchange=exchange)(proj, proj, proj, proj, proj, proj, proj, tab)


def _pool_fwd(proj, wg, scale, name):
    s = proj.shape[0]
    tb = 512
    e = tb + 16

    def body(u_ref, halo_ref, wg_ref, sc_ref, pooled_ref, mixed_ref):
        i = pl.program_id(0)
        cur = u_ref[...]
        prev = jnp.where(i > 0, halo_ref[...], 0.0)
        xs = jnp.concatenate([prev, cur], axis=0)
        t = i * tb + lax.broadcasted_iota(jnp.int32, (tb, 1), 0)
        for g, w in enumerate(POOL_WINDOWS):
            sl = slice(g * PG, (g + 1) * PG)
            a = xs[:, sl]
            sh = 1
            while sh < w:
                a = a + pltpu.roll(a, sh, 0)
                sh *= 2
            cnt = jnp.minimum(t + 1, w).astype(F32)
            pooled = (a[16:] / cnt - cur[:, sl]).astype(BF16)
            pooled_ref[:, sl] = pooled
            mixed_ref[:, sl] = (_dot(pooled, wg_ref[g].astype(BF16)) * sc_ref[:, sl]).astype(BF16)

    assert e % 8 == 0
    return _call(body, name, (s // tb,),
                 [pl.BlockSpec((tb, PW), lambda i: (i, 3)),
                  pl.BlockSpec((16, PW), lambda i: (jnp.maximum(i * (tb // 16) - 1, 0), 3)),
                  pl.BlockSpec((4, PG, PG), lambda i: (0, 0, 0)), pl.BlockSpec((1, PW), _fixed2)],
                 [pl.BlockSpec((tb, PW), _row), pl.BlockSpec((tb, PW), _row)],
                 [jax.ShapeDtypeStruct((s, PW), BF16), jax.ShapeDtypeStruct((s, PW), BF16)])(proj, proj, wg, scale)


def _branch_fwd(att, mixed, wao, wpo, proj, bgate, name):
    s = att.shape[0]
    tm = 512

    def body(att_ref, mx_ref, wao_ref, wpo_ref, ga_ref, gb_ref, ba_ref, bb_ref, z_ref, ya_ref, yb_ref):
        ya = _dot(att_ref[...], wao_ref[...])
        yb = _dot(mx_ref[...], wpo_ref[...])
        ga = _sigmoid(ga_ref[...] + ba_ref[...])
        gb = _sigmoid(gb_ref[...] + bb_ref[...])
        ya_ref[...] = ya
        yb_ref[...] = yb
        z_ref[...] = (ga * ya + gb * yb).astype(BF16)

    return _call(body, name, (s // tm,),
                 [pl.BlockSpec((tm, AW), _row), pl.BlockSpec((tm, PW), _row),
                  pl.BlockSpec((AW, D), _fixed2), pl.BlockSpec((PW, D), _fixed2),
                  pl.BlockSpec((tm, D), lambda i: (i, 2)), pl.BlockSpec((tm, D), lambda i: (i, 3)),
                  pl.BlockSpec((1, D), lambda i: (0, 0)), pl.BlockSpec((1, D), lambda i: (0, 1))],
                 [pl.BlockSpec((tm, D), _row)] * 3,
                 [jax.ShapeDtypeStruct((s, D), BF16), jax.ShapeDtypeStruct((s, D), F32),
                  jax.ShapeDtypeStruct((s, D), F32)])(att, mixed, wao, wpo, proj, proj, bgate, bgate)


def _mm_o_fwd(z, wo, x, g2, g3, name):
    s = z.shape[0]
    tm = 512

    def body(z_ref, w_ref, x_ref, g2_ref, g3_ref, mix_ref, x1_ref, h2_ref):
        mix = _dot(z_ref[...], w_ref[...])
        x1 = x_ref[...] + _rms(mix, g2_ref[...])
        mix_ref[...] = mix
        x1_ref[...] = x1
        h2_ref[...] = _rms(x1, g3_ref[...]).astype(BF16)

    return _call(body, name, (s // tm,),
                 [pl.BlockSpec((tm, D), _row), pl.BlockSpec((D, D), _fixed2), pl.BlockSpec((tm, D), _row),
                  pl.BlockSpec((1, D), _fixed2), pl.BlockSpec((1, D), _fixed2)],
                 [pl.BlockSpec((tm, D), _row)] * 3,
                 [jax.ShapeDtypeStruct((s, D), F32), jax.ShapeDtypeStruct((s, D), F32),
                  jax.ShapeDtypeStruct((s, D), BF16)])(z, wo, x, g2, g3)


def _mm_up_conv_gate(h2, wup, cw, cb, name, exchange=None):
    s = h2.shape[0]
    tm = 256

    def body(h_ref, w_ref, cw_ref, cb_ref, hu_ref, a_ref, halo):
        first = pl.program_id(0) == 0
        hv = h_ref[...]
        for j in range(NDEV):
            hu_ref[j] = _dot(hv, w_ref[j])
        for jj in range(NFF):
            for c0, w in LANE_COLUMNS:
                lanes = pl.ds(c0, w)
                cwb = [[jnp.broadcast_to(cw_ref[sd, jj, t:t + 1, lanes], (8, w)) for t in range(3)] for sd in (0, 1)]
                cbb = [jnp.broadcast_to(cb_ref[sd, jj, :, lanes], (8, w)) for sd in (0, 1)]
                row = lax.broadcasted_iota(jnp.int32, (8, w), 0)
                rolled = []
                for sd in (0, 1):
                    before = jnp.where(first, 0.0, halo[NFF * sd + jj, :, lanes])
                    rolled += [pltpu.roll(before, 1, 0), pltpu.roll(before, 2, 0)]
                for k in range(tm // 16):
                    outs = []
                    for r in (16 * k, 16 * k + 8):
                        hc, keep = [], []
                        for sd in (0, 1):
                            cur = hu_ref[NFF * sd + jj, pl.ds(r, 8), lanes]
                            r1, r2 = pltpu.roll(cur, 1, 0), pltpu.roll(cur, 2, 0)
                            hc.append(cbb[sd] + cwb[sd][2] * cur
                                      + cwb[sd][0] * jnp.where(row >= 2, r2, rolled[2 * sd + 1])
                                      + cwb[sd][1] * jnp.where(row >= 1, r1, rolled[2 * sd]))
                            keep += [r1, r2]
                        rolled = keep
                        outs.append(_gelu_parts(hc[1])[0] * hc[0])
                    a_ref[jj, pl.ds(16 * k, 16), lanes] = jnp.concatenate(outs, axis=0).astype(BF16)
        for j in range(NDEV):
            halo[j] = hu_ref[j, tm - 8:tm, :]

    small = lambda i: (0, 0, 0, 0)
    return _call(body, name, (s // tm,),
                 [pl.BlockSpec((tm, D), _row), _resident((NDEV, D, FS)), pl.BlockSpec((2, NFF, 3, FS), small),
                  pl.BlockSpec((2, NFF, 1, FS), small)],
                 [pl.BlockSpec((NDEV, tm, FS), lambda i: (0, i, 0)), pl.BlockSpec((NFF, tm, FS), lambda i: (0, i, 0))],
                 [jax.ShapeDtypeStruct((NDEV, s, FS), F32), jax.ShapeDtypeStruct((NFF, s, FS), BF16)],
                 [pltpu.VMEM((NDEV, 8, FS), F32)], exchange=exchange)(h2, wup, cw, cb)


LANE_COLUMNS = [(c0, min(128, FS - c0)) for c0 in range(0, FS, 128)]


def _mm_down_fwd(a, wd, x1, g4, gnext, name, exchange=None):
    s = a.shape[1]
    tm = 512

    def body(a_ref, w_ref, x1_ref, g4_ref, gn_ref, f_ref, x2_ref, hn_ref):
        f = _dot(a_ref[0], w_ref[0:FS, :])
        for j in range(1, NFF):
            f = f + _dot(a_ref[j], w_ref[j * FS:(j + 1) * FS, :])
        x2 = x1_ref[...] + _rms(f, g4_ref[...])
        f_ref[...] = f
        x2_ref[...] = x2
        hn_ref[...] = _rms(x2, gn_ref[...]).astype(BF16)

    return _call(body, name, (s // tm,),
                 [pl.BlockSpec((NFF, tm, FS), lambda i: (0, i, 0)), pl.BlockSpec((NFF * FS, D), _fixed2),
                  pl.BlockSpec((tm, D), _row), pl.BlockSpec((1, D), _fixed2), pl.BlockSpec((1, D), _fixed2)],
                 [pl.BlockSpec((tm, D), _row)] * 3,
                 [jax.ShapeDtypeStruct((s, D), F32), jax.ShapeDtypeStruct((s, D), F32),
                  jax.ShapeDtypeStruct((s, D), BF16)], exchange=exchange)(a, wd, x1, g4, gnext)


def _mm_down_loss(a, wd, x1, g4, target, name):
    s = a.shape[1]
    tm = 512

    def body(a_ref, w_ref, x1_ref, g4_ref, t_ref, f_ref, dy_ref, loss_ref):
        i = pl.program_id(0)
        f = _dot(a_ref[0], w_ref[0:FS, :])
        for j in range(1, NFF):
            f = f + _dot(a_ref[j], w_ref[j * FS:(j + 1) * FS, :])
        err = x1_ref[...] + _rms(f, g4_ref[...]) - t_ref[...]
        f_ref[...] = f
        dy_ref[...] = err * (1.0 / D)
        part = 0.5 * jnp.sum(jnp.mean(err * err, axis=-1, keepdims=True), axis=0, keepdims=True)

        @pl.when(i == 0)
        def _():
            loss_ref[...] = jnp.zeros_like(loss_ref)

        loss_ref[...] += jnp.broadcast_to(part, loss_ref.shape)

    return _call(body, name, (s // tm,),
                 [pl.BlockSpec((NFF, tm, FS), lambda i: (0, i, 0)), pl.BlockSpec((NFF * FS, D), _fixed2),
                  pl.BlockSpec((tm, D), _row), pl.BlockSpec((1, D), _fixed2), pl.BlockSpec((tm, D), _row)],
                 [pl.BlockSpec((tm, D), _row), pl.BlockSpec((tm, D), _row), pl.BlockSpec((8, 128), _fixed2)],
                 [jax.ShapeDtypeStruct((s, D), F32), jax.ShapeDtypeStruct((s, D), F32),
                  jax.ShapeDtypeStruct((8, 128), F32)])(a, wd, x1, g4, target)


def _norm_bwd(f, g, dy, name):
    s = f.shape[0]
    tm = 512

    def body(f_ref, g_ref, dy_ref, df_ref, dg_ref):
        i = pl.program_id(0)
        df, dg = _rms_bwd(f_ref[...], g_ref[...], dy_ref[...])
        df_ref[...] = df.astype(BF16)

        @pl.when(i == 0)
        def _():
            dg_ref[...] = jnp.zeros_like(dg_ref)

        dg_ref[...] += dg

    return _call(body, name, (s // tm,),
                 [pl.BlockSpec((tm, D), _row), pl.BlockSpec((1, D), _fixed2), pl.BlockSpec((tm, D), _row)],
                 [pl.BlockSpec((tm, D), _row), pl.BlockSpec((1, D), _fixed2)],
                 [jax.ShapeDtypeStruct((s, D), BF16), jax.ShapeDtypeStruct((1, D), F32)])(f, g, dy)


def _conv_gate_bwd(hu, df, wd, cw, cb, name, exchange=None):
    s = hu.shape[2]
    tb = 512
    nt = s // tb
    e = tb + 16

    def body(hu_ref, prev_ref, next_ref, df_ref, dfn_ref, wd_ref, cw_ref, cb_ref, dhu_ref, dcw_ref, dcb_ref,
             dabuf, dbuf):
        i = pl.program_id(1)
        first, last = i == 0, i == nt - 1
        dabuf[0:tb] = _dot_nt(df_ref[...], wd_ref[...])
        dabuf[tb:tb + 8] = jnp.where(last, 0.0, _dot_nt(dfn_ref[...], wd_ref[...])[0:8])

        @pl.when(i == 0)
        def _():
            dcw_ref[...] = jnp.zeros_like(dcw_ref)
            dcb_ref[...] = jnp.zeros_like(dcb_ref)

        for c0, w in LANE_COLUMNS:
            lanes = pl.ds(c0, w)
            cwb = [[jnp.broadcast_to(cw_ref[sd, t:t + 1, lanes], (8, w)) for t in range(3)] for sd in (0, 1)]
            cbb = [jnp.broadcast_to(cb_ref[sd, :, lanes], (8, w)) for sd in (0, 1)]

            row = lax.broadcasted_iota(jnp.int32, (8, w), 0)

            def tile(k, carry, summed=True):
                sums, rolled = carry[:8], carry[8:]
                r = 8 + 8 * k
                xs, keep = [], []
                for sd in (0, 1):
                    if summed:
                        cur = hu_ref[sd, pl.ds(r - 8, 8), lanes]
                    else:
                        cur = jnp.where(last, 0.0, next_ref[sd, :, lanes])
                    r1, r2 = pltpu.roll(cur, 1, 0), pltpu.roll(cur, 2, 0)
                    xs.append([jnp.where(row >= 2, r2, rolled[2 * sd + 1]), jnp.where(row >= 1, r1, rolled[2 * sd]),
                               cur])
                    keep += [r1, r2]
                hc = [cbb[sd] + cwb[sd][2] * xs[sd][2] + cwb[sd][0] * xs[sd][0] + cwb[sd][1] * xs[sd][1]
                      for sd in (0, 1)]
                da = dabuf[pl.ds(r - 8, 8), lanes]
                gl, dgl = _gelu_parts(hc[1])
                dhc = (da * gl, da * hc[0] * dgl)
                for sd in (0, 1):
                    dbuf[sd, pl.ds(r, 8), lanes] = dhc[sd]
                if not summed:
                    return carry
                new = []
                for sd in (0, 1):
                    new += [sums[4 * sd + t] + dhc[sd] * xs[sd][t] for t in range(3)] + [sums[4 * sd + 3] + dhc[sd]]
                return tuple(new + keep)

            start = [jnp.zeros((8, w), F32) for _ in range(8)]
            for sd in (0, 1):
                halo = jnp.where(first, 0.0, prev_ref[sd, :, lanes])
                start += [pltpu.roll(halo, 1, 0), pltpu.roll(halo, 2, 0)]
            def tiles(k4, carry):
                for u in range(4):
                    carry = tile(4 * k4 + u, carry)
                return carry

            carry = tuple(start)
            for k4 in range(tb // 32):
                carry = tiles(k4, carry)
            tile(tb // 8, carry, summed=False)
            sums = carry[:8]

            def up(v):
                return pltpu.roll(v, 7, 0), pltpu.roll(v, 6, 0)

            def out_tile(k, carry):
                r = 8 + 16 * k
                new = []
                for sd in (0, 1):
                    va, va1, va2 = carry[3 * sd:3 * sd + 3]
                    vb, vc = dbuf[sd, pl.ds(r + 8, 8), lanes], dbuf[sd, pl.ds(r + 16, 8), lanes]
                    (vb1, vb2), (vc1, vc2) = up(vb), up(vc)
                    c0b, c1b, c2b = cwb[sd]
                    top = c2b * va + c1b * jnp.where(row <= 6, va1, vb1) + c0b * jnp.where(row <= 5, va2, vb2)
                    bot = c2b * vb + c1b * jnp.where(row <= 6, vb1, vc1) + c0b * jnp.where(row <= 5, vb2, vc2)
                    dhu_ref[sd, pl.ds(16 * k, 16), lanes] = jnp.concatenate(
                        [top, bot], axis=0).astype(BF16)
                    new += [vc, vc1, vc2]
                return tuple(new)

            begin = []
            for sd in (0, 1):
                va = dbuf[sd, 8:16, lanes]
                begin += [va, *up(va)]
            def out_tiles(k2, carry):
                return out_tile(2 * k2 + 1, out_tile(2 * k2, carry))

            carry = tuple(begin)
            for k2 in range(tb // 32):
                carry = out_tiles(k2, carry)
            for sd in (0, 1):
                for t in range(3):
                    dcw_ref[sd, t:t + 1, lanes] += jnp.sum(sums[4 * sd + t], axis=0, keepdims=True)
                dcb_ref[sd, :, lanes] += jnp.sum(sums[4 * sd + 3], axis=0, keepdims=True)

    nb8, nb16 = s // 8, s // 16
    return _call(body, name, (NFF, nt),
                 [pl.BlockSpec((2, None, tb, FS), lambda j, i: (0, j, i, 0)),
                  pl.BlockSpec((2, None, 8, FS), lambda j, i: (0, j, jnp.maximum(i * (tb // 8) - 1, 0), 0)),
                  pl.BlockSpec((2, None, 8, FS), lambda j, i: (0, j, jnp.minimum((i + 1) * (tb // 8), nb8 - 1), 0)),
                  pl.BlockSpec((tb, D), lambda j, i: (i, 0)),
                  pl.BlockSpec((16, D), lambda j, i: (jnp.minimum((i + 1) * (tb // 16), nb16 - 1), 0)),
                  pl.BlockSpec((FS, D), lambda j, i: (j, 0)),
                  pl.BlockSpec((2, None, 3, FS), lambda j, i: (0, j, 0, 0)),
                  pl.BlockSpec((2, None, 1, FS), lambda j, i: (0, j, 0, 0))],
                 [pl.BlockSpec((2, None, tb, FS), lambda j, i: (0, j, i, 0)),
                  pl.BlockSpec((2, None, 3, FS), lambda j, i: (0, j, 0, 0)),
                  pl.BlockSpec((2, None, 1, FS), lambda j, i: (0, j, 0, 0))],
                 [jax.ShapeDtypeStruct((2, NFF, s, FS), BF16), jax.ShapeDtypeStruct((2, NFF, 3, FS), F32),
                  jax.ShapeDtypeStruct((2, NFF, 1, FS), F32)],
                 [pltpu.VMEM((tb + 8, FS), F32), pltpu.VMEM((2, e, FS), F32)],
                 exchange=exchange)(hu, hu, hu, df, df, wd, cw, cb)


def _dw_down(a, df, name):
    s = a.shape[1]
    tk = 512
    nk = s // tk

    def body(a_ref, df_ref, o_ref, acc):
        k = pl.program_id(1)

        @pl.when(k == 0)
        def _():
            acc[...] = jnp.zeros_like(acc)

        acc[...] += _dot_tn(a_ref[...], df_ref[...])

        @pl.when(k == nk - 1)
        def _():
            o_ref[0] = acc[0:FS // 2, :].astype(BF16)
            o_ref[1] = acc[FS // 2:FS, :].astype(BF16)

    return _call(body, name, (NFF, nk),
                 [pl.BlockSpec((None, tk, FS), lambda j, k: (j, k, 0)), pl.BlockSpec((tk, D), lambda j, k: (k, 0))],
                 pl.BlockSpec((2, FS // 2, D), lambda j, k: (j, 0, 0)),
                 jax.ShapeDtypeStruct((NDEV, FS // 2, D), BF16), [pltpu.VMEM((FS, D), F32)])(a, df)


def _mm_dh2(dhu, wup, x1, g3, dx2, mix, g2, name):
    s = x1.shape[0]
    tm = 512

    def body(dhu_ref, w_ref, x1_ref, g3_ref, dx2_ref, mix_ref, g2_ref, dx1_ref, dmix_ref, dg3_ref, dg2_ref):
        i = pl.program_id(0)

        @pl.when(i == 0)
        def _():
            dg3_ref[...] = jnp.zeros_like(dg3_ref)
            dg2_ref[...] = jnp.zeros_like(dg2_ref)

        dh2 = _dot_nt(dhu_ref[0, 0], w_ref[0])
        for j in range(1, NDEV):
            dh2 = dh2 + _dot_nt(dhu_ref[j // NFF, j % NFF], w_ref[j])
        dn, dg3 = _rms_bwd(x1_ref[...], g3_ref[...], dh2)
        dx1 = dx2_ref[...] + dn
        dmix, dg2 = _rms_bwd(mix_ref[...], g2_ref[...], dx1)
        dx1_ref[...] = dx1
        dmix_ref[...] = dmix.astype(BF16)
        dg3_ref[...] += dg3
        dg2_ref[...] += dg2

    return _call(body, name, (s // tm,),
                 [pl.BlockSpec((2, NFF, tm, FS), lambda i: (0, 0, i, 0)), _resident((NDEV, D, FS)),
                  pl.BlockSpec((tm, D), _row), pl.BlockSpec((1, D), _fixed2), pl.BlockSpec((tm, D), _row),
                  pl.BlockSpec((tm, D), _row), pl.BlockSpec((1, D), _fixed2)],
                 [pl.BlockSpec((tm, D), _row), pl.BlockSpec((tm, D), _row), pl.BlockSpec((1, D), _fixed2),
                  pl.BlockSpec((1, D), _fixed2)],
                 [jax.ShapeDtypeStruct((s, D), F32), jax.ShapeDtypeStruct((s, D), BF16),
                  jax.ShapeDtypeStruct((1, D), F32), jax.ShapeDtypeStruct((1, D), F32)])(
                     dhu, wup, x1, g3, dx2, mix, g2)


def _dw_up(h2, dhu, name):
    s = h2.shape[0]
    tk = 512
    nk = s // tk

    def body(h_ref, d_ref, o_ref, acc):
        k = pl.program_id(1)

        @pl.when(k == 0)
        def _():
            acc[...] = jnp.zeros_like(acc)

        ht = h_ref[...].T
        for j in range(NFF):
            acc[j] += _dot(ht, d_ref[j])

        @pl.when(k == nk - 1)
        def _():
            o_ref[...] = acc[...].astype(BF16)

    return _call(body, name, (2, nk),
                 [pl.BlockSpec((tk, D), lambda hf, k: (k, 0)),
                  pl.BlockSpec((None, NFF, tk, FS), lambda hf, k: (hf, 0, k, 0))],
                 pl.BlockSpec((NFF, D, FS), lambda hf, k: (hf, 0, 0)),
                 jax.ShapeDtypeStruct((NDEV, D, FS), BF16), [pltpu.VMEM((NFF, D, FS), F32)])(h2, dhu)


def _mm_dz(dmix, wo, proj, bgate, ya, yb, name):
    s = dmix.shape[0]
    tm = 512

    def body(dm_ref, w_ref, ga_ref, gb_ref, ba_ref, bb_ref, ya_ref, yb_ref, dya_ref, dyb_ref, dg_ref, dbg_ref):
        i = pl.program_id(0)
        dz = _dot_nt(dm_ref[...], w_ref[...])
        ga = _sigmoid(ga_ref[...] + ba_ref[...])
        gb = _sigmoid(gb_ref[...] + bb_ref[...])
        dya_ref[...] = (dz * ga).astype(BF16)
        dyb_ref[...] = (dz * gb).astype(BF16)
        dga = dz * ya_ref[...] * ga * (1.0 - ga)
        dgb = dz * yb_ref[...] * gb * (1.0 - gb)
        dg_ref[:, 0:D] = dga.astype(BF16)
        dg_ref[:, D:2 * D] = dgb.astype(BF16)

        @pl.when(i == 0)
        def _():
            dbg_ref[...] = jnp.zeros_like(dbg_ref)

        dbg_ref[:, 0:D] += jnp.sum(dga, axis=0, keepdims=True)
        dbg_ref[:, D:2 * D] += jnp.sum(dgb, axis=0, keepdims=True)

    return _call(body, name, (s // tm,),
                 [pl.BlockSpec((tm, D), _row), pl.BlockSpec((D, D), _fixed2),
                  pl.BlockSpec((tm, D), lambda i: (i, 2)), pl.BlockSpec((tm, D), lambda i: (i, 3)),
                  pl.BlockSpec((1, D), lambda i: (0, 0)), pl.BlockSpec((1, D), lambda i: (0, 1)),
                  pl.BlockSpec((tm, D), _row), pl.BlockSpec((tm, D), _row)],
                 [pl.BlockSpec((tm, D), _row), pl.BlockSpec((tm, D), _row), pl.BlockSpec((tm, 2 * D), _row),
                  pl.BlockSpec((1, 2 * D), _fixed2)],
                 [jax.ShapeDtypeStruct((s, D), BF16), jax.ShapeDtypeStruct((s, D), BF16),
                  jax.ShapeDtypeStruct((s, 2 * D), BF16), jax.ShapeDtypeStruct((1, 2 * D), F32)])(
                     dmix, wo, proj, proj, bgate, bgate, ya, yb)


def _dw_o(z, dmix, name):
    s = z.shape[0]
    tk = 512
    nk = s // tk

    def body(z_ref, d_ref, o_ref, acc):
        k = pl.program_id(0)

        @pl.when(k == 0)
        def _():
            acc[...] = jnp.zeros_like(acc)

        acc[...] += _dot_tn(z_ref[...], d_ref[...])

        @pl.when(k == nk - 1)
        def _():
            for j in range(NDEV):
                o_ref[j] = acc[j * 128:(j + 1) * 128, :].astype(BF16)

    return _call(body, name, (nk,),
                 [pl.BlockSpec((tk, D), _row), pl.BlockSpec((tk, D), _row)],
                 pl.BlockSpec((NDEV, 128, D), lambda k: (0, 0, 0)),
                 jax.ShapeDtypeStruct((NDEV, 128, D), BF16), [pltpu.VMEM((D, D), F32)])(z, dmix)


def _branch_bwd(dya, dyb, wao, wpo, name):
    s = dya.shape[0]
    tm = 512

    def body(dya_ref, dyb_ref, wao_ref, wpo_ref, datt_ref, dmx_ref):
        datt_ref[...] = _dot_nt(dya_ref[...], wao_ref[...]).astype(BF16)
        dmx_ref[...] = _dot_nt(dyb_ref[...], wpo_ref[...])

    return _call(body, name, (s // tm,),
                 [pl.BlockSpec((tm, D), _row), pl.BlockSpec((tm, D), _row),
                  pl.BlockSpec((AW, D), _fixed2), pl.BlockSpec((PW, D), _fixed2)],
                 [pl.BlockSpec((tm, AW), _row), pl.BlockSpec((tm, PW), _row)],
                 [jax.ShapeDtypeStruct((s, AW), BF16), jax.ShapeDtypeStruct((s, PW), F32)])(dya, dyb, wao, wpo)


def _dw_branch(att, mixed, dya, dyb, name):
    s = att.shape[0]
    tk = 512
    nk = s // tk

    def body(att_ref, mx_ref, dya_ref, dyb_ref, oa_ref, ob_ref, acca, accb):
        k = pl.program_id(0)

        @pl.when(k == 0)
        def _():
            acca[...] = jnp.zeros_like(acca)
            accb[...] = jnp.zeros_like(accb)

        acca[...] += _dot_tn(att_ref[...], dya_ref[...])
        accb[...] += _dot_tn(mx_ref[...], dyb_ref[...])

        @pl.when(k == nk - 1)
        def _():
            for j in range(NDEV):
                oa_ref[j] = acca[:, j * 128:(j + 1) * 128].astype(BF16)
                ob_ref[j] = accb[:, j * 128:(j + 1) * 128].astype(BF16)

    out = jax.ShapeDtypeStruct((NDEV, AW, 128), BF16)
    return _call(body, name, (nk,),
                 [pl.BlockSpec((tk, AW), _row), pl.BlockSpec((tk, PW), _row),
                  pl.BlockSpec((tk, D), _row), pl.BlockSpec((tk, D), _row)],
                 [pl.BlockSpec((NDEV, AW, 128), lambda k: (0, 0, 0))] * 2, [out, out],
                 [pltpu.VMEM((AW, D), F32), pltpu.VMEM((PW, D), F32)])(att, mixed, dya, dyb)


def _attn_bwd(proj, datt, tab, name, exchange=None):
    s = proj.shape[0]
    nq = s // QB

    def body(q_ref, k0, k1, k2, v0, v1, v2, do_ref, b_ref, dq_ref, dk_ref, dv_ref, db_ref, dka, dkb, dva, dvb):
        i = pl.program_id(1)

        @pl.when(i == 0)
        def _():
            for r in (dka, dkb, dva, dvb):
                r[...] = jnp.zeros_like(r)
            db_ref[...] = jnp.zeros_like(db_ref)

        @pl.when(i < nq)
        def _():
            qs, ps, k, lane = _attn_probs(q_ref, (k0, k1, k2), b_ref, i)
            v = jnp.concatenate([v0[...], v1[...], v2[...]], axis=0).astype(BF16)
            do = do_ref[...]
            dq = jnp.zeros((QB, 128), F32)
            dkw = jnp.zeros((KW, 128), F32)
            dvw = jnp.zeros((KW, 128), F32)
            for hh in (0, 1):
                mine = (lane < 64) if hh == 0 else (lane >= 64)
                doh = jnp.where(mine, do, jnp.zeros_like(do))
                kmask = lax.broadcasted_iota(jnp.int32, (KW, 128), 1)
                kh = jnp.where((kmask < 64) if hh == 0 else (kmask >= 64), k, jnp.zeros_like(k))
                p = ps[hh]
                dp = _dot_nt(doh, v)
                ds = p * (dp - jnp.sum(dp * p, axis=1, keepdims=True))
                db_ref[hh] += ds
                dsb = ds.astype(BF16)
                dq = dq + _dot(dsb, kh)
                dkw = dkw + _dot_tn(dsb, qs[hh])
                dvw = dvw + _dot_tn(p.astype(BF16), doh)
            dq_ref[...] = (dq * SCALE).astype(BF16)
            dk_ref[...] = (dka[...] + dkw[0:QB]).astype(BF16)
            dka[...] = dkb[...] + dkw[QB:2 * QB]
            dkb[...] = dkw[2 * QB:3 * QB]
            dv_ref[...] = (dva[...] + dvw[0:QB]).astype(BF16)
            dva[...] = dvb[...] + dvw[QB:2 * QB]
            dvb[...] = dvw[2 * QB:3 * QB]

        @pl.when(i >= nq)
        def _():
            dk_ref[...] = dka[...].astype(BF16)
            dka[...] = dkb[...]
            dkb[...] = jnp.zeros_like(dkb)
            dv_ref[...] = dva[...].astype(BF16)
            dva[...] = dvb[...]
            dvb[...] = jnp.zeros_like(dvb)

    def kv(col, d):
        return pl.BlockSpec((QB, 128), lambda p, i: (jnp.clip(i - 2 + d, 0, nq - 1), col + p))

    cur = lambda p, i: (jnp.minimum(i, nq - 1), p)
    done = lambda p, i: (jnp.maximum(i - 2, 0), p)
    in_specs = [pl.BlockSpec((QB, 128), cur)]
    in_specs += [kv(4, d) for d in range(3)] + [kv(8, d) for d in range(3)]
    in_specs += [pl.BlockSpec((QB, 128), cur), pl.BlockSpec((2, QB, KW), lambda p, i: (p, 0, 0))]
    o = jax.ShapeDtypeStruct((s, AW), BF16)
    return _call(body, name, (4, nq + 2), in_specs,
                 [pl.BlockSpec((QB, 128), cur), pl.BlockSpec((QB, 128), done), pl.BlockSpec((QB, 128), done),
                  pl.BlockSpec((2, QB, KW), lambda p, i: (p, 0, 0))],
                 [o, o, o, jax.ShapeDtypeStruct((HEADS, QB, KW), F32)],
                 [pltpu.VMEM((QB, 128), F32)] * 4, exchange=exchange)(
                     proj, proj, proj, proj, proj, proj, proj, datt, tab)


def _rel_bias_grad(dtab, name):
    wdt = 640

    def body(x_ref, o_ref):
        x = x_ref[...]
        xc = x[0:64, 0:wdt]
        for qc in range(1, QB // 64):
            xc = xc + pltpu.roll(x[qc * 64:(qc + 1) * 64, :], KW - qc * 64, 1)[:, 0:wdt]
        r = lax.broadcasted_iota(jnp.int32, (64, 64), 0)
        c = lax.broadcasted_iota(jnp.int32, (64, 64), 1)
        flip = (r + c == 63).astype(F32)
        y = jnp.dot(flip, xc, preferred_element_type=F32, precision=lax.Precision.HIGHEST)
        z = pltpu.roll(y, 0, 1, stride=1, stride_axis=0)
        t = jnp.broadcast_to(jnp.sum(z, axis=0, keepdims=True), (8, wdt))
        e = lax.broadcasted_iota(jnp.int32, (wdt, wdt), 0)
        rr = lax.broadcasted_iota(jnp.int32, (wdt, wdt), 1)
        onehot = (jnp.clip(BAND - 1 - e, -MAXREL, MAXREL) + MAXREL == rr).astype(F32)
        o_ref[...] = jnp.dot(t, onehot, preferred_element_type=F32, precision=lax.Precision.HIGHEST)

    return _call(body, name, (HEADS,), [pl.BlockSpec((None, QB, KW), lambda h: (h, 0, 0))],
                 pl.BlockSpec((None, 8, wdt), lambda h: (h, 0, 0)),
                 jax.ShapeDtypeStruct((HEADS, 8, wdt), F32))(dtab)


def _pool_bwd(dmixed, pooled, wg, scale, name):
    s = dmixed.shape[0]
    tb = 512
    nt = s // tb
    e = tb + 16

    def body(dm_ref, dmn_ref, pl_ref, wg_ref, sc_ref, du_ref, dwg_ref, dsc_ref):
        i = pl.program_id(0)
        dm = jnp.concatenate([dm_ref[...], jnp.where(i == nt - 1, 0.0, dmn_ref[...])], axis=0)
        t = i * tb + lax.broadcasted_iota(jnp.int32, (e, 1), 0)

        @pl.when(i == 0)
        def _():
            dwg_ref[...] = jnp.zeros_like(dwg_ref)
            dsc_ref[...] = jnp.zeros_like(dsc_ref)

        for g, w in enumerate(POOL_WINDOWS):
            sl = slice(g * PG, (g + 1) * PG)
            wgb = wg_ref[g].astype(BF16)
            pb = pl_ref[:, sl]
            dsc_ref[:, sl] += jnp.sum(dm[0:tb, sl] * _dot(pb, wgb), axis=0, keepdims=True)
            dpre = (dm[:, sl] * sc_ref[:, sl]).astype(BF16)
            dwg_ref[g] += _dot_tn(pb, dpre[0:tb])
            dpool = _dot_nt(dpre, wgb)
            a = dpool / jnp.minimum(t + 1, w).astype(F32)
            sh = 1
            while sh < w:
                a = a + pltpu.roll(a, e - sh, 0)
                sh *= 2
            du_ref[:, sl] = (a[0:tb] - dpool[0:tb]).astype(BF16)

    nb16 = s // 16
    return _call(body, name, (nt,),
                 [pl.BlockSpec((tb, PW), _row),
                  pl.BlockSpec((16, PW), lambda i: (jnp.minimum((i + 1) * (tb // 16), nb16 - 1), 0)),
                  pl.BlockSpec((tb, PW), _row), pl.BlockSpec((4, PG, PG), lambda i: (0, 0, 0)),
                  pl.BlockSpec((1, PW), _fixed2)],
                 [pl.BlockSpec((tb, PW), _row), pl.BlockSpec((4, PG, PG), lambda i: (0, 0, 0)),
                  pl.BlockSpec((1, PW), _fixed2)],
                 [jax.ShapeDtypeStruct((s, PW), BF16), jax.ShapeDtypeStruct((4, PG, PG), F32),
                  jax.ShapeDtypeStruct((1, PW), F32)])(dmixed, dmixed, pooled, wg, scale)


def _dproj_specs(t):
    return [pl.BlockSpec((t, 512), _row)] * 4 + [pl.BlockSpec((t, 2 * D), _row)]


def _dproj_segment(refs, n):
    return refs[n][...] if n < 4 else refs[4][:, (n - 4) * 512:(n - 3) * 512]


def _mm_dh(segs, win, x, g1, dx1, name, exchange=None):
    s = x.shape[0]
    tm = 512

    def body(dq_ref, dk_ref, dv_ref, du_ref, dg_ref, w_ref, x_ref, g1_ref, dx1_ref, dx_ref, dg1_ref):
        i = pl.program_id(0)

        @pl.when(i == 0)
        def _():
            dg1_ref[...] = jnp.zeros_like(dg1_ref)

        pieces = (dq_ref, dk_ref, dv_ref, du_ref, dg_ref)
        dh = _dot_nt(_dproj_segment(pieces, 0), w_ref[0])
        for n in range(1, NDEV):
            dh = dh + _dot_nt(_dproj_segment(pieces, n), w_ref[n])
        dn, dg1 = _rms_bwd(x_ref[...], g1_ref[...], dh)
        dx_ref[...] = dx1_ref[...] + dn
        dg1_ref[...] += dg1

    return _call(body, name, (s // tm,),
                 _dproj_specs(tm) + [_resident((NDEV, D, 512)), pl.BlockSpec((tm, D), _row),
                                     pl.BlockSpec((1, D), _fixed2), pl.BlockSpec((tm, D), _row)],
                 [pl.BlockSpec((tm, D), _row), pl.BlockSpec((1, D), _fixed2)],
                 [jax.ShapeDtypeStruct((s, D), F32), jax.ShapeDtypeStruct((1, D), F32)], exchange=exchange)(
                     *segs, win, x, g1, dx1)


def _dw_in(h, segs, name):
    s = h.shape[0]
    tk = 512
    nk = s // tk

    def body(h_ref, dq_ref, dk_ref, dv_ref, du_ref, dg_ref, o_ref, acc):
        k = pl.program_id(0)

        @pl.when(k == 0)
        def _():
            acc[...] = jnp.zeros_like(acc)

        ht = h_ref[...].T
        pieces = (dq_ref, dk_ref, dv_ref, du_ref, dg_ref)
        for n in range(NDEV):
            acc[n] += _dot(ht, _dproj_segment(pieces, n))

        @pl.when(k == nk - 1)
        def _():
            o_ref[...] = acc[...].astype(BF16)

    return _call(body, name, (nk,), [pl.BlockSpec((tk, D), _row)] + _dproj_specs(tk),
                 pl.BlockSpec((NDEV, D, 512), lambda k: (0, 0, 0)),
                 jax.ShapeDtypeStruct((NDEV, D, 512), BF16), [pltpu.VMEM((NDEV, D, 512), F32)])(h, *segs)


def _slot(px, py, pc):
    return 4 * px + 2 * py + pc


def _gather_exchange(shards, deliver):
    n = len(shards)

    def phases(ins, outs, sems):
        send_sems, recv_sems, local_sems = sems
        x, y, c = lax.axis_index("x"), lax.axis_index("y"), lax.axis_index("c")
        me, sibling = (x, y, c), (x, y, 1 - c)
        chips = [(1 - x, y), (x, 1 - y), (1 - x, 1 - y)]

        def copy(w, k, block, to, src=None):
            dst = outs[w].at[_slot(*block)]
            return pltpu.make_async_remote_copy(
                src_ref=dst if src is None else src, dst_ref=dst, send_sem=send_sems.at[7 * w + k],
                recv_sem=recv_sems.at[7 * w + k], device_id=to, device_id_type=MESH)

        def mine(w):
            return pltpu.make_async_copy(ins[w], outs[w].at[_slot(*me)], local_sems.at[w])

        def first(w):
            return [copy(w, 0, me, sibling, src=ins[w])] + [
                copy(w, 1 + j, me, (*chip, c), src=ins[w]) for j, chip in enumerate(chips)]

        def passed(w):
            return [copy(w, 4 + j, (*chip, c), sibling) for j, chip in enumerate(chips)]

        def send():
            for w in range(n):
                mine(w).start()
                for cp in first(w):
                    cp.start()

        def forward():
            for w in range(n):
                for j, chip in enumerate(chips):
                    copy(w, 1 + j, (*chip, c), me).wait_recv()
                    passed(w)[j].start()

        def finish():
            for w in range(n):
                copy(w, 0, sibling, me).wait_recv()
                for j, chip in enumerate(chips):
                    copy(w, 4 + j, (*chip, 1 - c), me).wait_recv()
            for w in range(n):
                for cp in first(w) + passed(w):
                    cp.wait_send()
                mine(w).wait()

        return [send, forward, finish]

    return _Exchange(shards, [jax.ShapeDtypeStruct((NDEV,) + a.shape, a.dtype) for a in shards],
                     [pltpu.SemaphoreType.DMA((7 * n,)), pltpu.SemaphoreType.DMA((7 * n,)),
                      pltpu.SemaphoreType.DMA((n,))], phases, deliver)


def _scatter_exchange(parts, deliver):
    n = len(parts)

    def phases(ins, outs, sems):
        send_sems, recv_sems, local_sems = sems
        x, y, c = lax.axis_index("x"), lax.axis_index("y"), lax.axis_index("c")
        me = _slot(x, y, c)
        peers = [((1 - x) if r & 4 else x, (1 - y) if r & 2 else y, (1 - c) if r & 1 else c) for r in range(1, NDEV)]

        def mine(w):
            return pltpu.make_async_copy(ins[w].at[me], outs[w].at[me], local_sems.at[w])

        def copy(w, r, block_here, block_there):
            return pltpu.make_async_remote_copy(
                src_ref=ins[w].at[block_here], dst_ref=outs[w].at[block_there], send_sem=send_sems.at[7 * w + r],
                recv_sem=recv_sems.at[7 * w + r], device_id=peers[r], device_id_type=MESH)

        def send():
            for w in range(n):
                mine(w).start()
                for r, peer in enumerate(peers):
                    copy(w, r, _slot(*peer), me).start()

        def finish():
            for w in range(n):
                for r, peer in enumerate(peers):
                    copy(w, r, me, _slot(*peer)).wait_recv()
            for w in range(n):
                for r, peer in enumerate(peers):
                    copy(w, r, _slot(*peer), me).wait_send()
                mine(w).wait()

        return [send, finish]

    return _Exchange(parts, [jax.ShapeDtypeStruct(a.shape, a.dtype) for a in parts],
                     [pltpu.SemaphoreType.DMA((7 * n,)), pltpu.SemaphoreType.DMA((7 * n,)),
                      pltpu.SemaphoreType.DMA((n,))], phases, deliver)


def _adamw(w, g, m, v):
    m = ADAM_B1 * m + (1.0 - ADAM_B1) * g
    v = ADAM_B2 * v + (1.0 - ADAM_B2) * (g * g)
    m_hat = m / (1.0 - ADAM_B1 ** ADAM_STEP)
    v_hat = v / (1.0 - ADAM_B2 ** ADAM_STEP)
    delta = -ADAM_LR * (m_hat / (jnp.sqrt(v_hat) + ADAM_EPS) + ADAM_WD * w)
    return delta, m, v


def _adamw_sharded(recvs, w, m, v, tr, name, exchange=None):
    _, r, cdim = w.shape
    nr = r // tr

    def body(r0_ref, r1_ref, w_ref, m_ref, v_ref, g_ref, d_ref, nm_ref, nv_ref):
        l = pl.program_id(0)

        def total(ref):
            acc = ref[0].astype(F32)
            for k in range(1, NDEV):
                acc = acc + ref[k].astype(F32)
            return acc

        g = jnp.where(l == 0, total(r0_ref), total(r1_ref))
        d, nm, nv = _adamw(w_ref[...], g, m_ref[...], v_ref[...])
        g_ref[...] = g
        d_ref[...] = d
        nm_ref[...] = nm
        nv_ref[...] = nv

    mine = pl.BlockSpec((None, tr, cdim), lambda l, i: (l, i, 0))
    out = jax.ShapeDtypeStruct(w.shape, F32)
    return _call(body, name, (DEPTH, nr),
                 [pl.BlockSpec((NDEV, tr, cdim), lambda l, i: (0, jnp.where(l == 0, i, nr - 1), 0)),
                  pl.BlockSpec((NDEV, tr, cdim), lambda l, i: (0, jnp.where(l == 1, i, 0), 0)),
                  mine, mine, mine],
                 [mine] * 4, [out] * 4, exchange=exchange)(recvs[0], recvs[1], w, m, v)


def _adamw_small(parts, w, m, v, name):
    r = w.shape[0]

    def body(p_ref, w_ref, m_ref, v_ref, g_ref, d_ref, nm_ref, nv_ref):
        g = p_ref[0]
        for k in range(1, NDEV):
            g = g + p_ref[k]
        d, nm, nv = _adamw(w_ref[...], g, m_ref[...], v_ref[...])
        g_ref[...] = g
        d_ref[...] = d
        nm_ref[...] = nm
        nv_ref[...] = nv

    whole = pl.BlockSpec((r, 1024), _fixed2)
    out = jax.ShapeDtypeStruct((r, 1024), F32)
    return _call(body, name, (1,), [pl.BlockSpec((NDEV, r, 1024), lambda i: (0, 0, 0)), whole, whole, whole],
                 [whole] * 4, [out] * 4)(parts, w, m, v)


SMALL = (("norm_mix_pre", (D,)), ("b_gate", (2 * D,)), ("rel_bias", (HEADS, NREL)), ("w_pool_group", (4, PG, PG)),
         ("pool_scale", (PW,)), ("norm_mix_post", (D,)), ("norm_ffn_pre", (D,)), ("conv_b", (NFF * 2 * FS,)),
         ("norm_ffn_post", (D,)))
SHARDED = (("w_in", 256), ("w_attn_out", 512), ("w_pool_out", 512), ("w_o", 128), ("w_up", 256), ("conv_w", 3),
           ("w_down", 176))
SMALL_ROWS = 168


def _pack_small(tree):
    flat = jnp.concatenate([tree[name].reshape(-1) for name, _ in SMALL])
    return jnp.pad(flat, (0, SMALL_ROWS * 1024 - flat.shape[0])).reshape(SMALL_ROWS, 1024)


def _unpack_small(packed):
    flat = packed.reshape(-1)
    out, at = {}, 0
    for name, shape in SMALL:
        size = DEPTH * int(np.prod(shape))
        out[name] = flat[at:at + size].reshape((DEPTH,) + shape)
        at += size
    return out


def _layer_fwd(l, x, h, p, gath, target, gnext, carry):
    n = f"l{l}"
    g = lambda name: p[name][l].reshape(1, -1)
    res = {"x": x, "h": h}
    res["tab"] = _bias_table(p["rel_bias"][l], f"bias_table_{n}")
    proj = _mm_in(h, gath["w_in"], f"mm_in_{n}", carry.get("mm_in"))
    att = _attn_fwd(proj, res["tab"], f"attn_fwd_{n}", carry.get("attn_fwd"))
    pooled, mixed = _pool_fwd(proj, p["w_pool_group"][l], g("pool_scale"), f"pool_fwd_{n}")
    bgate = g("b_gate")
    z, ya, yb = _branch_fwd(att, mixed, gath["w_attn_out"], gath["w_pool_out"], proj, bgate, f"branch_fwd_{n}")
    mix, x1, h2 = _mm_o_fwd(z, gath["w_o"], x, g("norm_mix_post"), g("norm_ffn_pre"), f"mm_o_fwd_{n}")
    cw = gath["conv_w"].reshape(2, NFF, 3, FS)
    cb = p["conv_b"][l].reshape(2, NFF, 1, FS)
    hu, a = _mm_up_conv_gate(h2, gath["w_up"], cw, cb, f"mm_up_conv_gate_{n}", carry.get("mm_up_conv_gate"))
    hu = hu.reshape(2, NFF, -1, FS)
    res.update(proj=proj, att=att, pooled=pooled, mixed=mixed, z=z, ya=ya, yb=yb, mix=mix, x1=x1, h2=h2, hu=hu,
               a=a, cw=cw, cb=cb, bgate=bgate)
    if target is None:
        f, x2, hn = _mm_down_fwd(a, gath["w_down"], x1, g("norm_ffn_post"), gnext, f"mm_down_fwd_{n}",
                                 carry.get("mm_down_fwd"))
        res["f"] = f
        return res, x2, hn
    f, dy, loss = _mm_down_loss(a, gath["w_down"], x1, g("norm_ffn_post"), target, f"mm_down_loss_{n}")
    res["f"] = f
    return res, dy, loss


def _layer_bwd(l, dx2, p, gath, res, carry):
    n = f"l{l}"
    g = lambda name: p[name][l].reshape(1, -1)
    big, small = {}, {}
    taken = lambda call: carry[call](big) if call in carry else None
    df, small["norm_ffn_post"] = _norm_bwd(res["f"], g("norm_ffn_post"), dx2, f"norm_bwd_{n}")
    dhu, dcw, dcb = _conv_gate_bwd(res["hu"], df, gath["w_down"], res["cw"], res["cb"], f"conv_gate_bwd_{n}",
                                   taken("conv_gate_bwd"))
    big["conv_w"] = dcw.reshape(NDEV, 3, FS)
    small["conv_b"] = dcb
    big["w_down"] = _dw_down(res["a"], df, f"dw_down_{n}")
    dx1, dmix, small["norm_ffn_pre"], small["norm_mix_post"] = _mm_dh2(
        dhu, gath["w_up"], res["x1"], g("norm_ffn_pre"), dx2, res["mix"], g("norm_mix_post"), f"mm_dh2_{n}")
    big["w_up"] = _dw_up(res["h2"], dhu, f"dw_up_{n}")
    dya, dyb, dgates, small["b_gate"] = _mm_dz(dmix, gath["w_o"], res["proj"], res["bgate"], res["ya"], res["yb"],
                                                f"mm_dz_{n}")
    big["w_o"] = _dw_o(res["z"], dmix, f"dw_o_{n}")
    datt, dmixed = _branch_bwd(dya, dyb, gath["w_attn_out"], gath["w_pool_out"], f"branch_bwd_{n}")
    big["w_attn_out"], big["w_pool_out"] = _dw_branch(res["att"], res["mixed"], dya, dyb, f"dw_branch_{n}")
    dq, dk, dv, dtab = _attn_bwd(res["proj"], datt, res["tab"], f"attn_bwd_{n}", taken("attn_bwd"))
    small["rel_bias"] = _rel_bias_grad(dtab, f"rel_bias_grad_{n}")[:, 0, :NREL]
    du, small["w_pool_group"], small["pool_scale"] = _pool_bwd(
        dmixed, res["pooled"], p["w_pool_group"][l], g("pool_scale"), f"pool_bwd_{n}")
    segs = (dq, dk, dv, du, dgates)
    big["w_in"] = _dw_in(res["h"], segs, f"dw_in_{n}")
    dx, small["norm_mix_pre"] = _mm_dh(segs, gath["w_in"], res["x"], g("norm_mix_pre"), dx1, f"mm_dh_{n}",
                                       taken("mm_dh"))
    return dx, big, small


def _gather_weights(gath, l, p, names):
    shards = [p[k][l] if k == "conv_w" else p[k][l].astype(BF16) for k in names]

    def deliver(results):
        for k, a in zip(names, results):
            if k in ("w_attn_out", "w_pool_out"):
                a = jnp.transpose(a, (1, 0, 2)).reshape(AW, D)
            elif k == "w_o":
                a = a.reshape(D, D)
            elif k == "w_down":
                a = a.reshape(NFF * FS, D)
            gath[k] = a

    return _gather_exchange(shards, deliver)


def _scatter_grads(recv, big, names):
    return _scatter_exchange([big[k] for k in names], lambda results: recv.update(zip(names, results)))


def kernel(x, norm_mix_pre, w_in, b_gate, rel_bias, w_attn_out, w_pool_group, pool_scale, w_pool_out, w_o, norm_mix_post, norm_ffn_pre, w_up, conv_w, conv_b, w_down, norm_ffn_post, loss_target, m_norm_mix_pre, m_w_in, m_b_gate, m_rel_bias, m_w_attn_out, m_w_pool_group, m_pool_scale, m_w_pool_out, m_w_o, m_norm_mix_post, m_norm_ffn_pre, m_w_up, m_conv_w, m_conv_b, m_w_down, m_norm_ffn_post, v_norm_mix_pre, v_w_in, v_b_gate, v_rel_bias, v_w_attn_out, v_w_pool_group, v_pool_scale, v_w_pool_out, v_w_o, v_norm_mix_post, v_norm_ffn_pre, v_w_up, v_conv_w, v_conv_b, v_w_down, v_norm_ffn_post):
    names = ("norm_mix_pre", "w_in", "b_gate", "rel_bias", "w_attn_out", "w_pool_group", "pool_scale", "w_pool_out",
             "w_o", "norm_mix_post", "norm_ffn_pre", "w_up", "conv_w", "conv_b", "w_down", "norm_ffn_post")
    p = dict(zip(names, (norm_mix_pre, w_in, b_gate, rel_bias, w_attn_out, w_pool_group, pool_scale, w_pool_out, w_o,
                         norm_mix_post, norm_ffn_pre, w_up, conv_w, conv_b, w_down, norm_ffn_post)))
    mom = dict(zip(names, (m_norm_mix_pre, m_w_in, m_b_gate, m_rel_bias, m_w_attn_out, m_w_pool_group, m_pool_scale,
                           m_w_pool_out, m_w_o, m_norm_mix_post, m_norm_ffn_pre, m_w_up, m_conv_w, m_conv_b, m_w_down,
                           m_norm_ffn_post)))
    var = dict(zip(names, (v_norm_mix_pre, v_w_in, v_b_gate, v_rel_bias, v_w_attn_out, v_w_pool_group, v_pool_scale,
                           v_w_pool_out, v_w_o, v_norm_mix_post, v_norm_ffn_pre, v_w_up, v_conv_w, v_conv_b, v_w_down,
                           v_norm_ffn_post)))
    s = x.shape[1]
    xs = x.reshape(s, D)
    target = loss_target.reshape(s, D)

    gath = [{}, {}]
    rest = ("w_attn_out", "w_pool_out", "w_o", "w_down", "conv_w")
    h0 = _rmsnorm_call(xs, p["norm_mix_pre"][0].reshape(1, D), "rmsnorm_l0",
                       _gather_weights(gath[0], 0, p, ("w_in",)))
    res0, x2, h1 = _layer_fwd(0, xs, h0, p, gath[0], None, p["norm_mix_pre"][1].reshape(1, D), {
        "mm_in": _gather_weights(gath[0], 0, p, rest),
        "attn_fwd": _gather_weights(gath[0], 0, p, ("w_up",)),
        "mm_up_conv_gate": _gather_weights(gath[1], 1, p, ("w_in",) + rest),
        "mm_down_fwd": _gather_weights(gath[1], 1, p, ("w_up",))})
    res1, dy, loss_part = _layer_fwd(1, x2, h1, p, gath[1], target, None, {})

    order = [k for k, _ in SHARDED]
    recv = [{}, {}]
    dx2, big1, small1 = _layer_bwd(1, dy, p, gath[1], res1, {})
    grad_x, big0, small0 = _layer_bwd(0, dx2, p, gath[0], res0, {
        "conv_gate_bwd": lambda big: _scatter_grads(recv[1], big1, order),
        "attn_bwd": lambda big: _scatter_grads(recv[0], big, [k for k in order if k != "w_in"]),
        "mm_dh": lambda big: _scatter_grads(recv[0], big, ["w_in"])})

    loss = lax.psum(loss_part[0, 0], ("x", "y", "c"))

    small = {k: jnp.stack([small0[k].reshape(shape), small1[k].reshape(shape)]) for k, shape in SMALL}
    parts = []
    carried = {"w_up": _gather_exchange([_pack_small(small)], parts.extend)}
    out = {}
    for k, tr in sorted(SHARDED, key=lambda kt: kt[0] not in carried):
        out[k] = _adamw_sharded((recv[0][k], recv[1][k]), p[k], mom[k], var[k], tr, f"adamw_{k}", carried.get(k))
    packed = _adamw_small(parts[0], _pack_small(p), _pack_small(mom), _pack_small(var), "adamw_small")
    unpacked = [_unpack_small(a) for a in packed]
    for k, _ in SMALL:
        out[k] = tuple(u[k] for u in unpacked)

    return (loss, grad_x.reshape(x.shape), *[out[k][0] for k in names], *[out[k][1] for k in names],
            *[out[k][2] for k in names], *[out[k][3] for k in names])
```

```python
import numpy as np
import jax
import jax.numpy as jnp
from jax import lax
from jax.experimental import pallas as pl
from jax.experimental.pallas import tpu as pltpu

F32, BF16 = jnp.float32, jnp.bfloat16

D = 1024
AW = 512
PW = 512
PG = 128
INW = 4096
FS = 704
NFF = 4
NDEV = 8
DEPTH = 2
HEADS = 8
NREL = 513
MAXREL = 256
POOL_WINDOWS = (2, 4, 8, 16)
EPS = 1e-6
SCALE = 0.125
NEG = -1e30
QB = 256
KW = 3 * QB
BAND = 576
ADAM_LR, ADAM_B1, ADAM_B2, ADAM_EPS, ADAM_WD, ADAM_STEP = 0.001, 0.9, 0.999, 1e-08, 0.01, 10
VMEM_LIMIT_V7X = 56 * 1024 * 1024
MESH = pl.DeviceIdType.MESH
GELU_C = 0.7978845608028654
GELU_A = 0.044715


HBM = pl.BlockSpec(memory_space=pltpu.HBM)


class _Exchange:
    def __init__(self, operands, out_shape, scratch, phases, deliver):
        self.operands, self.out_shape, self.scratch = list(operands), list(out_shape), list(scratch)
        self.phases, self.deliver = phases, deliver


def _call(body, name, grid, in_specs, out_specs, out_shape, scratch=(), exchange=None):
    params = pltpu.CompilerParams(vmem_limit_bytes=VMEM_LIMIT_V7X)
    if exchange is None:
        return pl.pallas_call(body, name=name, grid=grid, in_specs=in_specs, out_specs=out_specs, out_shape=out_shape,
                              scratch_shapes=list(scratch), compiler_params=params)
    single = not isinstance(out_shape, (list, tuple))
    outs, ospecs = ([out_shape], [out_specs]) if single else (list(out_shape), list(out_specs))
    n_in, n_out, n_scr = len(in_specs), len(outs), len(scratch)
    ne_in, ne_out = len(exchange.operands), len(exchange.out_shape)
    nsteps = int(np.prod(grid))

    def carried(*refs):
        cut = np.cumsum([0, n_in, ne_in, n_out, ne_out, n_scr])
        base_in, ex_in, base_out, ex_out, base_scr = (refs[cut[k]:cut[k + 1]] for k in range(5))
        step = pl.program_id(0)
        for axis in range(1, len(grid)):
            step = step * grid[axis] + pl.program_id(axis)
        phases = exchange.phases(ex_in, ex_out, refs[cut[5]:])
        pl.when(step == 0)(phases[0])
        body(*base_in, *base_out, *base_scr)
        at = {2: [nsteps - 1], 3: [(7 * nsteps) // 10, nsteps - 1]}[len(phases)]
        for phase, when in zip(phases[1:], at):
            pl.when(step == when)(phase)

    call = pl.pallas_call(
        carried, name=name, grid=grid, in_specs=list(in_specs) + [HBM] * ne_in, out_specs=ospecs + [HBM] * ne_out,
        out_shape=outs + exchange.out_shape, scratch_shapes=list(scratch) + exchange.scratch, compiler_params=params)

    def run(*args):
        res = call(*args, *exchange.operands)
        exchange.deliver(res[n_out:])
        return res[0] if single else res[:n_out]

    return run


def _dot(a, b):
    return jnp.dot(a, b, preferred_element_type=F32)


def _dot_nt(a, b):
    return lax.dot_general(a, b, (((1,), (1,)), ((), ())), preferred_element_type=F32)


def _dot_tn(a, b):
    return lax.dot_general(a, b, (((0,), (0,)), ((), ())), preferred_element_type=F32)


def _rms(x, g):
    r = lax.rsqrt(jnp.mean(x * x, axis=-1, keepdims=True) + EPS)
    return x * r * g


def _rms_bwd(x, g, dy):
    r = lax.rsqrt(jnp.mean(x * x, axis=-1, keepdims=True) + EPS)
    xh = x * r
    dg = jnp.sum(dy * xh, axis=0, keepdims=True)
    dxh = dy * g
    dx = r * (dxh - xh * jnp.mean(dxh * xh, axis=-1, keepdims=True))
    return dx, dg


def _sigmoid(x):
    return 1.0 / (1.0 + jnp.exp(-x))


def _gelu_parts(x):
    x2 = x * x
    th = jnp.tanh(x * (GELU_C + (GELU_C * GELU_A) * x2))
    s = 0.5 * th + 0.5
    dg = s * (1.0 + x * (1.0 - s) * (2.0 * GELU_C + (6.0 * GELU_C * GELU_A) * x2))
    return x * s, dg


def _row(i):
    return (i, 0)


def _fixed2(*_):
    return (0, 0)


def _rmsnorm_call(x, g, name, exchange=None):
    s = x.shape[0]
    tm = 512

    def body(x_ref, g_ref, o_ref):
        o_ref[...] = _rms(x_ref[...], g_ref[...]).astype(BF16)

    return _call(body, name, (s // tm,),
                 [pl.BlockSpec((tm, D), _row), pl.BlockSpec((1, D), _fixed2)],
                 pl.BlockSpec((tm, D), _row), jax.ShapeDtypeStruct((s, D), BF16), exchange=exchange)(x, g)


def _resident(shape):
    return pl.BlockSpec(shape, lambda *_: (0,) * len(shape), pipeline_mode=pl.Buffered(1))


def _mm_in(h, win, name, exchange=None):
    s = h.shape[0]
    tm = 512

    def body(h_ref, w_ref, o_ref):
        hv = h_ref[...]
        for j in range(NDEV):
            o_ref[:, j * 512:(j + 1) * 512] = _dot(hv, w_ref[j])

    return _call(body, name, (s // tm,),
                 [pl.BlockSpec((tm, D), _row), _resident((NDEV, D, 512))],
                 pl.BlockSpec((tm, INW), _row), jax.ShapeDtypeStruct((s, INW), F32), exchange=exchange)(h, win)


def _bias_table(rel_bias, name):
    wdt = 1024
    rel = jnp.pad(rel_bias, ((0, 0), (0, 640 - NREL))).reshape(HEADS, 1, 640)

    def body(r_ref, o_ref):
        rr = lax.broadcasted_iota(jnp.int32, (640, wdt), 0)
        m = lax.broadcasted_iota(jnp.int32, (640, wdt), 1)
        d = jnp.where(m < KW, m, m - wdt)
        onehot = (jnp.clip(512 - d, -MAXREL, MAXREL) + MAXREL == rr).astype(F32)
        row = jnp.dot(jnp.broadcast_to(r_ref[...], (8, 640)), onehot, preferred_element_type=F32,
                      precision=lax.Precision.HIGHEST)[0:1]
        t = pltpu.roll(jnp.broadcast_to(row, (QB, wdt)), 0, 1, stride=1, stride_axis=0)[:, 0:KW]
        qc = lax.broadcasted_iota(jnp.int32, (QB, KW), 0) // 64
        kc = lax.broadcasted_iota(jnp.int32, (QB, KW), 1) // 64
        o_ref[...] = jnp.where((kc >= qc) & (kc <= qc + 8), t, NEG)

    return _call(body, name, (HEADS,), [pl.BlockSpec((None, 1, 640), lambda h: (h, 0, 0))],
                 pl.BlockSpec((None, QB, KW), lambda h: (h, 0, 0)),
                 jax.ShapeDtypeStruct((HEADS, QB, KW), F32))(rel)


def _attn_probs(q_ref, k_refs, b_ref, i):
    lane = lax.broadcasted_iota(jnp.int32, (QB, 128), 1)
    q = q_ref[...] * SCALE
    qs = [jnp.where(lane < 64, q, 0.0).astype(BF16), jnp.where(lane >= 64, q, 0.0).astype(BF16)]
    k = jnp.concatenate([r[...] for r in k_refs], axis=0).astype(BF16)
    colb = lax.broadcasted_iota(jnp.int32, (1, KW), 1) // QB
    before = jnp.where(colb + i >= 2, 0.0, NEG)
    ps = []
    for hh in (0, 1):
        sc = _dot_nt(qs[hh], k) + b_ref[hh] + before
        e = jnp.exp(sc - jnp.max(sc, axis=1, keepdims=True))
        ps.append((e, 1.0 / jnp.sum(e, axis=1, keepdims=True)))
    return qs, ps, k, lane


def _attn_fwd(proj, tab, name, exchange=None):
    s = proj.shape[0]
    nq = s // QB

    def body(q_ref, k0, k1, k2, v0, v1, v2, b_ref, o_ref):
        i = pl.program_id(1)
        _, ps, _, lane = _attn_probs(q_ref, (k0, k1, k2), b_ref, i)
        v = jnp.concatenate([v0[...], v1[...], v2[...]], axis=0).astype(BF16)
        o = [_dot(e.astype(BF16), v) * linv for e, linv in ps]
        o_ref[...] = jnp.where(lane < 64, o[0], o[1]).astype(BF16)

    def kv(col, d):
        return pl.BlockSpec((QB, 128), lambda p, i: (jnp.maximum(i - 2 + d, 0), col + p))

    in_specs = [pl.BlockSpec((QB, 128), lambda p, i: (i, p))]
    in_specs += [kv(4, d) for d in range(3)] + [kv(8, d) for d in range(3)]
    in_specs += [pl.BlockSpec((2, QB, KW), lambda p, i: (p, 0, 0))]
    return _call(body, name, (4, nq), in_specs, pl.BlockSpec((QB, 128), lambda p, i: (i, p)),
                 jax.ShapeDtypeStruct((s, AW), BF16), exchange=exchange)(proj, proj, proj, proj, proj, proj, proj, tab)


def _pool_fwd(proj, wg, scale, name):
    s = proj.shape[0]
    tb = 512
    e = tb + 16

    def body(u_ref, halo_ref, wg_ref, sc_ref, pooled_ref, mixed_ref):
        i = pl.program_id(0)
        cur = u_ref[...]
        prev = jnp.where(i > 0, halo_ref[...], 0.0)
        xs = jnp.concatenate([prev, cur], axis=0)
        t = i * tb + lax.broadcasted_iota(jnp.int32, (tb, 1), 0)
        for g, w in enumerate(POOL_WINDOWS):
            sl = slice(g * PG, (g + 1) * PG)
            a = xs[:, sl]
            sh = 1
            while sh < w:
                a = a + pltpu.roll(a, sh, 0)
                sh *= 2
            cnt = jnp.minimum(t + 1, w).astype(F32)
            pooled = (a[16:] / cnt - cur[:, sl]).astype(BF16)
            pooled_ref[:, sl] = pooled
            mixed_ref[:, sl] = (_dot(pooled, wg_ref[g].astype(BF16)) * sc_ref[:, sl]).astype(BF16)

    assert e % 8 == 0
    return _call(body, name, (s // tb,),
                 [pl.BlockSpec((tb, PW), lambda i: (i, 3)),
                  pl.BlockSpec((16, PW), lambda i: (jnp.maximum(i * (tb // 16) - 1, 0), 3)),
                  pl.BlockSpec((4, PG, PG), lambda i: (0, 0, 0)), pl.BlockSpec((1, PW), _fixed2)],
                 [pl.BlockSpec((tb, PW), _row), pl.BlockSpec((tb, PW), _row)],
                 [jax.ShapeDtypeStruct((s, PW), BF16), jax.ShapeDtypeStruct((s, PW), BF16)])(proj, proj, wg, scale)


def _branch_fwd(att, mixed, wao, wpo, proj, bgate, name):
    s = att.shape[0]
    tm = 512

    def body(att_ref, mx_ref, wao_ref, wpo_ref, ga_ref, gb_ref, ba_ref, bb_ref, z_ref, ya_ref, yb_ref):
        ya = _dot(att_ref[...], wao_ref[...])
        yb = _dot(mx_ref[...], wpo_ref[...])
        ga = _sigmoid(ga_ref[...] + ba_ref[...])
        gb = _sigmoid(gb_ref[...] + bb_ref[...])
        ya_ref[...] = ya.astype(BF16)
        yb_ref[...] = yb.astype(BF16)
        z_ref[...] = (ga * ya + gb * yb).astype(BF16)

    return _call(body, name, (s // tm,),
                 [pl.BlockSpec((tm, AW), _row), pl.BlockSpec((tm, PW), _row),
                  pl.BlockSpec((AW, D), _fixed2), pl.BlockSpec((PW, D), _fixed2),
                  pl.BlockSpec((tm, D), lambda i: (i, 2)), pl.BlockSpec((tm, D), lambda i: (i, 3)),
                  pl.BlockSpec((1, D), lambda i: (0, 0)), pl.BlockSpec((1, D), lambda i: (0, 1))],
                 [pl.BlockSpec((tm, D), _row)] * 3,
                 [jax.ShapeDtypeStruct((s, D), BF16)] * 3)(att, mixed, wao, wpo, proj, proj, bgate, bgate)


def _mm_o_fwd(z, wo, x, g2, g3, name):
    s = z.shape[0]
    tm = 512

    def body(z_ref, w_ref, x_ref, g2_ref, g3_ref, mix_ref, x1_ref, h2_ref):
        mix = _dot(z_ref[...], w_ref[...])
        x1 = x_ref[...] + _rms(mix, g2_ref[...])
        mix_ref[...] = mix
        x1_ref[...] = x1
        h2_ref[...] = _rms(x1, g3_ref[...]).astype(BF16)

    return _call(body, name, (s // tm,),
                 [pl.BlockSpec((tm, D), _row), pl.BlockSpec((D, D), _fixed2), pl.BlockSpec((tm, D), _row),
                  pl.BlockSpec((1, D), _fixed2), pl.BlockSpec((1, D), _fixed2)],
                 [pl.BlockSpec((tm, D), _row)] * 3,
                 [jax.ShapeDtypeStruct((s, D), F32), jax.ShapeDtypeStruct((s, D), F32),
                  jax.ShapeDtypeStruct((s, D), BF16)])(z, wo, x, g2, g3)


def _mm_up_conv_gate(h2, wup, cw, cb, name, exchange=None):
    s = h2.shape[0]
    tm = 256

    def body(h_ref, w_ref, cw_ref, cb_ref, hu_ref, a_ref, halo):
        first = pl.program_id(0) == 0
        hv = h_ref[...]
        for j in range(NDEV):
            hu_ref[j] = _dot(hv, w_ref[j])
        for jj in range(NFF):
            for c0, w in LANE_COLUMNS:
                lanes = pl.ds(c0, w)
                cwb = [[jnp.broadcast_to(cw_ref[sd, jj, t:t + 1, lanes], (8, w)) for t in range(3)] for sd in (0, 1)]
                cbb = [jnp.broadcast_to(cb_ref[sd, jj, :, lanes], (8, w)) for sd in (0, 1)]
                row = lax.broadcasted_iota(jnp.int32, (8, w), 0)
                rolled = []
                for sd in (0, 1):
                    before = jnp.where(first, 0.0, halo[NFF * sd + jj, :, lanes])
                    rolled += [pltpu.roll(before, 1, 0), pltpu.roll(before, 2, 0)]
                for k in range(tm // 16):
                    outs = []
                    for r in (16 * k, 16 * k + 8):
                        hc, keep = [], []
                        for sd in (0, 1):
                            cur = hu_ref[NFF * sd + jj, pl.ds(r, 8), lanes]
                            r1, r2 = pltpu.roll(cur, 1, 0), pltpu.roll(cur, 2, 0)
                            hc.append(cbb[sd] + cwb[sd][2] * cur
                                      + cwb[sd][0] * jnp.where(row >= 2, r2, rolled[2 * sd + 1])
                                      + cwb[sd][1] * jnp.where(row >= 1, r1, rolled[2 * sd]))
                            keep += [r1, r2]
                        rolled = keep
                        outs.append(_gelu_parts(hc[1])[0] * hc[0])
                    a_ref[jj, pl.ds(16 * k, 16), lanes] = jnp.concatenate(outs, axis=0).astype(BF16)
        for j in range(NDEV):
            halo[j] = hu_ref[j, tm - 8:tm, :]

    small = lambda i: (0, 0, 0, 0)
    return _call(body, name, (s // tm,),
                 [pl.BlockSpec((tm, D), _row), _resident((NDEV, D, FS)), pl.BlockSpec((2, NFF, 3, FS), small),
                  pl.BlockSpec((2, NFF, 1, FS), small)],
                 [pl.BlockSpec((NDEV, tm, FS), lambda i: (0, i, 0)), pl.BlockSpec((NFF, tm, FS), lambda i: (0, i, 0))],
                 [jax.ShapeDtypeStruct((NDEV, s, FS), F32), jax.ShapeDtypeStruct((NFF, s, FS), BF16)],
                 [pltpu.VMEM((NDEV, 8, FS), F32)], exchange=exchange)(h2, wup, cw, cb)


LANE_COLUMNS = [(c0, min(128, FS - c0)) for c0 in range(0, FS, 128)]


def _mm_down_fwd(a, wd, x1, g4, gnext, name, exchange=None):
    s = a.shape[1]
    tm = 512

    def body(a_ref, w_ref, x1_ref, g4_ref, gn_ref, f_ref, x2_ref, hn_ref):
        f = _dot(a_ref[0], w_ref[0:FS, :])
        for j in range(1, NFF):
            f = f + _dot(a_ref[j], w_ref[j * FS:(j + 1) * FS, :])
        x2 = x1_ref[...] + _rms(f, g4_ref[...])
        f_ref[...] = f
        x2_ref[...] = x2
        hn_ref[...] = _rms(x2, gn_ref[...]).astype(BF16)

    return _call(body, name, (s // tm,),
                 [pl.BlockSpec((NFF, tm, FS), lambda i: (0, i, 0)), pl.BlockSpec((NFF * FS, D), _fixed2),
                  pl.BlockSpec((tm, D), _row), pl.BlockSpec((1, D), _fixed2), pl.BlockSpec((1, D), _fixed2)],
                 [pl.BlockSpec((tm, D), _row)] * 3,
                 [jax.ShapeDtypeStruct((s, D), F32), jax.ShapeDtypeStruct((s, D), F32),
                  jax.ShapeDtypeStruct((s, D), BF16)], exchange=exchange)(a, wd, x1, g4, gnext)


def _mm_down_loss(a, wd, x1, g4, target, name):
    s = a.shape[1]
    tm = 512

    def body(a_ref, w_ref, x1_ref, g4_ref, t_ref, dy_ref, df_ref, dg_ref, loss_ref):
        i = pl.program_id(0)
        f = _dot(a_ref[0], w_ref[0:FS, :])
        for j in range(1, NFF):
            f = f + _dot(a_ref[j], w_ref[j * FS:(j + 1) * FS, :])
        err = x1_ref[...] + _rms(f, g4_ref[...]) - t_ref[...]
        dy = err * (1.0 / D)
        df, dg = _rms_bwd(f, g4_ref[...], dy)
        dy_ref[...] = dy
        df_ref[...] = df.astype(BF16)
        part = 0.5 * jnp.sum(jnp.mean(err * err, axis=-1, keepdims=True), axis=0, keepdims=True)

        @pl.when(i == 0)
        def _():
            loss_ref[...] = jnp.zeros_like(loss_ref)
            dg_ref[...] = jnp.zeros_like(dg_ref)

        loss_ref[...] += jnp.broadcast_to(part, loss_ref.shape)
        dg_ref[...] += dg

    return _call(body, name, (s // tm,),
                 [pl.BlockSpec((NFF, tm, FS), lambda i: (0, i, 0)), pl.BlockSpec((NFF * FS, D), _fixed2),
                  pl.BlockSpec((tm, D), _row), pl.BlockSpec((1, D), _fixed2), pl.BlockSpec((tm, D), _row)],
                 [pl.BlockSpec((tm, D), _row), pl.BlockSpec((tm, D), _row), pl.BlockSpec((1, D), _fixed2),
                  pl.BlockSpec((8, 128), _fixed2)],
                 [jax.ShapeDtypeStruct((s, D), F32), jax.ShapeDtypeStruct((s, D), BF16),
                  jax.ShapeDtypeStruct((1, D), F32), jax.ShapeDtypeStruct((8, 128), F32)])(a, wd, x1, g4, target)


def _conv_gate_bwd(hu, df, wd, cw, cb, name, exchange=None):
    s = hu.shape[2]
    tb = 512
    nt = s // tb
    e = tb + 16

    def body(hu_ref, prev_ref, next_ref, df_ref, dfn_ref, wd_ref, cw_ref, cb_ref, dhu_ref, dcw_ref, dcb_ref,
             dabuf, dbuf):
        i = pl.program_id(1)
        first, last = i == 0, i == nt - 1
        dabuf[0:tb] = _dot_nt(df_ref[...], wd_ref[...])
        dabuf[tb:tb + 8] = jnp.where(last, 0.0, _dot_nt(dfn_ref[...], wd_ref[...])[0:8])

        @pl.when(i == 0)
        def _():
            dcw_ref[...] = jnp.zeros_like(dcw_ref)
            dcb_ref[...] = jnp.zeros_like(dcb_ref)

        for c0, w in LANE_COLUMNS:
            lanes = pl.ds(c0, w)
            cwb = [[jnp.broadcast_to(cw_ref[sd, t:t + 1, lanes], (8, w)) for t in range(3)] for sd in (0, 1)]
            cbb = [jnp.broadcast_to(cb_ref[sd, :, lanes], (8, w)) for sd in (0, 1)]

            row = lax.broadcasted_iota(jnp.int32, (8, w), 0)

            def tile(k, carry, summed=True):
                sums, rolled = carry[:8], carry[8:]
                r = 8 + 8 * k
                xs, keep = [], []
                for sd in (0, 1):
                    if summed:
                        cur = hu_ref[sd, pl.ds(r - 8, 8), lanes]
                    else:
                        cur = jnp.where(last, 0.0, next_ref[sd, :, lanes])
                    r1, r2 = pltpu.roll(cur, 1, 0), pltpu.roll(cur, 2, 0)
                    xs.append([jnp.where(row >= 2, r2, rolled[2 * sd + 1]), jnp.where(row >= 1, r1, rolled[2 * sd]),
                               cur])
                    keep += [r1, r2]
                hc = [cbb[sd] + cwb[sd][2] * xs[sd][2] + cwb[sd][0] * xs[sd][0] + cwb[sd][1] * xs[sd][1]
                      for sd in (0, 1)]
                da = dabuf[pl.ds(r - 8, 8), lanes]
                gl, dgl = _gelu_parts(hc[1])
                dhc = (da * gl, da * hc[0] * dgl)
                for sd in (0, 1):
                    dbuf[sd, pl.ds(r, 8), lanes] = dhc[sd]
                if not summed:
                    return carry
                new = []
                for sd in (0, 1):
                    new += [sums[4 * sd + t] + dhc[sd] * xs[sd][t] for t in range(3)] + [sums[4 * sd + 3] + dhc[sd]]
                return tuple(new + keep)

            start = [jnp.zeros((8, w), F32) for _ in range(8)]
            for sd in (0, 1):
                halo = jnp.where(first, 0.0, prev_ref[sd, :, lanes])
                start += [pltpu.roll(halo, 1, 0), pltpu.roll(halo, 2, 0)]
            def tiles(k4, carry):
                for u in range(4):
                    carry = tile(4 * k4 + u, carry)
                return carry

            carry = tuple(start)
            for k4 in range(tb // 32):
                carry = tiles(k4, carry)
            tile(tb // 8, carry, summed=False)
            sums = carry[:8]

            def up(v):
                return pltpu.roll(v, 7, 0), pltpu.roll(v, 6, 0)

            def out_tile(k, carry):
                r = 8 + 16 * k
                new = []
                for sd in (0, 1):
                    va, va1, va2 = carry[3 * sd:3 * sd + 3]
                    vb, vc = dbuf[sd, pl.ds(r + 8, 8), lanes], dbuf[sd, pl.ds(r + 16, 8), lanes]
                    (vb1, vb2), (vc1, vc2) = up(vb), up(vc)
                    c0b, c1b, c2b = cwb[sd]
                    top = c2b * va + c1b * jnp.where(row <= 6, va1, vb1) + c0b * jnp.where(row <= 5, va2, vb2)
                    bot = c2b * vb + c1b * jnp.where(row <= 6, vb1, vc1) + c0b * jnp.where(row <= 5, vb2, vc2)
                    dhu_ref[sd, pl.ds(16 * k, 16), lanes] = jnp.concatenate(
                        [top, bot], axis=0).astype(BF16)
                    new += [vc, vc1, vc2]
                return tuple(new)

            begin = []
            for sd in (0, 1):
                va = dbuf[sd, 8:16, lanes]
                begin += [va, *up(va)]
            def out_tiles(k2, carry):
                return out_tile(2 * k2 + 1, out_tile(2 * k2, carry))

            carry = tuple(begin)
            for k2 in range(tb // 32):
                carry = out_tiles(k2, carry)
            for sd in (0, 1):
                for t in range(3):
                    dcw_ref[sd, t:t + 1, lanes] += jnp.sum(sums[4 * sd + t], axis=0, keepdims=True)
                dcb_ref[sd, :, lanes] += jnp.sum(sums[4 * sd + 3], axis=0, keepdims=True)

    nb8, nb16 = s // 8, s // 16
    return _call(body, name, (NFF, nt),
                 [pl.BlockSpec((2, None, tb, FS), lambda j, i: (0, j, i, 0)),
                  pl.BlockSpec((2, None, 8, FS), lambda j, i: (0, j, jnp.maximum(i * (tb // 8) - 1, 0), 0)),
                  pl.BlockSpec((2, None, 8, FS), lambda j, i: (0, j, jnp.minimum((i + 1) * (tb // 8), nb8 - 1), 0)),
                  pl.BlockSpec((tb, D), lambda j, i: (i, 0)),
                  pl.BlockSpec((16, D), lambda j, i: (jnp.minimum((i + 1) * (tb // 16), nb16 - 1), 0)),
                  pl.BlockSpec((FS, D), lambda j, i: (j, 0)),
                  pl.BlockSpec((2, None, 3, FS), lambda j, i: (0, j, 0, 0)),
                  pl.BlockSpec((2, None, 1, FS), lambda j, i: (0, j, 0, 0))],
                 [pl.BlockSpec((2, None, tb, FS), lambda j, i: (0, j, i, 0)),
                  pl.BlockSpec((2, None, 3, FS), lambda j, i: (0, j, 0, 0)),
                  pl.BlockSpec((2, None, 1, FS), lambda j, i: (0, j, 0, 0))],
                 [jax.ShapeDtypeStruct((2, NFF, s, FS), BF16), jax.ShapeDtypeStruct((2, NFF, 3, FS), F32),
                  jax.ShapeDtypeStruct((2, NFF, 1, FS), F32)],
                 [pltpu.VMEM((tb + 8, FS), F32), pltpu.VMEM((2, e, FS), F32)],
                 exchange=exchange)(hu, hu, hu, df, df, wd, cw, cb)


def _dw_down(a, df, name):
    s = a.shape[1]
    tk = 512
    nk = s // tk

    def body(a_ref, df_ref, o_ref, acc):
        k = pl.program_id(1)

        @pl.when(k == 0)
        def _():
            acc[...] = jnp.zeros_like(acc)

        acc[...] += _dot_tn(a_ref[...], df_ref[...])

        @pl.when(k == nk - 1)
        def _():
            o_ref[0] = acc[0:FS // 2, :].astype(BF16)
            o_ref[1] = acc[FS // 2:FS, :].astype(BF16)

    return _call(body, name, (NFF, nk),
                 [pl.BlockSpec((None, tk, FS), lambda j, k: (j, k, 0)), pl.BlockSpec((tk, D), lambda j, k: (k, 0))],
                 pl.BlockSpec((2, FS // 2, D), lambda j, k: (j, 0, 0)),
                 jax.ShapeDtypeStruct((NDEV, FS // 2, D), BF16), [pltpu.VMEM((FS, D), F32)])(a, df)


def _mm_dh2(dhu, wup, x1, g3, dx2, mix, g2, name):
    s = x1.shape[0]
    tm = 512

    def body(dhu_ref, w_ref, x1_ref, g3_ref, dx2_ref, mix_ref, g2_ref, dx1_ref, dmix_ref, dg3_ref, dg2_ref):
        i = pl.program_id(0)

        @pl.when(i == 0)
        def _():
            dg3_ref[...] = jnp.zeros_like(dg3_ref)
            dg2_ref[...] = jnp.zeros_like(dg2_ref)

        dh2 = _dot_nt(dhu_ref[0, 0], w_ref[0])
        for j in range(1, NDEV):
            dh2 = dh2 + _dot_nt(dhu_ref[j // NFF, j % NFF], w_ref[j])
        dn, dg3 = _rms_bwd(x1_ref[...], g3_ref[...], dh2)
        dx1 = dx2_ref[...] + dn
        dmix, dg2 = _rms_bwd(mix_ref[...], g2_ref[...], dx1)
        dx1_ref[...] = dx1
        dmix_ref[...] = dmix.astype(BF16)
        dg3_ref[...] += dg3
        dg2_ref[...] += dg2

    return _call(body, name, (s // tm,),
                 [pl.BlockSpec((2, NFF, tm, FS), lambda i: (0, 0, i, 0)), _resident((NDEV, D, FS)),
                  pl.BlockSpec((tm, D), _row), pl.BlockSpec((1, D), _fixed2), pl.BlockSpec((tm, D), _row),
                  pl.BlockSpec((tm, D), _row), pl.BlockSpec((1, D), _fixed2)],
                 [pl.BlockSpec((tm, D), _row), pl.BlockSpec((tm, D), _row), pl.BlockSpec((1, D), _fixed2),
                  pl.BlockSpec((1, D), _fixed2)],
                 [jax.ShapeDtypeStruct((s, D), F32), jax.ShapeDtypeStruct((s, D), BF16),
                  jax.ShapeDtypeStruct((1, D), F32), jax.ShapeDtypeStruct((1, D), F32)])(
                     dhu, wup, x1, g3, dx2, mix, g2)


def _dw_up(h2, dhu, name):
    s = h2.shape[0]
    tk = 512
    nk = s // tk

    def body(h_ref, d_ref, o_ref, acc):
        k = pl.program_id(1)

        @pl.when(k == 0)
        def _():
            acc[...] = jnp.zeros_like(acc)

        ht = h_ref[...].T
        for j in range(NFF):
            acc[j] += _dot(ht, d_ref[j])

        @pl.when(k == nk - 1)
        def _():
            o_ref[...] = acc[...].astype(BF16)

    return _call(body, name, (2, nk),
                 [pl.BlockSpec((tk, D), lambda hf, k: (k, 0)),
                  pl.BlockSpec((None, NFF, tk, FS), lambda hf, k: (hf, 0, k, 0))],
                 pl.BlockSpec((NFF, D, FS), lambda hf, k: (hf, 0, 0)),
                 jax.ShapeDtypeStruct((NDEV, D, FS), BF16), [pltpu.VMEM((NFF, D, FS), F32)])(h2, dhu)


def _mm_dz(dmix, wo, proj, bgate, ya, yb, name):
    s = dmix.shape[0]
    tm = 512

    def body(dm_ref, w_ref, ga_ref, gb_ref, ba_ref, bb_ref, ya_ref, yb_ref, dya_ref, dyb_ref, dg_ref, dbg_ref):
        i = pl.program_id(0)
        dz = _dot_nt(dm_ref[...], w_ref[...])
        ga = _sigmoid(ga_ref[...] + ba_ref[...])
        gb = _sigmoid(gb_ref[...] + bb_ref[...])
        dya_ref[...] = (dz * ga).astype(BF16)
        dyb_ref[...] = (dz * gb).astype(BF16)
        dga = dz * ya_ref[...].astype(F32) * ga * (1.0 - ga)
        dgb = dz * yb_ref[...].astype(F32) * gb * (1.0 - gb)
        dg_ref[:, 0:D] = dga.astype(BF16)
        dg_ref[:, D:2 * D] = dgb.astype(BF16)

        @pl.when(i == 0)
        def _():
            dbg_ref[...] = jnp.zeros_like(dbg_ref)

        dbg_ref[:, 0:D] += jnp.sum(dga, axis=0, keepdims=True)
        dbg_ref[:, D:2 * D] += jnp.sum(dgb, axis=0, keepdims=True)

    return _call(body, name, (s // tm,),
                 [pl.BlockSpec((tm, D), _row), pl.BlockSpec((D, D), _fixed2),
                  pl.BlockSpec((tm, D), lambda i: (i, 2)), pl.BlockSpec((tm, D), lambda i: (i, 3)),
                  pl.BlockSpec((1, D), lambda i: (0, 0)), pl.BlockSpec((1, D), lambda i: (0, 1)),
                  pl.BlockSpec((tm, D), _row), pl.BlockSpec((tm, D), _row)],
                 [pl.BlockSpec((tm, D), _row), pl.BlockSpec((tm, D), _row), pl.BlockSpec((tm, 2 * D), _row),
                  pl.BlockSpec((1, 2 * D), _fixed2)],
                 [jax.ShapeDtypeStruct((s, D), BF16), jax.ShapeDtypeStruct((s, D), BF16),
                  jax.ShapeDtypeStruct((s, 2 * D), BF16), jax.ShapeDtypeStruct((1, 2 * D), F32)])(
                     dmix, wo, proj, proj, bgate, bgate, ya, yb)


def _dw_o(z, dmix, name):
    s = z.shape[0]
    tk = 512
    nk = s // tk

    def body(z_ref, d_ref, o_ref, acc):
        k = pl.program_id(0)

        @pl.when(k == 0)
        def _():
            acc[...] = jnp.zeros_like(acc)

        acc[...] += _dot_tn(z_ref[...], d_ref[...])

        @pl.when(k == nk - 1)
        def _():
            for j in range(NDEV):
                o_ref[j] = acc[j * 128:(j + 1) * 128, :].astype(BF16)

    return _call(body, name, (nk,),
                 [pl.BlockSpec((tk, D), _row), pl.BlockSpec((tk, D), _row)],
                 pl.BlockSpec((NDEV, 128, D), lambda k: (0, 0, 0)),
                 jax.ShapeDtypeStruct((NDEV, 128, D), BF16), [pltpu.VMEM((D, D), F32)])(z, dmix)


def _branch_bwd(dya, dyb, wao, wpo, name):
    s = dya.shape[0]
    tm = 512

    def body(dya_ref, dyb_ref, wao_ref, wpo_ref, datt_ref, dmx_ref):
        datt_ref[...] = _dot_nt(dya_ref[...], wao_ref[...]).astype(BF16)
        dmx_ref[...] = _dot_nt(dyb_ref[...], wpo_ref[...])

    return _call(body, name, (s // tm,),
                 [pl.BlockSpec((tm, D), _row), pl.BlockSpec((tm, D), _row),
                  pl.BlockSpec((AW, D), _fixed2), pl.BlockSpec((PW, D), _fixed2)],
                 [pl.BlockSpec((tm, AW), _row), pl.BlockSpec((tm, PW), _row)],
                 [jax.ShapeDtypeStruct((s, AW), BF16), jax.ShapeDtypeStruct((s, PW), F32)])(dya, dyb, wao, wpo)


def _dw_branch(att, mixed, dya, dyb, name):
    s = att.shape[0]
    tk = 512
    nk = s // tk

    def body(att_ref, mx_ref, dya_ref, dyb_ref, oa_ref, ob_ref, acca, accb):
        k = pl.program_id(0)

        @pl.when(k == 0)
        def _():
            acca[...] = jnp.zeros_like(acca)
            accb[...] = jnp.zeros_like(accb)

        acca[...] += _dot_tn(att_ref[...], dya_ref[...])
        accb[...] += _dot_tn(mx_ref[...], dyb_ref[...])

        @pl.when(k == nk - 1)
        def _():
            for j in range(NDEV):
                oa_ref[j] = acca[:, j * 128:(j + 1) * 128].astype(BF16)
                ob_ref[j] = accb[:, j * 128:(j + 1) * 128].astype(BF16)

    out = jax.ShapeDtypeStruct((NDEV, AW, 128), BF16)
    return _call(body, name, (nk,),
                 [pl.BlockSpec((tk, AW), _row), pl.BlockSpec((tk, PW), _row),
                  pl.BlockSpec((tk, D), _row), pl.BlockSpec((tk, D), _row)],
                 [pl.BlockSpec((NDEV, AW, 128), lambda k: (0, 0, 0))] * 2, [out, out],
                 [pltpu.VMEM((AW, D), F32), pltpu.VMEM((PW, D), F32)])(att, mixed, dya, dyb)


def _attn_bwd(proj, datt, tab, name, exchange=None):
    s = proj.shape[0]
    nq = s // QB

    def body(q_ref, k0, k1, k2, v0, v1, v2, do_ref, b_ref, dq_ref, dk_ref, dv_ref, db_ref, dka, dkb, dva, dvb):
        i = pl.program_id(1)

        @pl.when(i == 0)
        def _():
            for r in (dka, dkb, dva, dvb):
                r[...] = jnp.zeros_like(r)
            db_ref[...] = jnp.zeros_like(db_ref)

        @pl.when(i < nq)
        def _():
            qs, ps, k, lane = _attn_probs(q_ref, (k0, k1, k2), b_ref, i)
            v = jnp.concatenate([v0[...], v1[...], v2[...]], axis=0).astype(BF16)
            do = do_ref[...]
            dq = jnp.zeros((QB, 128), F32)
            dkw = jnp.zeros((KW, 128), F32)
            dvw = jnp.zeros((KW, 128), F32)
            for hh in (0, 1):
                mine = (lane < 64) if hh == 0 else (lane >= 64)
                doh = jnp.where(mine, do, jnp.zeros_like(do))
                kmask = lax.broadcasted_iota(jnp.int32, (KW, 128), 1)
                kh = jnp.where((kmask < 64) if hh == 0 else (kmask >= 64), k, jnp.zeros_like(k))
                p = ps[hh][0] * ps[hh][1]
                dp = _dot_nt(doh, v)
                ds = p * (dp - jnp.sum(dp * p, axis=1, keepdims=True))
                db_ref[hh] += ds
                dsb = ds.astype(BF16)
                dq = dq + _dot(dsb, kh)
                dkw = dkw + _dot_tn(dsb, qs[hh])
                dvw = dvw + _dot_tn(p.astype(BF16), doh)
            dq_ref[...] = (dq * SCALE).astype(BF16)
            dk_ref[...] = (dka[...] + dkw[0:QB]).astype(BF16)
            dka[...] = dkb[...] + dkw[QB:2 * QB]
            dkb[...] = dkw[2 * QB:3 * QB]
            dv_ref[...] = (dva[...] + dvw[0:QB]).astype(BF16)
            dva[...] = dvb[...] + dvw[QB:2 * QB]
            dvb[...] = dvw[2 * QB:3 * QB]

        @pl.when(i >= nq)
        def _():
            dk_ref[...] = dka[...].astype(BF16)
            dka[...] = dkb[...]
            dkb[...] = jnp.zeros_like(dkb)
            dv_ref[...] = dva[...].astype(BF16)
            dva[...] = dvb[...]
            dvb[...] = jnp.zeros_like(dvb)

    def kv(col, d):
        return pl.BlockSpec((QB, 128), lambda p, i: (jnp.clip(i - 2 + d, 0, nq - 1), col + p))

    cur = lambda p, i: (jnp.minimum(i, nq - 1), p)
    done = lambda p, i: (jnp.maximum(i - 2, 0), p)
    in_specs = [pl.BlockSpec((QB, 128), cur)]
    in_specs += [kv(4, d) for d in range(3)] + [kv(8, d) for d in range(3)]
    in_specs += [pl.BlockSpec((QB, 128), cur), pl.BlockSpec((2, QB, KW), lambda p, i: (p, 0, 0))]
    o = jax.ShapeDtypeStruct((s, AW), BF16)
    return _call(body, name, (4, nq + 2), in_specs,
                 [pl.BlockSpec((QB, 128), cur), pl.BlockSpec((QB, 128), done), pl.BlockSpec((QB, 128), done),
                  pl.BlockSpec((2, QB, KW), lambda p, i: (p, 0, 0))],
                 [o, o, o, jax.ShapeDtypeStruct((HEADS, QB, KW), F32)],
                 [pltpu.VMEM((QB, 128), F32)] * 4, exchange=exchange)(
                     proj, proj, proj, proj, proj, proj, proj, datt, tab)


def _rel_bias_grad(dtab, name):
    wdt = 640

    def body(x_ref, o_ref):
        x = x_ref[...]
        xc = x[0:64, 0:wdt]
        for qc in range(1, QB // 64):
            xc = xc + pltpu.roll(x[qc * 64:(qc + 1) * 64, :], KW - qc * 64, 1)[:, 0:wdt]
        r = lax.broadcasted_iota(jnp.int32, (64, 64), 0)
        c = lax.broadcasted_iota(jnp.int32, (64, 64), 1)
        flip = (r + c == 63).astype(F32)
        y = jnp.dot(flip, xc, preferred_element_type=F32, precision=lax.Precision.HIGHEST)
        z = pltpu.roll(y, 0, 1, stride=1, stride_axis=0)
        t = jnp.broadcast_to(jnp.sum(z, axis=0, keepdims=True), (8, wdt))
        e = lax.broadcasted_iota(jnp.int32, (wdt, wdt), 0)
        rr = lax.broadcasted_iota(jnp.int32, (wdt, wdt), 1)
        onehot = (jnp.clip(BAND - 1 - e, -MAXREL, MAXREL) + MAXREL == rr).astype(F32)
        o_ref[...] = jnp.dot(t, onehot, preferred_element_type=F32, precision=lax.Precision.HIGHEST)

    return _call(body, name, (HEADS,), [pl.BlockSpec((None, QB, KW), lambda h: (h, 0, 0))],
                 pl.BlockSpec((None, 8, wdt), lambda h: (h, 0, 0)),
                 jax.ShapeDtypeStruct((HEADS, 8, wdt), F32))(dtab)


def _pool_bwd(dmixed, pooled, wg, scale, name):
    s = dmixed.shape[0]
    tb = 512
    nt = s // tb
    e = tb + 16

    def body(dm_ref, dmn_ref, pl_ref, wg_ref, sc_ref, du_ref, dwg_ref, dsc_ref):
        i = pl.program_id(0)
        dm = jnp.concatenate([dm_ref[...], jnp.where(i == nt - 1, 0.0, dmn_ref[...])], axis=0)
        t = i * tb + lax.broadcasted_iota(jnp.int32, (e, 1), 0)

        @pl.when(i == 0)
        def _():
            dwg_ref[...] = jnp.zeros_like(dwg_ref)
            dsc_ref[...] = jnp.zeros_like(dsc_ref)

        for g, w in enumerate(POOL_WINDOWS):
            sl = slice(g * PG, (g + 1) * PG)
            wgb = wg_ref[g].astype(BF16)
            pb = pl_ref[:, sl]
            dsc_ref[:, sl] += jnp.sum(dm[0:tb, sl] * _dot(pb, wgb), axis=0, keepdims=True)
            dpre = (dm[:, sl] * sc_ref[:, sl]).astype(BF16)
            dwg_ref[g] += _dot_tn(pb, dpre[0:tb])
            dpool = _dot_nt(dpre, wgb)
            a = dpool / jnp.minimum(t + 1, w).astype(F32)
            sh = 1
            while sh < w:
                a = a + pltpu.roll(a, e - sh, 0)
                sh *= 2
            du_ref[:, sl] = (a[0:tb] - dpool[0:tb]).astype(BF16)

    nb16 = s // 16
    return _call(body, name, (nt,),
                 [pl.BlockSpec((tb, PW), _row),
                  pl.BlockSpec((16, PW), lambda i: (jnp.minimum((i + 1) * (tb // 16), nb16 - 1), 0)),
                  pl.BlockSpec((tb, PW), _row), pl.BlockSpec((4, PG, PG), lambda i: (0, 0, 0)),
                  pl.BlockSpec((1, PW), _fixed2)],
                 [pl.BlockSpec((tb, PW), _row), pl.BlockSpec((4, PG, PG), lambda i: (0, 0, 0)),
                  pl.BlockSpec((1, PW), _fixed2)],
                 [jax.ShapeDtypeStruct((s, PW), BF16), jax.ShapeDtypeStruct((4, PG, PG), F32),
                  jax.ShapeDtypeStruct((1, PW), F32)])(dmixed, dmixed, pooled, wg, scale)


def _dproj_specs(t):
    return [pl.BlockSpec((t, 512), _row)] * 4 + [pl.BlockSpec((t, 2 * D), _row)]


def _dproj_segment(refs, n):
    return refs[n][...] if n < 4 else refs[4][:, (n - 4) * 512:(n - 3) * 512]


def _mm_dh(segs, win, x, g1, dx1, below, name, exchange=None):
    s = x.shape[0]
    tm = 512

    def body(dq_ref, dk_ref, dv_ref, du_ref, dg_ref, w_ref, x_ref, g1_ref, dx1_ref, *rest):
        i = pl.program_id(0)
        outs = rest[2:] if below else rest

        @pl.when(i == 0)
        def _():
            for ref in outs[1::2]:
                ref[...] = jnp.zeros_like(ref)

        pieces = (dq_ref, dk_ref, dv_ref, du_ref, dg_ref)
        dh = _dot_nt(_dproj_segment(pieces, 0), w_ref[0])
        for n in range(1, NDEV):
            dh = dh + _dot_nt(_dproj_segment(pieces, n), w_ref[n])
        dn, dg1 = _rms_bwd(x_ref[...], g1_ref[...], dh)
        dx = dx1_ref[...] + dn
        outs[0][...] = dx
        outs[1][...] += dg1
        if below:
            df, dg4 = _rms_bwd(rest[0][...], rest[1][...], dx)
            outs[2][...] = df.astype(BF16)
            outs[3][...] += dg4

    rows, gain = pl.BlockSpec((tm, D), _row), pl.BlockSpec((1, D), _fixed2)
    vec = jax.ShapeDtypeStruct((1, D), F32)
    return _call(body, name, (s // tm,),
                 _dproj_specs(tm) + [_resident((NDEV, D, 512)), rows, gain, rows] + ([rows, gain] if below else []),
                 [rows, gain] + ([rows, gain] if below else []),
                 [jax.ShapeDtypeStruct((s, D), F32), vec] + ([jax.ShapeDtypeStruct((s, D), BF16), vec] if below else []),
                 exchange=exchange)(*segs, win, x, g1, dx1, *below)


def _dw_in(h, segs, name):
    s = h.shape[0]
    tk = 512
    nk = s // tk

    def body(h_ref, dq_ref, dk_ref, dv_ref, du_ref, dg_ref, o_ref, acc):
        k = pl.program_id(0)

        @pl.when(k == 0)
        def _():
            acc[...] = jnp.zeros_like(acc)

        ht = h_ref[...].T
        pieces = (dq_ref, dk_ref, dv_ref, du_ref, dg_ref)
        for n in range(NDEV):
            acc[n] += _dot(ht, _dproj_segment(pieces, n))

        @pl.when(k == nk - 1)
        def _():
            o_ref[...] = acc[...].astype(BF16)

    return _call(body, name, (nk,), [pl.BlockSpec((tk, D), _row)] + _dproj_specs(tk),
                 pl.BlockSpec((NDEV, D, 512), lambda k: (0, 0, 0)),
                 jax.ShapeDtypeStruct((NDEV, D, 512), BF16), [pltpu.VMEM((NDEV, D, 512), F32)])(h, *segs)


def _slot(px, py, pc):
    return 4 * px + 2 * py + pc


def _gather_exchange(shards, deliver):
    n = len(shards)

    def phases(ins, outs, sems):
        send_sems, recv_sems, local_sems = sems
        x, y, c = lax.axis_index("x"), lax.axis_index("y"), lax.axis_index("c")
        me, sibling = (x, y, c), (x, y, 1 - c)
        chips = [(1 - x, y), (x, 1 - y), (1 - x, 1 - y)]

        def copy(w, k, block, to, src=None):
            dst = outs[w].at[_slot(*block)]
            return pltpu.make_async_remote_copy(
                src_ref=dst if src is None else src, dst_ref=dst, send_sem=send_sems.at[7 * w + k],
                recv_sem=recv_sems.at[7 * w + k], device_id=to, device_id_type=MESH)

        def mine(w):
            return pltpu.make_async_copy(ins[w], outs[w].at[_slot(*me)], local_sems.at[w])

        def first(w):
            return [copy(w, 0, me, sibling, src=ins[w])] + [
                copy(w, 1 + j, me, (*chip, c), src=ins[w]) for j, chip in enumerate(chips)]

        def passed(w):
            return [copy(w, 4 + j, (*chip, c), sibling) for j, chip in enumerate(chips)]

        def send():
            for w in range(n):
                mine(w).start()
                for cp in first(w):
                    cp.start()

        def forward():
            for w in range(n):
                for j, chip in enumerate(chips):
                    copy(w, 1 + j, (*chip, c), me).wait_recv()
                    passed(w)[j].start()

        def finish():
            for w in range(n):
                copy(w, 0, sibling, me).wait_recv()
                for j, chip in enumerate(chips):
                    copy(w, 4 + j, (*chip, 1 - c), me).wait_recv()
            for w in range(n):
                for cp in first(w) + passed(w):
                    cp.wait_send()
                mine(w).wait()

        return [send, forward, finish]

    return _Exchange(shards, [jax.ShapeDtypeStruct((NDEV,) + a.shape, a.dtype) for a in shards],
                     [pltpu.SemaphoreType.DMA((7 * n,)), pltpu.SemaphoreType.DMA((7 * n,)),
                      pltpu.SemaphoreType.DMA((n,))], phases, deliver)


def _scatter_exchange(parts, deliver):
    n = len(parts)

    def phases(ins, outs, sems):
        send_sems, recv_sems, local_sems = sems
        x, y, c = lax.axis_index("x"), lax.axis_index("y"), lax.axis_index("c")
        me = _slot(x, y, c)
        peers = [((1 - x) if r & 4 else x, (1 - y) if r & 2 else y, (1 - c) if r & 1 else c) for r in range(1, NDEV)]

        def mine(w):
            return pltpu.make_async_copy(ins[w].at[me], outs[w].at[me], local_sems.at[w])

        def copy(w, r, block_here, block_there):
            return pltpu.make_async_remote_copy(
                src_ref=ins[w].at[block_here], dst_ref=outs[w].at[block_there], send_sem=send_sems.at[7 * w + r],
                recv_sem=recv_sems.at[7 * w + r], device_id=peers[r], device_id_type=MESH)

        def send():
            for w in range(n):
                mine(w).start()
                for r, peer in enumerate(peers):
                    copy(w, r, _slot(*peer), me).start()

        def finish():
            for w in range(n):
                for r, peer in enumerate(peers):
                    copy(w, r, me, _slot(*peer)).wait_recv()
            for w in range(n):
                for r, peer in enumerate(peers):
                    copy(w, r, _slot(*peer), me).wait_send()
                mine(w).wait()

        return [send, finish]

    return _Exchange(parts, [jax.ShapeDtypeStruct(a.shape, a.dtype) for a in parts],
                     [pltpu.SemaphoreType.DMA((7 * n,)), pltpu.SemaphoreType.DMA((7 * n,)),
                      pltpu.SemaphoreType.DMA((n,))], phases, deliver)


def _adamw(w, g, m, v):
    m = ADAM_B1 * m + (1.0 - ADAM_B1) * g
    v = ADAM_B2 * v + (1.0 - ADAM_B2) * (g * g)
    m_hat = m / (1.0 - ADAM_B1 ** ADAM_STEP)
    v_hat = v / (1.0 - ADAM_B2 ** ADAM_STEP)
    delta = -ADAM_LR * (m_hat / (jnp.sqrt(v_hat) + ADAM_EPS) + ADAM_WD * w)
    return delta, m, v


def _adamw_sharded(recvs, w, m, v, tr, name, exchange=None):
    _, r, cdim = w.shape
    nr = r // tr

    def body(r0_ref, r1_ref, w_ref, m_ref, v_ref, g_ref, d_ref, nm_ref, nv_ref):
        l = pl.program_id(0)

        def total(ref):
            acc = ref[0].astype(F32)
            for k in range(1, NDEV):
                acc = acc + ref[k].astype(F32)
            return acc

        g = jnp.where(l == 0, total(r0_ref), total(r1_ref))
        d, nm, nv = _adamw(w_ref[...], g, m_ref[...], v_ref[...])
        g_ref[...] = g
        d_ref[...] = d
        nm_ref[...] = nm
        nv_ref[...] = nv

    mine = pl.BlockSpec((None, tr, cdim), lambda l, i: (l, i, 0))
    out = jax.ShapeDtypeStruct(w.shape, F32)
    return _call(body, name, (DEPTH, nr),
                 [pl.BlockSpec((NDEV, tr, cdim), lambda l, i: (0, jnp.where(l == 0, i, nr - 1), 0)),
                  pl.BlockSpec((NDEV, tr, cdim), lambda l, i: (0, jnp.where(l == 1, i, 0), 0)),
                  mine, mine, mine],
                 [mine] * 4, [out] * 4, exchange=exchange)(recvs[0], recvs[1], w, m, v)


def _adamw_small(parts, w, m, v, name):
    r = w.shape[0]

    def body(p_ref, w_ref, m_ref, v_ref, g_ref, d_ref, nm_ref, nv_ref):
        g = p_ref[0]
        for k in range(1, NDEV):
            g = g + p_ref[k]
        d, nm, nv = _adamw(w_ref[...], g, m_ref[...], v_ref[...])
        g_ref[...] = g
        d_ref[...] = d
        nm_ref[...] = nm
        nv_ref[...] = nv

    whole = pl.BlockSpec((r, 1024), _fixed2)
    out = jax.ShapeDtypeStruct((r, 1024), F32)
    return _call(body, name, (1,), [pl.BlockSpec((NDEV, r, 1024), lambda i: (0, 0, 0)), whole, whole, whole],
                 [whole] * 4, [out] * 4)(parts, w, m, v)


SMALL = (("norm_mix_pre", (D,)), ("b_gate", (2 * D,)), ("rel_bias", (HEADS, NREL)), ("w_pool_group", (4, PG, PG)),
         ("pool_scale", (PW,)), ("norm_mix_post", (D,)), ("norm_ffn_pre", (D,)), ("conv_b", (NFF * 2 * FS,)),
         ("norm_ffn_post", (D,)))
SHARDED = (("w_in", 256), ("w_attn_out", 512), ("w_pool_out", 512), ("w_o", 128), ("w_up", 256), ("conv_w", 3),
           ("w_down", 176))
SMALL_ROWS = 168


def _pack_small(tree):
    flat = jnp.concatenate([tree[name].reshape(-1) for name, _ in SMALL])
    return jnp.pad(flat, (0, SMALL_ROWS * 1024 - flat.shape[0])).reshape(SMALL_ROWS, 1024)


def _unpack_small(packed):
    flat = packed.reshape(-1)
    out, at = {}, 0
    for name, shape in SMALL:
        size = DEPTH * int(np.prod(shape))
        out[name] = flat[at:at + size].reshape((DEPTH,) + shape)
        at += size
    return out


def _layer_fwd(l, x, h, p, gath, target, gnext, carry):
    n = f"l{l}"
    g = lambda name: p[name][l].reshape(1, -1)
    res = {"x": x, "h": h}
    res["tab"] = _bias_table(p["rel_bias"][l], f"bias_table_{n}")
    proj = _mm_in(h, gath["w_in"], f"mm_in_{n}", carry.get("mm_in"))
    att = _attn_fwd(proj, res["tab"], f"attn_fwd_{n}", carry.get("attn_fwd"))
    pooled, mixed = _pool_fwd(proj, p["w_pool_group"][l], g("pool_scale"), f"pool_fwd_{n}")
    bgate = g("b_gate")
    z, ya, yb = _branch_fwd(att, mixed, gath["w_attn_out"], gath["w_pool_out"], proj, bgate, f"branch_fwd_{n}")
    mix, x1, h2 = _mm_o_fwd(z, gath["w_o"], x, g("norm_mix_post"), g("norm_ffn_pre"), f"mm_o_fwd_{n}")
    cw = gath["conv_w"].reshape(2, NFF, 3, FS)
    cb = p["conv_b"][l].reshape(2, NFF, 1, FS)
    hu, a = _mm_up_conv_gate(h2, gath["w_up"], cw, cb, f"mm_up_conv_gate_{n}", carry.get("mm_up_conv_gate"))
    hu = hu.reshape(2, NFF, -1, FS)
    res.update(proj=proj, att=att, pooled=pooled, mixed=mixed, z=z, ya=ya, yb=yb, mix=mix, x1=x1, h2=h2, hu=hu,
               a=a, cw=cw, cb=cb, bgate=bgate)
    if target is None:
        f, x2, hn = _mm_down_fwd(a, gath["w_down"], x1, g("norm_ffn_post"), gnext, f"mm_down_fwd_{n}",
                                 carry.get("mm_down_fwd"))
        res["f"] = f
        return res, x2, hn
    dy, df, dg4, loss = _mm_down_loss(a, gath["w_down"], x1, g("norm_ffn_post"), target, f"mm_down_loss_{n}")
    return res, (dy, df, dg4), loss


def _layer_bwd(l, top, p, gath, res, carry, below):
    n = f"l{l}"
    g = lambda name: p[name][l].reshape(1, -1)
    big, small = {}, {}
    taken = lambda call: carry[call](big) if call in carry else None
    dx2, df, small["norm_ffn_post"] = top
    dhu, dcw, dcb = _conv_gate_bwd(res["hu"], df, gath["w_down"], res["cw"], res["cb"], f"conv_gate_bwd_{n}",
                                   taken("conv_gate_bwd"))
    big["conv_w"] = dcw.reshape(NDEV, 3, FS)
    small["conv_b"] = dcb
    big["w_down"] = _dw_down(res["a"], df, f"dw_down_{n}")
    dx1, dmix, small["norm_ffn_pre"], small["norm_mix_post"] = _mm_dh2(
        dhu, gath["w_up"], res["x1"], g("norm_ffn_pre"), dx2, res["mix"], g("norm_mix_post"), f"mm_dh2_{n}")
    big["w_up"] = _dw_up(res["h2"], dhu, f"dw_up_{n}")
    dya, dyb, dgates, small["b_gate"] = _mm_dz(dmix, gath["w_o"], res["proj"], res["bgate"], res["ya"], res["yb"],
                                                f"mm_dz_{n}")
    big["w_o"] = _dw_o(res["z"], dmix, f"dw_o_{n}")
    datt, dmixed = _branch_bwd(dya, dyb, gath["w_attn_out"], gath["w_pool_out"], f"branch_bwd_{n}")
    big["w_attn_out"], big["w_pool_out"] = _dw_branch(res["att"], res["mixed"], dya, dyb, f"dw_branch_{n}")
    dq, dk, dv, dtab = _attn_bwd(res["proj"], datt, res["tab"], f"attn_bwd_{n}", taken("attn_bwd"))
    small["rel_bias"] = _rel_bias_grad(dtab, f"rel_bias_grad_{n}")[:, 0, :NREL]
    du, small["w_pool_group"], small["pool_scale"] = _pool_bwd(
        dmixed, res["pooled"], p["w_pool_group"][l], g("pool_scale"), f"pool_bwd_{n}")
    segs = (dq, dk, dv, du, dgates)
    big["w_in"] = _dw_in(res["h"], segs, f"dw_in_{n}")
    dx, small["norm_mix_pre"], *lower = _mm_dh(segs, gath["w_in"], res["x"], g("norm_mix_pre"), dx1, below,
                                               f"mm_dh_{n}", taken("mm_dh"))
    return (dx, *lower), big, small


def _gather_weights(gath, l, p, names):
    shards = [p[k][l] if k == "conv_w" else p[k][l].astype(BF16) for k in names]

    def deliver(results):
        for k, a in zip(names, results):
            if k in ("w_attn_out", "w_pool_out"):
                a = jnp.transpose(a, (1, 0, 2)).reshape(AW, D)
            elif k == "w_o":
                a = a.reshape(D, D)
            elif k == "w_down":
                a = a.reshape(NFF * FS, D)
            gath[k] = a

    return _gather_exchange(shards, deliver)


def _scatter_grads(recv, big, names):
    return _scatter_exchange([big[k] for k in names], lambda results: recv.update(zip(names, results)))


def kernel(x, norm_mix_pre, w_in, b_gate, rel_bias, w_attn_out, w_pool_group, pool_scale, w_pool_out, w_o, norm_mix_post, norm_ffn_pre, w_up, conv_w, conv_b, w_down, norm_ffn_post, loss_target, m_norm_mix_pre, m_w_in, m_b_gate, m_rel_bias, m_w_attn_out, m_w_pool_group, m_pool_scale, m_w_pool_out, m_w_o, m_norm_mix_post, m_norm_ffn_pre, m_w_up, m_conv_w, m_conv_b, m_w_down, m_norm_ffn_post, v_norm_mix_pre, v_w_in, v_b_gate, v_rel_bias, v_w_attn_out, v_w_pool_group, v_pool_scale, v_w_pool_out, v_w_o, v_norm_mix_post, v_norm_ffn_pre, v_w_up, v_conv_w, v_conv_b, v_w_down, v_norm_ffn_post):
    names = ("norm_mix_pre", "w_in", "b_gate", "rel_bias", "w_attn_out", "w_pool_group", "pool_scale", "w_pool_out",
             "w_o", "norm_mix_post", "norm_ffn_pre", "w_up", "conv_w", "conv_b", "w_down", "norm_ffn_post")
    p = dict(zip(names, (norm_mix_pre, w_in, b_gate, rel_bias, w_attn_out, w_pool_group, pool_scale, w_pool_out, w_o,
                         norm_mix_post, norm_ffn_pre, w_up, conv_w, conv_b, w_down, norm_ffn_post)))
    mom = dict(zip(names, (m_norm_mix_pre, m_w_in, m_b_gate, m_rel_bias, m_w_attn_out, m_w_pool_group, m_pool_scale,
                           m_w_pool_out, m_w_o, m_norm_mix_post, m_norm_ffn_pre, m_w_up, m_conv_w, m_conv_b, m_w_down,
                           m_norm_ffn_post)))
    var = dict(zip(names, (v_norm_mix_pre, v_w_in, v_b_gate, v_rel_bias, v_w_attn_out, v_w_pool_group, v_pool_scale,
                           v_w_pool_out, v_w_o, v_norm_mix_post, v_norm_ffn_pre, v_w_up, v_conv_w, v_conv_b, v_w_down,
                           v_norm_ffn_post)))
    s = x.shape[1]
    xs = x.reshape(s, D)
    target = loss_target.reshape(s, D)

    gath = [{}, {}]
    rest = ("w_attn_out", "w_pool_out", "w_o", "w_down", "conv_w")
    h0 = _rmsnorm_call(xs, p["norm_mix_pre"][0].reshape(1, D), "rmsnorm_l0",
                       _gather_weights(gath[0], 0, p, ("w_in",)))
    res0, x2, h1 = _layer_fwd(0, xs, h0, p, gath[0], None, p["norm_mix_pre"][1].reshape(1, D), {
        "mm_in": _gather_weights(gath[0], 0, p, rest),
        "attn_fwd": _gather_weights(gath[0], 0, p, ("w_up",)),
        "mm_up_conv_gate": _gather_weights(gath[1], 1, p, ("w_in",) + rest),
        "mm_down_fwd": _gather_weights(gath[1], 1, p, ("w_up",))})
    res1, top1, loss_part = _layer_fwd(1, x2, h1, p, gath[1], target, None, {})

    order = [k for k, _ in SHARDED]
    recv = [{}, {}]
    top0, big1, small1 = _layer_bwd(1, top1, p, gath[1], res1, {}, (res0["f"], p["norm_ffn_post"][0].reshape(1, D)))
    (grad_x,), big0, small0 = _layer_bwd(0, top0, p, gath[0], res0, {
        "conv_gate_bwd": lambda big: _scatter_grads(recv[1], big1, order),
        "attn_bwd": lambda big: _scatter_grads(recv[0], big, [k for k in order if k != "w_in"]),
        "mm_dh": lambda big: _scatter_grads(recv[0], big, ["w_in"])}, ())

    loss = lax.psum(loss_part[0, 0], ("x", "y", "c"))

    small = {k: jnp.stack([small0[k].reshape(shape), small1[k].reshape(shape)]) for k, shape in SMALL}
    parts = []
    carried = {"w_up": _gather_exchange([_pack_small(small)], parts.extend)}
    out = {}
    for k, tr in sorted(SHARDED, key=lambda kt: kt[0] not in carried):
        out[k] = _adamw_sharded((recv[0][k], recv[1][k]), p[k], mom[k], var[k], tr, f"adamw_{k}", carried.get(k))
    packed = _adamw_small(parts[0], _pack_small(p), _pack_small(mom), _pack_small(var), "adamw_small")
    unpacked = [_unpack_small(a) for a in packed]
    for k, _ in SMALL:
        out[k] = tuple(u[k] for u in unpacked)

    return (loss, grad_x.reshape(x.shape), *[out[k][0] for k in names], *[out[k][1] for k in names],
            *[out[k][2] for k in names], *[out[k][3] for k in names])
```

```python
import numpy as np
import jax
import jax.numpy as jnp
from jax import lax
from jax.experimental import pallas as pl
from jax.experimental.pallas import tpu as pltpu

F32, BF16 = jnp.float32, jnp.bfloat16

D = 1024
AW = 512
PW = 512
PG = 128
INW = 4096
FS = 704
NFF = 4
NDEV = 8
DEPTH = 2
HEADS = 8
NREL = 513
MAXREL = 256
POOL_WINDOWS = (2, 4, 8, 16)
EPS = 1e-6
SCALE = 0.125
NEG = -1e30
QB = 256
KW = 3 * QB
BAND = 576
ADAM_LR, ADAM_B1, ADAM_B2, ADAM_EPS, ADAM_WD, ADAM_STEP = 0.001, 0.9, 0.999, 1e-08, 0.01, 10
VMEM_LIMIT_V7X = 56 * 1024 * 1024
MESH = pl.DeviceIdType.MESH
GELU_C = 0.7978845608028654
GELU_A = 0.044715


HBM = pl.BlockSpec(memory_space=pltpu.HBM)


class _Exchange:
    def __init__(self, operands, out_shape, scratch, phases, deliver):
        self.operands, self.out_shape, self.scratch = list(operands), list(out_shape), list(scratch)
        self.phases, self.deliver = phases, deliver


def _call(body, name, grid, in_specs, out_specs, out_shape, scratch=(), exchange=None):
    params = pltpu.CompilerParams(vmem_limit_bytes=VMEM_LIMIT_V7X)
    if exchange is None:
        return pl.pallas_call(body, name=name, grid=grid, in_specs=in_specs, out_specs=out_specs, out_shape=out_shape,
                              scratch_shapes=list(scratch), compiler_params=params)
    single = not isinstance(out_shape, (list, tuple))
    outs, ospecs = ([out_shape], [out_specs]) if single else (list(out_shape), list(out_specs))
    n_in, n_out, n_scr = len(in_specs), len(outs), len(scratch)
    ne_in, ne_out = len(exchange.operands), len(exchange.out_shape)
    nsteps = int(np.prod(grid))

    def carried(*refs):
        cut = np.cumsum([0, n_in, ne_in, n_out, ne_out, n_scr])
        base_in, ex_in, base_out, ex_out, base_scr = (refs[cut[k]:cut[k + 1]] for k in range(5))
        step = pl.program_id(0)
        for axis in range(1, len(grid)):
            step = step * grid[axis] + pl.program_id(axis)
        phases = exchange.phases(ex_in, ex_out, refs[cut[5]:])
        pl.when(step == 0)(phases[0])
        body(*base_in, *base_out, *base_scr)
        at = {2: [nsteps - 1], 3: [(7 * nsteps) // 10, nsteps - 1]}[len(phases)]
        for phase, when in zip(phases[1:], at):
            pl.when(step == when)(phase)

    call = pl.pallas_call(
        carried, name=name, grid=grid, in_specs=list(in_specs) + [HBM] * ne_in, out_specs=ospecs + [HBM] * ne_out,
        out_shape=outs + exchange.out_shape, scratch_shapes=list(scratch) + exchange.scratch, compiler_params=params)

    def run(*args):
        res = call(*args, *exchange.operands)
        exchange.deliver(res[n_out:])
        return res[0] if single else res[:n_out]

    return run


def _dot(a, b):
    return jnp.dot(a, b, preferred_element_type=F32)


def _dot_nt(a, b):
    return lax.dot_general(a, b, (((1,), (1,)), ((), ())), preferred_element_type=F32)


def _dot_tn(a, b):
    return lax.dot_general(a, b, (((0,), (0,)), ((), ())), preferred_element_type=F32)


def _rms(x, g):
    r = lax.rsqrt(jnp.mean(x * x, axis=-1, keepdims=True) + EPS)
    return x * r * g


def _rms_bwd(x, g, dy):
    r = lax.rsqrt(jnp.mean(x * x, axis=-1, keepdims=True) + EPS)
    xh = x * r
    dg = jnp.sum(dy * xh, axis=0, keepdims=True)
    dxh = dy * g
    dx = r * (dxh - xh * jnp.mean(dxh * xh, axis=-1, keepdims=True))
    return dx, dg


def _sigmoid(x):
    return 1.0 / (1.0 + jnp.exp(-x))


def _gelu_parts(x):
    x2 = x * x
    th = jnp.tanh(x * (GELU_C + (GELU_C * GELU_A) * x2))
    s = 0.5 * th + 0.5
    dg = s * (1.0 + x * (1.0 - s) * (2.0 * GELU_C + (6.0 * GELU_C * GELU_A) * x2))
    return x * s, dg


def _row(i):
    return (i, 0)


def _fixed2(*_):
    return (0, 0)


def _rmsnorm_call(x, g, name, exchange=None):
    s = x.shape[0]
    tm = 512

    def body(x_ref, g_ref, o_ref):
        o_ref[...] = _rms(x_ref[...], g_ref[...]).astype(BF16)

    return _call(body, name, (s // tm,),
                 [pl.BlockSpec((tm, D), _row), pl.BlockSpec((1, D), _fixed2)],
                 pl.BlockSpec((tm, D), _row), jax.ShapeDtypeStruct((s, D), BF16), exchange=exchange)(x, g)


def _resident(shape):
    return pl.BlockSpec(shape, lambda *_: (0,) * len(shape), pipeline_mode=pl.Buffered(1))


def _mm_in(h, win, name, exchange=None):
    s = h.shape[0]
    tm = 512

    def body(h_ref, w_ref, o_ref):
        hv = h_ref[...]
        for j in range(NDEV):
            o_ref[:, j * 512:(j + 1) * 512] = _dot(hv, w_ref[j])

    return _call(body, name, (s // tm,),
                 [pl.BlockSpec((tm, D), _row), _resident((NDEV, D, 512))],
                 pl.BlockSpec((tm, INW), _row), jax.ShapeDtypeStruct((s, INW), F32), exchange=exchange)(h, win)


def _bias_table(rel_bias, name):
    wdt = 1024
    rel = jnp.pad(rel_bias, ((0, 0), (0, 640 - NREL))).reshape(HEADS, 1, 640)

    def body(r_ref, o_ref):
        rr = lax.broadcasted_iota(jnp.int32, (640, wdt), 0)
        m = lax.broadcasted_iota(jnp.int32, (640, wdt), 1)
        d = jnp.where(m < KW, m, m - wdt)
        onehot = (jnp.clip(512 - d, -MAXREL, MAXREL) + MAXREL == rr).astype(F32)
        row = jnp.dot(jnp.broadcast_to(r_ref[...], (8, 640)), onehot, preferred_element_type=F32,
                      precision=lax.Precision.HIGHEST)[0:1]
        t = pltpu.roll(jnp.broadcast_to(row, (QB, wdt)), 0, 1, stride=1, stride_axis=0)[:, 0:KW]
        qc = lax.broadcasted_iota(jnp.int32, (QB, KW), 0) // 64
        kc = lax.broadcasted_iota(jnp.int32, (QB, KW), 1) // 64
        o_ref[...] = jnp.where((kc >= qc) & (kc <= qc + 8), t, NEG)

    return _call(body, name, (HEADS,), [pl.BlockSpec((None, 1, 640), lambda h: (h, 0, 0))],
                 pl.BlockSpec((None, QB, KW), lambda h: (h, 0, 0)),
                 jax.ShapeDtypeStruct((HEADS, QB, KW), F32))(rel)


def _attn_scores(q_ref, k_refs, b_ref, i):
    lane = lax.broadcasted_iota(jnp.int32, (QB, 128), 1)
    q = q_ref[...] * SCALE
    qs = [jnp.where(lane < 64, q, 0.0).astype(BF16), jnp.where(lane >= 64, q, 0.0).astype(BF16)]
    k = jnp.concatenate([r[...] for r in k_refs], axis=0).astype(BF16)
    colb = lax.broadcasted_iota(jnp.int32, (1, KW), 1) // QB
    before = jnp.where(colb + i >= 2, 0.0, NEG)
    return qs, [_dot_nt(qs[hh], k) + b_ref[hh] + before for hh in (0, 1)], k, lane


def _attn_fwd(proj, tab, name, exchange=None):
    s = proj.shape[0]
    nq = s // QB

    def body(q_ref, k0, k1, k2, v0, v1, v2, b_ref, o_ref, lse_ref):
        _, scs, _, lane = _attn_scores(q_ref, (k0, k1, k2), b_ref, pl.program_id(1))
        v = jnp.concatenate([v0[...], v1[...], v2[...]], axis=0).astype(BF16)
        v1s = jnp.concatenate([v, jnp.ones((KW, 128), BF16)], axis=1)
        o, lse = [], []
        for sc in scs:
            m = jnp.max(sc, axis=1, keepdims=True)
            ov = _dot(jnp.exp(sc - m).astype(BF16), v1s)
            o.append(ov[:, 0:128] / ov[:, 128:256])
            lse.append(m + jnp.log(ov[:, 128:256]))
        o_ref[...] = jnp.where(lane < 64, o[0], o[1]).astype(BF16)
        lse_ref[...] = jnp.where(lane < 64, lse[0], lse[1])

    def kv(col, d):
        return pl.BlockSpec((QB, 128), lambda p, i: (jnp.maximum(i - 2 + d, 0), col + p))

    in_specs = [pl.BlockSpec((QB, 128), lambda p, i: (i, p))]
    in_specs += [kv(4, d) for d in range(3)] + [kv(8, d) for d in range(3)]
    in_specs += [pl.BlockSpec((2, QB, KW), lambda p, i: (p, 0, 0))]
    out = pl.BlockSpec((QB, 128), lambda p, i: (i, p))
    return _call(body, name, (4, nq), in_specs, [out, out],
                 [jax.ShapeDtypeStruct((s, AW), BF16), jax.ShapeDtypeStruct((s, AW), F32)], exchange=exchange)(
                     proj, proj, proj, proj, proj, proj, proj, tab)


def _pool_fwd(proj, wg, scale, name):
    s = proj.shape[0]
    tb = 512
    e = tb + 16

    def body(u_ref, halo_ref, wg_ref, sc_ref, pooled_ref, mixed_ref):
        i = pl.program_id(0)
        cur = u_ref[...]
        prev = jnp.where(i > 0, halo_ref[...], 0.0)
        xs = jnp.concatenate([prev, cur], axis=0)
        t = i * tb + lax.broadcasted_iota(jnp.int32, (tb, 1), 0)
        for g, w in enumerate(POOL_WINDOWS):
            sl = slice(g * PG, (g + 1) * PG)
            a = xs[:, sl]
            sh = 1
            while sh < w:
                a = a + pltpu.roll(a, sh, 0)
                sh *= 2
            cnt = jnp.minimum(t + 1, w).astype(F32)
            pooled = (a[16:] / cnt - cur[:, sl]).astype(BF16)
            pooled_ref[:, sl] = pooled
            mixed_ref[:, sl] = (_dot(pooled, wg_ref[g].astype(BF16)) * sc_ref[:, sl]).astype(BF16)

    assert e % 8 == 0
    return _call(body, name, (s // tb,),
                 [pl.BlockSpec((tb, PW), lambda i: (i, 3)),
                  pl.BlockSpec((16, PW), lambda i: (jnp.maximum(i * (tb // 16) - 1, 0), 3)),
                  pl.BlockSpec((4, PG, PG), lambda i: (0, 0, 0)), pl.BlockSpec((1, PW), _fixed2)],
                 [pl.BlockSpec((tb, PW), _row), pl.BlockSpec((tb, PW), _row)],
                 [jax.ShapeDtypeStruct((s, PW), BF16), jax.ShapeDtypeStruct((s, PW), BF16)])(proj, proj, wg, scale)


def _branch_fwd(att, mixed, wao, wpo, proj, bgate, name):
    s = att.shape[0]
    tm = 512

    def body(att_ref, mx_ref, wao_ref, wpo_ref, ga_ref, gb_ref, ba_ref, bb_ref, z_ref, ya_ref, yb_ref):
        ya = _dot(att_ref[...], wao_ref[...])
        yb = _dot(mx_ref[...], wpo_ref[...])
        ga = _sigmoid(ga_ref[...] + ba_ref[...])
        gb = _sigmoid(gb_ref[...] + bb_ref[...])
        ya_ref[...] = ya.astype(BF16)
        yb_ref[...] = yb.astype(BF16)
        z_ref[...] = (ga * ya + gb * yb).astype(BF16)

    return _call(body, name, (s // tm,),
                 [pl.BlockSpec((tm, AW), _row), pl.BlockSpec((tm, PW), _row),
                  pl.BlockSpec((AW, D), _fixed2), pl.BlockSpec((PW, D), _fixed2),
                  pl.BlockSpec((tm, D), lambda i: (i, 2)), pl.BlockSpec((tm, D), lambda i: (i, 3)),
                  pl.BlockSpec((1, D), lambda i: (0, 0)), pl.BlockSpec((1, D), lambda i: (0, 1))],
                 [pl.BlockSpec((tm, D), _row)] * 3,
                 [jax.ShapeDtypeStruct((s, D), BF16)] * 3)(att, mixed, wao, wpo, proj, proj, bgate, bgate)


def _mm_o_fwd(z, wo, x, g2, g3, name):
    s = z.shape[0]
    tm = 512

    def body(z_ref, w_ref, x_ref, g2_ref, g3_ref, mix_ref, x1_ref, h2_ref):
        mix = _dot(z_ref[...], w_ref[...])
        x1 = x_ref[...] + _rms(mix, g2_ref[...])
        mix_ref[...] = mix
        x1_ref[...] = x1
        h2_ref[...] = _rms(x1, g3_ref[...]).astype(BF16)

    return _call(body, name, (s // tm,),
                 [pl.BlockSpec((tm, D), _row), pl.BlockSpec((D, D), _fixed2), pl.BlockSpec((tm, D), _row),
                  pl.BlockSpec((1, D), _fixed2), pl.BlockSpec((1, D), _fixed2)],
                 [pl.BlockSpec((tm, D), _row)] * 3,
                 [jax.ShapeDtypeStruct((s, D), F32), jax.ShapeDtypeStruct((s, D), F32),
                  jax.ShapeDtypeStruct((s, D), BF16)])(z, wo, x, g2, g3)


def _mm_up_conv_gate(h2, wup, cw, cb, name, exchange=None):
    s = h2.shape[0]
    tm = 256

    def body(h_ref, w_ref, cw_ref, cb_ref, hu_ref, a_ref, halo):
        first = pl.program_id(0) == 0
        hv = h_ref[...]
        for j in range(NDEV):
            hu_ref[j] = _dot(hv, w_ref[j])
        for jj in range(NFF):
            for c0, w in LANE_COLUMNS:
                lanes = pl.ds(c0, w)
                cwb = [[jnp.broadcast_to(cw_ref[sd, jj, t:t + 1, lanes], (8, w)) for t in range(3)] for sd in (0, 1)]
                cbb = [jnp.broadcast_to(cb_ref[sd, jj, :, lanes], (8, w)) for sd in (0, 1)]
                row = lax.broadcasted_iota(jnp.int32, (8, w), 0)
                rolled = []
                for sd in (0, 1):
                    before = jnp.where(first, 0.0, halo[NFF * sd + jj, :, lanes])
                    rolled += [pltpu.roll(before, 1, 0), pltpu.roll(before, 2, 0)]
                for k in range(tm // 16):
                    outs = []
                    for r in (16 * k, 16 * k + 8):
                        hc, keep = [], []
                        for sd in (0, 1):
                            cur = hu_ref[NFF * sd + jj, pl.ds(r, 8), lanes]
                            r1, r2 = pltpu.roll(cur, 1, 0), pltpu.roll(cur, 2, 0)
                            hc.append(cbb[sd] + cwb[sd][2] * cur
                                      + cwb[sd][0] * jnp.where(row >= 2, r2, rolled[2 * sd + 1])
                                      + cwb[sd][1] * jnp.where(row >= 1, r1, rolled[2 * sd]))
                            keep += [r1, r2]
                        rolled = keep
                        outs.append(_gelu_parts(hc[1])[0] * hc[0])
                    a_ref[jj, pl.ds(16 * k, 16), lanes] = jnp.concatenate(outs, axis=0).astype(BF16)
        for j in range(NDEV):
            halo[j] = hu_ref[j, tm - 8:tm, :]

    small = lambda i: (0, 0, 0, 0)
    return _call(body, name, (s // tm,),
                 [pl.BlockSpec((tm, D), _row), _resident((NDEV, D, FS)), pl.BlockSpec((2, NFF, 3, FS), small),
                  pl.BlockSpec((2, NFF, 1, FS), small)],
                 [pl.BlockSpec((NDEV, tm, FS), lambda i: (0, i, 0)), pl.BlockSpec((NFF, tm, FS), lambda i: (0, i, 0))],
                 [jax.ShapeDtypeStruct((NDEV, s, FS), F32), jax.ShapeDtypeStruct((NFF, s, FS), BF16)],
                 [pltpu.VMEM((NDEV, 8, FS), F32)], exchange=exchange)(h2, wup, cw, cb)


LANE_COLUMNS = [(c0, min(128, FS - c0)) for c0 in range(0, FS, 128)]


def _mm_down_fwd(a, wd, x1, g4, gnext, name, exchange=None):
    s = a.shape[1]
    tm = 512

    def body(a_ref, w_ref, x1_ref, g4_ref, gn_ref, f_ref, x2_ref, hn_ref):
        f = _dot(a_ref[0], w_ref[0:FS, :])
        for j in range(1, NFF):
            f = f + _dot(a_ref[j], w_ref[j * FS:(j + 1) * FS, :])
        x2 = x1_ref[...] + _rms(f, g4_ref[...])
        f_ref[...] = f
        x2_ref[...] = x2
        hn_ref[...] = _rms(x2, gn_ref[...]).astype(BF16)

    return _call(body, name, (s // tm,),
                 [pl.BlockSpec((NFF, tm, FS), lambda i: (0, i, 0)), pl.BlockSpec((NFF * FS, D), _fixed2),
                  pl.BlockSpec((tm, D), _row), pl.BlockSpec((1, D), _fixed2), pl.BlockSpec((1, D), _fixed2)],
                 [pl.BlockSpec((tm, D), _row)] * 3,
                 [jax.ShapeDtypeStruct((s, D), F32), jax.ShapeDtypeStruct((s, D), F32),
                  jax.ShapeDtypeStruct((s, D), BF16)], exchange=exchange)(a, wd, x1, g4, gnext)


def _mm_down_loss(a, wd, x1, g4, target, name):
    s = a.shape[1]
    tm = 512

    def body(a_ref, w_ref, x1_ref, g4_ref, t_ref, dy_ref, df_ref, dg_ref, loss_ref):
        i = pl.program_id(0)
        f = _dot(a_ref[0], w_ref[0:FS, :])
        for j in range(1, NFF):
            f = f + _dot(a_ref[j], w_ref[j * FS:(j + 1) * FS, :])
        err = x1_ref[...] + _rms(f, g4_ref[...]) - t_ref[...]
        dy = err * (1.0 / D)
        df, dg = _rms_bwd(f, g4_ref[...], dy)
        dy_ref[...] = dy
        df_ref[...] = df.astype(BF16)
        part = 0.5 * jnp.sum(jnp.mean(err * err, axis=-1, keepdims=True), axis=0, keepdims=True)

        @pl.when(i == 0)
        def _():
            loss_ref[...] = jnp.zeros_like(loss_ref)
            dg_ref[...] = jnp.zeros_like(dg_ref)

        loss_ref[...] += jnp.broadcast_to(part, loss_ref.shape)
        dg_ref[...] += dg

    return _call(body, name, (s // tm,),
                 [pl.BlockSpec((NFF, tm, FS), lambda i: (0, i, 0)), pl.BlockSpec((NFF * FS, D), _fixed2),
                  pl.BlockSpec((tm, D), _row), pl.BlockSpec((1, D), _fixed2), pl.BlockSpec((tm, D), _row)],
                 [pl.BlockSpec((tm, D), _row), pl.BlockSpec((tm, D), _row), pl.BlockSpec((1, D), _fixed2),
                  pl.BlockSpec((8, 128), _fixed2)],
                 [jax.ShapeDtypeStruct((s, D), F32), jax.ShapeDtypeStruct((s, D), BF16),
                  jax.ShapeDtypeStruct((1, D), F32), jax.ShapeDtypeStruct((8, 128), F32)])(a, wd, x1, g4, target)


def _conv_gate_bwd(hu, df, wd, cw, cb, name, exchange=None):
    s = hu.shape[2]
    tb = 512
    nt = s // tb
    e = tb + 16

    def body(hu_ref, prev_ref, next_ref, df_ref, dfn_ref, wd_ref, cw_ref, cb_ref, dhu_ref, dcw_ref, dcb_ref,
             dabuf, dbuf):
        i = pl.program_id(1)
        first, last = i == 0, i == nt - 1
        dabuf[0:tb] = _dot_nt(df_ref[...], wd_ref[...])
        dabuf[tb:tb + 8] = jnp.where(last, 0.0, _dot_nt(dfn_ref[...], wd_ref[...])[0:8])

        @pl.when(i == 0)
        def _():
            dcw_ref[...] = jnp.zeros_like(dcw_ref)
            dcb_ref[...] = jnp.zeros_like(dcb_ref)

        for c0, w in LANE_COLUMNS:
            lanes = pl.ds(c0, w)
            cwb = [[jnp.broadcast_to(cw_ref[sd, t:t + 1, lanes], (8, w)) for t in range(3)] for sd in (0, 1)]
            cbb = [jnp.broadcast_to(cb_ref[sd, :, lanes], (8, w)) for sd in (0, 1)]

            row = lax.broadcasted_iota(jnp.int32, (8, w), 0)

            def tile(k, carry, summed=True):
                sums, rolled = carry[:8], carry[8:]
                r = 8 + 8 * k
                xs, keep = [], []
                for sd in (0, 1):
                    if summed:
                        cur = hu_ref[sd, pl.ds(r - 8, 8), lanes]
                    else:
                        cur = jnp.where(last, 0.0, next_ref[sd, :, lanes])
                    r1, r2 = pltpu.roll(cur, 1, 0), pltpu.roll(cur, 2, 0)
                    xs.append([jnp.where(row >= 2, r2, rolled[2 * sd + 1]), jnp.where(row >= 1, r1, rolled[2 * sd]),
                               cur])
                    keep += [r1, r2]
                hc = [cbb[sd] + cwb[sd][2] * xs[sd][2] + cwb[sd][0] * xs[sd][0] + cwb[sd][1] * xs[sd][1]
                      for sd in (0, 1)]
                da = dabuf[pl.ds(r - 8, 8), lanes]
                gl, dgl = _gelu_parts(hc[1])
                dhc = (da * gl, da * hc[0] * dgl)
                for sd in (0, 1):
                    dbuf[sd, pl.ds(r, 8), lanes] = dhc[sd]
                if not summed:
                    return carry
                new = []
                for sd in (0, 1):
                    new += [sums[4 * sd + t] + dhc[sd] * xs[sd][t] for t in range(3)] + [sums[4 * sd + 3] + dhc[sd]]
                return tuple(new + keep)

            start = [jnp.zeros((8, w), F32) for _ in range(8)]
            for sd in (0, 1):
                halo = jnp.where(first, 0.0, prev_ref[sd, :, lanes])
                start += [pltpu.roll(halo, 1, 0), pltpu.roll(halo, 2, 0)]
            def tiles(k4, carry):
                for u in range(4):
                    carry = tile(4 * k4 + u, carry)
                return carry

            carry = tuple(start)
            for k4 in range(tb // 32):
                carry = tiles(k4, carry)
            tile(tb // 8, carry, summed=False)
            sums = carry[:8]

            def up(v):
                return pltpu.roll(v, 7, 0), pltpu.roll(v, 6, 0)

            def out_tile(k, carry):
                r = 8 + 16 * k
                new = []
                for sd in (0, 1):
                    va, va1, va2 = carry[3 * sd:3 * sd + 3]
                    vb, vc = dbuf[sd, pl.ds(r + 8, 8), lanes], dbuf[sd, pl.ds(r + 16, 8), lanes]
                    (vb1, vb2), (vc1, vc2) = up(vb), up(vc)
                    c0b, c1b, c2b = cwb[sd]
                    top = c2b * va + c1b * jnp.where(row <= 6, va1, vb1) + c0b * jnp.where(row <= 5, va2, vb2)
                    bot = c2b * vb + c1b * jnp.where(row <= 6, vb1, vc1) + c0b * jnp.where(row <= 5, vb2, vc2)
                    dhu_ref[sd, pl.ds(16 * k, 16), lanes] = jnp.concatenate(
                        [top, bot], axis=0).astype(BF16)
                    new += [vc, vc1, vc2]
                return tuple(new)

            begin = []
            for sd in (0, 1):
                va = dbuf[sd, 8:16, lanes]
                begin += [va, *up(va)]
            def out_tiles(k2, carry):
                return out_tile(2 * k2 + 1, out_tile(2 * k2, carry))

            carry = tuple(begin)
            for k2 in range(tb // 32):
                carry = out_tiles(k2, carry)
            for sd in (0, 1):
                for t in range(3):
                    dcw_ref[sd, t:t + 1, lanes] += jnp.sum(sums[4 * sd + t], axis=0, keepdims=True)
                dcb_ref[sd, :, lanes] += jnp.sum(sums[4 * sd + 3], axis=0, keepdims=True)

    nb8, nb16 = s // 8, s // 16
    return _call(body, name, (NFF, nt),
                 [pl.BlockSpec((2, None, tb, FS), lambda j, i: (0, j, i, 0)),
                  pl.BlockSpec((2, None, 8, FS), lambda j, i: (0, j, jnp.maximum(i * (tb // 8) - 1, 0), 0)),
                  pl.BlockSpec((2, None, 8, FS), lambda j, i: (0, j, jnp.minimum((i + 1) * (tb // 8), nb8 - 1), 0)),
                  pl.BlockSpec((tb, D), lambda j, i: (i, 0)),
                  pl.BlockSpec((16, D), lambda j, i: (jnp.minimum((i + 1) * (tb // 16), nb16 - 1), 0)),
                  pl.BlockSpec((FS, D), lambda j, i: (j, 0)),
                  pl.BlockSpec((2, None, 3, FS), lambda j, i: (0, j, 0, 0)),
                  pl.BlockSpec((2, None, 1, FS), lambda j, i: (0, j, 0, 0))],
                 [pl.BlockSpec((2, None, tb, FS), lambda j, i: (0, j, i, 0)),
                  pl.BlockSpec((2, None, 3, FS), lambda j, i: (0, j, 0, 0)),
                  pl.BlockSpec((2, None, 1, FS), lambda j, i: (0, j, 0, 0))],
                 [jax.ShapeDtypeStruct((2, NFF, s, FS), BF16), jax.ShapeDtypeStruct((2, NFF, 3, FS), F32),
                  jax.ShapeDtypeStruct((2, NFF, 1, FS), F32)],
                 [pltpu.VMEM((tb + 8, FS), F32), pltpu.VMEM((2, e, FS), F32)],
                 exchange=exchange)(hu, hu, hu, df, df, wd, cw, cb)


def _dw_down(a, df, name):
    s = a.shape[1]
    tk = 512
    nk = s // tk

    def body(a_ref, df_ref, o_ref, acc):
        k = pl.program_id(1)

        @pl.when(k == 0)
        def _():
            acc[...] = jnp.zeros_like(acc)

        acc[...] += _dot_tn(a_ref[...], df_ref[...])

        @pl.when(k == nk - 1)
        def _():
            o_ref[0] = acc[0:FS // 2, :].astype(BF16)
            o_ref[1] = acc[FS // 2:FS, :].astype(BF16)

    return _call(body, name, (NFF, nk),
                 [pl.BlockSpec((None, tk, FS), lambda j, k: (j, k, 0)), pl.BlockSpec((tk, D), lambda j, k: (k, 0))],
                 pl.BlockSpec((2, FS // 2, D), lambda j, k: (j, 0, 0)),
                 jax.ShapeDtypeStruct((NDEV, FS // 2, D), BF16), [pltpu.VMEM((FS, D), F32)])(a, df)


def _mm_dh2(dhu, wup, x1, g3, dx2, mix, g2, name):
    s = x1.shape[0]
    tm = 512

    def body(dhu_ref, w_ref, x1_ref, g3_ref, dx2_ref, mix_ref, g2_ref, dx1_ref, dmix_ref, dg3_ref, dg2_ref):
        i = pl.program_id(0)

        @pl.when(i == 0)
        def _():
            dg3_ref[...] = jnp.zeros_like(dg3_ref)
            dg2_ref[...] = jnp.zeros_like(dg2_ref)

        dh2 = _dot_nt(dhu_ref[0, 0], w_ref[0])
        for j in range(1, NDEV):
            dh2 = dh2 + _dot_nt(dhu_ref[j // NFF, j % NFF], w_ref[j])
        dn, dg3 = _rms_bwd(x1_ref[...], g3_ref[...], dh2)
        dx1 = dx2_ref[...] + dn
        dmix, dg2 = _rms_bwd(mix_ref[...], g2_ref[...], dx1)
        dx1_ref[...] = dx1
        dmix_ref[...] = dmix.astype(BF16)
        dg3_ref[...] += dg3
        dg2_ref[...] += dg2

    return _call(body, name, (s // tm,),
                 [pl.BlockSpec((2, NFF, tm, FS), lambda i: (0, 0, i, 0)), _resident((NDEV, D, FS)),
                  pl.BlockSpec((tm, D), _row), pl.BlockSpec((1, D), _fixed2), pl.BlockSpec((tm, D), _row),
                  pl.BlockSpec((tm, D), _row), pl.BlockSpec((1, D), _fixed2)],
                 [pl.BlockSpec((tm, D), _row), pl.BlockSpec((tm, D), _row), pl.BlockSpec((1, D), _fixed2),
                  pl.BlockSpec((1, D), _fixed2)],
                 [jax.ShapeDtypeStruct((s, D), F32), jax.ShapeDtypeStruct((s, D), BF16),
                  jax.ShapeDtypeStruct((1, D), F32), jax.ShapeDtypeStruct((1, D), F32)])(
                     dhu, wup, x1, g3, dx2, mix, g2)


def _dw_up(h2, dhu, name):
    s = h2.shape[0]
    tk = 512
    nk = s // tk

    def body(h_ref, d_ref, o_ref, acc):
        k = pl.program_id(1)

        @pl.when(k == 0)
        def _():
            acc[...] = jnp.zeros_like(acc)

        ht = h_ref[...].T
        for j in range(NFF):
            acc[j] += _dot(ht, d_ref[j])

        @pl.when(k == nk - 1)
        def _():
            o_ref[...] = acc[...].astype(BF16)

    return _call(body, name, (2, nk),
                 [pl.BlockSpec((tk, D), lambda hf, k: (k, 0)),
                  pl.BlockSpec((None, NFF, tk, FS), lambda hf, k: (hf, 0, k, 0))],
                 pl.BlockSpec((NFF, D, FS), lambda hf, k: (hf, 0, 0)),
                 jax.ShapeDtypeStruct((NDEV, D, FS), BF16), [pltpu.VMEM((NFF, D, FS), F32)])(h2, dhu)


def _mm_dz(dmix, wo, proj, bgate, ya, yb, name):
    s = dmix.shape[0]
    tm = 512

    def body(dm_ref, w_ref, ga_ref, gb_ref, ba_ref, bb_ref, ya_ref, yb_ref, dya_ref, dyb_ref, dg_ref, dbg_ref):
        i = pl.program_id(0)
        dz = _dot_nt(dm_ref[...], w_ref[...])
        ga = _sigmoid(ga_ref[...] + ba_ref[...])
        gb = _sigmoid(gb_ref[...] + bb_ref[...])
        dya_ref[...] = (dz * ga).astype(BF16)
        dyb_ref[...] = (dz * gb).astype(BF16)
        dga = dz * ya_ref[...].astype(F32) * ga * (1.0 - ga)
        dgb = dz * yb_ref[...].astype(F32) * gb * (1.0 - gb)
        dg_ref[:, 0:D] = dga.astype(BF16)
        dg_ref[:, D:2 * D] = dgb.astype(BF16)

        @pl.when(i == 0)
        def _():
            dbg_ref[...] = jnp.zeros_like(dbg_ref)

        dbg_ref[:, 0:D] += jnp.sum(dga, axis=0, keepdims=True)
        dbg_ref[:, D:2 * D] += jnp.sum(dgb, axis=0, keepdims=True)

    return _call(body, name, (s // tm,),
                 [pl.BlockSpec((tm, D), _row), pl.BlockSpec((D, D), _fixed2),
                  pl.BlockSpec((tm, D), lambda i: (i, 2)), pl.BlockSpec((tm, D), lambda i: (i, 3)),
                  pl.BlockSpec((1, D), lambda i: (0, 0)), pl.BlockSpec((1, D), lambda i: (0, 1)),
                  pl.BlockSpec((tm, D), _row), pl.BlockSpec((tm, D), _row)],
                 [pl.BlockSpec((tm, D), _row), pl.BlockSpec((tm, D), _row), pl.BlockSpec((tm, 2 * D), _row),
                  pl.BlockSpec((1, 2 * D), _fixed2)],
                 [jax.ShapeDtypeStruct((s, D), BF16), jax.ShapeDtypeStruct((s, D), BF16),
                  jax.ShapeDtypeStruct((s, 2 * D), BF16), jax.ShapeDtypeStruct((1, 2 * D), F32)])(
                     dmix, wo, proj, proj, bgate, bgate, ya, yb)


def _dw_o(z, dmix, name):
    s = z.shape[0]
    tk = 512
    nk = s // tk

    def body(z_ref, d_ref, o_ref, acc):
        k = pl.program_id(0)

        @pl.when(k == 0)
        def _():
            acc[...] = jnp.zeros_like(acc)

        acc[...] += _dot_tn(z_ref[...], d_ref[...])

        @pl.when(k == nk - 1)
        def _():
            for j in range(NDEV):
                o_ref[j] = acc[j * 128:(j + 1) * 128, :].astype(BF16)

    return _call(body, name, (nk,),
                 [pl.BlockSpec((tk, D), _row), pl.BlockSpec((tk, D), _row)],
                 pl.BlockSpec((NDEV, 128, D), lambda k: (0, 0, 0)),
                 jax.ShapeDtypeStruct((NDEV, 128, D), BF16), [pltpu.VMEM((D, D), F32)])(z, dmix)


def _branch_bwd(dya, dyb, wao, wpo, name):
    s = dya.shape[0]
    tm = 512

    def body(dya_ref, dyb_ref, wao_ref, wpo_ref, datt_ref, dmx_ref):
        datt_ref[...] = _dot_nt(dya_ref[...], wao_ref[...]).astype(BF16)
        dmx_ref[...] = _dot_nt(dyb_ref[...], wpo_ref[...])

    return _call(body, name, (s // tm,),
                 [pl.BlockSpec((tm, D), _row), pl.BlockSpec((tm, D), _row),
                  pl.BlockSpec((AW, D), _fixed2), pl.BlockSpec((PW, D), _fixed2)],
                 [pl.BlockSpec((tm, AW), _row), pl.BlockSpec((tm, PW), _row)],
                 [jax.ShapeDtypeStruct((s, AW), BF16), jax.ShapeDtypeStruct((s, PW), F32)])(dya, dyb, wao, wpo)


def _dw_branch(att, mixed, dya, dyb, name):
    s = att.shape[0]
    tk = 512
    nk = s // tk

    def body(att_ref, mx_ref, dya_ref, dyb_ref, oa_ref, ob_ref, acca, accb):
        k = pl.program_id(0)

        @pl.when(k == 0)
        def _():
            acca[...] = jnp.zeros_like(acca)
            accb[...] = jnp.zeros_like(accb)

        acca[...] += _dot_tn(att_ref[...], dya_ref[...])
        accb[...] += _dot_tn(mx_ref[...], dyb_ref[...])

        @pl.when(k == nk - 1)
        def _():
            for j in range(NDEV):
                oa_ref[j] = acca[:, j * 128:(j + 1) * 128].astype(BF16)
                ob_ref[j] = accb[:, j * 128:(j + 1) * 128].astype(BF16)

    out = jax.ShapeDtypeStruct((NDEV, AW, 128), BF16)
    return _call(body, name, (nk,),
                 [pl.BlockSpec((tk, AW), _row), pl.BlockSpec((tk, PW), _row),
                  pl.BlockSpec((tk, D), _row), pl.BlockSpec((tk, D), _row)],
                 [pl.BlockSpec((NDEV, AW, 128), lambda k: (0, 0, 0))] * 2, [out, out],
                 [pltpu.VMEM((AW, D), F32), pltpu.VMEM((PW, D), F32)])(att, mixed, dya, dyb)


def _attn_bwd(proj, datt, att, lse, tab, name, exchange=None):
    s = proj.shape[0]
    nq = s // QB

    def body(q_ref, k0, k1, k2, v0, v1, v2, do_ref, o_ref, lse_ref, b_ref, dq_ref, dk_ref, dv_ref, db_ref,
             dka, dkb, dva, dvb):
        i = pl.program_id(1)

        @pl.when(i == 0)
        def _():
            for r in (dka, dkb, dva, dvb):
                r[...] = jnp.zeros_like(r)
            db_ref[...] = jnp.zeros_like(db_ref)

        @pl.when(i < nq)
        def _():
            qs, scs, k, lane = _attn_scores(q_ref, (k0, k1, k2), b_ref, i)
            v = jnp.concatenate([v0[...], v1[...], v2[...]], axis=0).astype(BF16)
            do = do_ref[...]
            doo = do.astype(F32) * o_ref[...].astype(F32)
            lse = lse_ref[...]
            dq = jnp.zeros((QB, 128), F32)
            dkt = jnp.zeros((128, KW), F32)
            dvt = jnp.zeros((128, KW), F32)
            for hh in (0, 1):
                mine = (lane < 64) if hh == 0 else (lane >= 64)
                doh = jnp.where(mine, do, jnp.zeros_like(do))
                kmask = lax.broadcasted_iota(jnp.int32, (KW, 128), 1)
                kh = jnp.where((kmask < 64) if hh == 0 else (kmask >= 64), k, jnp.zeros_like(k))
                p = jnp.exp(scs[hh] - lse[:, 64 * hh:64 * hh + 1])
                drow = jnp.sum(jnp.where(mine, doo, 0.0), axis=1, keepdims=True)
                ds = p * (_dot_nt(doh, v) - drow)
                db_ref[hh] += ds
                dsb = ds.astype(BF16)
                dq = dq + _dot(dsb, kh)
                dkt = dkt + _dot_tn(qs[hh], dsb)
                dvt = dvt + _dot_tn(doh, p.astype(BF16))
            dkw, dvw = dkt.T, dvt.T
            dq_ref[...] = (dq * SCALE).astype(BF16)
            dk_ref[...] = (dka[...] + dkw[0:QB]).astype(BF16)
            dka[...] = dkb[...] + dkw[QB:2 * QB]
            dkb[...] = dkw[2 * QB:3 * QB]
            dv_ref[...] = (dva[...] + dvw[0:QB]).astype(BF16)
            dva[...] = dvb[...] + dvw[QB:2 * QB]
            dvb[...] = dvw[2 * QB:3 * QB]

        @pl.when(i >= nq)
        def _():
            dk_ref[...] = dka[...].astype(BF16)
            dka[...] = dkb[...]
            dkb[...] = jnp.zeros_like(dkb)
            dv_ref[...] = dva[...].astype(BF16)
            dva[...] = dvb[...]
            dvb[...] = jnp.zeros_like(dvb)

    def kv(col, d):
        return pl.BlockSpec((QB, 128), lambda p, i: (jnp.clip(i - 2 + d, 0, nq - 1), col + p))

    cur = lambda p, i: (jnp.minimum(i, nq - 1), p)
    done = lambda p, i: (jnp.maximum(i - 2, 0), p)
    in_specs = [pl.BlockSpec((QB, 128), cur)]
    in_specs += [kv(4, d) for d in range(3)] + [kv(8, d) for d in range(3)]
    in_specs += [pl.BlockSpec((QB, 128), cur)] * 3 + [pl.BlockSpec((2, QB, KW), lambda p, i: (p, 0, 0))]
    o = jax.ShapeDtypeStruct((s, AW), BF16)
    return _call(body, name, (4, nq + 2), in_specs,
                 [pl.BlockSpec((QB, 128), cur), pl.BlockSpec((QB, 128), done), pl.BlockSpec((QB, 128), done),
                  pl.BlockSpec((2, QB, KW), lambda p, i: (p, 0, 0))],
                 [o, o, o, jax.ShapeDtypeStruct((HEADS, QB, KW), F32)],
                 [pltpu.VMEM((QB, 128), F32)] * 4, exchange=exchange)(
                     proj, proj, proj, proj, proj, proj, proj, datt, att, lse, tab)


def _rel_bias_grad(dtab, name):
    wdt = 640

    def body(x_ref, o_ref):
        x = x_ref[...]
        xc = x[0:64, 0:wdt]
        for qc in range(1, QB // 64):
            xc = xc + pltpu.roll(x[qc * 64:(qc + 1) * 64, :], KW - qc * 64, 1)[:, 0:wdt]
        r = lax.broadcasted_iota(jnp.int32, (64, 64), 0)
        c = lax.broadcasted_iota(jnp.int32, (64, 64), 1)
        flip = (r + c == 63).astype(F32)
        y = jnp.dot(flip, xc, preferred_element_type=F32, precision=lax.Precision.HIGHEST)
        z = pltpu.roll(y, 0, 1, stride=1, stride_axis=0)
        t = jnp.broadcast_to(jnp.sum(z, axis=0, keepdims=True), (8, wdt))
        e = lax.broadcasted_iota(jnp.int32, (wdt, wdt), 0)
        rr = lax.broadcasted_iota(jnp.int32, (wdt, wdt), 1)
        onehot = (jnp.clip(BAND - 1 - e, -MAXREL, MAXREL) + MAXREL == rr).astype(F32)
        o_ref[...] = jnp.dot(t, onehot, preferred_element_type=F32, precision=lax.Precision.HIGHEST)

    return _call(body, name, (HEADS,), [pl.BlockSpec((None, QB, KW), lambda h: (h, 0, 0))],
                 pl.BlockSpec((None, 8, wdt), lambda h: (h, 0, 0)),
                 jax.ShapeDtypeStruct((HEADS, 8, wdt), F32))(dtab)


def _pool_bwd(dmixed, pooled, wg, scale, name):
    s = dmixed.shape[0]
    tb = 512
    nt = s // tb
    e = tb + 16

    def body(dm_ref, dmn_ref, pl_ref, wg_ref, sc_ref, du_ref, dwg_ref, dsc_ref):
        i = pl.program_id(0)
        dm = jnp.concatenate([dm_ref[...], jnp.where(i == nt - 1, 0.0, dmn_ref[...])], axis=0)
        t = i * tb + lax.broadcasted_iota(jnp.int32, (e, 1), 0)

        @pl.when(i == 0)
        def _():
            dwg_ref[...] = jnp.zeros_like(dwg_ref)
            dsc_ref[...] = jnp.zeros_like(dsc_ref)

        for g, w in enumerate(POOL_WINDOWS):
            sl = slice(g * PG, (g + 1) * PG)
            wgb = wg_ref[g].astype(BF16)
            pb = pl_ref[:, sl]
            dsc_ref[:, sl] += jnp.sum(dm[0:tb, sl] * _dot(pb, wgb), axis=0, keepdims=True)
            dpre = (dm[:, sl] * sc_ref[:, sl]).astype(BF16)
            dwg_ref[g] += _dot_tn(pb, dpre[0:tb])
            dpool = _dot_nt(dpre, wgb)
            a = dpool / jnp.minimum(t + 1, w).astype(F32)
            sh = 1
            while sh < w:
                a = a + pltpu.roll(a, e - sh, 0)
                sh *= 2
            du_ref[:, sl] = (a[0:tb] - dpool[0:tb]).astype(BF16)

    nb16 = s // 16
    return _call(body, name, (nt,),
                 [pl.BlockSpec((tb, PW), _row),
                  pl.BlockSpec((16, PW), lambda i: (jnp.minimum((i + 1) * (tb // 16), nb16 - 1), 0)),
                  pl.BlockSpec((tb, PW), _row), pl.BlockSpec((4, PG, PG), lambda i: (0, 0, 0)),
                  pl.BlockSpec((1, PW), _fixed2)],
                 [pl.BlockSpec((tb, PW), _row), pl.BlockSpec((4, PG, PG), lambda i: (0, 0, 0)),
                  pl.BlockSpec((1, PW), _fixed2)],
                 [jax.ShapeDtypeStruct((s, PW), BF16), jax.ShapeDtypeStruct((4, PG, PG), F32),
                  jax.ShapeDtypeStruct((1, PW), F32)])(dmixed, dmixed, pooled, wg, scale)


def _dproj_specs(t):
    return [pl.BlockSpec((t, 512), _row)] * 4 + [pl.BlockSpec((t, 2 * D), _row)]


def _dproj_segment(refs, n):
    return refs[n][...] if n < 4 else refs[4][:, (n - 4) * 512:(n - 3) * 512]


def _mm_dh(segs, win, x, g1, dx1, below, name, exchange=None):
    s = x.shape[0]
    tm = 512

    def body(dq_ref, dk_ref, dv_ref, du_ref, dg_ref, w_ref, x_ref, g1_ref, dx1_ref, *rest):
        i = pl.program_id(0)
        outs = rest[2:] if below else rest

        @pl.when(i == 0)
        def _():
            for ref in outs[1::2]:
                ref[...] = jnp.zeros_like(ref)

        pieces = (dq_ref, dk_ref, dv_ref, du_ref, dg_ref)
        dh = _dot_nt(_dproj_segment(pieces, 0), w_ref[0])
        for n in range(1, NDEV):
            dh = dh + _dot_nt(_dproj_segment(pieces, n), w_ref[n])
        dn, dg1 = _rms_bwd(x_ref[...], g1_ref[...], dh)
        dx = dx1_ref[...] + dn
        outs[0][...] = dx
        outs[1][...] += dg1
        if below:
            df, dg4 = _rms_bwd(rest[0][...], rest[1][...], dx)
            outs[2][...] = df.astype(BF16)
            outs[3][...] += dg4

    rows, gain = pl.BlockSpec((tm, D), _row), pl.BlockSpec((1, D), _fixed2)
    vec = jax.ShapeDtypeStruct((1, D), F32)
    return _call(body, name, (s // tm,),
                 _dproj_specs(tm) + [_resident((NDEV, D, 512)), rows, gain, rows] + ([rows, gain] if below else []),
                 [rows, gain] + ([rows, gain] if below else []),
                 [jax.ShapeDtypeStruct((s, D), F32), vec] + ([jax.ShapeDtypeStruct((s, D), BF16), vec] if below else []),
                 exchange=exchange)(*segs, win, x, g1, dx1, *below)


def _dw_in(h, segs, name):
    s = h.shape[0]
    tk = 512
    nk = s // tk

    def body(h_ref, dq_ref, dk_ref, dv_ref, du_ref, dg_ref, o_ref, acc):
        k = pl.program_id(0)

        @pl.when(k == 0)
        def _():
            acc[...] = jnp.zeros_like(acc)

        ht = h_ref[...].T
        pieces = (dq_ref, dk_ref, dv_ref, du_ref, dg_ref)
        for n in range(NDEV):
            acc[n] += _dot(ht, _dproj_segment(pieces, n))

        @pl.when(k == nk - 1)
        def _():
            o_ref[...] = acc[...].astype(BF16)

    return _call(body, name, (nk,), [pl.BlockSpec((tk, D), _row)] + _dproj_specs(tk),
                 pl.BlockSpec((NDEV, D, 512), lambda k: (0, 0, 0)),
                 jax.ShapeDtypeStruct((NDEV, D, 512), BF16), [pltpu.VMEM((NDEV, D, 512), F32)])(h, *segs)


def _slot(px, py, pc):
    return 4 * px + 2 * py + pc


def _gather_exchange(shards, deliver):
    n = len(shards)

    def phases(ins, outs, sems):
        send_sems, recv_sems, local_sems = sems
        x, y, c = lax.axis_index("x"), lax.axis_index("y"), lax.axis_index("c")
        me, sibling = (x, y, c), (x, y, 1 - c)
        chips = [(1 - x, y), (x, 1 - y), (1 - x, 1 - y)]

        def copy(w, k, block, to, src=None):
            dst = outs[w].at[_slot(*block)]
            return pltpu.make_async_remote_copy(
                src_ref=dst if src is None else src, dst_ref=dst, send_sem=send_sems.at[7 * w + k],
                recv_sem=recv_sems.at[7 * w + k], device_id=to, device_id_type=MESH)

        def mine(w):
            return pltpu.make_async_copy(ins[w], outs[w].at[_slot(*me)], local_sems.at[w])

        def first(w):
            return [copy(w, 0, me, sibling, src=ins[w])] + [
                copy(w, 1 + j, me, (*chip, c), src=ins[w]) for j, chip in enumerate(chips)]

        def passed(w):
            return [copy(w, 4 + j, (*chip, c), sibling) for j, chip in enumerate(chips)]

        def send():
            for w in range(n):
                mine(w).start()
                for cp in first(w):
                    cp.start()

        def forward():
            for w in range(n):
                for j, chip in enumerate(chips):
                    copy(w, 1 + j, (*chip, c), me).wait_recv()
                    passed(w)[j].start()

        def finish():
            for w in range(n):
                copy(w, 0, sibling, me).wait_recv()
                for j, chip in enumerate(chips):
                    copy(w, 4 + j, (*chip, 1 - c), me).wait_recv()
            for w in range(n):
                for cp in first(w) + passed(w):
                    cp.wait_send()
                mine(w).wait()

        return [send, forward, finish]

    return _Exchange(shards, [jax.ShapeDtypeStruct((NDEV,) + a.shape, a.dtype) for a in shards],
                     [pltpu.SemaphoreType.DMA((7 * n,)), pltpu.SemaphoreType.DMA((7 * n,)),
                      pltpu.SemaphoreType.DMA((n,))], phases, deliver)


def _scatter_exchange(parts, deliver):
    n = len(parts)

    def phases(ins, outs, sems):
        send_sems, recv_sems, local_sems = sems
        x, y, c = lax.axis_index("x"), lax.axis_index("y"), lax.axis_index("c")
        me = _slot(x, y, c)
        peers = [((1 - x) if r & 4 else x, (1 - y) if r & 2 else y, (1 - c) if r & 1 else c) for r in range(1, NDEV)]

        def mine(w):
            return pltpu.make_async_copy(ins[w].at[me], outs[w].at[me], local_sems.at[w])

        def copy(w, r, block_here, block_there):
            return pltpu.make_async_remote_copy(
                src_ref=ins[w].at[block_here], dst_ref=outs[w].at[block_there], send_sem=send_sems.at[7 * w + r],
                recv_sem=recv_sems.at[7 * w + r], device_id=peers[r], device_id_type=MESH)

        def send():
            for w in range(n):
                mine(w).start()
                for r, peer in enumerate(peers):
                    copy(w, r, _slot(*peer), me).start()

        def finish():
            for w in range(n):
                for r, peer in enumerate(peers):
                    copy(w, r, me, _slot(*peer)).wait_recv()
            for w in range(n):
                for r, peer in enumerate(peers):
                    copy(w, r, _slot(*peer), me).wait_send()
                mine(w).wait()

        return [send, finish]

    return _Exchange(parts, [jax.ShapeDtypeStruct(a.shape, a.dtype) for a in parts],
                     [pltpu.SemaphoreType.DMA((7 * n,)), pltpu.SemaphoreType.DMA((7 * n,)),
                      pltpu.SemaphoreType.DMA((n,))], phases, deliver)


def _adamw(w, g, m, v):
    m = ADAM_B1 * m + (1.0 - ADAM_B1) * g
    v = ADAM_B2 * v + (1.0 - ADAM_B2) * (g * g)
    m_hat = m / (1.0 - ADAM_B1 ** ADAM_STEP)
    v_hat = v / (1.0 - ADAM_B2 ** ADAM_STEP)
    delta = -ADAM_LR * (m_hat / (jnp.sqrt(v_hat) + ADAM_EPS) + ADAM_WD * w)
    return delta, m, v


def _adamw_sharded(recvs, w, m, v, tr, name, exchange=None):
    _, r, cdim = w.shape
    nr = r // tr

    def body(r0_ref, r1_ref, w_ref, m_ref, v_ref, g_ref, d_ref, nm_ref, nv_ref):
        l = pl.program_id(0)

        def total(ref):
            acc = ref[0].astype(F32)
            for k in range(1, NDEV):
                acc = acc + ref[k].astype(F32)
            return acc

        g = jnp.where(l == 0, total(r0_ref), total(r1_ref))
        d, nm, nv = _adamw(w_ref[...], g, m_ref[...], v_ref[...])
        g_ref[...] = g
        d_ref[...] = d
        nm_ref[...] = nm
        nv_ref[...] = nv

    mine = pl.BlockSpec((None, tr, cdim), lambda l, i: (l, i, 0))
    out = jax.ShapeDtypeStruct(w.shape, F32)
    return _call(body, name, (DEPTH, nr),
                 [pl.BlockSpec((NDEV, tr, cdim), lambda l, i: (0, jnp.where(l == 0, i, nr - 1), 0)),
                  pl.BlockSpec((NDEV, tr, cdim), lambda l, i: (0, jnp.where(l == 1, i, 0), 0)),
                  mine, mine, mine],
                 [mine] * 4, [out] * 4, exchange=exchange)(recvs[0], recvs[1], w, m, v)


def _adamw_small(parts, w, m, v, name):
    r = w.shape[0]

    def body(p_ref, w_ref, m_ref, v_ref, g_ref, d_ref, nm_ref, nv_ref):
        g = p_ref[0]
        for k in range(1, NDEV):
            g = g + p_ref[k]
        d, nm, nv = _adamw(w_ref[...], g, m_ref[...], v_ref[...])
        g_ref[...] = g
        d_ref[...] = d
        nm_ref[...] = nm
        nv_ref[...] = nv

    whole = pl.BlockSpec((r, 1024), _fixed2)
    out = jax.ShapeDtypeStruct((r, 1024), F32)
    return _call(body, name, (1,), [pl.BlockSpec((NDEV, r, 1024), lambda i: (0, 0, 0)), whole, whole, whole],
                 [whole] * 4, [out] * 4)(parts, w, m, v)


SMALL = (("norm_mix_pre", (D,)), ("b_gate", (2 * D,)), ("rel_bias", (HEADS, NREL)), ("w_pool_group", (4, PG, PG)),
         ("pool_scale", (PW,)), ("norm_mix_post", (D,)), ("norm_ffn_pre", (D,)), ("conv_b", (NFF * 2 * FS,)),
         ("norm_ffn_post", (D,)))
SHARDED = (("w_in", 256), ("w_attn_out", 512), ("w_pool_out", 512), ("w_o", 128), ("w_up", 256), ("conv_w", 3),
           ("w_down", 176))
SMALL_ROWS = 168


def _pack_small(tree):
    flat = jnp.concatenate([tree[name].reshape(-1) for name, _ in SMALL])
    return jnp.pad(flat, (0, SMALL_ROWS * 1024 - flat.shape[0])).reshape(SMALL_ROWS, 1024)


def _unpack_small(packed):
    flat = packed.reshape(-1)
    out, at = {}, 0
    for name, shape in SMALL:
        size = DEPTH * int(np.prod(shape))
        out[name] = flat[at:at + size].reshape((DEPTH,) + shape)
        at += size
    return out


def _layer_fwd(l, x, h, p, gath, target, gnext, carry):
    n = f"l{l}"
    g = lambda name: p[name][l].reshape(1, -1)
    res = {"x": x, "h": h}
    res["tab"] = _bias_table(p["rel_bias"][l], f"bias_table_{n}")
    proj = _mm_in(h, gath["w_in"], f"mm_in_{n}", carry.get("mm_in"))
    att, res["lse"] = _attn_fwd(proj, res["tab"], f"attn_fwd_{n}", carry.get("attn_fwd"))
    pooled, mixed = _pool_fwd(proj, p["w_pool_group"][l], g("pool_scale"), f"pool_fwd_{n}")
    bgate = g("b_gate")
    z, ya, yb = _branch_fwd(att, mixed, gath["w_attn_out"], gath["w_pool_out"], proj, bgate, f"branch_fwd_{n}")
    mix, x1, h2 = _mm_o_fwd(z, gath["w_o"], x, g("norm_mix_post"), g("norm_ffn_pre"), f"mm_o_fwd_{n}")
    cw = gath["conv_w"].reshape(2, NFF, 3, FS)
    cb = p["conv_b"][l].reshape(2, NFF, 1, FS)
    hu, a = _mm_up_conv_gate(h2, gath["w_up"], cw, cb, f"mm_up_conv_gate_{n}", carry.get("mm_up_conv_gate"))
    hu = hu.reshape(2, NFF, -1, FS)
    res.update(proj=proj, att=att, pooled=pooled, mixed=mixed, z=z, ya=ya, yb=yb, mix=mix, x1=x1, h2=h2, hu=hu,
               a=a, cw=cw, cb=cb, bgate=bgate)
    if target is None:
        f, x2, hn = _mm_down_fwd(a, gath["w_down"], x1, g("norm_ffn_post"), gnext, f"mm_down_fwd_{n}",
                                 carry.get("mm_down_fwd"))
        res["f"] = f
        return res, x2, hn
    dy, df, dg4, loss = _mm_down_loss(a, gath["w_down"], x1, g("norm_ffn_post"), target, f"mm_down_loss_{n}")
    return res, (dy, df, dg4), loss


def _layer_bwd(l, top, p, gath, res, carry, below):
    n = f"l{l}"
    g = lambda name: p[name][l].reshape(1, -1)
    big, small = {}, {}
    taken = lambda call: carry[call](big) if call in carry else None
    dx2, df, small["norm_ffn_post"] = top
    dhu, dcw, dcb = _conv_gate_bwd(res["hu"], df, gath["w_down"], res["cw"], res["cb"], f"conv_gate_bwd_{n}",
                                   taken("conv_gate_bwd"))
    big["conv_w"] = dcw.reshape(NDEV, 3, FS)
    small["conv_b"] = dcb
    big["w_down"] = _dw_down(res["a"], df, f"dw_down_{n}")
    dx1, dmix, small["norm_ffn_pre"], small["norm_mix_post"] = _mm_dh2(
        dhu, gath["w_up"], res["x1"], g("norm_ffn_pre"), dx2, res["mix"], g("norm_mix_post"), f"mm_dh2_{n}")
    big["w_up"] = _dw_up(res["h2"], dhu, f"dw_up_{n}")
    dya, dyb, dgates, small["b_gate"] = _mm_dz(dmix, gath["w_o"], res["proj"], res["bgate"], res["ya"], res["yb"],
                                                f"mm_dz_{n}")
    big["w_o"] = _dw_o(res["z"], dmix, f"dw_o_{n}")
    datt, dmixed = _branch_bwd(dya, dyb, gath["w_attn_out"], gath["w_pool_out"], f"branch_bwd_{n}")
    big["w_attn_out"], big["w_pool_out"] = _dw_branch(res["att"], res["mixed"], dya, dyb, f"dw_branch_{n}")
    dq, dk, dv, dtab = _attn_bwd(res["proj"], datt, res["att"], res["lse"], res["tab"], f"attn_bwd_{n}",
                                 taken("attn_bwd"))
    small["rel_bias"] = _rel_bias_grad(dtab, f"rel_bias_grad_{n}")[:, 0, :NREL]
    du, small["w_pool_group"], small["pool_scale"] = _pool_bwd(
        dmixed, res["pooled"], p["w_pool_group"][l], g("pool_scale"), f"pool_bwd_{n}")
    segs = (dq, dk, dv, du, dgates)
    big["w_in"] = _dw_in(res["h"], segs, f"dw_in_{n}")
    dx, small["norm_mix_pre"], *lower = _mm_dh(segs, gath["w_in"], res["x"], g("norm_mix_pre"), dx1, below,
                                               f"mm_dh_{n}", taken("mm_dh"))
    return (dx, *lower), big, small


def _gather_weights(gath, l, p, names):
    shards = [p[k][l] if k == "conv_w" else p[k][l].astype(BF16) for k in names]

    def deliver(results):
        for k, a in zip(names, results):
            if k in ("w_attn_out", "w_pool_out"):
                a = jnp.transpose(a, (1, 0, 2)).reshape(AW, D)
            elif k == "w_o":
                a = a.reshape(D, D)
            elif k == "w_down":
                a = a.reshape(NFF * FS, D)
            gath[k] = a

    return _gather_exchange(shards, deliver)


def _scatter_grads(recv, big, names):
    return _scatter_exchange([big[k] for k in names], lambda results: recv.update(zip(names, results)))


def kernel(x, norm_mix_pre, w_in, b_gate, rel_bias, w_attn_out, w_pool_group, pool_scale, w_pool_out, w_o, norm_mix_post, norm_ffn_pre, w_up, conv_w, conv_b, w_down, norm_ffn_post, loss_target, m_norm_mix_pre, m_w_in, m_b_gate, m_rel_bias, m_w_attn_out, m_w_pool_group, m_pool_scale, m_w_pool_out, m_w_o, m_norm_mix_post, m_norm_ffn_pre, m_w_up, m_conv_w, m_conv_b, m_w_down, m_norm_ffn_post, v_norm_mix_pre, v_w_in, v_b_gate, v_rel_bias, v_w_attn_out, v_w_pool_group, v_pool_scale, v_w_pool_out, v_w_o, v_norm_mix_post, v_norm_ffn_pre, v_w_up, v_conv_w, v_conv_b, v_w_down, v_norm_ffn_post):
    names = ("norm_mix_pre", "w_in", "b_gate", "rel_bias", "w_attn_out", "w_pool_group", "pool_scale", "w_pool_out",
             "w_o", "norm_mix_post", "norm_ffn_pre", "w_up", "conv_w", "conv_b", "w_down", "norm_ffn_post")
    p = dict(zip(names, (norm_mix_pre, w_in, b_gate, rel_bias, w_attn_out, w_pool_group, pool_scale, w_pool_out, w_o,
                         norm_mix_post, norm_ffn_pre, w_up, conv_w, conv_b, w_down, norm_ffn_post)))
    mom = dict(zip(names, (m_norm_mix_pre, m_w_in, m_b_gate, m_rel_bias, m_w_attn_out, m_w_pool_group, m_pool_scale,
                           m_w_pool_out, m_w_o, m_norm_mix_post, m_norm_ffn_pre, m_w_up, m_conv_w, m_conv_b, m_w_down,
                           m_norm_ffn_post)))
    var = dict(zip(names, (v_norm_mix_pre, v_w_in, v_b_gate, v_rel_bias, v_w_attn_out, v_w_pool_group, v_pool_scale,
                           v_w_pool_out, v_w_o, v_norm_mix_post, v_norm_ffn_pre, v_w_up, v_conv_w, v_conv_b, v_w_down,
                           v_norm_ffn_post)))
    s = x.shape[1]
    xs = x.reshape(s, D)
    target = loss_target.reshape(s, D)

    gath = [{}, {}]
    rest = ("w_attn_out", "w_pool_out", "w_o", "w_down", "conv_w")
    h0 = _rmsnorm_call(xs, p["norm_mix_pre"][0].reshape(1, D), "rmsnorm_l0",
                       _gather_weights(gath[0], 0, p, ("w_in",)))
    res0, x2, h1 = _layer_fwd(0, xs, h0, p, gath[0], None, p["norm_mix_pre"][1].reshape(1, D), {
        "mm_in": _gather_weights(gath[0], 0, p, rest),
        "attn_fwd": _gather_weights(gath[0], 0, p, ("w_up",)),
        "mm_up_conv_gate": _gather_weights(gath[1], 1, p, ("w_in",) + rest),
        "mm_down_fwd": _gather_weights(gath[1], 1, p, ("w_up",))})
    res1, top1, loss_part = _layer_fwd(1, x2, h1, p, gath[1], target, None, {})

    order = [k for k, _ in SHARDED]
    recv = [{}, {}]
    top0, big1, small1 = _layer_bwd(1, top1, p, gath[1], res1, {}, (res0["f"], p["norm_ffn_post"][0].reshape(1, D)))
    (grad_x,), big0, small0 = _layer_bwd(0, top0, p, gath[0], res0, {
        "conv_gate_bwd": lambda big: _scatter_grads(recv[1], big1, order),
        "attn_bwd": lambda big: _scatter_grads(recv[0], big, [k for k in order if k != "w_in"]),
        "mm_dh": lambda big: _scatter_grads(recv[0], big, ["w_in"])}, ())

    loss = lax.psum(loss_part[0, 0], ("x", "y", "c"))

    small = {k: jnp.stack([small0[k].reshape(shape), small1[k].reshape(shape)]) for k, shape in SMALL}
    parts = []
    carried = {"w_up": _gather_exchange([_pack_small(small)], parts.extend)}
    out = {}
    for k, tr in sorted(SHARDED, key=lambda kt: kt[0] not in carried):
        out[k] = _adamw_sharded((recv[0][k], recv[1][k]), p[k], mom[k], var[k], tr, f"adamw_{k}", carried.get(k))
    packed = _adamw_small(parts[0], _pack_small(p), _pack_small(mom), _pack_small(var), "adamw_small")
    unpacked = [_unpack_small(a) for a in packed]
    for k, _ in SMALL:
        out[k] = tuple(u[k] for u in unpacked)

    return (loss, grad_x.reshape(x.shape), *[out[k][0] for k in names], *[out[k][1] for k in names],
            *[out[k][2] for k in names], *[out[k][3] for k in names])
```

```python
import numpy as np
import jax
import jax.numpy as jnp
from jax import lax
from jax.experimental import pallas as pl
from jax.experimental.pallas import tpu as pltpu

F32, BF16 = jnp.float32, jnp.bfloat16

D = 1024
AW = 512
PW = 512
PG = 128
INW = 4096
FS = 704
NFF = 4
NDEV = 8
DEPTH = 2
HEADS = 8
NREL = 513
MAXREL = 256
POOL_WINDOWS = (2, 4, 8, 16)
EPS = 1e-6
SCALE = 0.125
NEG = -1e30
QB = 256
KW = 3 * QB
BAND = 576
ADAM_LR, ADAM_B1, ADAM_B2, ADAM_EPS, ADAM_WD, ADAM_STEP = 0.001, 0.9, 0.999, 1e-08, 0.01, 10
VMEM_LIMIT_V7X = 56 * 1024 * 1024
MESH = pl.DeviceIdType.MESH
GELU_C = 0.7978845608028654
GELU_A = 0.044715


HBM = pl.BlockSpec(memory_space=pltpu.HBM)


class _Exchange:
    def __init__(self, operands, out_shape, scratch, phases, deliver):
        self.operands, self.out_shape, self.scratch = list(operands), list(out_shape), list(scratch)
        self.phases, self.deliver = phases, deliver


def _call(body, name, grid, in_specs, out_specs, out_shape, scratch=(), exchange=None):
    params = pltpu.CompilerParams(vmem_limit_bytes=VMEM_LIMIT_V7X)
    if exchange is None:
        return pl.pallas_call(body, name=name, grid=grid, in_specs=in_specs, out_specs=out_specs, out_shape=out_shape,
                              scratch_shapes=list(scratch), compiler_params=params)
    single = not isinstance(out_shape, (list, tuple))
    outs, ospecs = ([out_shape], [out_specs]) if single else (list(out_shape), list(out_specs))
    n_in, n_out, n_scr = len(in_specs), len(outs), len(scratch)
    ne_in, ne_out = len(exchange.operands), len(exchange.out_shape)
    nsteps = int(np.prod(grid))

    def carried(*refs):
        cut = np.cumsum([0, n_in, ne_in, n_out, ne_out, n_scr])
        base_in, ex_in, base_out, ex_out, base_scr = (refs[cut[k]:cut[k + 1]] for k in range(5))
        step = pl.program_id(0)
        for axis in range(1, len(grid)):
            step = step * grid[axis] + pl.program_id(axis)
        phases = exchange.phases(ex_in, ex_out, refs[cut[5]:])
        pl.when(step == 0)(phases[0])
        body(*base_in, *base_out, *base_scr)
        at = {2: [nsteps - 1], 3: [(7 * nsteps) // 10, nsteps - 1]}[len(phases)]
        for phase, when in zip(phases[1:], at):
            pl.when(step == when)(phase)

    call = pl.pallas_call(
        carried, name=name, grid=grid, in_specs=list(in_specs) + [HBM] * ne_in, out_specs=ospecs + [HBM] * ne_out,
        out_shape=outs + exchange.out_shape, scratch_shapes=list(scratch) + exchange.scratch, compiler_params=params)

    def run(*args):
        res = call(*args, *exchange.operands)
        exchange.deliver(res[n_out:])
        return res[0] if single else res[:n_out]

    return run


def _dot(a, b):
    return jnp.dot(a, b, preferred_element_type=F32)


def _dot_nt(a, b):
    return lax.dot_general(a, b, (((1,), (1,)), ((), ())), preferred_element_type=F32)


def _dot_tn(a, b):
    return lax.dot_general(a, b, (((0,), (0,)), ((), ())), preferred_element_type=F32)


def _rms(x, g):
    r = lax.rsqrt(jnp.mean(x * x, axis=-1, keepdims=True) + EPS)
    return x * r * g


def _rms_bwd(x, g, dy):
    r = lax.rsqrt(jnp.mean(x * x, axis=-1, keepdims=True) + EPS)
    xh = x * r
    dg = jnp.sum(dy * xh, axis=0, keepdims=True)
    dxh = dy * g
    dx = r * (dxh - xh * jnp.mean(dxh * xh, axis=-1, keepdims=True))
    return dx, dg


def _sigmoid(x):
    return 1.0 / (1.0 + jnp.exp(-x))


def _gelu_parts(x):
    x2 = x * x
    th = jnp.tanh(x * (GELU_C + (GELU_C * GELU_A) * x2))
    s = 0.5 * th + 0.5
    dg = s * (1.0 + x * (1.0 - s) * (2.0 * GELU_C + (6.0 * GELU_C * GELU_A) * x2))
    return x * s, dg


def _row(i):
    return (i, 0)


def _fixed2(*_):
    return (0, 0)


def _rmsnorm_call(x, g, name, exchange=None):
    s = x.shape[0]
    tm = 512

    def body(x_ref, g_ref, o_ref):
        o_ref[...] = _rms(x_ref[...], g_ref[...]).astype(BF16)

    return _call(body, name, (s // tm,),
                 [pl.BlockSpec((tm, D), _row), pl.BlockSpec((1, D), _fixed2)],
                 pl.BlockSpec((tm, D), _row), jax.ShapeDtypeStruct((s, D), BF16), exchange=exchange)(x, g)


def _resident(shape):
    return pl.BlockSpec(shape, lambda *_: (0,) * len(shape), pipeline_mode=pl.Buffered(1))


def _mm_in(h, win, name, exchange=None):
    s = h.shape[0]
    tm = 512

    def body(h_ref, w_ref, o_ref):
        hv = h_ref[...]
        for j in range(NDEV):
            o_ref[:, j * 512:(j + 1) * 512] = _dot(hv, w_ref[j])

    return _call(body, name, (s // tm,),
                 [pl.BlockSpec((tm, D), _row), _resident((NDEV, D, 512))],
                 pl.BlockSpec((tm, INW), _row), jax.ShapeDtypeStruct((s, INW), F32), exchange=exchange)(h, win)


def _bias_table(rel_bias, name):
    wdt = 1024
    rel = jnp.pad(rel_bias, ((0, 0), (0, 640 - NREL))).reshape(HEADS, 1, 640)

    def body(r_ref, o_ref):
        rr = lax.broadcasted_iota(jnp.int32, (640, wdt), 0)
        m = lax.broadcasted_iota(jnp.int32, (640, wdt), 1)
        d = jnp.where(m < KW, m, m - wdt)
        onehot = (jnp.clip(512 - d, -MAXREL, MAXREL) + MAXREL == rr).astype(F32)
        row = jnp.dot(jnp.broadcast_to(r_ref[...], (8, 640)), onehot, preferred_element_type=F32,
                      precision=lax.Precision.HIGHEST)[0:1]
        t = pltpu.roll(jnp.broadcast_to(row, (QB, wdt)), 0, 1, stride=1, stride_axis=0)[:, 0:KW]
        qc = lax.broadcasted_iota(jnp.int32, (QB, KW), 0) // 64
        kc = lax.broadcasted_iota(jnp.int32, (QB, KW), 1) // 64
        o_ref[...] = jnp.where((kc >= qc) & (kc <= qc + 8), t, NEG)

    return _call(body, name, (HEADS,), [pl.BlockSpec((None, 1, 640), lambda h: (h, 0, 0))],
                 pl.BlockSpec((None, QB, KW), lambda h: (h, 0, 0)),
                 jax.ShapeDtypeStruct((HEADS, QB, KW), F32))(rel)


def _attn_scores(q_ref, k_refs, b_ref, i):
    lane = lax.broadcasted_iota(jnp.int32, (QB, 128), 1)
    q = q_ref[...] * SCALE
    qs = [jnp.where(lane < 64, q, 0.0).astype(BF16), jnp.where(lane >= 64, q, 0.0).astype(BF16)]
    k = jnp.concatenate([r[...] for r in k_refs], axis=0).astype(BF16)
    colb = lax.broadcasted_iota(jnp.int32, (1, KW), 1) // QB
    before = jnp.where(colb + i >= 2, 0.0, NEG)
    return qs, [_dot_nt(qs[hh], k) + b_ref[hh] + before for hh in (0, 1)], k, lane


def _attn_fwd(proj, tab, name, exchange=None):
    s = proj.shape[0]
    nq = s // QB

    def body(q_ref, k0, k1, k2, v0, v1, v2, b_ref, o_ref, lse_ref):
        _, scs, _, lane = _attn_scores(q_ref, (k0, k1, k2), b_ref, pl.program_id(1))
        v = jnp.concatenate([v0[...], v1[...], v2[...]], axis=0).astype(BF16)
        v1s = jnp.concatenate([v, jnp.ones((KW, 128), BF16)], axis=1)
        o, lse = [], []
        for sc in scs:
            m = jnp.max(sc, axis=1, keepdims=True)
            ov = _dot(jnp.exp(sc - m).astype(BF16), v1s)
            o.append(ov[:, 0:128] / ov[:, 128:256])
            lse.append(m + jnp.log(ov[:, 128:256]))
        o_ref[...] = jnp.where(lane < 64, o[0], o[1]).astype(BF16)
        lse_ref[...] = jnp.where(lane < 64, lse[0], lse[1])

    def kv(col, d):
        return pl.BlockSpec((QB, 128), lambda p, i: (jnp.maximum(i - 2 + d, 0), col + p))

    in_specs = [pl.BlockSpec((QB, 128), lambda p, i: (i, p))]
    in_specs += [kv(4, d) for d in range(3)] + [kv(8, d) for d in range(3)]
    in_specs += [pl.BlockSpec((2, QB, KW), lambda p, i: (p, 0, 0))]
    out = pl.BlockSpec((QB, 128), lambda p, i: (i, p))
    return _call(body, name, (4, nq), in_specs, [out, out],
                 [jax.ShapeDtypeStruct((s, AW), BF16), jax.ShapeDtypeStruct((s, AW), F32)], exchange=exchange)(
                     proj, proj, proj, proj, proj, proj, proj, tab)


def _pool_fwd(proj, wg, scale, name):
    s = proj.shape[0]
    tb = 512
    e = tb + 16

    def body(u_ref, halo_ref, wg_ref, sc_ref, pooled_ref, mixed_ref):
        i = pl.program_id(0)
        cur = u_ref[...]
        prev = jnp.where(i > 0, halo_ref[...], 0.0)
        xs = jnp.concatenate([prev, cur], axis=0)
        t = i * tb + lax.broadcasted_iota(jnp.int32, (tb, 1), 0)
        for g, w in enumerate(POOL_WINDOWS):
            sl = slice(g * PG, (g + 1) * PG)
            a = xs[:, sl]
            sh = 1
            while sh < w:
                a = a + pltpu.roll(a, sh, 0)
                sh *= 2
            cnt = jnp.minimum(t + 1, w).astype(F32)
            pooled = (a[16:] / cnt - cur[:, sl]).astype(BF16)
            pooled_ref[:, sl] = pooled
            mixed_ref[:, sl] = (_dot(pooled, wg_ref[g].astype(BF16)) * sc_ref[:, sl]).astype(BF16)

    assert e % 8 == 0
    return _call(body, name, (s // tb,),
                 [pl.BlockSpec((tb, PW), lambda i: (i, 3)),
                  pl.BlockSpec((16, PW), lambda i: (jnp.maximum(i * (tb // 16) - 1, 0), 3)),
                  pl.BlockSpec((4, PG, PG), lambda i: (0, 0, 0)), pl.BlockSpec((1, PW), _fixed2)],
                 [pl.BlockSpec((tb, PW), _row), pl.BlockSpec((tb, PW), _row)],
                 [jax.ShapeDtypeStruct((s, PW), BF16), jax.ShapeDtypeStruct((s, PW), BF16)])(proj, proj, wg, scale)


def _mix_fwd(att, mixed, wao, wpo, proj, bgate, wo, x, g2, g3, name):
    s = att.shape[0]
    tm = 512

    def body(att_ref, mx_ref, wao_ref, wpo_ref, ga_ref, gb_ref, ba_ref, bb_ref, wo_ref, x_ref, g2_ref, g3_ref,
             z_ref, ya_ref, yb_ref, mix_ref, x1_ref, h2_ref):
        ya = _dot(att_ref[...], wao_ref[...])
        yb = _dot(mx_ref[...], wpo_ref[...])
        ga = _sigmoid(ga_ref[...] + ba_ref[...])
        gb = _sigmoid(gb_ref[...] + bb_ref[...])
        z = (ga * ya + gb * yb).astype(BF16)
        mix = _dot(z, wo_ref[...])
        x1 = x_ref[...] + _rms(mix, g2_ref[...])
        ya_ref[...] = ya.astype(BF16)
        yb_ref[...] = yb.astype(BF16)
        z_ref[...] = z
        mix_ref[...] = mix
        x1_ref[...] = x1
        h2_ref[...] = _rms(x1, g3_ref[...]).astype(BF16)

    rows, gain = pl.BlockSpec((tm, D), _row), pl.BlockSpec((1, D), _fixed2)
    half, full = jax.ShapeDtypeStruct((s, D), BF16), jax.ShapeDtypeStruct((s, D), F32)
    return _call(body, name, (s // tm,),
                 [pl.BlockSpec((tm, AW), _row), pl.BlockSpec((tm, PW), _row),
                  _resident((AW, D)), _resident((PW, D)),
                  pl.BlockSpec((tm, D), lambda i: (i, 2)), pl.BlockSpec((tm, D), lambda i: (i, 3)),
                  pl.BlockSpec((1, D), lambda i: (0, 0)), pl.BlockSpec((1, D), lambda i: (0, 1)),
                  _resident((D, D)), rows, gain, gain],
                 [rows] * 6, [half, half, half, full, full, half])(
                     att, mixed, wao, wpo, proj, proj, bgate, bgate, wo, x, g2, g3)


def _mm_up_conv_gate(h2, wup, cw, cb, name, exchange=None):
    s = h2.shape[0]
    tm = 256

    def body(h_ref, w_ref, cw_ref, cb_ref, hu_ref, a_ref, halo):
        first = pl.program_id(0) == 0
        hv = h_ref[...]
        for j in range(NDEV):
            hu_ref[j] = _dot(hv, w_ref[j])
        for jj in range(NFF):
            for c0, w in LANE_COLUMNS:
                lanes = pl.ds(c0, w)
                cwb = [[jnp.broadcast_to(cw_ref[sd, jj, t:t + 1, lanes], (8, w)) for t in range(3)] for sd in (0, 1)]
                cbb = [jnp.broadcast_to(cb_ref[sd, jj, :, lanes], (8, w)) for sd in (0, 1)]
                row = lax.broadcasted_iota(jnp.int32, (8, w), 0)
                rolled = []
                for sd in (0, 1):
                    before = jnp.where(first, 0.0, halo[NFF * sd + jj, :, lanes])
                    rolled += [pltpu.roll(before, 1, 0), pltpu.roll(before, 2, 0)]
                for k in range(tm // 16):
                    outs = []
                    for r in (16 * k, 16 * k + 8):
                        hc, keep = [], []
                        for sd in (0, 1):
                            cur = hu_ref[NFF * sd + jj, pl.ds(r, 8), lanes]
                            r1, r2 = pltpu.roll(cur, 1, 0), pltpu.roll(cur, 2, 0)
                            hc.append(cbb[sd] + cwb[sd][2] * cur
                                      + cwb[sd][0] * jnp.where(row >= 2, r2, rolled[2 * sd + 1])
                                      + cwb[sd][1] * jnp.where(row >= 1, r1, rolled[2 * sd]))
                            keep += [r1, r2]
                        rolled = keep
                        outs.append(_gelu_parts(hc[1])[0] * hc[0])
                    a_ref[jj, pl.ds(16 * k, 16), lanes] = jnp.concatenate(outs, axis=0).astype(BF16)
        for j in range(NDEV):
            halo[j] = hu_ref[j, tm - 8:tm, :]

    small = lambda i: (0, 0, 0, 0)
    return _call(body, name, (s // tm,),
                 [pl.BlockSpec((tm, D), _row), _resident((NDEV, D, FS)), pl.BlockSpec((2, NFF, 3, FS), small),
                  pl.BlockSpec((2, NFF, 1, FS), small)],
                 [pl.BlockSpec((NDEV, tm, FS), lambda i: (0, i, 0)), pl.BlockSpec((NFF, tm, FS), lambda i: (0, i, 0))],
                 [jax.ShapeDtypeStruct((NDEV, s, FS), F32), jax.ShapeDtypeStruct((NFF, s, FS), BF16)],
                 [pltpu.VMEM((NDEV, 8, FS), F32)], exchange=exchange)(h2, wup, cw, cb)


LANE_COLUMNS = [(c0, min(128, FS - c0)) for c0 in range(0, FS, 128)]


def _mm_down_fwd(a, wd, x1, g4, gnext, name, exchange=None):
    s = a.shape[1]
    tm = 512

    def body(a_ref, w_ref, x1_ref, g4_ref, gn_ref, f_ref, x2_ref, hn_ref):
        f = _dot(a_ref[0], w_ref[0:FS, :])
        for j in range(1, NFF):
            f = f + _dot(a_ref[j], w_ref[j * FS:(j + 1) * FS, :])
        x2 = x1_ref[...] + _rms(f, g4_ref[...])
        f_ref[...] = f
        x2_ref[...] = x2
        hn_ref[...] = _rms(x2, gn_ref[...]).astype(BF16)

    return _call(body, name, (s // tm,),
                 [pl.BlockSpec((NFF, tm, FS), lambda i: (0, i, 0)), pl.BlockSpec((NFF * FS, D), _fixed2),
                  pl.BlockSpec((tm, D), _row), pl.BlockSpec((1, D), _fixed2), pl.BlockSpec((1, D), _fixed2)],
                 [pl.BlockSpec((tm, D), _row)] * 3,
                 [jax.ShapeDtypeStruct((s, D), F32), jax.ShapeDtypeStruct((s, D), F32),
                  jax.ShapeDtypeStruct((s, D), BF16)], exchange=exchange)(a, wd, x1, g4, gnext)


def _mm_down_loss(a, wd, x1, g4, target, name):
    s = a.shape[1]
    tm = 512

    def body(a_ref, w_ref, x1_ref, g4_ref, t_ref, dy_ref, df_ref, dg_ref, loss_ref):
        i = pl.program_id(0)
        f = _dot(a_ref[0], w_ref[0:FS, :])
        for j in range(1, NFF):
            f = f + _dot(a_ref[j], w_ref[j * FS:(j + 1) * FS, :])
        err = x1_ref[...] + _rms(f, g4_ref[...]) - t_ref[...]
        dy = err * (1.0 / D)
        df, dg = _rms_bwd(f, g4_ref[...], dy)
        dy_ref[...] = dy
        df_ref[...] = df.astype(BF16)
        part = 0.5 * jnp.sum(jnp.mean(err * err, axis=-1, keepdims=True), axis=0, keepdims=True)

        @pl.when(i == 0)
        def _():
            loss_ref[...] = jnp.zeros_like(loss_ref)
            dg_ref[...] = jnp.zeros_like(dg_ref)

        loss_ref[...] += jnp.broadcast_to(part, loss_ref.shape)
        dg_ref[...] += dg

    return _call(body, name, (s // tm,),
                 [pl.BlockSpec((NFF, tm, FS), lambda i: (0, i, 0)), pl.BlockSpec((NFF * FS, D), _fixed2),
                  pl.BlockSpec((tm, D), _row), pl.BlockSpec((1, D), _fixed2), pl.BlockSpec((tm, D), _row)],
                 [pl.BlockSpec((tm, D), _row), pl.BlockSpec((tm, D), _row), pl.BlockSpec((1, D), _fixed2),
                  pl.BlockSpec((8, 128), _fixed2)],
                 [jax.ShapeDtypeStruct((s, D), F32), jax.ShapeDtypeStruct((s, D), BF16),
                  jax.ShapeDtypeStruct((1, D), F32), jax.ShapeDtypeStruct((8, 128), F32)])(a, wd, x1, g4, target)


def _conv_gate_bwd(hu, df, wd, cw, cb, a, name, exchange=None):
    s = hu.shape[2]
    tb = 512
    nt = s // tb
    e = tb + 16

    def body(hu_ref, prev_ref, next_ref, df_ref, dfn_ref, wd_ref, cw_ref, cb_ref, a_ref, dhu_ref, dcw_ref, dcb_ref,
             dwd_ref, dabuf, dbuf, dwd_acc):
        i = pl.program_id(1)
        first, last = i == 0, i == nt - 1

        @pl.when(i == 0)
        def _():
            dcw_ref[...] = jnp.zeros_like(dcw_ref)
            dcb_ref[...] = jnp.zeros_like(dcb_ref)
            dwd_acc[...] = jnp.zeros_like(dwd_acc)

        dabuf[0:tb] = _dot_nt(df_ref[...], wd_ref[...])
        dabuf[tb:tb + 8] = jnp.where(last, 0.0, _dot_nt(dfn_ref[...], wd_ref[...])[0:8])
        dwd_acc[...] += _dot_tn(a_ref[...], df_ref[...])

        for c0, w in LANE_COLUMNS:
            lanes = pl.ds(c0, w)
            cwb = [[jnp.broadcast_to(cw_ref[sd, t:t + 1, lanes], (8, w)) for t in range(3)] for sd in (0, 1)]
            cbb = [jnp.broadcast_to(cb_ref[sd, :, lanes], (8, w)) for sd in (0, 1)]

            row = lax.broadcasted_iota(jnp.int32, (8, w), 0)

            def tile(k, carry, summed=True):
                sums, rolled = carry[:8], carry[8:]
                r = 8 + 8 * k
                xs, keep = [], []
                for sd in (0, 1):
                    if summed:
                        cur = hu_ref[sd, pl.ds(r - 8, 8), lanes]
                    else:
                        cur = jnp.where(last, 0.0, next_ref[sd, :, lanes])
                    r1, r2 = pltpu.roll(cur, 1, 0), pltpu.roll(cur, 2, 0)
                    xs.append([jnp.where(row >= 2, r2, rolled[2 * sd + 1]), jnp.where(row >= 1, r1, rolled[2 * sd]),
                               cur])
                    keep += [r1, r2]
                hc = [cbb[sd] + cwb[sd][2] * xs[sd][2] + cwb[sd][0] * xs[sd][0] + cwb[sd][1] * xs[sd][1]
                      for sd in (0, 1)]
                da = dabuf[pl.ds(r - 8, 8), lanes]
                gl, dgl = _gelu_parts(hc[1])
                dhc = (da * gl, da * hc[0] * dgl)
                for sd in (0, 1):
                    dbuf[sd, pl.ds(r, 8), lanes] = dhc[sd]
                if not summed:
                    return carry
                new = []
                for sd in (0, 1):
                    new += [sums[4 * sd + t] + dhc[sd] * xs[sd][t] for t in range(3)] + [sums[4 * sd + 3] + dhc[sd]]
                return tuple(new + keep)

            start = [jnp.zeros((8, w), F32) for _ in range(8)]
            for sd in (0, 1):
                halo = jnp.where(first, 0.0, prev_ref[sd, :, lanes])
                start += [pltpu.roll(halo, 1, 0), pltpu.roll(halo, 2, 0)]
            def tiles(k4, carry):
                for u in range(4):
                    carry = tile(4 * k4 + u, carry)
                return carry

            carry = tuple(start)
            for k4 in range(tb // 32):
                carry = tiles(k4, carry)
            tile(tb // 8, carry, summed=False)
            sums = carry[:8]

            def up(v):
                return pltpu.roll(v, 7, 0), pltpu.roll(v, 6, 0)

            def out_tile(k, carry):
                r = 8 + 16 * k
                new = []
                for sd in (0, 1):
                    va, va1, va2 = carry[3 * sd:3 * sd + 3]
                    vb, vc = dbuf[sd, pl.ds(r + 8, 8), lanes], dbuf[sd, pl.ds(r + 16, 8), lanes]
                    (vb1, vb2), (vc1, vc2) = up(vb), up(vc)
                    c0b, c1b, c2b = cwb[sd]
                    top = c2b * va + c1b * jnp.where(row <= 6, va1, vb1) + c0b * jnp.where(row <= 5, va2, vb2)
                    bot = c2b * vb + c1b * jnp.where(row <= 6, vb1, vc1) + c0b * jnp.where(row <= 5, vb2, vc2)
                    dhu_ref[sd, pl.ds(16 * k, 16), lanes] = jnp.concatenate(
                        [top, bot], axis=0).astype(BF16)
                    new += [vc, vc1, vc2]
                return tuple(new)

            begin = []
            for sd in (0, 1):
                va = dbuf[sd, 8:16, lanes]
                begin += [va, *up(va)]
            def out_tiles(k2, carry):
                return out_tile(2 * k2 + 1, out_tile(2 * k2, carry))

            carry = tuple(begin)
            for k2 in range(tb // 32):
                carry = out_tiles(k2, carry)
            for sd in (0, 1):
                for t in range(3):
                    dcw_ref[sd, t:t + 1, lanes] += jnp.sum(sums[4 * sd + t], axis=0, keepdims=True)
                dcb_ref[sd, :, lanes] += jnp.sum(sums[4 * sd + 3], axis=0, keepdims=True)

        @pl.when(last)
        def _():
            dwd_ref[0] = dwd_acc[0:FS // 2, :].astype(BF16)
            dwd_ref[1] = dwd_acc[FS // 2:FS, :].astype(BF16)

    nb8, nb16 = s // 8, s // 16
    return _call(body, name, (NFF, nt),
                 [pl.BlockSpec((2, None, tb, FS), lambda j, i: (0, j, i, 0)),
                  pl.BlockSpec((2, None, 8, FS), lambda j, i: (0, j, jnp.maximum(i * (tb // 8) - 1, 0), 0)),
                  pl.BlockSpec((2, None, 8, FS), lambda j, i: (0, j, jnp.minimum((i + 1) * (tb // 8), nb8 - 1), 0)),
                  pl.BlockSpec((tb, D), lambda j, i: (i, 0)),
                  pl.BlockSpec((16, D), lambda j, i: (jnp.minimum((i + 1) * (tb // 16), nb16 - 1), 0)),
                  pl.BlockSpec((FS, D), lambda j, i: (j, 0)),
                  pl.BlockSpec((2, None, 3, FS), lambda j, i: (0, j, 0, 0)),
                  pl.BlockSpec((2, None, 1, FS), lambda j, i: (0, j, 0, 0)),
                  pl.BlockSpec((None, tb, FS), lambda j, i: (j, i, 0))],
                 [pl.BlockSpec((2, None, tb, FS), lambda j, i: (0, j, i, 0)),
                  pl.BlockSpec((2, None, 3, FS), lambda j, i: (0, j, 0, 0)),
                  pl.BlockSpec((2, None, 1, FS), lambda j, i: (0, j, 0, 0)),
                  pl.BlockSpec((2, FS // 2, D), lambda j, i: (j, 0, 0))],
                 [jax.ShapeDtypeStruct((2, NFF, s, FS), BF16), jax.ShapeDtypeStruct((2, NFF, 3, FS), F32),
                  jax.ShapeDtypeStruct((2, NFF, 1, FS), F32), jax.ShapeDtypeStruct((NDEV, FS // 2, D), BF16)],
                 [pltpu.VMEM((tb + 8, FS), F32), pltpu.VMEM((2, e, FS), F32), pltpu.VMEM((FS, D), F32)],
                 exchange=exchange)(hu, hu, hu, df, df, wd, cw, cb, a)


def _mm_dh2(dhu, wup, x1, g3, dx2, mix, g2, name):
    s = x1.shape[0]
    tm = 512

    def body(dhu_ref, w_ref, x1_ref, g3_ref, dx2_ref, mix_ref, g2_ref, dx1_ref, dmix_ref, dg3_ref, dg2_ref):
        i = pl.program_id(0)

        @pl.when(i == 0)
        def _():
            dg3_ref[...] = jnp.zeros_like(dg3_ref)
            dg2_ref[...] = jnp.zeros_like(dg2_ref)

        dh2 = _dot_nt(dhu_ref[0, 0], w_ref[0])
        for j in range(1, NDEV):
            dh2 = dh2 + _dot_nt(dhu_ref[j // NFF, j % NFF], w_ref[j])
        dn, dg3 = _rms_bwd(x1_ref[...], g3_ref[...], dh2)
        dx1 = dx2_ref[...] + dn
        dmix, dg2 = _rms_bwd(mix_ref[...], g2_ref[...], dx1)
        dx1_ref[...] = dx1
        dmix_ref[...] = dmix.astype(BF16)
        dg3_ref[...] += dg3
        dg2_ref[...] += dg2

    return _call(body, name, (s // tm,),
                 [pl.BlockSpec((2, NFF, tm, FS), lambda i: (0, 0, i, 0)), _resident((NDEV, D, FS)),
                  pl.BlockSpec((tm, D), _row), pl.BlockSpec((1, D), _fixed2), pl.BlockSpec((tm, D), _row),
                  pl.BlockSpec((tm, D), _row), pl.BlockSpec((1, D), _fixed2)],
                 [pl.BlockSpec((tm, D), _row), pl.BlockSpec((tm, D), _row), pl.BlockSpec((1, D), _fixed2),
                  pl.BlockSpec((1, D), _fixed2)],
                 [jax.ShapeDtypeStruct((s, D), F32), jax.ShapeDtypeStruct((s, D), BF16),
                  jax.ShapeDtypeStruct((1, D), F32), jax.ShapeDtypeStruct((1, D), F32)])(
                     dhu, wup, x1, g3, dx2, mix, g2)


def _dw_up(h2, dhu, name):
    s = h2.shape[0]
    tk = 512
    nk = s // tk

    def body(h_ref, d_ref, o_ref, acc):
        k = pl.program_id(1)

        @pl.when(k == 0)
        def _():
            acc[...] = jnp.zeros_like(acc)

        ht = h_ref[...].T
        for j in range(NFF):
            acc[j] += _dot(ht, d_ref[j])

        @pl.when(k == nk - 1)
        def _():
            o_ref[...] = acc[...].astype(BF16)

    return _call(body, name, (2, nk),
                 [pl.BlockSpec((tk, D), lambda hf, k: (k, 0)),
                  pl.BlockSpec((None, NFF, tk, FS), lambda hf, k: (hf, 0, k, 0))],
                 pl.BlockSpec((NFF, D, FS), lambda hf, k: (hf, 0, 0)),
                 jax.ShapeDtypeStruct((NDEV, D, FS), BF16), [pltpu.VMEM((NFF, D, FS), F32)])(h2, dhu)


def _mm_dz(dmix, wo, proj, bgate, ya, yb, wao, wpo, name):
    s = dmix.shape[0]
    tm = 512

    def body(dm_ref, w_ref, ga_ref, gb_ref, ba_ref, bb_ref, ya_ref, yb_ref, wao_ref, wpo_ref,
             dya_ref, dyb_ref, dg_ref, dbg_ref, datt_ref, dmx_ref):
        i = pl.program_id(0)

        @pl.when(i == 0)
        def _():
            dbg_ref[...] = jnp.zeros_like(dbg_ref)

        dz = _dot_nt(dm_ref[...], w_ref[...])
        ga = _sigmoid(ga_ref[...] + ba_ref[...])
        gb = _sigmoid(gb_ref[...] + bb_ref[...])
        dya = (dz * ga).astype(BF16)
        dyb = (dz * gb).astype(BF16)
        dya_ref[...] = dya
        dyb_ref[...] = dyb
        datt_ref[...] = _dot_nt(dya, wao_ref[...]).astype(BF16)
        dmx_ref[...] = _dot_nt(dyb, wpo_ref[...])
        dga = dz * ya_ref[...].astype(F32) * ga * (1.0 - ga)
        dgb = dz * yb_ref[...].astype(F32) * gb * (1.0 - gb)
        dg_ref[:, 0:D] = dga.astype(BF16)
        dg_ref[:, D:2 * D] = dgb.astype(BF16)
        dbg_ref[:, 0:D] += jnp.sum(dga, axis=0, keepdims=True)
        dbg_ref[:, D:2 * D] += jnp.sum(dgb, axis=0, keepdims=True)

    rows = pl.BlockSpec((tm, D), _row)
    return _call(body, name, (s // tm,),
                 [rows, _resident((D, D)),
                  pl.BlockSpec((tm, D), lambda i: (i, 2)), pl.BlockSpec((tm, D), lambda i: (i, 3)),
                  pl.BlockSpec((1, D), lambda i: (0, 0)), pl.BlockSpec((1, D), lambda i: (0, 1)),
                  rows, rows, _resident((AW, D)), _resident((PW, D))],
                 [rows, rows, pl.BlockSpec((tm, 2 * D), _row), pl.BlockSpec((1, 2 * D), _fixed2),
                  pl.BlockSpec((tm, AW), _row), pl.BlockSpec((tm, PW), _row)],
                 [jax.ShapeDtypeStruct((s, D), BF16), jax.ShapeDtypeStruct((s, D), BF16),
                  jax.ShapeDtypeStruct((s, 2 * D), BF16), jax.ShapeDtypeStruct((1, 2 * D), F32),
                  jax.ShapeDtypeStruct((s, AW), BF16), jax.ShapeDtypeStruct((s, PW), F32)])(
                     dmix, wo, proj, proj, bgate, bgate, ya, yb, wao, wpo)


def _dw_o(z, dmix, name):
    s = z.shape[0]
    tk = 512
    nk = s // tk

    def body(z_ref, d_ref, o_ref, acc):
        k = pl.program_id(0)

        @pl.when(k == 0)
        def _():
            acc[...] = jnp.zeros_like(acc)

        acc[...] += _dot_tn(z_ref[...], d_ref[...])

        @pl.when(k == nk - 1)
        def _():
            for j in range(NDEV):
                o_ref[j] = acc[j * 128:(j + 1) * 128, :].astype(BF16)

    return _call(body, name, (nk,),
                 [pl.BlockSpec((tk, D), _row), pl.BlockSpec((tk, D), _row)],
                 pl.BlockSpec((NDEV, 128, D), lambda k: (0, 0, 0)),
                 jax.ShapeDtypeStruct((NDEV, 128, D), BF16), [pltpu.VMEM((D, D), F32)])(z, dmix)


def _dw_branch(att, mixed, dya, dyb, name):
    s = att.shape[0]
    tk = 512
    nk = s // tk

    def body(att_ref, mx_ref, dya_ref, dyb_ref, oa_ref, ob_ref, acca, accb):
        k = pl.program_id(0)

        @pl.when(k == 0)
        def _():
            acca[...] = jnp.zeros_like(acca)
            accb[...] = jnp.zeros_like(accb)

        acca[...] += _dot_tn(att_ref[...], dya_ref[...])
        accb[...] += _dot_tn(mx_ref[...], dyb_ref[...])

        @pl.when(k == nk - 1)
        def _():
            for j in range(NDEV):
                oa_ref[j] = acca[:, j * 128:(j + 1) * 128].astype(BF16)
                ob_ref[j] = accb[:, j * 128:(j + 1) * 128].astype(BF16)

    out = jax.ShapeDtypeStruct((NDEV, AW, 128), BF16)
    return _call(body, name, (nk,),
                 [pl.BlockSpec((tk, AW), _row), pl.BlockSpec((tk, PW), _row),
                  pl.BlockSpec((tk, D), _row), pl.BlockSpec((tk, D), _row)],
                 [pl.BlockSpec((NDEV, AW, 128), lambda k: (0, 0, 0))] * 2, [out, out],
                 [pltpu.VMEM((AW, D), F32), pltpu.VMEM((PW, D), F32)])(att, mixed, dya, dyb)


def _attn_bwd(proj, datt, att, lse, tab, name, exchange=None):
    s = proj.shape[0]
    nq = s // QB

    def body(q_ref, k0, k1, k2, v0, v1, v2, do_ref, o_ref, lse_ref, b_ref, dq_ref, dk_ref, dv_ref, db_ref,
             dka, dkb, dva, dvb):
        i = pl.program_id(1)

        @pl.when(i == 0)
        def _():
            for r in (dka, dkb, dva, dvb):
                r[...] = jnp.zeros_like(r)
            db_ref[...] = jnp.zeros_like(db_ref)

        @pl.when(i < nq)
        def _():
            qs, scs, k, lane = _attn_scores(q_ref, (k0, k1, k2), b_ref, i)
            v = jnp.concatenate([v0[...], v1[...], v2[...]], axis=0).astype(BF16)
            do = do_ref[...]
            doo = do.astype(F32) * o_ref[...].astype(F32)
            lse = lse_ref[...]
            dq = jnp.zeros((QB, 128), F32)
            dkt = jnp.zeros((128, KW), F32)
            dvt = jnp.zeros((128, KW), F32)
            for hh in (0, 1):
                mine = (lane < 64) if hh == 0 else (lane >= 64)
                doh = jnp.where(mine, do, jnp.zeros_like(do))
                kmask = lax.broadcasted_iota(jnp.int32, (KW, 128), 1)
                kh = jnp.where((kmask < 64) if hh == 0 else (kmask >= 64), k, jnp.zeros_like(k))
                p = jnp.exp(scs[hh] - lse[:, 64 * hh:64 * hh + 1])
                drow = jnp.sum(jnp.where(mine, doo, 0.0), axis=1, keepdims=True)
                ds = p * (_dot_nt(doh, v) - drow)
                db_ref[hh] += ds
                dsb = ds.astype(BF16)
                dq = dq + _dot(dsb, kh)
                dkt = dkt + _dot_tn(qs[hh], dsb)
                dvt = dvt + _dot_tn(doh, p.astype(BF16))
            dkw, dvw = dkt.T, dvt.T
            dq_ref[...] = (dq * SCALE).astype(BF16)
            dk_ref[...] = (dka[...] + dkw[0:QB]).astype(BF16)
            dka[...] = dkb[...] + dkw[QB:2 * QB]
            dkb[...] = dkw[2 * QB:3 * QB]
            dv_ref[...] = (dva[...] + dvw[0:QB]).astype(BF16)
            dva[...] = dvb[...] + dvw[QB:2 * QB]
            dvb[...] = dvw[2 * QB:3 * QB]

        @pl.when(i >= nq)
        def _():
            dk_ref[...] = dka[...].astype(BF16)
            dka[...] = dkb[...]
            dkb[...] = jnp.zeros_like(dkb)
            dv_ref[...] = dva[...].astype(BF16)
            dva[...] = dvb[...]
            dvb[...] = jnp.zeros_like(dvb)

    def kv(col, d):
        return pl.BlockSpec((QB, 128), lambda p, i: (jnp.clip(i - 2 + d, 0, nq - 1), col + p))

    cur = lambda p, i: (jnp.minimum(i, nq - 1), p)
    done = lambda p, i: (jnp.maximum(i - 2, 0), p)
    in_specs = [pl.BlockSpec((QB, 128), cur)]
    in_specs += [kv(4, d) for d in range(3)] + [kv(8, d) for d in range(3)]
    in_specs += [pl.BlockSpec((QB, 128), cur)] * 3 + [pl.BlockSpec((2, QB, KW), lambda p, i: (p, 0, 0))]
    o = jax.ShapeDtypeStruct((s, AW), BF16)
    return _call(body, name, (4, nq + 2), in_specs,
                 [pl.BlockSpec((QB, 128), cur), pl.BlockSpec((QB, 128), done), pl.BlockSpec((QB, 128), done),
                  pl.BlockSpec((2, QB, KW), lambda p, i: (p, 0, 0))],
                 [o, o, o, jax.ShapeDtypeStruct((HEADS, QB, KW), F32)],
                 [pltpu.VMEM((QB, 128), F32)] * 4, exchange=exchange)(
                     proj, proj, proj, proj, proj, proj, proj, datt, att, lse, tab)


def _rel_bias_grad(dtab, name):
    wdt = 640

    def body(x_ref, o_ref):
        x = x_ref[...]
        xc = x[0:64, 0:wdt]
        for qc in range(1, QB // 64):
            xc = xc + pltpu.roll(x[qc * 64:(qc + 1) * 64, :], KW - qc * 64, 1)[:, 0:wdt]
        r = lax.broadcasted_iota(jnp.int32, (64, 64), 0)
        c = lax.broadcasted_iota(jnp.int32, (64, 64), 1)
        flip = (r + c == 63).astype(F32)
        y = jnp.dot(flip, xc, preferred_element_type=F32, precision=lax.Precision.HIGHEST)
        z = pltpu.roll(y, 0, 1, stride=1, stride_axis=0)
        t = jnp.broadcast_to(jnp.sum(z, axis=0, keepdims=True), (8, wdt))
        e = lax.broadcasted_iota(jnp.int32, (wdt, wdt), 0)
        rr = lax.broadcasted_iota(jnp.int32, (wdt, wdt), 1)
        onehot = (jnp.clip(BAND - 1 - e, -MAXREL, MAXREL) + MAXREL == rr).astype(F32)
        o_ref[...] = jnp.dot(t, onehot, preferred_element_type=F32, precision=lax.Precision.HIGHEST)

    return _call(body, name, (HEADS,), [pl.BlockSpec((None, QB, KW), lambda h: (h, 0, 0))],
                 pl.BlockSpec((None, 8, wdt), lambda h: (h, 0, 0)),
                 jax.ShapeDtypeStruct((HEADS, 8, wdt), F32))(dtab)


def _pool_bwd(dmixed, pooled, wg, scale, name):
    s = dmixed.shape[0]
    tb = 512
    nt = s // tb
    e = tb + 16

    def body(dm_ref, dmn_ref, pl_ref, wg_ref, sc_ref, du_ref, dwg_ref, dsc_ref):
        i = pl.program_id(0)
        dm = jnp.concatenate([dm_ref[...], jnp.where(i == nt - 1, 0.0, dmn_ref[...])], axis=0)
        t = i * tb + lax.broadcasted_iota(jnp.int32, (e, 1), 0)

        @pl.when(i == 0)
        def _():
            dwg_ref[...] = jnp.zeros_like(dwg_ref)
            dsc_ref[...] = jnp.zeros_like(dsc_ref)

        for g, w in enumerate(POOL_WINDOWS):
            sl = slice(g * PG, (g + 1) * PG)
            wgb = wg_ref[g].astype(BF16)
            pb = pl_ref[:, sl]
            dsc_ref[:, sl] += jnp.sum(dm[0:tb, sl] * _dot(pb, wgb), axis=0, keepdims=True)
            dpre = (dm[:, sl] * sc_ref[:, sl]).astype(BF16)
            dwg_ref[g] += _dot_tn(pb, dpre[0:tb])
            dpool = _dot_nt(dpre, wgb)
            a = dpool / jnp.minimum(t + 1, w).astype(F32)
            sh = 1
            while sh < w:
                a = a + pltpu.roll(a, e - sh, 0)
                sh *= 2
            du_ref[:, sl] = (a[0:tb] - dpool[0:tb]).astype(BF16)

    nb16 = s // 16
    return _call(body, name, (nt,),
                 [pl.BlockSpec((tb, PW), _row),
                  pl.BlockSpec((16, PW), lambda i: (jnp.minimum((i + 1) * (tb // 16), nb16 - 1), 0)),
                  pl.BlockSpec((tb, PW), _row), pl.BlockSpec((4, PG, PG), lambda i: (0, 0, 0)),
                  pl.BlockSpec((1, PW), _fixed2)],
                 [pl.BlockSpec((tb, PW), _row), pl.BlockSpec((4, PG, PG), lambda i: (0, 0, 0)),
                  pl.BlockSpec((1, PW), _fixed2)],
                 [jax.ShapeDtypeStruct((s, PW), BF16), jax.ShapeDtypeStruct((4, PG, PG), F32),
                  jax.ShapeDtypeStruct((1, PW), F32)])(dmixed, dmixed, pooled, wg, scale)


def _dproj_specs(t):
    return [pl.BlockSpec((t, 512), _row)] * 4 + [pl.BlockSpec((t, 2 * D), _row)]


def _dproj_segment(refs, n):
    return refs[n][...] if n < 4 else refs[4][:, (n - 4) * 512:(n - 3) * 512]


def _mm_dh(segs, win, x, g1, dx1, below, name, exchange=None):
    s = x.shape[0]
    tm = 512

    def body(dq_ref, dk_ref, dv_ref, du_ref, dg_ref, w_ref, x_ref, g1_ref, dx1_ref, *rest):
        i = pl.program_id(0)
        outs = rest[2:] if below else rest

        @pl.when(i == 0)
        def _():
            for ref in outs[1::2]:
                ref[...] = jnp.zeros_like(ref)

        pieces = (dq_ref, dk_ref, dv_ref, du_ref, dg_ref)
        dh = _dot_nt(_dproj_segment(pieces, 0), w_ref[0])
        for n in range(1, NDEV):
            dh = dh + _dot_nt(_dproj_segment(pieces, n), w_ref[n])
        dn, dg1 = _rms_bwd(x_ref[...], g1_ref[...], dh)
        dx = dx1_ref[...] + dn
        outs[0][...] = dx
        outs[1][...] += dg1
        if below:
            df, dg4 = _rms_bwd(rest[0][...], rest[1][...], dx)
            outs[2][...] = df.astype(BF16)
            outs[3][...] += dg4

    rows, gain = pl.BlockSpec((tm, D), _row), pl.BlockSpec((1, D), _fixed2)
    vec = jax.ShapeDtypeStruct((1, D), F32)
    return _call(body, name, (s // tm,),
                 _dproj_specs(tm) + [_resident((NDEV, D, 512)), rows, gain, rows] + ([rows, gain] if below else []),
                 [rows, gain] + ([rows, gain] if below else []),
                 [jax.ShapeDtypeStruct((s, D), F32), vec] + ([jax.ShapeDtypeStruct((s, D), BF16), vec] if below else []),
                 exchange=exchange)(*segs, win, x, g1, dx1, *below)


def _dw_in(h, segs, name):
    s = h.shape[0]
    tk = 512
    nk = s // tk

    def body(h_ref, dq_ref, dk_ref, dv_ref, du_ref, dg_ref, o_ref, acc):
        k = pl.program_id(0)

        @pl.when(k == 0)
        def _():
            acc[...] = jnp.zeros_like(acc)

        ht = h_ref[...].T
        pieces = (dq_ref, dk_ref, dv_ref, du_ref, dg_ref)
        for n in range(NDEV):
            acc[n] += _dot(ht, _dproj_segment(pieces, n))

        @pl.when(k == nk - 1)
        def _():
            o_ref[...] = acc[...].astype(BF16)

    return _call(body, name, (nk,), [pl.BlockSpec((tk, D), _row)] + _dproj_specs(tk),
                 pl.BlockSpec((NDEV, D, 512), lambda k: (0, 0, 0)),
                 jax.ShapeDtypeStruct((NDEV, D, 512), BF16), [pltpu.VMEM((NDEV, D, 512), F32)])(h, *segs)


def _slot(px, py, pc):
    return 4 * px + 2 * py + pc


def _gather_exchange(shards, deliver):
    n = len(shards)

    def phases(ins, outs, sems):
        send_sems, recv_sems, local_sems = sems
        x, y, c = lax.axis_index("x"), lax.axis_index("y"), lax.axis_index("c")
        me, sibling = (x, y, c), (x, y, 1 - c)
        chips = [(1 - x, y), (x, 1 - y), (1 - x, 1 - y)]

        def copy(w, k, block, to, src=None):
            dst = outs[w].at[_slot(*block)]
            return pltpu.make_async_remote_copy(
                src_ref=dst if src is None else src, dst_ref=dst, send_sem=send_sems.at[7 * w + k],
                recv_sem=recv_sems.at[7 * w + k], device_id=to, device_id_type=MESH)

        def mine(w):
            return pltpu.make_async_copy(ins[w], outs[w].at[_slot(*me)], local_sems.at[w])

        def first(w):
            return [copy(w, 0, me, sibling, src=ins[w])] + [
                copy(w, 1 + j, me, (*chip, c), src=ins[w]) for j, chip in enumerate(chips)]

        def passed(w):
            return [copy(w, 4 + j, (*chip, c), sibling) for j, chip in enumerate(chips)]

        def send():
            for w in range(n):
                mine(w).start()
                for cp in first(w):
                    cp.start()

        def forward():
            for w in range(n):
                for j, chip in enumerate(chips):
                    copy(w, 1 + j, (*chip, c), me).wait_recv()
                    passed(w)[j].start()

        def finish():
            for w in range(n):
                copy(w, 0, sibling, me).wait_recv()
                for j, chip in enumerate(chips):
                    copy(w, 4 + j, (*chip, 1 - c), me).wait_recv()
            for w in range(n):
                for cp in first(w) + passed(w):
                    cp.wait_send()
                mine(w).wait()

        return [send, forward, finish]

    return _Exchange(shards, [jax.ShapeDtypeStruct((NDEV,) + a.shape, a.dtype) for a in shards],
                     [pltpu.SemaphoreType.DMA((7 * n,)), pltpu.SemaphoreType.DMA((7 * n,)),
                      pltpu.SemaphoreType.DMA((n,))], phases, deliver)


def _scatter_exchange(parts, deliver):
    n = len(parts)

    def phases(ins, outs, sems):
        send_sems, recv_sems, local_sems = sems
        x, y, c = lax.axis_index("x"), lax.axis_index("y"), lax.axis_index("c")
        me = _slot(x, y, c)
        peers = [((1 - x) if r & 4 else x, (1 - y) if r & 2 else y, (1 - c) if r & 1 else c) for r in range(1, NDEV)]

        def mine(w):
            return pltpu.make_async_copy(ins[w].at[me], outs[w].at[me], local_sems.at[w])

        def copy(w, r, block_here, block_there):
            return pltpu.make_async_remote_copy(
                src_ref=ins[w].at[block_here], dst_ref=outs[w].at[block_there], send_sem=send_sems.at[7 * w + r],
                recv_sem=recv_sems.at[7 * w + r], device_id=peers[r], device_id_type=MESH)

        def send():
            for w in range(n):
                mine(w).start()
                for r, peer in enumerate(peers):
                    copy(w, r, _slot(*peer), me).start()

        def finish():
            for w in range(n):
                for r, peer in enumerate(peers):
                    copy(w, r, me, _slot(*peer)).wait_recv()
            for w in range(n):
                for r, peer in enumerate(peers):
                    copy(w, r, _slot(*peer), me).wait_send()
                mine(w).wait()

        return [send, finish]

    return _Exchange(parts, [jax.ShapeDtypeStruct(a.shape, a.dtype) for a in parts],
                     [pltpu.SemaphoreType.DMA((7 * n,)), pltpu.SemaphoreType.DMA((7 * n,)),
                      pltpu.SemaphoreType.DMA((n,))], phases, deliver)


def _adamw(w, g, m, v):
    m = ADAM_B1 * m + (1.0 - ADAM_B1) * g
    v = ADAM_B2 * v + (1.0 - ADAM_B2) * (g * g)
    m_hat = m / (1.0 - ADAM_B1 ** ADAM_STEP)
    v_hat = v / (1.0 - ADAM_B2 ** ADAM_STEP)
    delta = -ADAM_LR * (m_hat / (jnp.sqrt(v_hat) + ADAM_EPS) + ADAM_WD * w)
    return delta, m, v


def _adamw_sharded(recvs, w, m, v, tr, name, exchange=None):
    _, r, cdim = w.shape
    nr = r // tr

    def body(r0_ref, r1_ref, w_ref, m_ref, v_ref, g_ref, d_ref, nm_ref, nv_ref):
        l = pl.program_id(0)

        def total(ref):
            acc = ref[0].astype(F32)
            for k in range(1, NDEV):
                acc = acc + ref[k].astype(F32)
            return acc

        g = jnp.where(l == 0, total(r0_ref), total(r1_ref))
        d, nm, nv = _adamw(w_ref[...], g, m_ref[...], v_ref[...])
        g_ref[...] = g
        d_ref[...] = d
        nm_ref[...] = nm
        nv_ref[...] = nv

    mine = pl.BlockSpec((None, tr, cdim), lambda l, i: (l, i, 0))
    out = jax.ShapeDtypeStruct(w.shape, F32)
    return _call(body, name, (DEPTH, nr),
                 [pl.BlockSpec((NDEV, tr, cdim), lambda l, i: (0, jnp.where(l == 0, i, nr - 1), 0)),
                  pl.BlockSpec((NDEV, tr, cdim), lambda l, i: (0, jnp.where(l == 1, i, 0), 0)),
                  mine, mine, mine],
                 [mine] * 4, [out] * 4, exchange=exchange)(recvs[0], recvs[1], w, m, v)


def _adamw_small(parts, w, m, v, name):
    r = w.shape[0]

    def body(p_ref, w_ref, m_ref, v_ref, g_ref, d_ref, nm_ref, nv_ref):
        g = p_ref[0]
        for k in range(1, NDEV):
            g = g + p_ref[k]
        d, nm, nv = _adamw(w_ref[...], g, m_ref[...], v_ref[...])
        g_ref[...] = g
        d_ref[...] = d
        nm_ref[...] = nm
        nv_ref[...] = nv

    whole = pl.BlockSpec((r, 1024), _fixed2)
    out = jax.ShapeDtypeStruct((r, 1024), F32)
    return _call(body, name, (1,), [pl.BlockSpec((NDEV, r, 1024), lambda i: (0, 0, 0)), whole, whole, whole],
                 [whole] * 4, [out] * 4)(parts, w, m, v)


SMALL = (("norm_mix_pre", (D,)), ("b_gate", (2 * D,)), ("rel_bias", (HEADS, NREL)), ("w_pool_group", (4, PG, PG)),
         ("pool_scale", (PW,)), ("norm_mix_post", (D,)), ("norm_ffn_pre", (D,)), ("conv_b", (NFF * 2 * FS,)),
         ("norm_ffn_post", (D,)))
SHARDED = (("w_in", 256), ("w_attn_out", 512), ("w_pool_out", 512), ("w_o", 128), ("w_up", 256), ("conv_w", 3),
           ("w_down", 176))
SMALL_ROWS = 168


def _pack_small(tree):
    flat = jnp.concatenate([tree[name].reshape(-1) for name, _ in SMALL])
    return jnp.pad(flat, (0, SMALL_ROWS * 1024 - flat.shape[0])).reshape(SMALL_ROWS, 1024)


def _unpack_small(packed):
    flat = packed.reshape(-1)
    out, at = {}, 0
    for name, shape in SMALL:
        size = DEPTH * int(np.prod(shape))
        out[name] = flat[at:at + size].reshape((DEPTH,) + shape)
        at += size
    return out


def _layer_fwd(l, x, h, p, gath, target, gnext, carry):
    n = f"l{l}"
    g = lambda name: p[name][l].reshape(1, -1)
    res = {"x": x, "h": h}
    res["tab"] = _bias_table(p["rel_bias"][l], f"bias_table_{n}")
    proj = _mm_in(h, gath["w_in"], f"mm_in_{n}", carry.get("mm_in"))
    att, res["lse"] = _attn_fwd(proj, res["tab"], f"attn_fwd_{n}", carry.get("attn_fwd"))
    pooled, mixed = _pool_fwd(proj, p["w_pool_group"][l], g("pool_scale"), f"pool_fwd_{n}")
    bgate = g("b_gate")
    z, ya, yb, mix, x1, h2 = _mix_fwd(att, mixed, gath["w_attn_out"], gath["w_pool_out"], proj, bgate, gath["w_o"],
                                      x, g("norm_mix_post"), g("norm_ffn_pre"), f"mix_fwd_{n}")
    cw = gath["conv_w"].reshape(2, NFF, 3, FS)
    cb = p["conv_b"][l].reshape(2, NFF, 1, FS)
    hu, a = _mm_up_conv_gate(h2, gath["w_up"], cw, cb, f"mm_up_conv_gate_{n}", carry.get("mm_up_conv_gate"))
    hu = hu.reshape(2, NFF, -1, FS)
    res.update(proj=proj, att=att, pooled=pooled, mixed=mixed, z=z, ya=ya, yb=yb, mix=mix, x1=x1, h2=h2, hu=hu,
               a=a, cw=cw, cb=cb, bgate=bgate)
    if target is None:
        f, x2, hn = _mm_down_fwd(a, gath["w_down"], x1, g("norm_ffn_post"), gnext, f"mm_down_fwd_{n}",
                                 carry.get("mm_down_fwd"))
        res["f"] = f
        return res, x2, hn
    dy, df, dg4, loss = _mm_down_loss(a, gath["w_down"], x1, g("norm_ffn_post"), target, f"mm_down_loss_{n}")
    return res, (dy, df, dg4), loss


def _layer_bwd(l, top, p, gath, res, carry, below):
    n = f"l{l}"
    g = lambda name: p[name][l].reshape(1, -1)
    big, small = {}, {}
    taken = lambda call: carry[call](big) if call in carry else None
    dx2, df, small["norm_ffn_post"] = top
    dhu, dcw, dcb, big["w_down"] = _conv_gate_bwd(res["hu"], df, gath["w_down"], res["cw"], res["cb"], res["a"],
                                                  f"conv_gate_bwd_{n}", taken("conv_gate_bwd"))
    big["conv_w"] = dcw.reshape(NDEV, 3, FS)
    small["conv_b"] = dcb
    dx1, dmix, small["norm_ffn_pre"], small["norm_mix_post"] = _mm_dh2(
        dhu, gath["w_up"], res["x1"], g("norm_ffn_pre"), dx2, res["mix"], g("norm_mix_post"), f"mm_dh2_{n}")
    big["w_up"] = _dw_up(res["h2"], dhu, f"dw_up_{n}")
    dya, dyb, dgates, small["b_gate"], datt, dmixed = _mm_dz(
        dmix, gath["w_o"], res["proj"], res["bgate"], res["ya"], res["yb"], gath["w_attn_out"], gath["w_pool_out"],
        f"mm_dz_{n}")
    big["w_o"] = _dw_o(res["z"], dmix, f"dw_o_{n}")
    big["w_attn_out"], big["w_pool_out"] = _dw_branch(res["att"], res["mixed"], dya, dyb, f"dw_branch_{n}")
    dq, dk, dv, dtab = _attn_bwd(res["proj"], datt, res["att"], res["lse"], res["tab"], f"attn_bwd_{n}",
                                 taken("attn_bwd"))
    small["rel_bias"] = _rel_bias_grad(dtab, f"rel_bias_grad_{n}")[:, 0, :NREL]
    du, small["w_pool_group"], small["pool_scale"] = _pool_bwd(
        dmixed, res["pooled"], p["w_pool_group"][l], g("pool_scale"), f"pool_bwd_{n}")
    segs = (dq, dk, dv, du, dgates)
    big["w_in"] = _dw_in(res["h"], segs, f"dw_in_{n}")
    dx, small["norm_mix_pre"], *lower = _mm_dh(segs, gath["w_in"], res["x"], g("norm_mix_pre"), dx1, below,
                                               f"mm_dh_{n}", taken("mm_dh"))
    return (dx, *lower), big, small


def _gather_weights(gath, l, p, names):
    shards = [p[k][l] if k == "conv_w" else p[k][l].astype(BF16) for k in names]

    def deliver(results):
        for k, a in zip(names, results):
            if k in ("w_attn_out", "w_pool_out"):
                a = jnp.transpose(a, (1, 0, 2)).reshape(AW, D)
            elif k == "w_o":
                a = a.reshape(D, D)
            elif k == "w_down":
                a = a.reshape(NFF * FS, D)
            gath[k] = a

    return _gather_exchange(shards, deliver)


def _scatter_grads(recv, big, names):
    return _scatter_exchange([big[k] for k in names], lambda results: recv.update(zip(names, results)))


def kernel(x, norm_mix_pre, w_in, b_gate, rel_bias, w_attn_out, w_pool_group, pool_scale, w_pool_out, w_o, norm_mix_post, norm_ffn_pre, w_up, conv_w, conv_b, w_down, norm_ffn_post, loss_target, m_norm_mix_pre, m_w_in, m_b_gate, m_rel_bias, m_w_attn_out, m_w_pool_group, m_pool_scale, m_w_pool_out, m_w_o, m_norm_mix_post, m_norm_ffn_pre, m_w_up, m_conv_w, m_conv_b, m_w_down, m_norm_ffn_post, v_norm_mix_pre, v_w_in, v_b_gate, v_rel_bias, v_w_attn_out, v_w_pool_group, v_pool_scale, v_w_pool_out, v_w_o, v_norm_mix_post, v_norm_ffn_pre, v_w_up, v_conv_w, v_conv_b, v_w_down, v_norm_ffn_post):
    names = ("norm_mix_pre", "w_in", "b_gate", "rel_bias", "w_attn_out", "w_pool_group", "pool_scale", "w_pool_out",
             "w_o", "norm_mix_post", "norm_ffn_pre", "w_up", "conv_w", "conv_b", "w_down", "norm_ffn_post")
    p = dict(zip(names, (norm_mix_pre, w_in, b_gate, rel_bias, w_attn_out, w_pool_group, pool_scale, w_pool_out, w_o,
                         norm_mix_post, norm_ffn_pre, w_up, conv_w, conv_b, w_down, norm_ffn_post)))
    mom = dict(zip(names, (m_norm_mix_pre, m_w_in, m_b_gate, m_rel_bias, m_w_attn_out, m_w_pool_group, m_pool_scale,
                           m_w_pool_out, m_w_o, m_norm_mix_post, m_norm_ffn_pre, m_w_up, m_conv_w, m_conv_b, m_w_down,
                           m_norm_ffn_post)))
    var = dict(zip(names, (v_norm_mix_pre, v_w_in, v_b_gate, v_rel_bias, v_w_attn_out, v_w_pool_group, v_pool_scale,
                           v_w_pool_out, v_w_o, v_norm_mix_post, v_norm_ffn_pre, v_w_up, v_conv_w, v_conv_b, v_w_down,
                           v_norm_ffn_post)))
    s = x.shape[1]
    xs = x.reshape(s, D)
    target = loss_target.reshape(s, D)

    gath = [{}, {}]
    rest = ("w_attn_out", "w_pool_out", "w_o", "w_down", "conv_w")
    h0 = _rmsnorm_call(xs, p["norm_mix_pre"][0].reshape(1, D), "rmsnorm_l0",
                       _gather_weights(gath[0], 0, p, ("w_in",)))
    res0, x2, h1 = _layer_fwd(0, xs, h0, p, gath[0], None, p["norm_mix_pre"][1].reshape(1, D), {
        "mm_in": _gather_weights(gath[0], 0, p, rest),
        "attn_fwd": _gather_weights(gath[0], 0, p, ("w_up",)),
        "mm_up_conv_gate": _gather_weights(gath[1], 1, p, ("w_in",) + rest),
        "mm_down_fwd": _gather_weights(gath[1], 1, p, ("w_up",))})
    res1, top1, loss_part = _layer_fwd(1, x2, h1, p, gath[1], target, None, {})

    order = [k for k, _ in SHARDED]
    recv = [{}, {}]
    top0, big1, small1 = _layer_bwd(1, top1, p, gath[1], res1, {}, (res0["f"], p["norm_ffn_post"][0].reshape(1, D)))
    (grad_x,), big0, small0 = _layer_bwd(0, top0, p, gath[0], res0, {
        "conv_gate_bwd": lambda big: _scatter_grads(recv[1], big1, order),
        "attn_bwd": lambda big: _scatter_grads(recv[0], big, [k for k in order if k != "w_in"]),
        "mm_dh": lambda big: _scatter_grads(recv[0], big, ["w_in"])}, ())

    loss = lax.psum(loss_part[0, 0], ("x", "y", "c"))

    small = {k: jnp.stack([small0[k].reshape(shape), small1[k].reshape(shape)]) for k, shape in SMALL}
    parts = []
    carried = {"w_up": _gather_exchange([_pack_small(small)], parts.extend)}
    out = {}
    for k, tr in sorted(SHARDED, key=lambda kt: kt[0] not in carried):
        out[k] = _adamw_sharded((recv[0][k], recv[1][k]), p[k], mom[k], var[k], tr, f"adamw_{k}", carried.get(k))
    packed = _adamw_small(parts[0], _pack_small(p), _pack_small(mom), _pack_small(var), "adamw_small")
    unpacked = [_unpack_small(a) for a in packed]
    for k, _ in SMALL:
        out[k] = tuple(u[k] for u in unpacked)

    return (loss, grad_x.reshape(x.shape), *[out[k][0] for k in names], *[out[k][1] for k in names],
            *[out[k][2] for k in names], *[out[k][3] for k in names])
```

```python
import numpy as np
import jax
import jax.numpy as jnp
from jax import lax
from jax.experimental import pallas as pl
from jax.experimental.pallas import tpu as pltpu

F32, BF16 = jnp.float32, jnp.bfloat16

D = 1024
AW = 512
PW = 512
PG = 128
INW = 4096
FS = 704
NFF = 4
NDEV = 8
DEPTH = 2
HEADS = 8
NREL = 513
MAXREL = 256
POOL_WINDOWS = (2, 4, 8, 16)
EPS = 1e-6
SCALE = 0.125
NEG = -1e30
QB = 256
KW = 3 * QB
BAND = 576
ADAM_LR, ADAM_B1, ADAM_B2, ADAM_EPS, ADAM_WD, ADAM_STEP = 0.001, 0.9, 0.999, 1e-08, 0.01, 10
VMEM_LIMIT_V7X = 56 * 1024 * 1024
MESH = pl.DeviceIdType.MESH
GELU_C = 0.7978845608028654
GELU_A = 0.044715


HBM = pl.BlockSpec(memory_space=pltpu.HBM)


class _Exchange:
    def __init__(self, operands, out_shape, scratch, phases, deliver):
        self.operands, self.out_shape, self.scratch = list(operands), list(out_shape), list(scratch)
        self.phases, self.deliver = phases, deliver


def _call(body, name, grid, in_specs, out_specs, out_shape, scratch=(), exchange=None):
    params = pltpu.CompilerParams(vmem_limit_bytes=VMEM_LIMIT_V7X)
    if exchange is None:
        return pl.pallas_call(body, name=name, grid=grid, in_specs=in_specs, out_specs=out_specs, out_shape=out_shape,
                              scratch_shapes=list(scratch), compiler_params=params)
    single = not isinstance(out_shape, (list, tuple))
    outs, ospecs = ([out_shape], [out_specs]) if single else (list(out_shape), list(out_specs))
    n_in, n_out, n_scr = len(in_specs), len(outs), len(scratch)
    ne_in, ne_out = len(exchange.operands), len(exchange.out_shape)
    nsteps = int(np.prod(grid))

    def carried(*refs):
        cut = np.cumsum([0, n_in, ne_in, n_out, ne_out, n_scr])
        base_in, ex_in, base_out, ex_out, base_scr = (refs[cut[k]:cut[k + 1]] for k in range(5))
        step = pl.program_id(0)
        for axis in range(1, len(grid)):
            step = step * grid[axis] + pl.program_id(axis)
        phases = exchange.phases(ex_in, ex_out, refs[cut[5]:])
        pl.when(step == 0)(phases[0])
        body(*base_in, *base_out, *base_scr)
        at = {2: [nsteps - 1], 3: [(7 * nsteps) // 10, nsteps - 1]}[len(phases)]
        for phase, when in zip(phases[1:], at):
            pl.when(step == when)(phase)

    call = pl.pallas_call(
        carried, name=name, grid=grid, in_specs=list(in_specs) + [HBM] * ne_in, out_specs=ospecs + [HBM] * ne_out,
        out_shape=outs + exchange.out_shape, scratch_shapes=list(scratch) + exchange.scratch, compiler_params=params)

    def run(*args):
        res = call(*args, *exchange.operands)
        exchange.deliver(res[n_out:])
        return res[0] if single else res[:n_out]

    return run


def _dot(a, b):
    return jnp.dot(a, b, preferred_element_type=F32)


def _dot_nt(a, b):
    return lax.dot_general(a, b, (((1,), (1,)), ((), ())), preferred_element_type=F32)


def _dot_tn(a, b):
    return lax.dot_general(a, b, (((0,), (0,)), ((), ())), preferred_element_type=F32)


def _rms(x, g):
    r = lax.rsqrt(jnp.mean(x * x, axis=-1, keepdims=True) + EPS)
    return x * r * g


def _rms_bwd(x, g, dy):
    r = lax.rsqrt(jnp.mean(x * x, axis=-1, keepdims=True) + EPS)
    xh = x * r
    dg = jnp.sum(dy * xh, axis=0, keepdims=True)
    dxh = dy * g
    dx = r * (dxh - xh * jnp.mean(dxh * xh, axis=-1, keepdims=True))
    return dx, dg


def _sigmoid(x):
    return 1.0 / (1.0 + jnp.exp(-x))


def _gelu_parts(x):
    x2 = x * x
    th = jnp.tanh(x * (GELU_C + (GELU_C * GELU_A) * x2))
    s = 0.5 * th + 0.5
    dg = s * (1.0 + x * (1.0 - s) * (2.0 * GELU_C + (6.0 * GELU_C * GELU_A) * x2))
    return x * s, dg


def _row(i):
    return (i, 0)


def _fixed2(*_):
    return (0, 0)


def _rmsnorm_call(x, g, name, exchange=None):
    s = x.shape[0]
    tm = 512

    def body(x_ref, g_ref, o_ref):
        o_ref[...] = _rms(x_ref[...], g_ref[...]).astype(BF16)

    return _call(body, name, (s // tm,),
                 [pl.BlockSpec((tm, D), _row), pl.BlockSpec((1, D), _fixed2)],
                 pl.BlockSpec((tm, D), _row), jax.ShapeDtypeStruct((s, D), BF16), exchange=exchange)(x, g)


def _resident(shape):
    return pl.BlockSpec(shape, lambda *_: (0,) * len(shape), pipeline_mode=pl.Buffered(1))


def _mm_in(h, win, name, exchange=None):
    s = h.shape[0]
    tm = 512

    def body(h_ref, w_ref, o_ref):
        hv = h_ref[...]
        for j in range(NDEV):
            o_ref[:, j * 512:(j + 1) * 512] = _dot(hv, w_ref[j])

    return _call(body, name, (s // tm,),
                 [pl.BlockSpec((tm, D), _row), _resident((NDEV, D, 512))],
                 pl.BlockSpec((tm, INW), _row), jax.ShapeDtypeStruct((s, INW), F32), exchange=exchange)(h, win)


def _bias_table(rel_bias, name):
    wdt = 1024
    rel = jnp.pad(rel_bias, ((0, 0), (0, 640 - NREL))).reshape(HEADS, 1, 640)

    def body(r_ref, o_ref):
        rr = lax.broadcasted_iota(jnp.int32, (640, wdt), 0)
        m = lax.broadcasted_iota(jnp.int32, (640, wdt), 1)
        d = jnp.where(m < KW, m, m - wdt)
        onehot = (jnp.clip(512 - d, -MAXREL, MAXREL) + MAXREL == rr).astype(F32)
        row = jnp.dot(jnp.broadcast_to(r_ref[...], (8, 640)), onehot, preferred_element_type=F32,
                      precision=lax.Precision.HIGHEST)[0:1]
        t = pltpu.roll(jnp.broadcast_to(row, (QB, wdt)), 0, 1, stride=1, stride_axis=0)[:, 0:KW]
        qc = lax.broadcasted_iota(jnp.int32, (QB, KW), 0) // 64
        kc = lax.broadcasted_iota(jnp.int32, (QB, KW), 1) // 64
        o_ref[...] = jnp.where((kc >= qc) & (kc <= qc + 8), t, NEG)

    return _call(body, name, (HEADS,), [pl.BlockSpec((None, 1, 640), lambda h: (h, 0, 0))],
                 pl.BlockSpec((None, QB, KW), lambda h: (h, 0, 0)),
                 jax.ShapeDtypeStruct((HEADS, QB, KW), F32))(rel)


def _attn_scores(q_ref, k_refs, b_ref, i):
    lane = lax.broadcasted_iota(jnp.int32, (QB, 128), 1)
    q = q_ref[...] * SCALE
    qs = [jnp.where(lane < 64, q, 0.0).astype(BF16), jnp.where(lane >= 64, q, 0.0).astype(BF16)]
    k = jnp.concatenate([r[...] for r in k_refs], axis=0).astype(BF16)
    colb = lax.broadcasted_iota(jnp.int32, (1, KW), 1) // QB
    before = jnp.where(colb + i >= 2, 0.0, NEG)
    return qs, [_dot_nt(qs[hh], k) + b_ref[hh] + before for hh in (0, 1)], k, lane


def _attn_fwd(proj, tab, name, exchange=None):
    s = proj.shape[0]
    nq = s // QB

    def body(q_ref, k0, k1, k2, v0, v1, v2, b_ref, o_ref, lse_ref):
        _, scs, _, lane = _attn_scores(q_ref, (k0, k1, k2), b_ref, pl.program_id(1))
        v = jnp.concatenate([v0[...], v1[...], v2[...]], axis=0).astype(BF16)
        v1s = jnp.concatenate([v, jnp.ones((KW, 128), BF16)], axis=1)
        o, lse = [], []
        for sc in scs:
            m = jnp.max(sc, axis=1, keepdims=True)
            ov = _dot(jnp.exp(sc - m).astype(BF16), v1s)
            o.append(ov[:, 0:128] / ov[:, 128:256])
            lse.append(m + jnp.log(ov[:, 128:256]))
        o_ref[...] = jnp.where(lane < 64, o[0], o[1]).astype(BF16)
        lse_ref[...] = jnp.where(lane < 64, lse[0], lse[1])

    def kv(col, d):
        return pl.BlockSpec((QB, 128), lambda p, i: (jnp.maximum(i - 2 + d, 0), col + p))

    in_specs = [pl.BlockSpec((QB, 128), lambda p, i: (i, p))]
    in_specs += [kv(4, d) for d in range(3)] + [kv(8, d) for d in range(3)]
    in_specs += [pl.BlockSpec((2, QB, KW), lambda p, i: (p, 0, 0))]
    out = pl.BlockSpec((QB, 128), lambda p, i: (i, p))
    return _call(body, name, (4, nq), in_specs, [out, out],
                 [jax.ShapeDtypeStruct((s, AW), BF16), jax.ShapeDtypeStruct((s, AW), F32)], exchange=exchange)(
                     proj, proj, proj, proj, proj, proj, proj, tab)


def _pool_fwd(proj, wg, scale, name):
    s = proj.shape[0]
    tb = 512
    e = tb + 16

    def body(u_ref, halo_ref, wg_ref, sc_ref, pooled_ref, mixed_ref):
        i = pl.program_id(0)
        cur = u_ref[...]
        prev = jnp.where(i > 0, halo_ref[...], 0.0)
        xs = jnp.concatenate([prev, cur], axis=0)
        t = i * tb + lax.broadcasted_iota(jnp.int32, (tb, 1), 0)
        for g, w in enumerate(POOL_WINDOWS):
            sl = slice(g * PG, (g + 1) * PG)
            a = xs[:, sl]
            sh = 1
            while sh < w:
                a = a + pltpu.roll(a, sh, 0)
                sh *= 2
            cnt = jnp.minimum(t + 1, w).astype(F32)
            pooled = (a[16:] / cnt - cur[:, sl]).astype(BF16)
            pooled_ref[:, sl] = pooled
            mixed_ref[:, sl] = (_dot(pooled, wg_ref[g].astype(BF16)) * sc_ref[:, sl]).astype(BF16)

    assert e % 8 == 0
    return _call(body, name, (s // tb,),
                 [pl.BlockSpec((tb, PW), lambda i: (i, 3)),
                  pl.BlockSpec((16, PW), lambda i: (jnp.maximum(i * (tb // 16) - 1, 0), 3)),
                  pl.BlockSpec((4, PG, PG), lambda i: (0, 0, 0)), pl.BlockSpec((1, PW), _fixed2)],
                 [pl.BlockSpec((tb, PW), _row), pl.BlockSpec((tb, PW), _row)],
                 [jax.ShapeDtypeStruct((s, PW), BF16), jax.ShapeDtypeStruct((s, PW), BF16)])(proj, proj, wg, scale)


def _mix_fwd(att, mixed, wao, wpo, proj, bgate, wo, x, g2, g3, name):
    s = att.shape[0]
    tm = 512

    def body(att_ref, mx_ref, wao_ref, wpo_ref, ga_ref, gb_ref, ba_ref, bb_ref, wo_ref, x_ref, g2_ref, g3_ref,
             z_ref, ya_ref, yb_ref, mix_ref, x1_ref, h2_ref):
        ya = _dot(att_ref[...], wao_ref[...])
        yb = _dot(mx_ref[...], wpo_ref[...])
        ga = _sigmoid(ga_ref[...] + ba_ref[...])
        gb = _sigmoid(gb_ref[...] + bb_ref[...])
        z = (ga * ya + gb * yb).astype(BF16)
        mix = _dot(z, wo_ref[...])
        x1 = x_ref[...] + _rms(mix, g2_ref[...])
        ya_ref[...] = ya.astype(BF16)
        yb_ref[...] = yb.astype(BF16)
        z_ref[...] = z
        mix_ref[...] = mix
        x1_ref[...] = x1
        h2_ref[...] = _rms(x1, g3_ref[...]).astype(BF16)

    rows, gain = pl.BlockSpec((tm, D), _row), pl.BlockSpec((1, D), _fixed2)
    half, full = jax.ShapeDtypeStruct((s, D), BF16), jax.ShapeDtypeStruct((s, D), F32)
    return _call(body, name, (s // tm,),
                 [pl.BlockSpec((tm, AW), _row), pl.BlockSpec((tm, PW), _row),
                  _resident((AW, D)), _resident((PW, D)),
                  pl.BlockSpec((tm, D), lambda i: (i, 2)), pl.BlockSpec((tm, D), lambda i: (i, 3)),
                  pl.BlockSpec((1, D), lambda i: (0, 0)), pl.BlockSpec((1, D), lambda i: (0, 1)),
                  _resident((D, D)), rows, gain, gain],
                 [rows] * 6, [half, half, half, full, full, half])(
                     att, mixed, wao, wpo, proj, proj, bgate, bgate, wo, x, g2, g3)


def _mm_up_conv_gate(h2, wup, cw, cb, name, exchange=None):
    s = h2.shape[0]
    tm = 256

    def body(h_ref, w_ref, cw_ref, cb_ref, hu_ref, a_ref, halo):
        first = pl.program_id(0) == 0
        hv = h_ref[...]
        for j in range(NDEV):
            hu_ref[j] = _dot(hv, w_ref[j])
        for jj in range(NFF):
            for c0, w in LANE_COLUMNS:
                lanes = pl.ds(c0, w)
                cwb = [[jnp.broadcast_to(cw_ref[sd, jj, t:t + 1, lanes], (8, w)) for t in range(3)] for sd in (0, 1)]
                cbb = [jnp.broadcast_to(cb_ref[sd, jj, :, lanes], (8, w)) for sd in (0, 1)]
                row = lax.broadcasted_iota(jnp.int32, (8, w), 0)
                rolled = []
                for sd in (0, 1):
                    before = jnp.where(first, 0.0, halo[NFF * sd + jj, :, lanes])
                    rolled += [pltpu.roll(before, 1, 0), pltpu.roll(before, 2, 0)]
                for k in range(tm // 16):
                    outs = []
                    for r in (16 * k, 16 * k + 8):
                        hc, keep = [], []
                        for sd in (0, 1):
                            cur = hu_ref[NFF * sd + jj, pl.ds(r, 8), lanes]
                            r1, r2 = pltpu.roll(cur, 1, 0), pltpu.roll(cur, 2, 0)
                            hc.append(cbb[sd] + cwb[sd][2] * cur
                                      + cwb[sd][0] * jnp.where(row >= 2, r2, rolled[2 * sd + 1])
                                      + cwb[sd][1] * jnp.where(row >= 1, r1, rolled[2 * sd]))
                            keep += [r1, r2]
                        rolled = keep
                        outs.append(_gelu_parts(hc[1])[0] * hc[0])
                    a_ref[jj, pl.ds(16 * k, 16), lanes] = jnp.concatenate(outs, axis=0).astype(BF16)
        for j in range(NDEV):
            halo[j] = hu_ref[j, tm - 8:tm, :]

    small = lambda i: (0, 0, 0, 0)
    return _call(body, name, (s // tm,),
                 [pl.BlockSpec((tm, D), _row), _resident((NDEV, D, FS)), pl.BlockSpec((2, NFF, 3, FS), small),
                  pl.BlockSpec((2, NFF, 1, FS), small)],
                 [pl.BlockSpec((NDEV, tm, FS), lambda i: (0, i, 0)), pl.BlockSpec((NFF, tm, FS), lambda i: (0, i, 0))],
                 [jax.ShapeDtypeStruct((NDEV, s, FS), F32), jax.ShapeDtypeStruct((NFF, s, FS), BF16)],
                 [pltpu.VMEM((NDEV, 8, FS), F32)], exchange=exchange)(h2, wup, cw, cb)


LANE_COLUMNS = [(c0, min(128, FS - c0)) for c0 in range(0, FS, 128)]


def _mm_down_fwd(a, wd, x1, g4, gnext, name, exchange=None):
    s = a.shape[1]
    tm = 512

    def body(a_ref, w_ref, x1_ref, g4_ref, gn_ref, f_ref, x2_ref, hn_ref):
        f = _dot(a_ref[0], w_ref[0:FS, :])
        for j in range(1, NFF):
            f = f + _dot(a_ref[j], w_ref[j * FS:(j + 1) * FS, :])
        x2 = x1_ref[...] + _rms(f, g4_ref[...])
        f_ref[...] = f
        x2_ref[...] = x2
        hn_ref[...] = _rms(x2, gn_ref[...]).astype(BF16)

    return _call(body, name, (s // tm,),
                 [pl.BlockSpec((NFF, tm, FS), lambda i: (0, i, 0)), pl.BlockSpec((NFF * FS, D), _fixed2),
                  pl.BlockSpec((tm, D), _row), pl.BlockSpec((1, D), _fixed2), pl.BlockSpec((1, D), _fixed2)],
                 [pl.BlockSpec((tm, D), _row)] * 3,
                 [jax.ShapeDtypeStruct((s, D), F32), jax.ShapeDtypeStruct((s, D), F32),
                  jax.ShapeDtypeStruct((s, D), BF16)], exchange=exchange)(a, wd, x1, g4, gnext)


def _mm_down_loss(a, wd, x1, g4, target, name):
    s = a.shape[1]
    tm = 512

    def body(a_ref, w_ref, x1_ref, g4_ref, t_ref, dy_ref, df_ref, dg_ref, loss_ref):
        i = pl.program_id(0)
        f = _dot(a_ref[0], w_ref[0:FS, :])
        for j in range(1, NFF):
            f = f + _dot(a_ref[j], w_ref[j * FS:(j + 1) * FS, :])
        err = x1_ref[...] + _rms(f, g4_ref[...]) - t_ref[...]
        dy = err * (1.0 / D)
        df, dg = _rms_bwd(f, g4_ref[...], dy)
        dy_ref[...] = dy
        df_ref[...] = df.astype(BF16)
        part = 0.5 * jnp.sum(jnp.mean(err * err, axis=-1, keepdims=True), axis=0, keepdims=True)

        @pl.when(i == 0)
        def _():
            loss_ref[...] = jnp.zeros_like(loss_ref)
            dg_ref[...] = jnp.zeros_like(dg_ref)

        loss_ref[...] += jnp.broadcast_to(part, loss_ref.shape)
        dg_ref[...] += dg

    return _call(body, name, (s // tm,),
                 [pl.BlockSpec((NFF, tm, FS), lambda i: (0, i, 0)), pl.BlockSpec((NFF * FS, D), _fixed2),
                  pl.BlockSpec((tm, D), _row), pl.BlockSpec((1, D), _fixed2), pl.BlockSpec((tm, D), _row)],
                 [pl.BlockSpec((tm, D), _row), pl.BlockSpec((tm, D), _row), pl.BlockSpec((1, D), _fixed2),
                  pl.BlockSpec((8, 128), _fixed2)],
                 [jax.ShapeDtypeStruct((s, D), F32), jax.ShapeDtypeStruct((s, D), BF16),
                  jax.ShapeDtypeStruct((1, D), F32), jax.ShapeDtypeStruct((8, 128), F32)])(a, wd, x1, g4, target)


def _conv_gate_bwd(hu, df, wd, cw, cb, a, name, exchange=None):
    s = hu.shape[2]
    tb = 512
    nt = s // tb
    e = tb + 16

    def body(hu_ref, prev_ref, next_ref, df_ref, dfn_ref, wd_ref, cw_ref, cb_ref, a_ref, dhu_ref, dcw_ref, dcb_ref,
             dwd_ref, dabuf, dbuf, dwd_acc):
        i = pl.program_id(1)
        first, last = i == 0, i == nt - 1

        @pl.when(i == 0)
        def _():
            dcw_ref[...] = jnp.zeros_like(dcw_ref)
            dcb_ref[...] = jnp.zeros_like(dcb_ref)
            dwd_acc[...] = jnp.zeros_like(dwd_acc)

        dabuf[0:tb] = _dot_nt(df_ref[...], wd_ref[...])
        dabuf[tb:tb + 8] = jnp.where(last, 0.0, _dot_nt(dfn_ref[...], wd_ref[...])[0:8])
        dwd_acc[...] += _dot_tn(a_ref[...], df_ref[...])

        for c0, w in LANE_COLUMNS:
            lanes = pl.ds(c0, w)
            cwb = [[jnp.broadcast_to(cw_ref[sd, t:t + 1, lanes], (8, w)) for t in range(3)] for sd in (0, 1)]
            cbb = [jnp.broadcast_to(cb_ref[sd, :, lanes], (8, w)) for sd in (0, 1)]

            row = lax.broadcasted_iota(jnp.int32, (8, w), 0)

            def tile(k, carry, summed=True):
                sums, rolled = carry[:8], carry[8:]
                r = 8 + 8 * k
                xs, keep = [], []
                for sd in (0, 1):
                    if summed:
                        cur = hu_ref[sd, pl.ds(r - 8, 8), lanes]
                    else:
                        cur = jnp.where(last, 0.0, next_ref[sd, :, lanes])
                    r1, r2 = pltpu.roll(cur, 1, 0), pltpu.roll(cur, 2, 0)
                    xs.append([jnp.where(row >= 2, r2, rolled[2 * sd + 1]), jnp.where(row >= 1, r1, rolled[2 * sd]),
                               cur])
                    keep += [r1, r2]
                hc = [cbb[sd] + cwb[sd][2] * xs[sd][2] + cwb[sd][0] * xs[sd][0] + cwb[sd][1] * xs[sd][1]
                      for sd in (0, 1)]
                da = dabuf[pl.ds(r - 8, 8), lanes]
                gl, dgl = _gelu_parts(hc[1])
                dhc = (da * gl, da * hc[0] * dgl)
                for sd in (0, 1):
                    dbuf[sd, pl.ds(r, 8), lanes] = dhc[sd]
                if not summed:
                    return carry
                new = []
                for sd in (0, 1):
                    new += [sums[4 * sd + t] + dhc[sd] * xs[sd][t] for t in range(3)] + [sums[4 * sd + 3] + dhc[sd]]
                return tuple(new + keep)

            start = [jnp.zeros((8, w), F32) for _ in range(8)]
            for sd in (0, 1):
                halo = jnp.where(first, 0.0, prev_ref[sd, :, lanes])
                start += [pltpu.roll(halo, 1, 0), pltpu.roll(halo, 2, 0)]
            def tiles(k4, carry):
                for u in range(4):
                    carry = tile(4 * k4 + u, carry)
                return carry

            carry = tuple(start)
            for k4 in range(tb // 32):
                carry = tiles(k4, carry)
            tile(tb // 8, carry, summed=False)
            sums = carry[:8]

            def up(v):
                return pltpu.roll(v, 7, 0), pltpu.roll(v, 6, 0)

            def out_tile(k, carry):
                r = 8 + 16 * k
                new = []
                for sd in (0, 1):
                    va, va1, va2 = carry[3 * sd:3 * sd + 3]
                    vb, vc = dbuf[sd, pl.ds(r + 8, 8), lanes], dbuf[sd, pl.ds(r + 16, 8), lanes]
                    (vb1, vb2), (vc1, vc2) = up(vb), up(vc)
                    c0b, c1b, c2b = cwb[sd]
                    top = c2b * va + c1b * jnp.where(row <= 6, va1, vb1) + c0b * jnp.where(row <= 5, va2, vb2)
                    bot = c2b * vb + c1b * jnp.where(row <= 6, vb1, vc1) + c0b * jnp.where(row <= 5, vb2, vc2)
                    dhu_ref[sd, pl.ds(16 * k, 16), lanes] = jnp.concatenate(
                        [top, bot], axis=0).astype(BF16)
                    new += [vc, vc1, vc2]
                return tuple(new)

            begin = []
            for sd in (0, 1):
                va = dbuf[sd, 8:16, lanes]
                begin += [va, *up(va)]
            def out_tiles(k2, carry):
                return out_tile(2 * k2 + 1, out_tile(2 * k2, carry))

            carry = tuple(begin)
            for k2 in range(tb // 32):
                carry = out_tiles(k2, carry)
            for sd in (0, 1):
                for t in range(3):
                    dcw_ref[sd, t:t + 1, lanes] += jnp.sum(sums[4 * sd + t], axis=0, keepdims=True)
                dcb_ref[sd, :, lanes] += jnp.sum(sums[4 * sd + 3], axis=0, keepdims=True)

        @pl.when(last)
        def _():
            dwd_ref[0] = dwd_acc[0:FS // 2, :].astype(BF16)
            dwd_ref[1] = dwd_acc[FS // 2:FS, :].astype(BF16)

    nb8, nb16 = s // 8, s // 16
    return _call(body, name, (NFF, nt),
                 [pl.BlockSpec((2, None, tb, FS), lambda j, i: (0, j, i, 0)),
                  pl.BlockSpec((2, None, 8, FS), lambda j, i: (0, j, jnp.maximum(i * (tb // 8) - 1, 0), 0)),
                  pl.BlockSpec((2, None, 8, FS), lambda j, i: (0, j, jnp.minimum((i + 1) * (tb // 8), nb8 - 1), 0)),
                  pl.BlockSpec((tb, D), lambda j, i: (i, 0)),
                  pl.BlockSpec((16, D), lambda j, i: (jnp.minimum((i + 1) * (tb // 16), nb16 - 1), 0)),
                  pl.BlockSpec((FS, D), lambda j, i: (j, 0)),
                  pl.BlockSpec((2, None, 3, FS), lambda j, i: (0, j, 0, 0)),
                  pl.BlockSpec((2, None, 1, FS), lambda j, i: (0, j, 0, 0)),
                  pl.BlockSpec((None, tb, FS), lambda j, i: (j, i, 0))],
                 [pl.BlockSpec((2, None, tb, FS), lambda j, i: (0, j, i, 0)),
                  pl.BlockSpec((2, None, 3, FS), lambda j, i: (0, j, 0, 0)),
                  pl.BlockSpec((2, None, 1, FS), lambda j, i: (0, j, 0, 0)),
                  pl.BlockSpec((2, FS // 2, D), lambda j, i: (j, 0, 0))],
                 [jax.ShapeDtypeStruct((2, NFF, s, FS), BF16), jax.ShapeDtypeStruct((2, NFF, 3, FS), F32),
                  jax.ShapeDtypeStruct((2, NFF, 1, FS), F32), jax.ShapeDtypeStruct((NDEV, FS // 2, D), BF16)],
                 [pltpu.VMEM((tb + 8, FS), F32), pltpu.VMEM((2, e, FS), F32), pltpu.VMEM((FS, D), F32)],
                 exchange=exchange)(hu, hu, hu, df, df, wd, cw, cb, a)


def _mm_dh2(dhu, wup, x1, g3, dx2, mix, g2, name):
    s = x1.shape[0]
    tm = 512

    def body(dhu_ref, w_ref, x1_ref, g3_ref, dx2_ref, mix_ref, g2_ref, dx1_ref, dmix_ref, dg3_ref, dg2_ref):
        i = pl.program_id(0)

        @pl.when(i == 0)
        def _():
            dg3_ref[...] = jnp.zeros_like(dg3_ref)
            dg2_ref[...] = jnp.zeros_like(dg2_ref)

        dh2 = _dot_nt(dhu_ref[0, 0], w_ref[0])
        for j in range(1, NDEV):
            dh2 = dh2 + _dot_nt(dhu_ref[j // NFF, j % NFF], w_ref[j])
        dn, dg3 = _rms_bwd(x1_ref[...], g3_ref[...], dh2)
        dx1 = dx2_ref[...] + dn
        dmix, dg2 = _rms_bwd(mix_ref[...], g2_ref[...], dx1)
        dx1_ref[...] = dx1
        dmix_ref[...] = dmix.astype(BF16)
        dg3_ref[...] += dg3
        dg2_ref[...] += dg2

    return _call(body, name, (s // tm,),
                 [pl.BlockSpec((2, NFF, tm, FS), lambda i: (0, 0, i, 0)), _resident((NDEV, D, FS)),
                  pl.BlockSpec((tm, D), _row), pl.BlockSpec((1, D), _fixed2), pl.BlockSpec((tm, D), _row),
                  pl.BlockSpec((tm, D), _row), pl.BlockSpec((1, D), _fixed2)],
                 [pl.BlockSpec((tm, D), _row), pl.BlockSpec((tm, D), _row), pl.BlockSpec((1, D), _fixed2),
                  pl.BlockSpec((1, D), _fixed2)],
                 [jax.ShapeDtypeStruct((s, D), F32), jax.ShapeDtypeStruct((s, D), BF16),
                  jax.ShapeDtypeStruct((1, D), F32), jax.ShapeDtypeStruct((1, D), F32)])(
                     dhu, wup, x1, g3, dx2, mix, g2)


def _dw_up(h2, dhu, name):
    s = h2.shape[0]
    tk = 512
    nk = s // tk

    def body(h_ref, d_ref, o_ref, acc):
        k = pl.program_id(1)

        @pl.when(k == 0)
        def _():
            acc[...] = jnp.zeros_like(acc)

        ht = h_ref[...].T
        for j in range(NFF):
            acc[j] += _dot(ht, d_ref[j])

        @pl.when(k == nk - 1)
        def _():
            o_ref[...] = acc[...].astype(BF16)

    return _call(body, name, (2, nk),
                 [pl.BlockSpec((tk, D), lambda hf, k: (k, 0)),
                  pl.BlockSpec((None, NFF, tk, FS), lambda hf, k: (hf, 0, k, 0))],
                 pl.BlockSpec((NFF, D, FS), lambda hf, k: (hf, 0, 0)),
                 jax.ShapeDtypeStruct((NDEV, D, FS), BF16), [pltpu.VMEM((NFF, D, FS), F32)])(h2, dhu)


def _mix_bwd(dmix, wo, proj, bgate, ya, yb, wao, wpo, z, att, mixed, name):
    s = dmix.shape[0]
    tm = 256
    nt = s // tm

    def body(dm_ref, w_ref, ga_ref, gb_ref, ba_ref, bb_ref, ya_ref, yb_ref, wao_ref, wpo_ref, z_ref, att_ref, mx_ref,
             dg_ref, dbg_ref, datt_ref, dmx_ref, dwo_ref, dwa_ref, dwb_ref, acco, acca, accb):
        i = pl.program_id(0)

        @pl.when(i == 0)
        def _():
            for ref in (dbg_ref, acco, acca, accb):
                ref[...] = jnp.zeros_like(ref)

        dm = dm_ref[...]
        dz = _dot_nt(dm, w_ref[...])
        ga = _sigmoid(ga_ref[...] + ba_ref[...])
        gb = _sigmoid(gb_ref[...] + bb_ref[...])
        dya = (dz * ga).astype(BF16)
        dyb = (dz * gb).astype(BF16)
        datt_ref[...] = _dot_nt(dya, wao_ref[...]).astype(BF16)
        dmx_ref[...] = _dot_nt(dyb, wpo_ref[...])
        acco[...] += _dot_tn(z_ref[...], dm)
        acca[...] += _dot_tn(att_ref[...], dya)
        accb[...] += _dot_tn(mx_ref[...], dyb)
        dga = dz * ya_ref[...].astype(F32) * ga * (1.0 - ga)
        dgb = dz * yb_ref[...].astype(F32) * gb * (1.0 - gb)
        dg_ref[:, 0:D] = dga.astype(BF16)
        dg_ref[:, D:2 * D] = dgb.astype(BF16)
        dbg_ref[:, 0:D] += jnp.sum(dga, axis=0, keepdims=True)
        dbg_ref[:, D:2 * D] += jnp.sum(dgb, axis=0, keepdims=True)

        @pl.when(i == nt - 1)
        def _():
            for j in range(NDEV):
                dwo_ref[j] = acco[j * 128:(j + 1) * 128, :].astype(BF16)
                dwa_ref[j] = acca[:, j * 128:(j + 1) * 128].astype(BF16)
                dwb_ref[j] = accb[:, j * 128:(j + 1) * 128].astype(BF16)

    rows = pl.BlockSpec((tm, D), _row)
    whole3 = lambda i: (0, 0, 0)
    return _call(body, name, (nt,),
                 [rows, _resident((D, D)),
                  pl.BlockSpec((tm, D), lambda i: (i, 2)), pl.BlockSpec((tm, D), lambda i: (i, 3)),
                  pl.BlockSpec((1, D), lambda i: (0, 0)), pl.BlockSpec((1, D), lambda i: (0, 1)),
                  rows, rows, _resident((AW, D)), _resident((PW, D)), rows, pl.BlockSpec((tm, AW), _row),
                  pl.BlockSpec((tm, PW), _row)],
                 [pl.BlockSpec((tm, 2 * D), _row), pl.BlockSpec((1, 2 * D), _fixed2),
                  pl.BlockSpec((tm, AW), _row), pl.BlockSpec((tm, PW), _row),
                  pl.BlockSpec((NDEV, 128, D), whole3), pl.BlockSpec((NDEV, AW, 128), whole3),
                  pl.BlockSpec((NDEV, PW, 128), whole3)],
                 [jax.ShapeDtypeStruct((s, 2 * D), BF16), jax.ShapeDtypeStruct((1, 2 * D), F32),
                  jax.ShapeDtypeStruct((s, AW), BF16), jax.ShapeDtypeStruct((s, PW), F32),
                  jax.ShapeDtypeStruct((NDEV, 128, D), BF16), jax.ShapeDtypeStruct((NDEV, AW, 128), BF16),
                  jax.ShapeDtypeStruct((NDEV, PW, 128), BF16)],
                 [pltpu.VMEM((D, D), F32), pltpu.VMEM((AW, D), F32), pltpu.VMEM((PW, D), F32)])(
                     dmix, wo, proj, proj, bgate, bgate, ya, yb, wao, wpo, z, att, mixed)


def _attn_bwd(proj, datt, att, lse, tab, name, exchange=None):
    s = proj.shape[0]
    nq = s // QB

    def body(q_ref, k0, k1, k2, v0, v1, v2, do_ref, o_ref, lse_ref, b_ref, dq_ref, dk_ref, dv_ref, db_ref,
             dka, dkb, dva, dvb):
        i = pl.program_id(1)

        @pl.when(i == 0)
        def _():
            for r in (dka, dkb, dva, dvb):
                r[...] = jnp.zeros_like(r)
            db_ref[...] = jnp.zeros_like(db_ref)

        @pl.when(i < nq)
        def _():
            qs, scs, k, lane = _attn_scores(q_ref, (k0, k1, k2), b_ref, i)
            v = jnp.concatenate([v0[...], v1[...], v2[...]], axis=0).astype(BF16)
            do = do_ref[...]
            doo = do.astype(F32) * o_ref[...].astype(F32)
            lse = lse_ref[...]
            dq = jnp.zeros((QB, 128), F32)
            dkt = jnp.zeros((128, KW), F32)
            dvt = jnp.zeros((128, KW), F32)
            for hh in (0, 1):
                mine = (lane < 64) if hh == 0 else (lane >= 64)
                doh = jnp.where(mine, do, jnp.zeros_like(do))
                kmask = lax.broadcasted_iota(jnp.int32, (KW, 128), 1)
                kh = jnp.where((kmask < 64) if hh == 0 else (kmask >= 64), k, jnp.zeros_like(k))
                p = jnp.exp(scs[hh] - lse[:, 64 * hh:64 * hh + 1])
                drow = jnp.sum(jnp.where(mine, doo, 0.0), axis=1, keepdims=True)
                ds = p * (_dot_nt(doh, v) - drow)
                db_ref[hh] += ds
                dsb = ds.astype(BF16)
                dq = dq + _dot(dsb, kh)
                dkt = dkt + _dot_tn(qs[hh], dsb)
                dvt = dvt + _dot_tn(doh, p.astype(BF16))
            dkw, dvw = dkt.T, dvt.T
            dq_ref[...] = (dq * SCALE).astype(BF16)
            dk_ref[...] = (dka[...] + dkw[0:QB]).astype(BF16)
            dka[...] = dkb[...] + dkw[QB:2 * QB]
            dkb[...] = dkw[2 * QB:3 * QB]
            dv_ref[...] = (dva[...] + dvw[0:QB]).astype(BF16)
            dva[...] = dvb[...] + dvw[QB:2 * QB]
            dvb[...] = dvw[2 * QB:3 * QB]

        @pl.when(i >= nq)
        def _():
            dk_ref[...] = dka[...].astype(BF16)
            dka[...] = dkb[...]
            dkb[...] = jnp.zeros_like(dkb)
            dv_ref[...] = dva[...].astype(BF16)
            dva[...] = dvb[...]
            dvb[...] = jnp.zeros_like(dvb)

    def kv(col, d):
        return pl.BlockSpec((QB, 128), lambda p, i: (jnp.clip(i - 2 + d, 0, nq - 1), col + p))

    cur = lambda p, i: (jnp.minimum(i, nq - 1), p)
    done = lambda p, i: (jnp.maximum(i - 2, 0), p)
    in_specs = [pl.BlockSpec((QB, 128), cur)]
    in_specs += [kv(4, d) for d in range(3)] + [kv(8, d) for d in range(3)]
    in_specs += [pl.BlockSpec((QB, 128), cur)] * 3 + [pl.BlockSpec((2, QB, KW), lambda p, i: (p, 0, 0))]
    o = jax.ShapeDtypeStruct((s, AW), BF16)
    return _call(body, name, (4, nq + 2), in_specs,
                 [pl.BlockSpec((QB, 128), cur), pl.BlockSpec((QB, 128), done), pl.BlockSpec((QB, 128), done),
                  pl.BlockSpec((2, QB, KW), lambda p, i: (p, 0, 0))],
                 [o, o, o, jax.ShapeDtypeStruct((HEADS, QB, KW), F32)],
                 [pltpu.VMEM((QB, 128), F32)] * 4, exchange=exchange)(
                     proj, proj, proj, proj, proj, proj, proj, datt, att, lse, tab)


def _rel_bias_grad(dtab, name):
    wdt = 640

    def body(x_ref, o_ref):
        x = x_ref[...]
        xc = x[0:64, 0:wdt]
        for qc in range(1, QB // 64):
            xc = xc + pltpu.roll(x[qc * 64:(qc + 1) * 64, :], KW - qc * 64, 1)[:, 0:wdt]
        r = lax.broadcasted_iota(jnp.int32, (64, 64), 0)
        c = lax.broadcasted_iota(jnp.int32, (64, 64), 1)
        flip = (r + c == 63).astype(F32)
        y = jnp.dot(flip, xc, preferred_element_type=F32, precision=lax.Precision.HIGHEST)
        z = pltpu.roll(y, 0, 1, stride=1, stride_axis=0)
        t = jnp.broadcast_to(jnp.sum(z, axis=0, keepdims=True), (8, wdt))
        e = lax.broadcasted_iota(jnp.int32, (wdt, wdt), 0)
        rr = lax.broadcasted_iota(jnp.int32, (wdt, wdt), 1)
        onehot = (jnp.clip(BAND - 1 - e, -MAXREL, MAXREL) + MAXREL == rr).astype(F32)
        o_ref[...] = jnp.dot(t, onehot, preferred_element_type=F32, precision=lax.Precision.HIGHEST)

    return _call(body, name, (HEADS,), [pl.BlockSpec((None, QB, KW), lambda h: (h, 0, 0))],
                 pl.BlockSpec((None, 8, wdt), lambda h: (h, 0, 0)),
                 jax.ShapeDtypeStruct((HEADS, 8, wdt), F32))(dtab)


def _pool_bwd(dmixed, pooled, wg, scale, name):
    s = dmixed.shape[0]
    tb = 512
    nt = s // tb
    e = tb + 16

    def body(dm_ref, dmn_ref, pl_ref, wg_ref, sc_ref, du_ref, dwg_ref, dsc_ref):
        i = pl.program_id(0)
        dm = jnp.concatenate([dm_ref[...], jnp.where(i == nt - 1, 0.0, dmn_ref[...])], axis=0)
        t = i * tb + lax.broadcasted_iota(jnp.int32, (e, 1), 0)

        @pl.when(i == 0)
        def _():
            dwg_ref[...] = jnp.zeros_like(dwg_ref)
            dsc_ref[...] = jnp.zeros_like(dsc_ref)

        for g, w in enumerate(POOL_WINDOWS):
            sl = slice(g * PG, (g + 1) * PG)
            wgb = wg_ref[g].astype(BF16)
            pb = pl_ref[:, sl]
            dsc_ref[:, sl] += jnp.sum(dm[0:tb, sl] * _dot(pb, wgb), axis=0, keepdims=True)
            dpre = (dm[:, sl] * sc_ref[:, sl]).astype(BF16)
            dwg_ref[g] += _dot_tn(pb, dpre[0:tb])
            dpool = _dot_nt(dpre, wgb)
            a = dpool / jnp.minimum(t + 1, w).astype(F32)
            sh = 1
            while sh < w:
                a = a + pltpu.roll(a, e - sh, 0)
                sh *= 2
            du_ref[:, sl] = (a[0:tb] - dpool[0:tb]).astype(BF16)

    nb16 = s // 16
    return _call(body, name, (nt,),
                 [pl.BlockSpec((tb, PW), _row),
                  pl.BlockSpec((16, PW), lambda i: (jnp.minimum((i + 1) * (tb // 16), nb16 - 1), 0)),
                  pl.BlockSpec((tb, PW), _row), pl.BlockSpec((4, PG, PG), lambda i: (0, 0, 0)),
                  pl.BlockSpec((1, PW), _fixed2)],
                 [pl.BlockSpec((tb, PW), _row), pl.BlockSpec((4, PG, PG), lambda i: (0, 0, 0)),
                  pl.BlockSpec((1, PW), _fixed2)],
                 [jax.ShapeDtypeStruct((s, PW), BF16), jax.ShapeDtypeStruct((4, PG, PG), F32),
                  jax.ShapeDtypeStruct((1, PW), F32)])(dmixed, dmixed, pooled, wg, scale)


def _dproj_specs(t):
    return [pl.BlockSpec((t, 512), _row)] * 4 + [pl.BlockSpec((t, 2 * D), _row)]


def _dproj_segment(refs, n):
    return refs[n][...] if n < 4 else refs[4][:, (n - 4) * 512:(n - 3) * 512]


def _mm_dh(segs, win, x, g1, dx1, below, name, exchange=None):
    s = x.shape[0]
    tm = 512

    def body(dq_ref, dk_ref, dv_ref, du_ref, dg_ref, w_ref, x_ref, g1_ref, dx1_ref, *rest):
        i = pl.program_id(0)
        outs = rest[2:] if below else rest

        @pl.when(i == 0)
        def _():
            for ref in outs[1::2]:
                ref[...] = jnp.zeros_like(ref)

        pieces = (dq_ref, dk_ref, dv_ref, du_ref, dg_ref)
        dh = _dot_nt(_dproj_segment(pieces, 0), w_ref[0])
        for n in range(1, NDEV):
            dh = dh + _dot_nt(_dproj_segment(pieces, n), w_ref[n])
        dn, dg1 = _rms_bwd(x_ref[...], g1_ref[...], dh)
        dx = dx1_ref[...] + dn
        outs[0][...] = dx
        outs[1][...] += dg1
        if below:
            df, dg4 = _rms_bwd(rest[0][...], rest[1][...], dx)
            outs[2][...] = df.astype(BF16)
            outs[3][...] += dg4

    rows, gain = pl.BlockSpec((tm, D), _row), pl.BlockSpec((1, D), _fixed2)
    vec = jax.ShapeDtypeStruct((1, D), F32)
    return _call(body, name, (s // tm,),
                 _dproj_specs(tm) + [_resident((NDEV, D, 512)), rows, gain, rows] + ([rows, gain] if below else []),
                 [rows, gain] + ([rows, gain] if below else []),
                 [jax.ShapeDtypeStruct((s, D), F32), vec] + ([jax.ShapeDtypeStruct((s, D), BF16), vec] if below else []),
                 exchange=exchange)(*segs, win, x, g1, dx1, *below)


def _dw_in(h, segs, name):
    s = h.shape[0]
    tk = 512
    nk = s // tk

    def body(h_ref, dq_ref, dk_ref, dv_ref, du_ref, dg_ref, o_ref, acc):
        k = pl.program_id(0)

        @pl.when(k == 0)
        def _():
            acc[...] = jnp.zeros_like(acc)

        ht = h_ref[...].T
        pieces = (dq_ref, dk_ref, dv_ref, du_ref, dg_ref)
        for n in range(NDEV):
            acc[n] += _dot(ht, _dproj_segment(pieces, n))

        @pl.when(k == nk - 1)
        def _():
            o_ref[...] = acc[...].astype(BF16)

    return _call(body, name, (nk,), [pl.BlockSpec((tk, D), _row)] + _dproj_specs(tk),
                 pl.BlockSpec((NDEV, D, 512), lambda k: (0, 0, 0)),
                 jax.ShapeDtypeStruct((NDEV, D, 512), BF16), [pltpu.VMEM((NDEV, D, 512), F32)])(h, *segs)


def _slot(px, py, pc):
    return 4 * px + 2 * py + pc


def _gather_exchange(shards, deliver):
    n = len(shards)

    def phases(ins, outs, sems):
        send_sems, recv_sems, local_sems = sems
        x, y, c = lax.axis_index("x"), lax.axis_index("y"), lax.axis_index("c")
        me, sibling = (x, y, c), (x, y, 1 - c)
        chips = [(1 - x, y), (x, 1 - y), (1 - x, 1 - y)]

        def copy(w, k, block, to, src=None):
            dst = outs[w].at[_slot(*block)]
            return pltpu.make_async_remote_copy(
                src_ref=dst if src is None else src, dst_ref=dst, send_sem=send_sems.at[7 * w + k],
                recv_sem=recv_sems.at[7 * w + k], device_id=to, device_id_type=MESH)

        def mine(w):
            return pltpu.make_async_copy(ins[w], outs[w].at[_slot(*me)], local_sems.at[w])

        def first(w):
            return [copy(w, 0, me, sibling, src=ins[w])] + [
                copy(w, 1 + j, me, (*chip, c), src=ins[w]) for j, chip in enumerate(chips)]

        def passed(w):
            return [copy(w, 4 + j, (*chip, c), sibling) for j, chip in enumerate(chips)]

        def send():
            for w in range(n):
                mine(w).start()
                for cp in first(w):
                    cp.start()

        def forward():
            for w in range(n):
                for j, chip in enumerate(chips):
                    copy(w, 1 + j, (*chip, c), me).wait_recv()
                    passed(w)[j].start()

        def finish():
            for w in range(n):
                copy(w, 0, sibling, me).wait_recv()
                for j, chip in enumerate(chips):
                    copy(w, 4 + j, (*chip, 1 - c), me).wait_recv()
            for w in range(n):
                for cp in first(w) + passed(w):
                    cp.wait_send()
                mine(w).wait()

        return [send, forward, finish]

    return _Exchange(shards, [jax.ShapeDtypeStruct((NDEV,) + a.shape, a.dtype) for a in shards],
                     [pltpu.SemaphoreType.DMA((7 * n,)), pltpu.SemaphoreType.DMA((7 * n,)),
                      pltpu.SemaphoreType.DMA((n,))], phases, deliver)


def _scatter_exchange(parts, deliver):
    n = len(parts)

    def phases(ins, outs, sems):
        send_sems, recv_sems, local_sems = sems
        x, y, c = lax.axis_index("x"), lax.axis_index("y"), lax.axis_index("c")
        me = _slot(x, y, c)
        peers = [((1 - x) if r & 4 else x, (1 - y) if r & 2 else y, (1 - c) if r & 1 else c) for r in range(1, NDEV)]

        def mine(w):
            return pltpu.make_async_copy(ins[w].at[me], outs[w].at[me], local_sems.at[w])

        def copy(w, r, block_here, block_there):
            return pltpu.make_async_remote_copy(
                src_ref=ins[w].at[block_here], dst_ref=outs[w].at[block_there], send_sem=send_sems.at[7 * w + r],
                recv_sem=recv_sems.at[7 * w + r], device_id=peers[r], device_id_type=MESH)

        def send():
            for w in range(n):
                mine(w).start()
                for r, peer in enumerate(peers):
                    copy(w, r, _slot(*peer), me).start()

        def finish():
            for w in range(n):
                for r, peer in enumerate(peers):
                    copy(w, r, me, _slot(*peer)).wait_recv()
            for w in range(n):
                for r, peer in enumerate(peers):
                    copy(w, r, _slot(*peer), me).wait_send()
                mine(w).wait()

        return [send, finish]

    return _Exchange(parts, [jax.ShapeDtypeStruct(a.shape, a.dtype) for a in parts],
                     [pltpu.SemaphoreType.DMA((7 * n,)), pltpu.SemaphoreType.DMA((7 * n,)),
                      pltpu.SemaphoreType.DMA((n,))], phases, deliver)


def _adamw(w, g, m, v):
    m = ADAM_B1 * m + (1.0 - ADAM_B1) * g
    v = ADAM_B2 * v + (1.0 - ADAM_B2) * (g * g)
    m_hat = m / (1.0 - ADAM_B1 ** ADAM_STEP)
    v_hat = v / (1.0 - ADAM_B2 ** ADAM_STEP)
    delta = -ADAM_LR * (m_hat / (jnp.sqrt(v_hat) + ADAM_EPS) + ADAM_WD * w)
    return delta, m, v


def _adamw_sharded(recvs, w, m, v, tr, name, exchange=None):
    _, r, cdim = w.shape
    nr = r // tr

    def body(r0_ref, r1_ref, w_ref, m_ref, v_ref, g_ref, d_ref, nm_ref, nv_ref):
        l = pl.program_id(0)

        def total(ref):
            acc = ref[0].astype(F32)
            for k in range(1, NDEV):
                acc = acc + ref[k].astype(F32)
            return acc

        g = jnp.where(l == 0, total(r0_ref), total(r1_ref))
        d, nm, nv = _adamw(w_ref[...], g, m_ref[...], v_ref[...])
        g_ref[...] = g
        d_ref[...] = d
        nm_ref[...] = nm
        nv_ref[...] = nv

    mine = pl.BlockSpec((None, tr, cdim), lambda l, i: (l, i, 0))
    out = jax.ShapeDtypeStruct(w.shape, F32)
    return _call(body, name, (DEPTH, nr),
                 [pl.BlockSpec((NDEV, tr, cdim), lambda l, i: (0, jnp.where(l == 0, i, nr - 1), 0)),
                  pl.BlockSpec((NDEV, tr, cdim), lambda l, i: (0, jnp.where(l == 1, i, 0), 0)),
                  mine, mine, mine],
                 [mine] * 4, [out] * 4, exchange=exchange)(recvs[0], recvs[1], w, m, v)


def _adamw_small(parts, w, m, v, name):
    r = w.shape[0]

    def body(p_ref, w_ref, m_ref, v_ref, g_ref, d_ref, nm_ref, nv_ref):
        g = p_ref[0]
        for k in range(1, NDEV):
            g = g + p_ref[k]
        d, nm, nv = _adamw(w_ref[...], g, m_ref[...], v_ref[...])
        g_ref[...] = g
        d_ref[...] = d
        nm_ref[...] = nm
        nv_ref[...] = nv

    whole = pl.BlockSpec((r, 1024), _fixed2)
    out = jax.ShapeDtypeStruct((r, 1024), F32)
    return _call(body, name, (1,), [pl.BlockSpec((NDEV, r, 1024), lambda i: (0, 0, 0)), whole, whole, whole],
                 [whole] * 4, [out] * 4)(parts, w, m, v)


SMALL = (("norm_mix_pre", (D,)), ("b_gate", (2 * D,)), ("rel_bias", (HEADS, NREL)), ("w_pool_group", (4, PG, PG)),
         ("pool_scale", (PW,)), ("norm_mix_post", (D,)), ("norm_ffn_pre", (D,)), ("conv_b", (NFF * 2 * FS,)),
         ("norm_ffn_post", (D,)))
SHARDED = (("w_in", 256), ("w_attn_out", 512), ("w_pool_out", 512), ("w_o", 128), ("w_up", 256), ("conv_w", 3),
           ("w_down", 176))
SMALL_ROWS = 168


def _pack_small(tree):
    flat = jnp.concatenate([tree[name].reshape(-1) for name, _ in SMALL])
    return jnp.pad(flat, (0, SMALL_ROWS * 1024 - flat.shape[0])).reshape(SMALL_ROWS, 1024)


def _unpack_small(packed):
    flat = packed.reshape(-1)
    out, at = {}, 0
    for name, shape in SMALL:
        size = DEPTH * int(np.prod(shape))
        out[name] = flat[at:at + size].reshape((DEPTH,) + shape)
        at += size
    return out


def _layer_fwd(l, x, h, p, gath, target, gnext, carry):
    n = f"l{l}"
    g = lambda name: p[name][l].reshape(1, -1)
    res = {"x": x, "h": h}
    res["tab"] = _bias_table(p["rel_bias"][l], f"bias_table_{n}")
    proj = _mm_in(h, gath["w_in"], f"mm_in_{n}", carry.get("mm_in"))
    att, res["lse"] = _attn_fwd(proj, res["tab"], f"attn_fwd_{n}", carry.get("attn_fwd"))
    pooled, mixed = _pool_fwd(proj, p["w_pool_group"][l], g("pool_scale"), f"pool_fwd_{n}")
    bgate = g("b_gate")
    z, ya, yb, mix, x1, h2 = _mix_fwd(att, mixed, gath["w_attn_out"], gath["w_pool_out"], proj, bgate, gath["w_o"],
                                      x, g("norm_mix_post"), g("norm_ffn_pre"), f"mix_fwd_{n}")
    cw = gath["conv_w"].reshape(2, NFF, 3, FS)
    cb = p["conv_b"][l].reshape(2, NFF, 1, FS)
    hu, a = _mm_up_conv_gate(h2, gath["w_up"], cw, cb, f"mm_up_conv_gate_{n}", carry.get("mm_up_conv_gate"))
    hu = hu.reshape(2, NFF, -1, FS)
    res.update(proj=proj, att=att, pooled=pooled, mixed=mixed, z=z, ya=ya, yb=yb, mix=mix, x1=x1, h2=h2, hu=hu,
               a=a, cw=cw, cb=cb, bgate=bgate)
    if target is None:
        f, x2, hn = _mm_down_fwd(a, gath["w_down"], x1, g("norm_ffn_post"), gnext, f"mm_down_fwd_{n}",
                                 carry.get("mm_down_fwd"))
        res["f"] = f
        return res, x2, hn
    dy, df, dg4, loss = _mm_down_loss(a, gath["w_down"], x1, g("norm_ffn_post"), target, f"mm_down_loss_{n}")
    return res, (dy, df, dg4), loss


def _layer_bwd(l, top, p, gath, res, carry, below):
    n = f"l{l}"
    g = lambda name: p[name][l].reshape(1, -1)
    big, small = {}, {}
    taken = lambda call: carry[call](big) if call in carry else None
    dx2, df, small["norm_ffn_post"] = top
    dhu, dcw, dcb, big["w_down"] = _conv_gate_bwd(res["hu"], df, gath["w_down"], res["cw"], res["cb"], res["a"],
                                                  f"conv_gate_bwd_{n}", taken("conv_gate_bwd"))
    big["conv_w"] = dcw.reshape(NDEV, 3, FS)
    small["conv_b"] = dcb
    dx1, dmix, small["norm_ffn_pre"], small["norm_mix_post"] = _mm_dh2(
        dhu, gath["w_up"], res["x1"], g("norm_ffn_pre"), dx2, res["mix"], g("norm_mix_post"), f"mm_dh2_{n}")
    big["w_up"] = _dw_up(res["h2"], dhu, f"dw_up_{n}")
    dgates, small["b_gate"], datt, dmixed, big["w_o"], big["w_attn_out"], big["w_pool_out"] = _mix_bwd(
        dmix, gath["w_o"], res["proj"], res["bgate"], res["ya"], res["yb"], gath["w_attn_out"], gath["w_pool_out"],
        res["z"], res["att"], res["mixed"], f"mix_bwd_{n}")
    dq, dk, dv, dtab = _attn_bwd(res["proj"], datt, res["att"], res["lse"], res["tab"], f"attn_bwd_{n}",
                                 taken("attn_bwd"))
    small["rel_bias"] = _rel_bias_grad(dtab, f"rel_bias_grad_{n}")[:, 0, :NREL]
    du, small["w_pool_group"], small["pool_scale"] = _pool_bwd(
        dmixed, res["pooled"], p["w_pool_group"][l], g("pool_scale"), f"pool_bwd_{n}")
    segs = (dq, dk, dv, du, dgates)
    big["w_in"] = _dw_in(res["h"], segs, f"dw_in_{n}")
    dx, small["norm_mix_pre"], *lower = _mm_dh(segs, gath["w_in"], res["x"], g("norm_mix_pre"), dx1, below,
                                               f"mm_dh_{n}", taken("mm_dh"))
    return (dx, *lower), big, small


def _gather_weights(gath, l, p, names):
    shards = [p[k][l] if k == "conv_w" else p[k][l].astype(BF16) for k in names]

    def deliver(results):
        for k, a in zip(names, results):
            if k in ("w_attn_out", "w_pool_out"):
                a = jnp.transpose(a, (1, 0, 2)).reshape(AW, D)
            elif k == "w_o":
                a = a.reshape(D, D)
            elif k == "w_down":
                a = a.reshape(NFF * FS, D)
            gath[k] = a

    return _gather_exchange(shards, deliver)


def _scatter_grads(recv, big, names):
    return _scatter_exchange([big[k] for k in names], lambda results: recv.update(zip(names, results)))


def kernel(x, norm_mix_pre, w_in, b_gate, rel_bias, w_attn_out, w_pool_group, pool_scale, w_pool_out, w_o, norm_mix_post, norm_ffn_pre, w_up, conv_w, conv_b, w_down, norm_ffn_post, loss_target, m_norm_mix_pre, m_w_in, m_b_gate, m_rel_bias, m_w_attn_out, m_w_pool_group, m_pool_scale, m_w_pool_out, m_w_o, m_norm_mix_post, m_norm_ffn_pre, m_w_up, m_conv_w, m_conv_b, m_w_down, m_norm_ffn_post, v_norm_mix_pre, v_w_in, v_b_gate, v_rel_bias, v_w_attn_out, v_w_pool_group, v_pool_scale, v_w_pool_out, v_w_o, v_norm_mix_post, v_norm_ffn_pre, v_w_up, v_conv_w, v_conv_b, v_w_down, v_norm_ffn_post):
    names = ("norm_mix_pre", "w_in", "b_gate", "rel_bias", "w_attn_out", "w_pool_group", "pool_scale", "w_pool_out",
             "w_o", "norm_mix_post", "norm_ffn_pre", "w_up", "conv_w", "conv_b", "w_down", "norm_ffn_post")
    p = dict(zip(names, (norm_mix_pre, w_in, b_gate, rel_bias, w_attn_out, w_pool_group, pool_scale, w_pool_out, w_o,
                         norm_mix_post, norm_ffn_pre, w_up, conv_w, conv_b, w_down, norm_ffn_post)))
    mom = dict(zip(names, (m_norm_mix_pre, m_w_in, m_b_gate, m_rel_bias, m_w_attn_out, m_w_pool_group, m_pool_scale,
                           m_w_pool_out, m_w_o, m_norm_mix_post, m_norm_ffn_pre, m_w_up, m_conv_w, m_conv_b, m_w_down,
                           m_norm_ffn_post)))
    var = dict(zip(names, (v_norm_mix_pre, v_w_in, v_b_gate, v_rel_bias, v_w_attn_out, v_w_pool_group, v_pool_scale,
                           v_w_pool_out, v_w_o, v_norm_mix_post, v_norm_ffn_pre, v_w_up, v_conv_w, v_conv_b, v_w_down,
                           v_norm_ffn_post)))
    s = x.shape[1]
    xs = x.reshape(s, D)
    target = loss_target.reshape(s, D)

    gath = [{}, {}]
    rest = ("w_attn_out", "w_pool_out", "w_o", "w_down", "conv_w")
    h0 = _rmsnorm_call(xs, p["norm_mix_pre"][0].reshape(1, D), "rmsnorm_l0",
                       _gather_weights(gath[0], 0, p, ("w_in",)))
    res0, x2, h1 = _layer_fwd(0, xs, h0, p, gath[0], None, p["norm_mix_pre"][1].reshape(1, D), {
        "mm_in": _gather_weights(gath[0], 0, p, rest),
        "attn_fwd": _gather_weights(gath[0], 0, p, ("w_up",)),
        "mm_up_conv_gate": _gather_weights(gath[1], 1, p, ("w_in",) + rest),
        "mm_down_fwd": _gather_weights(gath[1], 1, p, ("w_up",))})
    res1, top1, loss_part = _layer_fwd(1, x2, h1, p, gath[1], target, None, {})

    order = [k for k, _ in SHARDED]
    recv = [{}, {}]
    top0, big1, small1 = _layer_bwd(1, top1, p, gath[1], res1, {}, (res0["f"], p["norm_ffn_post"][0].reshape(1, D)))
    (grad_x,), big0, small0 = _layer_bwd(0, top0, p, gath[0], res0, {
        "conv_gate_bwd": lambda big: _scatter_grads(recv[1], big1, order),
        "attn_bwd": lambda big: _scatter_grads(recv[0], big, [k for k in order if k != "w_in"]),
        "mm_dh": lambda big: _scatter_grads(recv[0], big, ["w_in"])}, ())

    loss = lax.psum(loss_part[0, 0], ("x", "y", "c"))

    small = {k: jnp.stack([small0[k].reshape(shape), small1[k].reshape(shape)]) for k, shape in SMALL}
    parts = []
    carried = {"w_up": _gather_exchange([_pack_small(small)], parts.extend)}
    out = {}
    for k, tr in sorted(SHARDED, key=lambda kt: kt[0] not in carried):
        out[k] = _adamw_sharded((recv[0][k], recv[1][k]), p[k], mom[k], var[k], tr, f"adamw_{k}", carried.get(k))
    packed = _adamw_small(parts[0], _pack_small(p), _pack_small(mom), _pack_small(var), "adamw_small")
    unpacked = [_unpack_small(a) for a in packed]
    for k, _ in SMALL:
        out[k] = tuple(u[k] for u in unpacked)

    return (loss, grad_x.reshape(x.shape), *[out[k][0] for k in names], *[out[k][1] for k in names],
            *[out[k][2] for k in names], *[out[k][3] for k in names])
```

```python
import numpy as np
import jax
import jax.numpy as jnp
from jax import lax
from jax.experimental import pallas as pl
from jax.experimental.pallas import tpu as pltpu

F32, BF16 = jnp.float32, jnp.bfloat16

D = 1024
AW = 512
PW = 512
PG = 128
INW = 4096
FS = 704
NFF = 4
NDEV = 8
DEPTH = 2
HEADS = 8
NREL = 513
MAXREL = 256
POOL_WINDOWS = (2, 4, 8, 16)
EPS = 1e-6
SCALE = 0.125
NEG = -1e30
QB = 256
KW = 3 * QB
BAND = 576
ADAM_LR, ADAM_B1, ADAM_B2, ADAM_EPS, ADAM_WD, ADAM_STEP = 0.001, 0.9, 0.999, 1e-08, 0.01, 10
VMEM_LIMIT_V7X = 56 * 1024 * 1024
MESH = pl.DeviceIdType.MESH
GELU_C = 0.7978845608028654
GELU_A = 0.044715


HBM = pl.BlockSpec(memory_space=pltpu.HBM)


class _Exchange:
    def __init__(self, operands, out_shape, scratch, phases, deliver):
        self.operands, self.out_shape, self.scratch = list(operands), list(out_shape), list(scratch)
        self.phases, self.deliver = phases, deliver


def _call(body, name, grid, in_specs, out_specs, out_shape, scratch=(), exchange=None):
    params = pltpu.CompilerParams(vmem_limit_bytes=VMEM_LIMIT_V7X)
    if exchange is None:
        return pl.pallas_call(body, name=name, grid=grid, in_specs=in_specs, out_specs=out_specs, out_shape=out_shape,
                              scratch_shapes=list(scratch), compiler_params=params)
    single = not isinstance(out_shape, (list, tuple))
    outs, ospecs = ([out_shape], [out_specs]) if single else (list(out_shape), list(out_specs))
    n_in, n_out, n_scr = len(in_specs), len(outs), len(scratch)
    ne_in, ne_out = len(exchange.operands), len(exchange.out_shape)
    nsteps = int(np.prod(grid))

    def carried(*refs):
        cut = np.cumsum([0, n_in, ne_in, n_out, ne_out, n_scr])
        base_in, ex_in, base_out, ex_out, base_scr = (refs[cut[k]:cut[k + 1]] for k in range(5))
        step = pl.program_id(0)
        for axis in range(1, len(grid)):
            step = step * grid[axis] + pl.program_id(axis)
        phases = exchange.phases(ex_in, ex_out, refs[cut[5]:])
        pl.when(step == 0)(phases[0])
        body(*base_in, *base_out, *base_scr)
        at = {2: [nsteps - 1], 3: [(7 * nsteps) // 10, nsteps - 1]}[len(phases)]
        for phase, when in zip(phases[1:], at):
            pl.when(step == when)(phase)

    call = pl.pallas_call(
        carried, name=name, grid=grid, in_specs=list(in_specs) + [HBM] * ne_in, out_specs=ospecs + [HBM] * ne_out,
        out_shape=outs + exchange.out_shape, scratch_shapes=list(scratch) + exchange.scratch, compiler_params=params)

    def run(*args):
        res = call(*args, *exchange.operands)
        exchange.deliver(res[n_out:])
        return res[0] if single else res[:n_out]

    return run


def _dot(a, b):
    return jnp.dot(a, b, preferred_element_type=F32)


def _dot_nt(a, b):
    return lax.dot_general(a, b, (((1,), (1,)), ((), ())), preferred_element_type=F32)


def _dot_tn(a, b):
    return lax.dot_general(a, b, (((0,), (0,)), ((), ())), preferred_element_type=F32)


def _rms(x, g):
    r = lax.rsqrt(jnp.mean(x * x, axis=-1, keepdims=True) + EPS)
    return x * r * g


def _rms_bwd(x, g, dy):
    r = lax.rsqrt(jnp.mean(x * x, axis=-1, keepdims=True) + EPS)
    xh = x * r
    dg = jnp.sum(dy * xh, axis=0, keepdims=True)
    dxh = dy * g
    dx = r * (dxh - xh * jnp.mean(dxh * xh, axis=-1, keepdims=True))
    return dx, dg


def _sigmoid(x):
    return 1.0 / (1.0 + jnp.exp(-x))


def _gelu_parts(x):
    x2 = x * x
    th = jnp.tanh(x * (GELU_C + (GELU_C * GELU_A) * x2))
    s = 0.5 * th + 0.5
    dg = s * (1.0 + x * (1.0 - s) * (2.0 * GELU_C + (6.0 * GELU_C * GELU_A) * x2))
    return x * s, dg


def _row(i):
    return (i, 0)


def _fixed2(*_):
    return (0, 0)


def _rmsnorm_call(x, g, name, exchange=None):
    s = x.shape[0]
    tm = 512

    def body(x_ref, g_ref, o_ref):
        o_ref[...] = _rms(x_ref[...], g_ref[...]).astype(BF16)

    return _call(body, name, (s // tm,),
                 [pl.BlockSpec((tm, D), _row), pl.BlockSpec((1, D), _fixed2)],
                 pl.BlockSpec((tm, D), _row), jax.ShapeDtypeStruct((s, D), BF16), exchange=exchange)(x, g)


def _resident(shape):
    return pl.BlockSpec(shape, lambda *_: (0,) * len(shape), pipeline_mode=pl.Buffered(1))


def _mm_in(h, win, name, exchange=None):
    s = h.shape[0]
    tm = 512

    def body(h_ref, w_ref, o_ref):
        hv = h_ref[...]
        for j in range(NDEV):
            o_ref[:, j * 512:(j + 1) * 512] = _dot(hv, w_ref[j])

    return _call(body, name, (s // tm,),
                 [pl.BlockSpec((tm, D), _row), _resident((NDEV, D, 512))],
                 pl.BlockSpec((tm, INW), _row), jax.ShapeDtypeStruct((s, INW), F32), exchange=exchange)(h, win)


def _bias_table(rel_bias, name):
    wdt = 1024
    rel = jnp.pad(rel_bias, ((0, 0), (0, 640 - NREL))).reshape(HEADS, 1, 640)

    def body(r_ref, o_ref):
        rr = lax.broadcasted_iota(jnp.int32, (640, wdt), 0)
        m = lax.broadcasted_iota(jnp.int32, (640, wdt), 1)
        d = jnp.where(m < KW, m, m - wdt)
        onehot = (jnp.clip(512 - d, -MAXREL, MAXREL) + MAXREL == rr).astype(F32)
        row = jnp.dot(jnp.broadcast_to(r_ref[...], (8, 640)), onehot, preferred_element_type=F32,
                      precision=lax.Precision.HIGHEST)[0:1]
        t = pltpu.roll(jnp.broadcast_to(row, (QB, wdt)), 0, 1, stride=1, stride_axis=0)[:, 0:KW]
        qc = lax.broadcasted_iota(jnp.int32, (QB, KW), 0) // 64
        kc = lax.broadcasted_iota(jnp.int32, (QB, KW), 1) // 64
        o_ref[...] = jnp.where((kc >= qc) & (kc <= qc + 8), t, NEG)

    return _call(body, name, (HEADS,), [pl.BlockSpec((None, 1, 640), lambda h: (h, 0, 0))],
                 pl.BlockSpec((None, QB, KW), lambda h: (h, 0, 0)),
                 jax.ShapeDtypeStruct((HEADS, QB, KW), F32))(rel)


def _attn_scores(q_ref, k_refs, b_ref, i):
    lane = lax.broadcasted_iota(jnp.int32, (QB, 128), 1)
    q = q_ref[...] * SCALE
    qs = [jnp.where(lane < 64, q, 0.0).astype(BF16), jnp.where(lane >= 64, q, 0.0).astype(BF16)]
    k = jnp.concatenate([r[...] for r in k_refs], axis=0).astype(BF16)
    colb = lax.broadcasted_iota(jnp.int32, (1, KW), 1) // QB
    before = jnp.where(colb + i >= 2, 0.0, NEG)
    return qs, [_dot_nt(qs[hh], k) + b_ref[hh] + before for hh in (0, 1)], k, lane


def _attn_fwd(proj, tab, name, exchange=None):
    s = proj.shape[0]
    nq = s // QB

    def body(q_ref, k0, k1, k2, v0, v1, v2, b_ref, o_ref, lse_ref):
        _, scs, _, lane = _attn_scores(q_ref, (k0, k1, k2), b_ref, pl.program_id(1))
        v = jnp.concatenate([v0[...], v1[...], v2[...]], axis=0).astype(BF16)
        v1s = jnp.concatenate([v, jnp.ones((KW, 128), BF16)], axis=1)
        o, lse = [], []
        for sc in scs:
            m = jnp.max(sc, axis=1, keepdims=True)
            ov = _dot(jnp.exp(sc - m).astype(BF16), v1s)
            o.append(ov[:, 0:128] / ov[:, 128:256])
            lse.append(m + jnp.log(ov[:, 128:256]))
        o_ref[...] = jnp.where(lane < 64, o[0], o[1]).astype(BF16)
        lse_ref[...] = jnp.where(lane < 64, lse[0], lse[1])

    def kv(col, d):
        return pl.BlockSpec((QB, 128), lambda p, i: (jnp.maximum(i - 2 + d, 0), col + p))

    in_specs = [pl.BlockSpec((QB, 128), lambda p, i: (i, p))]
    in_specs += [kv(4, d) for d in range(3)] + [kv(8, d) for d in range(3)]
    in_specs += [pl.BlockSpec((2, QB, KW), lambda p, i: (p, 0, 0))]
    out = pl.BlockSpec((QB, 128), lambda p, i: (i, p))
    return _call(body, name, (4, nq), in_specs, [out, out],
                 [jax.ShapeDtypeStruct((s, AW), BF16), jax.ShapeDtypeStruct((s, AW), F32)], exchange=exchange)(
                     proj, proj, proj, proj, proj, proj, proj, tab)


def _pool_fwd(proj, wg, scale, name):
    s = proj.shape[0]
    tb = 512
    e = tb + 16

    def body(u_ref, halo_ref, wg_ref, sc_ref, pooled_ref, mixed_ref):
        i = pl.program_id(0)
        cur = u_ref[...]
        prev = jnp.where(i > 0, halo_ref[...], 0.0)
        xs = jnp.concatenate([prev, cur], axis=0)
        t = i * tb + lax.broadcasted_iota(jnp.int32, (tb, 1), 0)
        for g, w in enumerate(POOL_WINDOWS):
            sl = slice(g * PG, (g + 1) * PG)
            a = xs[:, sl]
            sh = 1
            while sh < w:
                a = a + pltpu.roll(a, sh, 0)
                sh *= 2
            cnt = jnp.minimum(t + 1, w).astype(F32)
            pooled = (a[16:] / cnt - cur[:, sl]).astype(BF16)
            pooled_ref[:, sl] = pooled
            mixed_ref[:, sl] = (_dot(pooled, wg_ref[g].astype(BF16)) * sc_ref[:, sl]).astype(BF16)

    assert e % 8 == 0
    return _call(body, name, (s // tb,),
                 [pl.BlockSpec((tb, PW), lambda i: (i, 3)),
                  pl.BlockSpec((16, PW), lambda i: (jnp.maximum(i * (tb // 16) - 1, 0), 3)),
                  pl.BlockSpec((4, PG, PG), lambda i: (0, 0, 0)), pl.BlockSpec((1, PW), _fixed2)],
                 [pl.BlockSpec((tb, PW), _row), pl.BlockSpec((tb, PW), _row)],
                 [jax.ShapeDtypeStruct((s, PW), BF16), jax.ShapeDtypeStruct((s, PW), BF16)])(proj, proj, wg, scale)


def _mix_fwd(att, mixed, wao, wpo, proj, bgate, wo, x, g2, g3, name):
    s = att.shape[0]
    tm = 512

    def body(att_ref, mx_ref, wao_ref, wpo_ref, ga_ref, gb_ref, ba_ref, bb_ref, wo_ref, x_ref, g2_ref, g3_ref,
             z_ref, ya_ref, yb_ref, mix_ref, x1_ref, h2_ref):
        ya = _dot(att_ref[...], wao_ref[...])
        yb = _dot(mx_ref[...], wpo_ref[...])
        ga = _sigmoid(ga_ref[...] + ba_ref[...])
        gb = _sigmoid(gb_ref[...] + bb_ref[...])
        z = (ga * ya + gb * yb).astype(BF16)
        mix = _dot(z, wo_ref[...])
        x1 = x_ref[...] + _rms(mix, g2_ref[...])
        ya_ref[...] = ya.astype(BF16)
        yb_ref[...] = yb.astype(BF16)
        z_ref[...] = z
        mix_ref[...] = mix
        x1_ref[...] = x1
        h2_ref[...] = _rms(x1, g3_ref[...]).astype(BF16)

    rows, gain = pl.BlockSpec((tm, D), _row), pl.BlockSpec((1, D), _fixed2)
    half, full = jax.ShapeDtypeStruct((s, D), BF16), jax.ShapeDtypeStruct((s, D), F32)
    return _call(body, name, (s // tm,),
                 [pl.BlockSpec((tm, AW), _row), pl.BlockSpec((tm, PW), _row),
                  _resident((AW, D)), _resident((PW, D)),
                  pl.BlockSpec((tm, D), lambda i: (i, 2)), pl.BlockSpec((tm, D), lambda i: (i, 3)),
                  pl.BlockSpec((1, D), lambda i: (0, 0)), pl.BlockSpec((1, D), lambda i: (0, 1)),
                  _resident((D, D)), rows, gain, gain],
                 [rows] * 6, [half, half, half, full, full, half])(
                     att, mixed, wao, wpo, proj, proj, bgate, bgate, wo, x, g2, g3)


def _mm_up_conv_gate(h2, wup, cw, cb, name, exchange=None):
    s = h2.shape[0]
    tm = 256

    def body(h_ref, w_ref, cw_ref, cb_ref, hu_ref, a_ref, halo):
        first = pl.program_id(0) == 0
        hv = h_ref[...]
        for j in range(NDEV):
            hu_ref[j] = _dot(hv, w_ref[j])
        for jj in range(NFF):
            for c0, w in LANE_COLUMNS:
                lanes = pl.ds(c0, w)
                cwb = [[jnp.broadcast_to(cw_ref[sd, jj, t:t + 1, lanes], (8, w)) for t in range(3)] for sd in (0, 1)]
                cbb = [jnp.broadcast_to(cb_ref[sd, jj, :, lanes], (8, w)) for sd in (0, 1)]
                row = lax.broadcasted_iota(jnp.int32, (8, w), 0)
                rolled = []
                for sd in (0, 1):
                    before = jnp.where(first, 0.0, halo[NFF * sd + jj, :, lanes])
                    rolled += [pltpu.roll(before, 1, 0), pltpu.roll(before, 2, 0)]
                for k in range(tm // 16):
                    outs = []
                    for r in (16 * k, 16 * k + 8):
                        hc, keep = [], []
                        for sd in (0, 1):
                            cur = hu_ref[NFF * sd + jj, pl.ds(r, 8), lanes]
                            r1, r2 = pltpu.roll(cur, 1, 0), pltpu.roll(cur, 2, 0)
                            hc.append(cbb[sd] + cwb[sd][2] * cur
                                      + cwb[sd][0] * jnp.where(row >= 2, r2, rolled[2 * sd + 1])
                                      + cwb[sd][1] * jnp.where(row >= 1, r1, rolled[2 * sd]))
                            keep += [r1, r2]
                        rolled = keep
                        outs.append(_gelu_parts(hc[1])[0] * hc[0])
                    a_ref[jj, pl.ds(16 * k, 16), lanes] = jnp.concatenate(outs, axis=0).astype(BF16)
        for j in range(NDEV):
            halo[j] = hu_ref[j, tm - 8:tm, :]

    small = lambda i: (0, 0, 0, 0)
    return _call(body, name, (s // tm,),
                 [pl.BlockSpec((tm, D), _row), _resident((NDEV, D, FS)), pl.BlockSpec((2, NFF, 3, FS), small),
                  pl.BlockSpec((2, NFF, 1, FS), small)],
                 [pl.BlockSpec((NDEV, tm, FS), lambda i: (0, i, 0)), pl.BlockSpec((NFF, tm, FS), lambda i: (0, i, 0))],
                 [jax.ShapeDtypeStruct((NDEV, s, FS), F32), jax.ShapeDtypeStruct((NFF, s, FS), BF16)],
                 [pltpu.VMEM((NDEV, 8, FS), F32)], exchange=exchange)(h2, wup, cw, cb)


LANE_COLUMNS = [(c0, min(128, FS - c0)) for c0 in range(0, FS, 128)]


def _mm_down_fwd(a, wd, x1, g4, gnext, name, exchange=None):
    s = a.shape[1]
    tm = 512

    def body(a_ref, w_ref, x1_ref, g4_ref, gn_ref, f_ref, x2_ref, hn_ref):
        f = _dot(a_ref[0], w_ref[0:FS, :])
        for j in range(1, NFF):
            f = f + _dot(a_ref[j], w_ref[j * FS:(j + 1) * FS, :])
        x2 = x1_ref[...] + _rms(f, g4_ref[...])
        f_ref[...] = f
        x2_ref[...] = x2
        hn_ref[...] = _rms(x2, gn_ref[...]).astype(BF16)

    return _call(body, name, (s // tm,),
                 [pl.BlockSpec((NFF, tm, FS), lambda i: (0, i, 0)), pl.BlockSpec((NFF * FS, D), _fixed2),
                  pl.BlockSpec((tm, D), _row), pl.BlockSpec((1, D), _fixed2), pl.BlockSpec((1, D), _fixed2)],
                 [pl.BlockSpec((tm, D), _row)] * 3,
                 [jax.ShapeDtypeStruct((s, D), F32), jax.ShapeDtypeStruct((s, D), F32),
                  jax.ShapeDtypeStruct((s, D), BF16)], exchange=exchange)(a, wd, x1, g4, gnext)


def _mm_down_loss(a, wd, x1, g4, target, name):
    s = a.shape[1]
    tm = 512

    def body(a_ref, w_ref, x1_ref, g4_ref, t_ref, dy_ref, df_ref, dg_ref, loss_ref):
        i = pl.program_id(0)
        f = _dot(a_ref[0], w_ref[0:FS, :])
        for j in range(1, NFF):
            f = f + _dot(a_ref[j], w_ref[j * FS:(j + 1) * FS, :])
        err = x1_ref[...] + _rms(f, g4_ref[...]) - t_ref[...]
        dy = err * (1.0 / D)
        df, dg = _rms_bwd(f, g4_ref[...], dy)
        dy_ref[...] = dy
        df_ref[...] = df.astype(BF16)
        part = 0.5 * jnp.sum(jnp.mean(err * err, axis=-1, keepdims=True), axis=0, keepdims=True)

        @pl.when(i == 0)
        def _():
            loss_ref[...] = jnp.zeros_like(loss_ref)
            dg_ref[...] = jnp.zeros_like(dg_ref)

        loss_ref[...] += jnp.broadcast_to(part, loss_ref.shape)
        dg_ref[...] += dg

    return _call(body, name, (s // tm,),
                 [pl.BlockSpec((NFF, tm, FS), lambda i: (0, i, 0)), pl.BlockSpec((NFF * FS, D), _fixed2),
                  pl.BlockSpec((tm, D), _row), pl.BlockSpec((1, D), _fixed2), pl.BlockSpec((tm, D), _row)],
                 [pl.BlockSpec((tm, D), _row), pl.BlockSpec((tm, D), _row), pl.BlockSpec((1, D), _fixed2),
                  pl.BlockSpec((8, 128), _fixed2)],
                 [jax.ShapeDtypeStruct((s, D), F32), jax.ShapeDtypeStruct((s, D), BF16),
                  jax.ShapeDtypeStruct((1, D), F32), jax.ShapeDtypeStruct((8, 128), F32)])(a, wd, x1, g4, target)


def _conv_gate_bwd(hu, df, wd, cw, cb, a, name, exchange=None):
    s = hu.shape[2]
    tb = 512
    nt = s // tb
    e = tb + 16

    def body(hu_ref, prev_ref, next_ref, df_ref, dfn_ref, wd_ref, cw_ref, cb_ref, a_ref, dhu_ref, dcw_ref, dcb_ref,
             dwd_ref, dabuf, dbuf, dwd_acc):
        i = pl.program_id(1)
        first, last = i == 0, i == nt - 1

        @pl.when(i == 0)
        def _():
            dcw_ref[...] = jnp.zeros_like(dcw_ref)
            dcb_ref[...] = jnp.zeros_like(dcb_ref)
            dwd_acc[...] = jnp.zeros_like(dwd_acc)

        dabuf[0:tb] = _dot_nt(df_ref[...], wd_ref[...])
        dabuf[tb:tb + 8] = jnp.where(last, 0.0, _dot_nt(dfn_ref[...], wd_ref[...])[0:8])
        dwd_acc[...] += _dot_tn(a_ref[...], df_ref[...])

        for c0, w in LANE_COLUMNS:
            lanes = pl.ds(c0, w)
            cwb = [[jnp.broadcast_to(cw_ref[sd, t:t + 1, lanes], (8, w)) for t in range(3)] for sd in (0, 1)]
            cbb = [jnp.broadcast_to(cb_ref[sd, :, lanes], (8, w)) for sd in (0, 1)]

            row = lax.broadcasted_iota(jnp.int32, (8, w), 0)

            def tile(k, carry, summed=True):
                sums, rolled = carry[:8], carry[8:]
                r = 8 + 8 * k
                xs, keep = [], []
                for sd in (0, 1):
                    if summed:
                        cur = hu_ref[sd, pl.ds(r - 8, 8), lanes]
                    else:
                        cur = jnp.where(last, 0.0, next_ref[sd, :, lanes])
                    r1, r2 = pltpu.roll(cur, 1, 0), pltpu.roll(cur, 2, 0)
                    xs.append([jnp.where(row >= 2, r2, rolled[2 * sd + 1]), jnp.where(row >= 1, r1, rolled[2 * sd]),
                               cur])
                    keep += [r1, r2]
                hc = [cbb[sd] + cwb[sd][2] * xs[sd][2] + cwb[sd][0] * xs[sd][0] + cwb[sd][1] * xs[sd][1]
                      for sd in (0, 1)]
                da = dabuf[pl.ds(r - 8, 8), lanes]
                gl, dgl = _gelu_parts(hc[1])
                dhc = (da * gl, da * hc[0] * dgl)
                for sd in (0, 1):
                    dbuf[sd, pl.ds(r, 8), lanes] = dhc[sd]
                if not summed:
                    return carry
                new = []
                for sd in (0, 1):
                    new += [sums[4 * sd + t] + dhc[sd] * xs[sd][t] for t in range(3)] + [sums[4 * sd + 3] + dhc[sd]]
                return tuple(new + keep)

            start = [jnp.zeros((8, w), F32) for _ in range(8)]
            for sd in (0, 1):
                halo = jnp.where(first, 0.0, prev_ref[sd, :, lanes])
                start += [pltpu.roll(halo, 1, 0), pltpu.roll(halo, 2, 0)]
            def tiles(k4, carry):
                for u in range(4):
                    carry = tile(4 * k4 + u, carry)
                return carry

            carry = tuple(start)
            for k4 in range(tb // 32):
                carry = tiles(k4, carry)
            tile(tb // 8, carry, summed=False)
            sums = carry[:8]

            def up(v):
                return pltpu.roll(v, 7, 0), pltpu.roll(v, 6, 0)

            def out_tile(k, carry):
                r = 8 + 16 * k
                new = []
                for sd in (0, 1):
                    va, va1, va2 = carry[3 * sd:3 * sd + 3]
                    vb, vc = dbuf[sd, pl.ds(r + 8, 8), lanes], dbuf[sd, pl.ds(r + 16, 8), lanes]
                    (vb1, vb2), (vc1, vc2) = up(vb), up(vc)
                    c0b, c1b, c2b = cwb[sd]
                    top = c2b * va + c1b * jnp.where(row <= 6, va1, vb1) + c0b * jnp.where(row <= 5, va2, vb2)
                    bot = c2b * vb + c1b * jnp.where(row <= 6, vb1, vc1) + c0b * jnp.where(row <= 5, vb2, vc2)
                    dhu_ref[sd, pl.ds(16 * k, 16), lanes] = jnp.concatenate(
                        [top, bot], axis=0).astype(BF16)
                    new += [vc, vc1, vc2]
                return tuple(new)

            begin = []
            for sd in (0, 1):
                va = dbuf[sd, 8:16, lanes]
                begin += [va, *up(va)]
            def out_tiles(k2, carry):
                return out_tile(2 * k2 + 1, out_tile(2 * k2, carry))

            carry = tuple(begin)
            for k2 in range(tb // 32):
                carry = out_tiles(k2, carry)
            for sd in (0, 1):
                for t in range(3):
                    dcw_ref[sd, t:t + 1, lanes] += jnp.sum(sums[4 * sd + t], axis=0, keepdims=True)
                dcb_ref[sd, :, lanes] += jnp.sum(sums[4 * sd + 3], axis=0, keepdims=True)

        @pl.when(last)
        def _():
            dwd_ref[0] = dwd_acc[0:FS // 2, :].astype(BF16)
            dwd_ref[1] = dwd_acc[FS // 2:FS, :].astype(BF16)

    nb8, nb16 = s // 8, s // 16
    return _call(body, name, (NFF, nt),
                 [pl.BlockSpec((2, None, tb, FS), lambda j, i: (0, j, i, 0)),
                  pl.BlockSpec((2, None, 8, FS), lambda j, i: (0, j, jnp.maximum(i * (tb // 8) - 1, 0), 0)),
                  pl.BlockSpec((2, None, 8, FS), lambda j, i: (0, j, jnp.minimum((i + 1) * (tb // 8), nb8 - 1), 0)),
                  pl.BlockSpec((tb, D), lambda j, i: (i, 0)),
                  pl.BlockSpec((16, D), lambda j, i: (jnp.minimum((i + 1) * (tb // 16), nb16 - 1), 0)),
                  pl.BlockSpec((FS, D), lambda j, i: (j, 0)),
                  pl.BlockSpec((2, None, 3, FS), lambda j, i: (0, j, 0, 0)),
                  pl.BlockSpec((2, None, 1, FS), lambda j, i: (0, j, 0, 0)),
                  pl.BlockSpec((None, tb, FS), lambda j, i: (j, i, 0))],
                 [pl.BlockSpec((2, None, tb, FS), lambda j, i: (0, j, i, 0)),
                  pl.BlockSpec((2, None, 3, FS), lambda j, i: (0, j, 0, 0)),
                  pl.BlockSpec((2, None, 1, FS), lambda j, i: (0, j, 0, 0)),
                  pl.BlockSpec((2, FS // 2, D), lambda j, i: (j, 0, 0))],
                 [jax.ShapeDtypeStruct((2, NFF, s, FS), BF16), jax.ShapeDtypeStruct((2, NFF, 3, FS), F32),
                  jax.ShapeDtypeStruct((2, NFF, 1, FS), F32), jax.ShapeDtypeStruct((NDEV, FS // 2, D), BF16)],
                 [pltpu.VMEM((tb + 8, FS), F32), pltpu.VMEM((2, e, FS), F32), pltpu.VMEM((FS, D), F32)],
                 exchange=exchange)(hu, hu, hu, df, df, wd, cw, cb, a)


def _mm_dh2(dhu, wup, x1, g3, dx2, mix, g2, name):
    s = x1.shape[0]
    tm = 512

    def body(dhu_ref, w_ref, x1_ref, g3_ref, dx2_ref, mix_ref, g2_ref, dx1_ref, dmix_ref, dg3_ref, dg2_ref):
        i = pl.program_id(0)

        @pl.when(i == 0)
        def _():
            dg3_ref[...] = jnp.zeros_like(dg3_ref)
            dg2_ref[...] = jnp.zeros_like(dg2_ref)

        dh2 = _dot_nt(dhu_ref[0, 0], w_ref[0])
        for j in range(1, NDEV):
            dh2 = dh2 + _dot_nt(dhu_ref[j // NFF, j % NFF], w_ref[j])
        dn, dg3 = _rms_bwd(x1_ref[...], g3_ref[...], dh2)
        dx1 = dx2_ref[...] + dn
        dmix, dg2 = _rms_bwd(mix_ref[...], g2_ref[...], dx1)
        dx1_ref[...] = dx1
        dmix_ref[...] = dmix.astype(BF16)
        dg3_ref[...] += dg3
        dg2_ref[...] += dg2

    return _call(body, name, (s // tm,),
                 [pl.BlockSpec((2, NFF, tm, FS), lambda i: (0, 0, i, 0)), _resident((NDEV, D, FS)),
                  pl.BlockSpec((tm, D), _row), pl.BlockSpec((1, D), _fixed2), pl.BlockSpec((tm, D), _row),
                  pl.BlockSpec((tm, D), _row), pl.BlockSpec((1, D), _fixed2)],
                 [pl.BlockSpec((tm, D), _row), pl.BlockSpec((tm, D), _row), pl.BlockSpec((1, D), _fixed2),
                  pl.BlockSpec((1, D), _fixed2)],
                 [jax.ShapeDtypeStruct((s, D), F32), jax.ShapeDtypeStruct((s, D), BF16),
                  jax.ShapeDtypeStruct((1, D), F32), jax.ShapeDtypeStruct((1, D), F32)])(
                     dhu, wup, x1, g3, dx2, mix, g2)


def _dw_up(h2, dhu, name):
    s = h2.shape[0]
    tk = 512
    nk = s // tk

    def body(h_ref, d_ref, o_ref, acc):
        k = pl.program_id(1)

        @pl.when(k == 0)
        def _():
            acc[...] = jnp.zeros_like(acc)

        ht = h_ref[...].T
        for j in range(NFF):
            acc[j] += _dot(ht, d_ref[j])

        @pl.when(k == nk - 1)
        def _():
            o_ref[...] = acc[...].astype(BF16)

    return _call(body, name, (2, nk),
                 [pl.BlockSpec((tk, D), lambda hf, k: (k, 0)),
                  pl.BlockSpec((None, NFF, tk, FS), lambda hf, k: (hf, 0, k, 0))],
                 pl.BlockSpec((NFF, D, FS), lambda hf, k: (hf, 0, 0)),
                 jax.ShapeDtypeStruct((NDEV, D, FS), BF16), [pltpu.VMEM((NFF, D, FS), F32)])(h2, dhu)


def _mix_bwd(dmix, wo, proj, bgate, ya, yb, wao, wpo, z, att, mixed, name):
    s = dmix.shape[0]
    tm = 256
    nt = s // tm

    def body(dm_ref, w_ref, ga_ref, gb_ref, ba_ref, bb_ref, ya_ref, yb_ref, wao_ref, wpo_ref, z_ref, att_ref, mx_ref,
             dg_ref, dbg_ref, datt_ref, dmx_ref, dwo_ref, dwa_ref, dwb_ref, acco, acca, accb):
        i = pl.program_id(0)

        @pl.when(i == 0)
        def _():
            for ref in (dbg_ref, acco, acca, accb):
                ref[...] = jnp.zeros_like(ref)

        dm = dm_ref[...]
        dz = _dot_nt(dm, w_ref[...])
        ga = _sigmoid(ga_ref[...] + ba_ref[...])
        gb = _sigmoid(gb_ref[...] + bb_ref[...])
        dya = (dz * ga).astype(BF16)
        dyb = (dz * gb).astype(BF16)
        datt_ref[...] = _dot_nt(dya, wao_ref[...]).astype(BF16)
        dmx_ref[...] = _dot_nt(dyb, wpo_ref[...])
        acco[...] += _dot_tn(z_ref[...], dm)
        acca[...] += _dot_tn(att_ref[...], dya)
        accb[...] += _dot_tn(mx_ref[...], dyb)
        dga = dz * ya_ref[...].astype(F32) * ga * (1.0 - ga)
        dgb = dz * yb_ref[...].astype(F32) * gb * (1.0 - gb)
        dg_ref[:, 0:D] = dga.astype(BF16)
        dg_ref[:, D:2 * D] = dgb.astype(BF16)
        dbg_ref[:, 0:D] += jnp.sum(dga, axis=0, keepdims=True)
        dbg_ref[:, D:2 * D] += jnp.sum(dgb, axis=0, keepdims=True)

        @pl.when(i == nt - 1)
        def _():
            for j in range(NDEV):
                dwo_ref[j] = acco[j * 128:(j + 1) * 128, :].astype(BF16)
                dwa_ref[j] = acca[:, j * 128:(j + 1) * 128].astype(BF16)
                dwb_ref[j] = accb[:, j * 128:(j + 1) * 128].astype(BF16)

    rows = pl.BlockSpec((tm, D), _row)
    whole3 = lambda i: (0, 0, 0)
    return _call(body, name, (nt,),
                 [rows, _resident((D, D)),
                  pl.BlockSpec((tm, D), lambda i: (i, 2)), pl.BlockSpec((tm, D), lambda i: (i, 3)),
                  pl.BlockSpec((1, D), lambda i: (0, 0)), pl.BlockSpec((1, D), lambda i: (0, 1)),
                  rows, rows, _resident((AW, D)), _resident((PW, D)), rows, pl.BlockSpec((tm, AW), _row),
                  pl.BlockSpec((tm, PW), _row)],
                 [pl.BlockSpec((tm, 2 * D), _row), pl.BlockSpec((1, 2 * D), _fixed2),
                  pl.BlockSpec((tm, AW), _row), pl.BlockSpec((tm, PW), _row),
                  pl.BlockSpec((NDEV, 128, D), whole3), pl.BlockSpec((NDEV, AW, 128), whole3),
                  pl.BlockSpec((NDEV, PW, 128), whole3)],
                 [jax.ShapeDtypeStruct((s, 2 * D), BF16), jax.ShapeDtypeStruct((1, 2 * D), F32),
                  jax.ShapeDtypeStruct((s, AW), BF16), jax.ShapeDtypeStruct((s, PW), F32),
                  jax.ShapeDtypeStruct((NDEV, 128, D), BF16), jax.ShapeDtypeStruct((NDEV, AW, 128), BF16),
                  jax.ShapeDtypeStruct((NDEV, PW, 128), BF16)],
                 [pltpu.VMEM((D, D), F32), pltpu.VMEM((AW, D), F32), pltpu.VMEM((PW, D), F32)])(
                     dmix, wo, proj, proj, bgate, bgate, ya, yb, wao, wpo, z, att, mixed)


def _attn_bwd(proj, datt, att, lse, tab, name, exchange=None):
    s = proj.shape[0]
    nq = s // QB

    def body(q_ref, k0, k1, k2, v0, v1, v2, do_ref, o_ref, lse_ref, b_ref, dq_ref, dk_ref, dv_ref, db_ref,
             dka, dkb, dva, dvb):
        i = pl.program_id(1)

        @pl.when(i == 0)
        def _():
            for r in (dka, dkb, dva, dvb):
                r[...] = jnp.zeros_like(r)
            db_ref[...] = jnp.zeros_like(db_ref)

        @pl.when(i < nq)
        def _():
            qs, scs, k, lane = _attn_scores(q_ref, (k0, k1, k2), b_ref, i)
            v = jnp.concatenate([v0[...], v1[...], v2[...]], axis=0).astype(BF16)
            do = do_ref[...]
            doo = do.astype(F32) * o_ref[...].astype(F32)
            lse = lse_ref[...]
            dq = jnp.zeros((QB, 128), F32)
            dkt = jnp.zeros((128, KW), F32)
            dvt = jnp.zeros((128, KW), F32)
            for hh in (0, 1):
                mine = (lane < 64) if hh == 0 else (lane >= 64)
                doh = jnp.where(mine, do, jnp.zeros_like(do))
                kmask = lax.broadcasted_iota(jnp.int32, (KW, 128), 1)
                kh = jnp.where((kmask < 64) if hh == 0 else (kmask >= 64), k, jnp.zeros_like(k))
                p = jnp.exp(scs[hh] - lse[:, 64 * hh:64 * hh + 1])
                drow = jnp.sum(jnp.where(mine, doo, 0.0), axis=1, keepdims=True)
                ds = p * (_dot_nt(doh, v) - drow)
                db_ref[hh] += ds
                dsb = ds.astype(BF16)
                dq = dq + _dot(dsb, kh)
                dkt = dkt + _dot_tn(qs[hh], dsb)
                dvt = dvt + _dot_tn(doh, p.astype(BF16))
            dkw, dvw = dkt.T, dvt.T
            dq_ref[...] = (dq * SCALE).astype(BF16)
            dk_ref[...] = (dka[...] + dkw[0:QB]).astype(BF16)
            dka[...] = dkb[...] + dkw[QB:2 * QB]
            dkb[...] = dkw[2 * QB:3 * QB]
            dv_ref[...] = (dva[...] + dvw[0:QB]).astype(BF16)
            dva[...] = dvb[...] + dvw[QB:2 * QB]
            dvb[...] = dvw[2 * QB:3 * QB]

        @pl.when(i >= nq)
        def _():
            dk_ref[...] = dka[...].astype(BF16)
            dka[...] = dkb[...]
            dkb[...] = jnp.zeros_like(dkb)
            dv_ref[...] = dva[...].astype(BF16)
            dva[...] = dvb[...]
            dvb[...] = jnp.zeros_like(dvb)

    def kv(col, d):
        return pl.BlockSpec((QB, 128), lambda p, i: (jnp.clip(i - 2 + d, 0, nq - 1), col + p))

    cur = lambda p, i: (jnp.minimum(i, nq - 1), p)
    done = lambda p, i: (jnp.maximum(i - 2, 0), p)
    in_specs = [pl.BlockSpec((QB, 128), cur)]
    in_specs += [kv(4, d) for d in range(3)] + [kv(8, d) for d in range(3)]
    in_specs += [pl.BlockSpec((QB, 128), cur)] * 3 + [pl.BlockSpec((2, QB, KW), lambda p, i: (p, 0, 0))]
    o = jax.ShapeDtypeStruct((s, AW), BF16)
    return _call(body, name, (4, nq + 2), in_specs,
                 [pl.BlockSpec((QB, 128), cur), pl.BlockSpec((QB, 128), done), pl.BlockSpec((QB, 128), done),
                  pl.BlockSpec((2, QB, KW), lambda p, i: (p, 0, 0))],
                 [o, o, o, jax.ShapeDtypeStruct((HEADS, QB, KW), F32)],
                 [pltpu.VMEM((QB, 128), F32)] * 4, exchange=exchange)(
                     proj, proj, proj, proj, proj, proj, proj, datt, att, lse, tab)


def _rel_bias_grad(dtab, name):
    wdt = 640

    def body(x_ref, o_ref):
        x = x_ref[...]
        xc = x[0:64, 0:wdt]
        for qc in range(1, QB // 64):
            xc = xc + pltpu.roll(x[qc * 64:(qc + 1) * 64, :], KW - qc * 64, 1)[:, 0:wdt]
        r = lax.broadcasted_iota(jnp.int32, (64, 64), 0)
        c = lax.broadcasted_iota(jnp.int32, (64, 64), 1)
        flip = (r + c == 63).astype(F32)
        y = jnp.dot(flip, xc, preferred_element_type=F32, precision=lax.Precision.HIGHEST)
        z = pltpu.roll(y, 0, 1, stride=1, stride_axis=0)
        t = jnp.broadcast_to(jnp.sum(z, axis=0, keepdims=True), (8, wdt))
        e = lax.broadcasted_iota(jnp.int32, (wdt, wdt), 0)
        rr = lax.broadcasted_iota(jnp.int32, (wdt, wdt), 1)
        onehot = (jnp.clip(BAND - 1 - e, -MAXREL, MAXREL) + MAXREL == rr).astype(F32)
        o_ref[...] = jnp.dot(t, onehot, preferred_element_type=F32, precision=lax.Precision.HIGHEST)

    return _call(body, name, (HEADS,), [pl.BlockSpec((None, QB, KW), lambda h: (h, 0, 0))],
                 pl.BlockSpec((None, 8, wdt), lambda h: (h, 0, 0)),
                 jax.ShapeDtypeStruct((HEADS, 8, wdt), F32))(dtab)


def _pool_bwd(dmixed, pooled, wg, scale, name):
    s = dmixed.shape[0]
    tb = 512
    nt = s // tb
    e = tb + 16

    def body(dm_ref, dmn_ref, pl_ref, wg_ref, sc_ref, du_ref, dwg_ref, dsc_ref):
        i = pl.program_id(0)
        dm = jnp.concatenate([dm_ref[...], jnp.where(i == nt - 1, 0.0, dmn_ref[...])], axis=0)
        t = i * tb + lax.broadcasted_iota(jnp.int32, (e, 1), 0)

        @pl.when(i == 0)
        def _():
            dwg_ref[...] = jnp.zeros_like(dwg_ref)
            dsc_ref[...] = jnp.zeros_like(dsc_ref)

        for g, w in enumerate(POOL_WINDOWS):
            sl = slice(g * PG, (g + 1) * PG)
            wgb = wg_ref[g].astype(BF16)
            pb = pl_ref[:, sl]
            dsc_ref[:, sl] += jnp.sum(dm[0:tb, sl] * _dot(pb, wgb), axis=0, keepdims=True)
            dpre = (dm[:, sl] * sc_ref[:, sl]).astype(BF16)
            dwg_ref[g] += _dot_tn(pb, dpre[0:tb])
            dpool = _dot_nt(dpre, wgb)
            a = dpool / jnp.minimum(t + 1, w).astype(F32)
            sh = 1
            while sh < w:
                a = a + pltpu.roll(a, e - sh, 0)
                sh *= 2
            du_ref[:, sl] = (a[0:tb] - dpool[0:tb]).astype(BF16)

    nb16 = s // 16
    return _call(body, name, (nt,),
                 [pl.BlockSpec((tb, PW), _row),
                  pl.BlockSpec((16, PW), lambda i: (jnp.minimum((i + 1) * (tb // 16), nb16 - 1), 0)),
                  pl.BlockSpec((tb, PW), _row), pl.BlockSpec((4, PG, PG), lambda i: (0, 0, 0)),
                  pl.BlockSpec((1, PW), _fixed2)],
                 [pl.BlockSpec((tb, PW), _row), pl.BlockSpec((4, PG, PG), lambda i: (0, 0, 0)),
                  pl.BlockSpec((1, PW), _fixed2)],
                 [jax.ShapeDtypeStruct((s, PW), BF16), jax.ShapeDtypeStruct((4, PG, PG), F32),
                  jax.ShapeDtypeStruct((1, PW), F32)])(dmixed, dmixed, pooled, wg, scale)


def _dproj_specs(t):
    return [pl.BlockSpec((t, 512), _row)] * 4 + [pl.BlockSpec((t, 2 * D), _row)]


def _dproj_segment(refs, n):
    return refs[n][...] if n < 4 else refs[4][:, (n - 4) * 512:(n - 3) * 512]


def _mm_dh(segs, win, x, g1, dx1, below, name, exchange=None):
    s = x.shape[0]
    tm = 512

    def body(dq_ref, dk_ref, dv_ref, du_ref, dg_ref, w_ref, x_ref, g1_ref, dx1_ref, *rest):
        i = pl.program_id(0)
        outs = rest[2:] if below else rest

        @pl.when(i == 0)
        def _():
            for ref in outs[1::2]:
                ref[...] = jnp.zeros_like(ref)

        pieces = (dq_ref, dk_ref, dv_ref, du_ref, dg_ref)
        dh = _dot_nt(_dproj_segment(pieces, 0), w_ref[0])
        for n in range(1, NDEV):
            dh = dh + _dot_nt(_dproj_segment(pieces, n), w_ref[n])
        dn, dg1 = _rms_bwd(x_ref[...], g1_ref[...], dh)
        dx = dx1_ref[...] + dn
        outs[0][...] = dx
        outs[1][...] += dg1
        if below:
            df, dg4 = _rms_bwd(rest[0][...], rest[1][...], dx)
            outs[2][...] = df.astype(BF16)
            outs[3][...] += dg4

    rows, gain = pl.BlockSpec((tm, D), _row), pl.BlockSpec((1, D), _fixed2)
    vec = jax.ShapeDtypeStruct((1, D), F32)
    return _call(body, name, (s // tm,),
                 _dproj_specs(tm) + [_resident((NDEV, D, 512)), rows, gain, rows] + ([rows, gain] if below else []),
                 [rows, gain] + ([rows, gain] if below else []),
                 [jax.ShapeDtypeStruct((s, D), F32), vec] + ([jax.ShapeDtypeStruct((s, D), BF16), vec] if below else []),
                 exchange=exchange)(*segs, win, x, g1, dx1, *below)


def _dw_in(h, segs, name):
    s = h.shape[0]
    tk = 512
    nk = s // tk

    def body(h_ref, dq_ref, dk_ref, dv_ref, du_ref, dg_ref, o_ref, acc):
        k = pl.program_id(0)

        @pl.when(k == 0)
        def _():
            acc[...] = jnp.zeros_like(acc)

        ht = h_ref[...].T
        pieces = (dq_ref, dk_ref, dv_ref, du_ref, dg_ref)
        for n in range(NDEV):
            acc[n] += _dot(ht, _dproj_segment(pieces, n))

        @pl.when(k == nk - 1)
        def _():
            o_ref[...] = acc[...].astype(BF16)

    return _call(body, name, (nk,), [pl.BlockSpec((tk, D), _row)] + _dproj_specs(tk),
                 pl.BlockSpec((NDEV, D, 512), lambda k: (0, 0, 0)),
                 jax.ShapeDtypeStruct((NDEV, D, 512), BF16), [pltpu.VMEM((NDEV, D, 512), F32)])(h, *segs)


def _slot(px, py, pc):
    return 4 * px + 2 * py + pc


def _gather_exchange(shards, deliver):
    n = len(shards)

    def phases(ins, outs, sems):
        send_sems, recv_sems, local_sems = sems
        x, y, c = lax.axis_index("x"), lax.axis_index("y"), lax.axis_index("c")
        me, sibling = (x, y, c), (x, y, 1 - c)
        chips = [(1 - x, y), (x, 1 - y), (1 - x, 1 - y)]

        def copy(w, k, block, to, src=None):
            dst = outs[w].at[_slot(*block)]
            return pltpu.make_async_remote_copy(
                src_ref=dst if src is None else src, dst_ref=dst, send_sem=send_sems.at[7 * w + k],
                recv_sem=recv_sems.at[7 * w + k], device_id=to, device_id_type=MESH)

        def mine(w):
            return pltpu.make_async_copy(ins[w], outs[w].at[_slot(*me)], local_sems.at[w])

        def first(w):
            return [copy(w, 0, me, sibling, src=ins[w])] + [
                copy(w, 1 + j, me, (*chip, c), src=ins[w]) for j, chip in enumerate(chips)]

        def passed(w):
            return [copy(w, 4 + j, (*chip, c), sibling) for j, chip in enumerate(chips)]

        def send():
            for w in range(n):
                mine(w).start()
                for cp in first(w):
                    cp.start()

        def forward():
            for w in range(n):
                for j, chip in enumerate(chips):
                    copy(w, 1 + j, (*chip, c), me).wait_recv()
                    passed(w)[j].start()

        def finish():
            for w in range(n):
                copy(w, 0, sibling, me).wait_recv()
                for j, chip in enumerate(chips):
                    copy(w, 4 + j, (*chip, 1 - c), me).wait_recv()
            for w in range(n):
                for cp in first(w) + passed(w):
                    cp.wait_send()
                mine(w).wait()

        return [send, forward, finish]

    return _Exchange(shards, [jax.ShapeDtypeStruct((NDEV,) + a.shape, a.dtype) for a in shards],
                     [pltpu.SemaphoreType.DMA((7 * n,)), pltpu.SemaphoreType.DMA((7 * n,)),
                      pltpu.SemaphoreType.DMA((n,))], phases, deliver)


def _scatter_exchange(parts, deliver):
    n = len(parts)

    def phases(ins, outs, sems):
        send_sems, recv_sems, local_sems = sems
        x, y, c = lax.axis_index("x"), lax.axis_index("y"), lax.axis_index("c")
        me = _slot(x, y, c)
        peers = [((1 - x) if r & 4 else x, (1 - y) if r & 2 else y, (1 - c) if r & 1 else c) for r in range(1, NDEV)]

        def mine(w):
            return pltpu.make_async_copy(ins[w].at[me], outs[w].at[me], local_sems.at[w])

        def copy(w, r, block_here, block_there):
            return pltpu.make_async_remote_copy(
                src_ref=ins[w].at[block_here], dst_ref=outs[w].at[block_there], send_sem=send_sems.at[7 * w + r],
                recv_sem=recv_sems.at[7 * w + r], device_id=peers[r], device_id_type=MESH)

        def send():
            for w in range(n):
                mine(w).start()
                for r, peer in enumerate(peers):
                    copy(w, r, _slot(*peer), me).start()

        def finish():
            for w in range(n):
                for r, peer in enumerate(peers):
                    copy(w, r, me, _slot(*peer)).wait_recv()
            for w in range(n):
                for r, peer in enumerate(peers):
                    copy(w, r, _slot(*peer), me).wait_send()
                mine(w).wait()

        return [send, finish]

    return _Exchange(parts, [jax.ShapeDtypeStruct(a.shape, a.dtype) for a in parts],
                     [pltpu.SemaphoreType.DMA((7 * n,)), pltpu.SemaphoreType.DMA((7 * n,)),
                      pltpu.SemaphoreType.DMA((n,))], phases, deliver)


def _adamw(w, g, m, v):
    m = ADAM_B1 * m + (1.0 - ADAM_B1) * g
    v = ADAM_B2 * v + (1.0 - ADAM_B2) * (g * g)
    m_hat = m / (1.0 - ADAM_B1 ** ADAM_STEP)
    v_hat = v / (1.0 - ADAM_B2 ** ADAM_STEP)
    delta = -ADAM_LR * (m_hat / (jnp.sqrt(v_hat) + ADAM_EPS) + ADAM_WD * w)
    return delta, m, v


def _adamw_sharded(recvs, w, m, v, tr, name, exchange=None):
    _, r, cdim = w.shape
    nr = r // tr

    def body(r0_ref, r1_ref, w_ref, m_ref, v_ref, g_ref, d_ref, nm_ref, nv_ref):
        l = pl.program_id(0)

        def total(ref):
            acc = ref[0].astype(F32)
            for k in range(1, NDEV):
                acc = acc + ref[k].astype(F32)
            return acc

        g = jnp.where(l == 0, total(r0_ref), total(r1_ref))
        d, nm, nv = _adamw(w_ref[...], g, m_ref[...], v_ref[...])
        g_ref[...] = g
        d_ref[...] = d
        nm_ref[...] = nm
        nv_ref[...] = nv

    mine = pl.BlockSpec((None, tr, cdim), lambda l, i: (l, i, 0))
    out = jax.ShapeDtypeStruct(w.shape, F32)
    return _call(body, name, (DEPTH, nr),
                 [pl.BlockSpec((NDEV, tr, cdim), lambda l, i: (0, jnp.where(l == 0, i, nr - 1), 0)),
                  pl.BlockSpec((NDEV, tr, cdim), lambda l, i: (0, jnp.where(l == 1, i, 0), 0)),
                  mine, mine, mine],
                 [mine] * 4, [out] * 4, exchange=exchange)(recvs[0], recvs[1], w, m, v)


def _adamw_small(parts, w, m, v, name):
    r = w.shape[0]

    def body(p_ref, w_ref, m_ref, v_ref, g_ref, d_ref, nm_ref, nv_ref):
        g = p_ref[0]
        for k in range(1, NDEV):
            g = g + p_ref[k]
        d, nm, nv = _adamw(w_ref[...], g, m_ref[...], v_ref[...])
        g_ref[...] = g
        d_ref[...] = d
        nm_ref[...] = nm
        nv_ref[...] = nv

    whole = pl.BlockSpec((r, 1024), _fixed2)
    out = jax.ShapeDtypeStruct((r, 1024), F32)
    return _call(body, name, (1,), [pl.BlockSpec((NDEV, r, 1024), lambda i: (0, 0, 0)), whole, whole, whole],
                 [whole] * 4, [out] * 4)(parts, w, m, v)


SMALL = (("norm_mix_pre", (D,)), ("b_gate", (2 * D,)), ("rel_bias", (HEADS, NREL)), ("w_pool_group", (4, PG, PG)),
         ("pool_scale", (PW,)), ("norm_mix_post", (D,)), ("norm_ffn_pre", (D,)), ("conv_b", (NFF * 2 * FS,)),
         ("norm_ffn_post", (D,)))
SHARDED = (("w_in", 256), ("w_attn_out", 512), ("w_pool_out", 512), ("w_o", 128), ("w_up", 256), ("conv_w", 3),
           ("w_down", 176))
SMALL_ROWS = 168


def _pack_small(tree):
    flat = jnp.concatenate([tree[name].reshape(-1) for name, _ in SMALL])
    return jnp.pad(flat, (0, SMALL_ROWS * 1024 - flat.shape[0])).reshape(SMALL_ROWS, 1024)


def _unpack_small(packed):
    flat = packed.reshape(-1)
    out, at = {}, 0
    for name, shape in SMALL:
        size = DEPTH * int(np.prod(shape))
        out[name] = flat[at:at + size].reshape((DEPTH,) + shape)
        at += size
    return out


def _layer_fwd(l, x, h, p, gath, target, gnext, carry):
    n = f"l{l}"
    g = lambda name: p[name][l].reshape(1, -1)
    res = {"x": x, "h": h}
    res["tab"] = _bias_table(p["rel_bias"][l], f"bias_table_{n}")
    proj = _mm_in(h, gath["w_in"], f"mm_in_{n}", carry.get("mm_in"))
    att, res["lse"] = _attn_fwd(proj, res["tab"], f"attn_fwd_{n}", carry.get("attn_fwd"))
    pooled, mixed = _pool_fwd(proj, p["w_pool_group"][l], g("pool_scale"), f"pool_fwd_{n}")
    bgate = g("b_gate")
    z, ya, yb, mix, x1, h2 = _mix_fwd(att, mixed, gath["w_attn_out"], gath["w_pool_out"], proj, bgate, gath["w_o"],
                                      x, g("norm_mix_post"), g("norm_ffn_pre"), f"mix_fwd_{n}")
    cw = gath["conv_w"].reshape(2, NFF, 3, FS)
    cb = p["conv_b"][l].reshape(2, NFF, 1, FS)
    hu, a = _mm_up_conv_gate(h2, gath["w_up"], cw, cb, f"mm_up_conv_gate_{n}", carry.get("mm_up_conv_gate"))
    hu = hu.reshape(2, NFF, -1, FS)
    res.update(proj=proj, att=att, pooled=pooled, mixed=mixed, z=z, ya=ya, yb=yb, mix=mix, x1=x1, h2=h2, hu=hu,
               a=a, cw=cw, cb=cb, bgate=bgate)
    if target is None:
        f, x2, hn = _mm_down_fwd(a, gath["w_down"], x1, g("norm_ffn_post"), gnext, f"mm_down_fwd_{n}",
                                 carry.get("mm_down_fwd"))
        res["f"] = f
        return res, x2, hn
    dy, df, dg4, loss = _mm_down_loss(a, gath["w_down"], x1, g("norm_ffn_post"), target, f"mm_down_loss_{n}")
    return res, (dy, df, dg4), loss


def _layer_bwd(l, top, p, gath, res, carry, below):
    n = f"l{l}"
    g = lambda name: p[name][l].reshape(1, -1)
    big, small = {}, {}
    taken = lambda call: carry[call](big) if call in carry else None
    dx2, df, small["norm_ffn_post"] = top
    dhu, dcw, dcb, big["w_down"] = _conv_gate_bwd(res["hu"], df, gath["w_down"], res["cw"], res["cb"], res["a"],
                                                  f"conv_gate_bwd_{n}", taken("conv_gate_bwd"))
    big["conv_w"] = dcw.reshape(NDEV, 3, FS)
    small["conv_b"] = dcb
    dx1, dmix, small["norm_ffn_pre"], small["norm_mix_post"] = _mm_dh2(
        dhu, gath["w_up"], res["x1"], g("norm_ffn_pre"), dx2, res["mix"], g("norm_mix_post"), f"mm_dh2_{n}")
    big["w_up"] = _dw_up(res["h2"], dhu, f"dw_up_{n}")
    dgates, small["b_gate"], datt, dmixed, big["w_o"], big["w_attn_out"], big["w_pool_out"] = _mix_bwd(
        dmix, gath["w_o"], res["proj"], res["bgate"], res["ya"], res["yb"], gath["w_attn_out"], gath["w_pool_out"],
        res["z"], res["att"], res["mixed"], f"mix_bwd_{n}")
    dq, dk, dv, dtab = _attn_bwd(res["proj"], datt, res["att"], res["lse"], res["tab"], f"attn_bwd_{n}",
                                 taken("attn_bwd"))
    small["rel_bias"] = _rel_bias_grad(dtab, f"rel_bias_grad_{n}")[:, 0, :NREL]
    du, small["w_pool_group"], small["pool_scale"] = _pool_bwd(
        dmixed, res["pooled"], p["w_pool_group"][l], g("pool_scale"), f"pool_bwd_{n}")
    segs = (dq, dk, dv, du, dgates)
    big["w_in"] = _dw_in(res["h"], segs, f"dw_in_{n}")
    dx, small["norm_mix_pre"], *lower = _mm_dh(segs, gath["w_in"], res["x"], g("norm_mix_pre"), dx1, below,
                                               f"mm_dh_{n}", taken("mm_dh"))
    return (dx, *lower), big, small


def _gather_weights(gath, l, p, names):
    shards = [p[k][l] if k == "conv_w" else p[k][l].astype(BF16) for k in names]

    def deliver(results):
        for k, a in zip(names, results):
            if k in ("w_attn_out", "w_pool_out"):
                a = jnp.transpose(a, (1, 0, 2)).reshape(AW, D)
            elif k == "w_o":
                a = a.reshape(D, D)
            elif k == "w_down":
                a = a.reshape(NFF * FS, D)
            gath[k] = a

    return _gather_exchange(shards, deliver)


def _scatter_grads(recv, big, names):
    return _scatter_exchange([big[k] for k in names], lambda results: recv.update(zip(names, results)))


def kernel(x, norm_mix_pre, w_in, b_gate, rel_bias, w_attn_out, w_pool_group, pool_scale, w_pool_out, w_o, norm_mix_post, norm_ffn_pre, w_up, conv_w, conv_b, w_down, norm_ffn_post, loss_target, m_norm_mix_pre, m_w_in, m_b_gate, m_rel_bias, m_w_attn_out, m_w_pool_group, m_pool_scale, m_w_pool_out, m_w_o, m_norm_mix_post, m_norm_ffn_pre, m_w_up, m_conv_w, m_conv_b, m_w_down, m_norm_ffn_post, v_norm_mix_pre, v_w_in, v_b_gate, v_rel_bias, v_w_attn_out, v_w_pool_group, v_pool_scale, v_w_pool_out, v_w_o, v_norm_mix_post, v_norm_ffn_pre, v_w_up, v_conv_w, v_conv_b, v_w_down, v_norm_ffn_post):
    names = ("norm_mix_pre", "w_in", "b_gate", "rel_bias", "w_attn_out", "w_pool_group", "pool_scale", "w_pool_out",
             "w_o", "norm_mix_post", "norm_ffn_pre", "w_up", "conv_w", "conv_b", "w_down", "norm_ffn_post")
    p = dict(zip(names, (norm_mix_pre, w_in, b_gate, rel_bias, w_attn_out, w_pool_group, pool_scale, w_pool_out, w_o,
                         norm_mix_post, norm_ffn_pre, w_up, conv_w, conv_b, w_down, norm_ffn_post)))
    mom = dict(zip(names, (m_norm_mix_pre, m_w_in, m_b_gate, m_rel_bias, m_w_attn_out, m_w_pool_group, m_pool_scale,
                           m_w_pool_out, m_w_o, m_norm_mix_post, m_norm_ffn_pre, m_w_up, m_conv_w, m_conv_b, m_w_down,
                           m_norm_ffn_post)))
    var = dict(zip(names, (v_norm_mix_pre, v_w_in, v_b_gate, v_rel_bias, v_w_attn_out, v_w_pool_group, v_pool_scale,
                           v_w_pool_out, v_w_o, v_norm_mix_post, v_norm_ffn_pre, v_w_up, v_conv_w, v_conv_b, v_w_down,
                           v_norm_ffn_post)))
    s = x.shape[1]
    xs = x.reshape(s, D)
    target = loss_target.reshape(s, D)

    gath = [{}, {}]
    rest = ("w_attn_out", "w_pool_out", "w_o", "w_down", "conv_w")
    h0 = _rmsnorm_call(xs, p["norm_mix_pre"][0].reshape(1, D), "rmsnorm_l0",
                       _gather_weights(gath[0], 0, p, ("w_in",)))
    res0, x2, h1 = _layer_fwd(0, xs, h0, p, gath[0], None, p["norm_mix_pre"][1].reshape(1, D), {
        "mm_in": _gather_weights(gath[0], 0, p, rest),
        "attn_fwd": _gather_weights(gath[0], 0, p, ("w_up",)),
        "mm_up_conv_gate": _gather_weights(gath[1], 1, p, ("w_in", "w_up") + rest)})
    res1, top1, loss_part = _layer_fwd(1, x2, h1, p, gath[1], target, None, {})

    order = [k for k, _ in SHARDED]
    recv = [{}, {}]
    top0, big1, small1 = _layer_bwd(1, top1, p, gath[1], res1, {}, (res0["f"], p["norm_ffn_post"][0].reshape(1, D)))
    (grad_x,), big0, small0 = _layer_bwd(0, top0, p, gath[0], res0, {
        "conv_gate_bwd": lambda big: _scatter_grads(recv[1], big1, order),
        "attn_bwd": lambda big: _scatter_grads(recv[0], big, [k for k in order if k != "w_in"]),
        "mm_dh": lambda big: _scatter_grads(recv[0], big, ["w_in"])}, ())

    loss = lax.psum(loss_part[0, 0], ("x", "y", "c"))

    small = {k: jnp.stack([small0[k].reshape(shape), small1[k].reshape(shape)]) for k, shape in SMALL}
    parts = []
    carried = {"conv_w": _gather_exchange([_pack_small(small)], parts.extend)}
    out = {}
    for k, tr in sorted(SHARDED, key=lambda kt: kt[0] not in carried):
        out[k] = _adamw_sharded((recv[0][k], recv[1][k]), p[k], mom[k], var[k], tr, f"adamw_{k}", carried.get(k))
    packed = _adamw_small(parts[0], _pack_small(p), _pack_small(mom), _pack_small(var), "adamw_small")
    unpacked = [_unpack_small(a) for a in packed]
    for k, _ in SMALL:
        out[k] = tuple(u[k] for u in unpacked)

    return (loss, grad_x.reshape(x.shape), *[out[k][0] for k in names], *[out[k][1] for k in names],
            *[out[k][2] for k in names], *[out[k][3] for k in names])
```

```python
import numpy as np
import jax
import jax.numpy as jnp
from jax import lax
from jax.experimental import pallas as pl
from jax.experimental.pallas import tpu as pltpu

F32, BF16 = jnp.float32, jnp.bfloat16

D = 1024
AW = 512
PW = 512
PG = 128
INW = 4096
FS = 704
NFF = 4
NDEV = 8
DEPTH = 2
HEADS = 8
NREL = 513
MAXREL = 256
POOL_WINDOWS = (2, 4, 8, 16)
EPS = 1e-6
SCALE = 0.125
NEG = -1e30
QB = 256
KW = 3 * QB
BAND = 576
ADAM_LR, ADAM_B1, ADAM_B2, ADAM_EPS, ADAM_WD, ADAM_STEP = 0.001, 0.9, 0.999, 1e-08, 0.01, 10
VMEM_LIMIT_V7X = 56 * 1024 * 1024
MESH = pl.DeviceIdType.MESH
GELU_C = 0.7978845608028654
GELU_A = 0.044715


HBM = pl.BlockSpec(memory_space=pltpu.HBM)


class _Exchange:
    def __init__(self, operands, out_shape, scratch, phases, deliver):
        self.operands, self.out_shape, self.scratch = list(operands), list(out_shape), list(scratch)
        self.phases, self.deliver = phases, deliver


def _call(body, name, grid, in_specs, out_specs, out_shape, scratch=(), exchange=None):
    params = pltpu.CompilerParams(vmem_limit_bytes=VMEM_LIMIT_V7X)
    if exchange is None:
        return pl.pallas_call(body, name=name, grid=grid, in_specs=in_specs, out_specs=out_specs, out_shape=out_shape,
                              scratch_shapes=list(scratch), compiler_params=params)
    single = not isinstance(out_shape, (list, tuple))
    outs, ospecs = ([out_shape], [out_specs]) if single else (list(out_shape), list(out_specs))
    n_in, n_out, n_scr = len(in_specs), len(outs), len(scratch)
    ne_in, ne_out = len(exchange.operands), len(exchange.out_shape)
    nsteps = int(np.prod(grid))

    def carried(*refs):
        cut = np.cumsum([0, n_in, ne_in, n_out, ne_out, n_scr])
        base_in, ex_in, base_out, ex_out, base_scr = (refs[cut[k]:cut[k + 1]] for k in range(5))
        step = pl.program_id(0)
        for axis in range(1, len(grid)):
            step = step * grid[axis] + pl.program_id(axis)
        phases = exchange.phases(ex_in, ex_out, refs[cut[5]:])
        pl.when(step == 0)(phases[0])
        body(*base_in, *base_out, *base_scr)
        at = {2: [nsteps - 1], 3: [(7 * nsteps) // 10, nsteps - 1]}[len(phases)]
        for phase, when in zip(phases[1:], at):
            pl.when(step == when)(phase)

    call = pl.pallas_call(
        carried, name=name, grid=grid, in_specs=list(in_specs) + [HBM] * ne_in, out_specs=ospecs + [HBM] * ne_out,
        out_shape=outs + exchange.out_shape, scratch_shapes=list(scratch) + exchange.scratch, compiler_params=params)

    def run(*args):
        res = call(*args, *exchange.operands)
        exchange.deliver(res[n_out:])
        return res[0] if single else res[:n_out]

    return run


def _dot(a, b):
    return jnp.dot(a, b, preferred_element_type=F32)


def _dot_nt(a, b):
    return lax.dot_general(a, b, (((1,), (1,)), ((), ())), preferred_element_type=F32)


def _dot_tn(a, b):
    return lax.dot_general(a, b, (((0,), (0,)), ((), ())), preferred_element_type=F32)


def _rms(x, g):
    r = lax.rsqrt(jnp.mean(x * x, axis=-1, keepdims=True) + EPS)
    return x * r * g


def _rms_bwd(x, g, dy):
    r = lax.rsqrt(jnp.mean(x * x, axis=-1, keepdims=True) + EPS)
    xh = x * r
    dg = jnp.sum(dy * xh, axis=0, keepdims=True)
    dxh = dy * g
    dx = r * (dxh - xh * jnp.mean(dxh * xh, axis=-1, keepdims=True))
    return dx, dg


def _sigmoid(x):
    return 1.0 / (1.0 + jnp.exp(-x))


def _gelu_parts(x):
    x2 = x * x
    th = jnp.tanh(x * (GELU_C + (GELU_C * GELU_A) * x2))
    s = 0.5 * th + 0.5
    dg = s * (1.0 + x * (1.0 - s) * (2.0 * GELU_C + (6.0 * GELU_C * GELU_A) * x2))
    return x * s, dg


def _row(i):
    return (i, 0)


def _fixed2(*_):
    return (0, 0)


def _rmsnorm_call(x, g, name, exchange=None):
    s = x.shape[0]
    tm = 512

    def body(x_ref, g_ref, o_ref):
        o_ref[...] = _rms(x_ref[...], g_ref[...]).astype(BF16)

    return _call(body, name, (s // tm,),
                 [pl.BlockSpec((tm, D), _row), pl.BlockSpec((1, D), _fixed2)],
                 pl.BlockSpec((tm, D), _row), jax.ShapeDtypeStruct((s, D), BF16), exchange=exchange)(x, g)


def _resident(shape):
    return pl.BlockSpec(shape, lambda *_: (0,) * len(shape), pipeline_mode=pl.Buffered(1))


def _mm_in(h, win, name, exchange=None):
    s = h.shape[0]
    tm = 512

    def body(h_ref, w_ref, o_ref):
        hv = h_ref[...]
        for j in range(NDEV):
            o_ref[:, j * 512:(j + 1) * 512] = _dot(hv, w_ref[j])

    return _call(body, name, (s // tm,),
                 [pl.BlockSpec((tm, D), _row), _resident((NDEV, D, 512))],
                 pl.BlockSpec((tm, INW), _row), jax.ShapeDtypeStruct((s, INW), F32), exchange=exchange)(h, win)


def _bias_table(rel_bias, name):
    wdt = 1024
    rel = jnp.pad(rel_bias, ((0, 0), (0, 640 - NREL))).reshape(HEADS, 1, 640)

    def body(r_ref, o_ref):
        rr = lax.broadcasted_iota(jnp.int32, (640, wdt), 0)
        m = lax.broadcasted_iota(jnp.int32, (640, wdt), 1)
        d = jnp.where(m < KW, m, m - wdt)
        onehot = (jnp.clip(512 - d, -MAXREL, MAXREL) + MAXREL == rr).astype(F32)
        row = jnp.dot(jnp.broadcast_to(r_ref[...], (8, 640)), onehot, preferred_element_type=F32,
                      precision=lax.Precision.HIGHEST)[0:1]
        t = pltpu.roll(jnp.broadcast_to(row, (QB, wdt)), 0, 1, stride=1, stride_axis=0)[:, 0:KW]
        qc = lax.broadcasted_iota(jnp.int32, (QB, KW), 0) // 64
        kc = lax.broadcasted_iota(jnp.int32, (QB, KW), 1) // 64
        o_ref[...] = jnp.where((kc >= qc) & (kc <= qc + 8), t, NEG)

    return _call(body, name, (HEADS,), [pl.BlockSpec((None, 1, 640), lambda h: (h, 0, 0))],
                 pl.BlockSpec((None, QB, KW), lambda h: (h, 0, 0)),
                 jax.ShapeDtypeStruct((HEADS, QB, KW), F32))(rel)


def _attn_scores(q_ref, k_refs, b_ref, i):
    lane = lax.broadcasted_iota(jnp.int32, (QB, 128), 1)
    q = q_ref[...] * SCALE
    qs = [jnp.where(lane < 64, q, 0.0).astype(BF16), jnp.where(lane >= 64, q, 0.0).astype(BF16)]
    k = jnp.concatenate([r[...] for r in k_refs], axis=0).astype(BF16)
    colb = lax.broadcasted_iota(jnp.int32, (1, KW), 1) // QB
    before = jnp.where(colb + i >= 2, 0.0, NEG)
    return qs, [_dot_nt(qs[hh], k) + b_ref[hh] + before for hh in (0, 1)], k, lane


def _attn_fwd(proj, tab, name, exchange=None):
    s = proj.shape[0]
    nq = s // QB

    def body(q_ref, k0, k1, k2, v0, v1, v2, b_ref, o_ref, lse_ref):
        _, scs, _, lane = _attn_scores(q_ref, (k0, k1, k2), b_ref, pl.program_id(1))
        v = jnp.concatenate([v0[...], v1[...], v2[...]], axis=0).astype(BF16)
        v1s = jnp.concatenate([v, jnp.ones((KW, 128), BF16)], axis=1)
        o, lse = [], []
        for sc in scs:
            m = jnp.max(sc, axis=1, keepdims=True)
            ov = _dot(jnp.exp(sc - m).astype(BF16), v1s)
            o.append(ov[:, 0:128] / ov[:, 128:256])
            lse.append(m + jnp.log(ov[:, 128:256]))
        o_ref[...] = jnp.where(lane < 64, o[0], o[1]).astype(BF16)
        lse_ref[...] = jnp.where(lane < 64, lse[0], lse[1])

    def kv(col, d):
        return pl.BlockSpec((QB, 128), lambda p, i: (jnp.maximum(i - 2 + d, 0), col + p))

    in_specs = [pl.BlockSpec((QB, 128), lambda p, i: (i, p))]
    in_specs += [kv(4, d) for d in range(3)] + [kv(8, d) for d in range(3)]
    in_specs += [pl.BlockSpec((2, QB, KW), lambda p, i: (p, 0, 0))]
    out = pl.BlockSpec((QB, 128), lambda p, i: (i, p))
    return _call(body, name, (4, nq), in_specs, [out, out],
                 [jax.ShapeDtypeStruct((s, AW), BF16), jax.ShapeDtypeStruct((s, AW), F32)], exchange=exchange)(
                     proj, proj, proj, proj, proj, proj, proj, tab)


def _pool_fwd(proj, wg, scale, name):
    s = proj.shape[0]
    tb = 512
    e = tb + 16

    def body(u_ref, halo_ref, wg_ref, sc_ref, pooled_ref, mixed_ref):
        i = pl.program_id(0)
        cur = u_ref[...]
        prev = jnp.where(i > 0, halo_ref[...], 0.0)
        xs = jnp.concatenate([prev, cur], axis=0)
        t = i * tb + lax.broadcasted_iota(jnp.int32, (tb, 1), 0)
        for g, w in enumerate(POOL_WINDOWS):
            sl = slice(g * PG, (g + 1) * PG)
            a = xs[:, sl]
            sh = 1
            while sh < w:
                a = a + pltpu.roll(a, sh, 0)
                sh *= 2
            cnt = jnp.minimum(t + 1, w).astype(F32)
            pooled = (a[16:] / cnt - cur[:, sl]).astype(BF16)
            pooled_ref[:, sl] = pooled
            mixed_ref[:, sl] = (_dot(pooled, wg_ref[g].astype(BF16)) * sc_ref[:, sl]).astype(BF16)

    assert e % 8 == 0
    return _call(body, name, (s // tb,),
                 [pl.BlockSpec((tb, PW), lambda i: (i, 3)),
                  pl.BlockSpec((16, PW), lambda i: (jnp.maximum(i * (tb // 16) - 1, 0), 3)),
                  pl.BlockSpec((4, PG, PG), lambda i: (0, 0, 0)), pl.BlockSpec((1, PW), _fixed2)],
                 [pl.BlockSpec((tb, PW), _row), pl.BlockSpec((tb, PW), _row)],
                 [jax.ShapeDtypeStruct((s, PW), BF16), jax.ShapeDtypeStruct((s, PW), BF16)])(proj, proj, wg, scale)


def _mix_fwd(att, mixed, wao, wpo, proj, bgate, wo, x, g2, g3, name):
    s = att.shape[0]
    tm = 512

    def body(att_ref, mx_ref, wao_ref, wpo_ref, ga_ref, gb_ref, ba_ref, bb_ref, wo_ref, x_ref, g2_ref, g3_ref,
             z_ref, ya_ref, yb_ref, mix_ref, x1_ref, h2_ref):
        ya = _dot(att_ref[...], wao_ref[...])
        yb = _dot(mx_ref[...], wpo_ref[...])
        ga = _sigmoid(ga_ref[...] + ba_ref[...])
        gb = _sigmoid(gb_ref[...] + bb_ref[...])
        z = (ga * ya + gb * yb).astype(BF16)
        mix = _dot(z, wo_ref[...])
        x1 = x_ref[...] + _rms(mix, g2_ref[...])
        ya_ref[...] = ya.astype(BF16)
        yb_ref[...] = yb.astype(BF16)
        z_ref[...] = z
        mix_ref[...] = mix
        x1_ref[...] = x1
        h2_ref[...] = _rms(x1, g3_ref[...]).astype(BF16)

    rows, gain = pl.BlockSpec((tm, D), _row), pl.BlockSpec((1, D), _fixed2)
    half, full = jax.ShapeDtypeStruct((s, D), BF16), jax.ShapeDtypeStruct((s, D), F32)
    return _call(body, name, (s // tm,),
                 [pl.BlockSpec((tm, AW), _row), pl.BlockSpec((tm, PW), _row),
                  _resident((AW, D)), _resident((PW, D)),
                  pl.BlockSpec((tm, D), lambda i: (i, 2)), pl.BlockSpec((tm, D), lambda i: (i, 3)),
                  pl.BlockSpec((1, D), lambda i: (0, 0)), pl.BlockSpec((1, D), lambda i: (0, 1)),
                  _resident((D, D)), rows, gain, gain],
                 [rows] * 6, [half, half, half, full, full, half])(
                     att, mixed, wao, wpo, proj, proj, bgate, bgate, wo, x, g2, g3)


def _mm_up_conv_gate(h2, wup, cw, cb, name, exchange=None):
    s = h2.shape[0]
    tm = 256

    def body(h_ref, w_ref, cw_ref, cb_ref, hu_ref, a_ref, halo):
        first = pl.program_id(0) == 0
        hv = h_ref[...]
        for j in range(NDEV):
            hu_ref[j] = _dot(hv, w_ref[j])
        for jj in range(NFF):
            for c0, w in LANE_COLUMNS:
                lanes = pl.ds(c0, w)
                cwb = [[jnp.broadcast_to(cw_ref[sd, jj, t:t + 1, lanes], (8, w)) for t in range(3)] for sd in (0, 1)]
                cbb = [jnp.broadcast_to(cb_ref[sd, jj, :, lanes], (8, w)) for sd in (0, 1)]
                row = lax.broadcasted_iota(jnp.int32, (8, w), 0)
                rolled = []
                for sd in (0, 1):
                    before = jnp.where(first, 0.0, halo[NFF * sd + jj, :, lanes])
                    rolled += [pltpu.roll(before, 1, 0), pltpu.roll(before, 2, 0)]
                for k in range(tm // 16):
                    outs = []
                    for r in (16 * k, 16 * k + 8):
                        hc, keep = [], []
                        for sd in (0, 1):
                            cur = hu_ref[NFF * sd + jj, pl.ds(r, 8), lanes]
                            r1, r2 = pltpu.roll(cur, 1, 0), pltpu.roll(cur, 2, 0)
                            hc.append(cbb[sd] + cwb[sd][2] * cur
                                      + cwb[sd][0] * jnp.where(row >= 2, r2, rolled[2 * sd + 1])
                                      + cwb[sd][1] * jnp.where(row >= 1, r1, rolled[2 * sd]))
                            keep += [r1, r2]
                        rolled = keep
                        outs.append(_gelu_parts(hc[1])[0] * hc[0])
                    a_ref[jj, pl.ds(16 * k, 16), lanes] = jnp.concatenate(outs, axis=0).astype(BF16)
        for j in range(NDEV):
            halo[j] = hu_ref[j, tm - 8:tm, :]

    small = lambda i: (0, 0, 0, 0)
    return _call(body, name, (s // tm,),
                 [pl.BlockSpec((tm, D), _row), _resident((NDEV, D, FS)), pl.BlockSpec((2, NFF, 3, FS), small),
                  pl.BlockSpec((2, NFF, 1, FS), small)],
                 [pl.BlockSpec((NDEV, tm, FS), lambda i: (0, i, 0)), pl.BlockSpec((NFF, tm, FS), lambda i: (0, i, 0))],
                 [jax.ShapeDtypeStruct((NDEV, s, FS), F32), jax.ShapeDtypeStruct((NFF, s, FS), BF16)],
                 [pltpu.VMEM((NDEV, 8, FS), F32)], exchange=exchange)(h2, wup, cw, cb)


LANE_COLUMNS = [(c0, min(128, FS - c0)) for c0 in range(0, FS, 128)]


def _mm_down_fwd(a, wd, x1, g4, gnext, name, exchange=None):
    s = a.shape[1]
    tm = 512

    def body(a_ref, w_ref, x1_ref, g4_ref, gn_ref, f_ref, x2_ref, hn_ref):
        f = _dot(a_ref[0], w_ref[0:FS, :])
        for j in range(1, NFF):
            f = f + _dot(a_ref[j], w_ref[j * FS:(j + 1) * FS, :])
        x2 = x1_ref[...] + _rms(f, g4_ref[...])
        f_ref[...] = f
        x2_ref[...] = x2
        hn_ref[...] = _rms(x2, gn_ref[...]).astype(BF16)

    return _call(body, name, (s // tm,),
                 [pl.BlockSpec((NFF, tm, FS), lambda i: (0, i, 0)), pl.BlockSpec((NFF * FS, D), _fixed2),
                  pl.BlockSpec((tm, D), _row), pl.BlockSpec((1, D), _fixed2), pl.BlockSpec((1, D), _fixed2)],
                 [pl.BlockSpec((tm, D), _row)] * 3,
                 [jax.ShapeDtypeStruct((s, D), F32), jax.ShapeDtypeStruct((s, D), F32),
                  jax.ShapeDtypeStruct((s, D), BF16)], exchange=exchange)(a, wd, x1, g4, gnext)


def _mm_down_loss(a, wd, x1, g4, target, name):
    s = a.shape[1]
    tm = 512

    def body(a_ref, w_ref, x1_ref, g4_ref, t_ref, dy_ref, df_ref, dg_ref, loss_ref):
        i = pl.program_id(0)
        f = _dot(a_ref[0], w_ref[0:FS, :])
        for j in range(1, NFF):
            f = f + _dot(a_ref[j], w_ref[j * FS:(j + 1) * FS, :])
        err = x1_ref[...] + _rms(f, g4_ref[...]) - t_ref[...]
        dy = err * (1.0 / D)
        df, dg = _rms_bwd(f, g4_ref[...], dy)
        dy_ref[...] = dy
        df_ref[...] = df.astype(BF16)
        part = 0.5 * jnp.sum(jnp.mean(err * err, axis=-1, keepdims=True), axis=0, keepdims=True)

        @pl.when(i == 0)
        def _():
            loss_ref[...] = jnp.zeros_like(loss_ref)
            dg_ref[...] = jnp.zeros_like(dg_ref)

        loss_ref[...] += jnp.broadcast_to(part, loss_ref.shape)
        dg_ref[...] += dg

    return _call(body, name, (s // tm,),
                 [pl.BlockSpec((NFF, tm, FS), lambda i: (0, i, 0)), pl.BlockSpec((NFF * FS, D), _fixed2),
                  pl.BlockSpec((tm, D), _row), pl.BlockSpec((1, D), _fixed2), pl.BlockSpec((tm, D), _row)],
                 [pl.BlockSpec((tm, D), _row), pl.BlockSpec((tm, D), _row), pl.BlockSpec((1, D), _fixed2),
                  pl.BlockSpec((8, 128), _fixed2)],
                 [jax.ShapeDtypeStruct((s, D), F32), jax.ShapeDtypeStruct((s, D), BF16),
                  jax.ShapeDtypeStruct((1, D), F32), jax.ShapeDtypeStruct((8, 128), F32)])(a, wd, x1, g4, target)


def _conv_gate_bwd(hu, df, wd, cw, cb, a, name, exchange=None):
    s = hu.shape[2]
    tb = 1024
    nt = s // tb
    e = tb + 16

    def body(hu_ref, prev_ref, next_ref, df_ref, dfn_ref, wd_ref, cw_ref, cb_ref, a_ref, dhu_ref, dcw_ref, dcb_ref,
             dwd_ref, dabuf, dbuf, dwd_acc):
        i = pl.program_id(1)
        first, last = i == 0, i == nt - 1

        @pl.when(i == 0)
        def _():
            dcw_ref[...] = jnp.zeros_like(dcw_ref)
            dcb_ref[...] = jnp.zeros_like(dcb_ref)
            dwd_acc[...] = jnp.zeros_like(dwd_acc)

        dabuf[0:tb] = _dot_nt(df_ref[...], wd_ref[...])
        dabuf[tb:tb + 8] = jnp.where(last, 0.0, _dot_nt(dfn_ref[...], wd_ref[...])[0:8])
        dwd_acc[...] += _dot_tn(a_ref[...], df_ref[...])

        for c0, w in LANE_COLUMNS:
            lanes = pl.ds(c0, w)
            cwb = [[jnp.broadcast_to(cw_ref[sd, t:t + 1, lanes], (8, w)) for t in range(3)] for sd in (0, 1)]
            cbb = [jnp.broadcast_to(cb_ref[sd, :, lanes], (8, w)) for sd in (0, 1)]

            row = lax.broadcasted_iota(jnp.int32, (8, w), 0)

            def tile(k, carry, summed=True):
                sums, rolled = carry[:8], carry[8:]
                r = 8 + 8 * k
                xs, keep = [], []
                for sd in (0, 1):
                    if summed:
                        cur = hu_ref[sd, pl.ds(r - 8, 8), lanes]
                    else:
                        cur = jnp.where(last, 0.0, next_ref[sd, :, lanes])
                    r1, r2 = pltpu.roll(cur, 1, 0), pltpu.roll(cur, 2, 0)
                    xs.append([jnp.where(row >= 2, r2, rolled[2 * sd + 1]), jnp.where(row >= 1, r1, rolled[2 * sd]),
                               cur])
                    keep += [r1, r2]
                hc = [cbb[sd] + cwb[sd][2] * xs[sd][2] + cwb[sd][0] * xs[sd][0] + cwb[sd][1] * xs[sd][1]
                      for sd in (0, 1)]
                da = dabuf[pl.ds(r - 8, 8), lanes]
                gl, dgl = _gelu_parts(hc[1])
                dhc = (da * gl, da * hc[0] * dgl)
                for sd in (0, 1):
                    dbuf[sd, pl.ds(r, 8), lanes] = dhc[sd]
                if not summed:
                    return carry
                new = []
                for sd in (0, 1):
                    new += [sums[4 * sd + t] + dhc[sd] * xs[sd][t] for t in range(3)] + [sums[4 * sd + 3] + dhc[sd]]
                return tuple(new + keep)

            start = [jnp.zeros((8, w), F32) for _ in range(8)]
            for sd in (0, 1):
                halo = jnp.where(first, 0.0, prev_ref[sd, :, lanes])
                start += [pltpu.roll(halo, 1, 0), pltpu.roll(halo, 2, 0)]
            def tiles(k4, carry):
                for u in range(4):
                    carry = tile(4 * k4 + u, carry)
                return carry

            carry = tuple(start)
            for k4 in range(tb // 32):
                carry = tiles(k4, carry)
            tile(tb // 8, carry, summed=False)
            sums = carry[:8]

            def up(v):
                return pltpu.roll(v, 7, 0), pltpu.roll(v, 6, 0)

            def out_tile(k, carry):
                r = 8 + 16 * k
                new = []
                for sd in (0, 1):
                    va, va1, va2 = carry[3 * sd:3 * sd + 3]
                    vb, vc = dbuf[sd, pl.ds(r + 8, 8), lanes], dbuf[sd, pl.ds(r + 16, 8), lanes]
                    (vb1, vb2), (vc1, vc2) = up(vb), up(vc)
                    c0b, c1b, c2b = cwb[sd]
                    top = c2b * va + c1b * jnp.where(row <= 6, va1, vb1) + c0b * jnp.where(row <= 5, va2, vb2)
                    bot = c2b * vb + c1b * jnp.where(row <= 6, vb1, vc1) + c0b * jnp.where(row <= 5, vb2, vc2)
                    dhu_ref[sd, pl.ds(16 * k, 16), lanes] = jnp.concatenate(
                        [top, bot], axis=0).astype(BF16)
                    new += [vc, vc1, vc2]
                return tuple(new)

            begin = []
            for sd in (0, 1):
                va = dbuf[sd, 8:16, lanes]
                begin += [va, *up(va)]
            def out_tiles(k2, carry):
                return out_tile(2 * k2 + 1, out_tile(2 * k2, carry))

            carry = tuple(begin)
            for k2 in range(tb // 32):
                carry = out_tiles(k2, carry)
            for sd in (0, 1):
                for t in range(3):
                    dcw_ref[sd, t:t + 1, lanes] += jnp.sum(sums[4 * sd + t], axis=0, keepdims=True)
                dcb_ref[sd, :, lanes] += jnp.sum(sums[4 * sd + 3], axis=0, keepdims=True)

        @pl.when(last)
        def _():
            dwd_ref[0] = dwd_acc[0:FS // 2, :].astype(BF16)
            dwd_ref[1] = dwd_acc[FS // 2:FS, :].astype(BF16)

    nb8, nb16 = s // 8, s // 16
    return _call(body, name, (NFF, nt),
                 [pl.BlockSpec((2, None, tb, FS), lambda j, i: (0, j, i, 0)),
                  pl.BlockSpec((2, None, 8, FS), lambda j, i: (0, j, jnp.maximum(i * (tb // 8) - 1, 0), 0)),
                  pl.BlockSpec((2, None, 8, FS), lambda j, i: (0, j, jnp.minimum((i + 1) * (tb // 8), nb8 - 1), 0)),
                  pl.BlockSpec((tb, D), lambda j, i: (i, 0)),
                  pl.BlockSpec((16, D), lambda j, i: (jnp.minimum((i + 1) * (tb // 16), nb16 - 1), 0)),
                  pl.BlockSpec((FS, D), lambda j, i: (j, 0)),
                  pl.BlockSpec((2, None, 3, FS), lambda j, i: (0, j, 0, 0)),
                  pl.BlockSpec((2, None, 1, FS), lambda j, i: (0, j, 0, 0)),
                  pl.BlockSpec((None, tb, FS), lambda j, i: (j, i, 0))],
                 [pl.BlockSpec((2, None, tb, FS), lambda j, i: (0, j, i, 0)),
                  pl.BlockSpec((2, None, 3, FS), lambda j, i: (0, j, 0, 0)),
                  pl.BlockSpec((2, None, 1, FS), lambda j, i: (0, j, 0, 0)),
                  pl.BlockSpec((2, FS // 2, D), lambda j, i: (j, 0, 0))],
                 [jax.ShapeDtypeStruct((2, NFF, s, FS), BF16), jax.ShapeDtypeStruct((2, NFF, 3, FS), F32),
                  jax.ShapeDtypeStruct((2, NFF, 1, FS), F32), jax.ShapeDtypeStruct((NDEV, FS // 2, D), BF16)],
                 [pltpu.VMEM((tb + 8, FS), F32), pltpu.VMEM((2, e, FS), F32), pltpu.VMEM((FS, D), F32)],
                 exchange=exchange)(hu, hu, hu, df, df, wd, cw, cb, a)


def _mm_dh2(dhu, wup, x1, g3, dx2, mix, g2, name):
    s = x1.shape[0]
    tm = 512

    def body(dhu_ref, w_ref, x1_ref, g3_ref, dx2_ref, mix_ref, g2_ref, dx1_ref, dmix_ref, dg3_ref, dg2_ref):
        i = pl.program_id(0)

        @pl.when(i == 0)
        def _():
            dg3_ref[...] = jnp.zeros_like(dg3_ref)
            dg2_ref[...] = jnp.zeros_like(dg2_ref)

        dh2 = _dot_nt(dhu_ref[0, 0], w_ref[0])
        for j in range(1, NDEV):
            dh2 = dh2 + _dot_nt(dhu_ref[j // NFF, j % NFF], w_ref[j])
        dn, dg3 = _rms_bwd(x1_ref[...], g3_ref[...], dh2)
        dx1 = dx2_ref[...] + dn
        dmix, dg2 = _rms_bwd(mix_ref[...], g2_ref[...], dx1)
        dx1_ref[...] = dx1
        dmix_ref[...] = dmix.astype(BF16)
        dg3_ref[...] += dg3
        dg2_ref[...] += dg2

    return _call(body, name, (s // tm,),
                 [pl.BlockSpec((2, NFF, tm, FS), lambda i: (0, 0, i, 0)), _resident((NDEV, D, FS)),
                  pl.BlockSpec((tm, D), _row), pl.BlockSpec((1, D), _fixed2), pl.BlockSpec((tm, D), _row),
                  pl.BlockSpec((tm, D), _row), pl.BlockSpec((1, D), _fixed2)],
                 [pl.BlockSpec((tm, D), _row), pl.BlockSpec((tm, D), _row), pl.BlockSpec((1, D), _fixed2),
                  pl.BlockSpec((1, D), _fixed2)],
                 [jax.ShapeDtypeStruct((s, D), F32), jax.ShapeDtypeStruct((s, D), BF16),
                  jax.ShapeDtypeStruct((1, D), F32), jax.ShapeDtypeStruct((1, D), F32)])(
                     dhu, wup, x1, g3, dx2, mix, g2)


def _dw_up(h2, dhu, name):
    s = h2.shape[0]
    tk = 512
    nk = s // tk

    def body(h_ref, d_ref, o_ref, acc):
        k = pl.program_id(1)

        @pl.when(k == 0)
        def _():
            acc[...] = jnp.zeros_like(acc)

        ht = h_ref[...].T
        for j in range(NFF):
            acc[j] += _dot(ht, d_ref[j])

        @pl.when(k == nk - 1)
        def _():
            o_ref[...] = acc[...].astype(BF16)

    return _call(body, name, (2, nk),
                 [pl.BlockSpec((tk, D), lambda hf, k: (k, 0)),
                  pl.BlockSpec((None, NFF, tk, FS), lambda hf, k: (hf, 0, k, 0))],
                 pl.BlockSpec((NFF, D, FS), lambda hf, k: (hf, 0, 0)),
                 jax.ShapeDtypeStruct((NDEV, D, FS), BF16), [pltpu.VMEM((NFF, D, FS), F32)])(h2, dhu)


def _mix_bwd(dmix, wo, proj, bgate, ya, yb, wao, wpo, z, att, mixed, name):
    s = dmix.shape[0]
    tm = 256
    nt = s // tm

    def body(dm_ref, w_ref, ga_ref, gb_ref, ba_ref, bb_ref, ya_ref, yb_ref, wao_ref, wpo_ref, z_ref, att_ref, mx_ref,
             dg_ref, dbg_ref, datt_ref, dmx_ref, dwo_ref, dwa_ref, dwb_ref, acco, acca, accb):
        i = pl.program_id(0)

        @pl.when(i == 0)
        def _():
            for ref in (dbg_ref, acco, acca, accb):
                ref[...] = jnp.zeros_like(ref)

        dm = dm_ref[...]
        dz = _dot_nt(dm, w_ref[...])
        ga = _sigmoid(ga_ref[...] + ba_ref[...])
        gb = _sigmoid(gb_ref[...] + bb_ref[...])
        dya = (dz * ga).astype(BF16)
        dyb = (dz * gb).astype(BF16)
        datt_ref[...] = _dot_nt(dya, wao_ref[...]).astype(BF16)
        dmx_ref[...] = _dot_nt(dyb, wpo_ref[...])
        acco[...] += _dot_tn(z_ref[...], dm)
        acca[...] += _dot_tn(att_ref[...], dya)
        accb[...] += _dot_tn(mx_ref[...], dyb)
        dga = dz * ya_ref[...].astype(F32) * ga * (1.0 - ga)
        dgb = dz * yb_ref[...].astype(F32) * gb * (1.0 - gb)
        dg_ref[:, 0:D] = dga.astype(BF16)
        dg_ref[:, D:2 * D] = dgb.astype(BF16)
        dbg_ref[:, 0:D] += jnp.sum(dga, axis=0, keepdims=True)
        dbg_ref[:, D:2 * D] += jnp.sum(dgb, axis=0, keepdims=True)

        @pl.when(i == nt - 1)
        def _():
            for j in range(NDEV):
                dwo_ref[j] = acco[j * 128:(j + 1) * 128, :].astype(BF16)
                dwa_ref[j] = acca[:, j * 128:(j + 1) * 128].astype(BF16)
                dwb_ref[j] = accb[:, j * 128:(j + 1) * 128].astype(BF16)

    rows = pl.BlockSpec((tm, D), _row)
    whole3 = lambda i: (0, 0, 0)
    return _call(body, name, (nt,),
                 [rows, _resident((D, D)),
                  pl.BlockSpec((tm, D), lambda i: (i, 2)), pl.BlockSpec((tm, D), lambda i: (i, 3)),
                  pl.BlockSpec((1, D), lambda i: (0, 0)), pl.BlockSpec((1, D), lambda i: (0, 1)),
                  rows, rows, _resident((AW, D)), _resident((PW, D)), rows, pl.BlockSpec((tm, AW), _row),
                  pl.BlockSpec((tm, PW), _row)],
                 [pl.BlockSpec((tm, 2 * D), _row), pl.BlockSpec((1, 2 * D), _fixed2),
                  pl.BlockSpec((tm, AW), _row), pl.BlockSpec((tm, PW), _row),
                  pl.BlockSpec((NDEV, 128, D), whole3), pl.BlockSpec((NDEV, AW, 128), whole3),
                  pl.BlockSpec((NDEV, PW, 128), whole3)],
                 [jax.ShapeDtypeStruct((s, 2 * D), BF16), jax.ShapeDtypeStruct((1, 2 * D), F32),
                  jax.ShapeDtypeStruct((s, AW), BF16), jax.ShapeDtypeStruct((s, PW), F32),
                  jax.ShapeDtypeStruct((NDEV, 128, D), BF16), jax.ShapeDtypeStruct((NDEV, AW, 128), BF16),
                  jax.ShapeDtypeStruct((NDEV, PW, 128), BF16)],
                 [pltpu.VMEM((D, D), F32), pltpu.VMEM((AW, D), F32), pltpu.VMEM((PW, D), F32)])(
                     dmix, wo, proj, proj, bgate, bgate, ya, yb, wao, wpo, z, att, mixed)


def _attn_bwd(proj, datt, att, lse, tab, name, exchange=None):
    s = proj.shape[0]
    nq = s // QB

    def body(q_ref, k0, k1, k2, v0, v1, v2, do_ref, o_ref, lse_ref, b_ref, dq_ref, dk_ref, dv_ref, db_ref,
             dka, dkb, dva, dvb):
        i = pl.program_id(1)

        @pl.when(i == 0)
        def _():
            for r in (dka, dkb, dva, dvb):
                r[...] = jnp.zeros_like(r)
            db_ref[...] = jnp.zeros_like(db_ref)

        @pl.when(i < nq)
        def _():
            qs, scs, k, lane = _attn_scores(q_ref, (k0, k1, k2), b_ref, i)
            v = jnp.concatenate([v0[...], v1[...], v2[...]], axis=0).astype(BF16)
            do = do_ref[...]
            doo = do.astype(F32) * o_ref[...].astype(F32)
            lse = lse_ref[...]
            dq = jnp.zeros((QB, 128), F32)
            dkt = jnp.zeros((128, KW), F32)
            dvt = jnp.zeros((128, KW), F32)
            for hh in (0, 1):
                mine = (lane < 64) if hh == 0 else (lane >= 64)
                doh = jnp.where(mine, do, jnp.zeros_like(do))
                kmask = lax.broadcasted_iota(jnp.int32, (KW, 128), 1)
                kh = jnp.where((kmask < 64) if hh == 0 else (kmask >= 64), k, jnp.zeros_like(k))
                p = jnp.exp(scs[hh] - lse[:, 64 * hh:64 * hh + 1])
                drow = jnp.sum(jnp.where(mine, doo, 0.0), axis=1, keepdims=True)
                ds = p * (_dot_nt(doh, v) - drow)
                db_ref[hh] += ds
                dsb = ds.astype(BF16)
                dq = dq + _dot(dsb, kh)
                dkt = dkt + _dot_tn(qs[hh], dsb)
                dvt = dvt + _dot_tn(doh, p.astype(BF16))
            dkw, dvw = dkt.T, dvt.T
            dq_ref[...] = (dq * SCALE).astype(BF16)
            dk_ref[...] = (dka[...] + dkw[0:QB]).astype(BF16)
            dka[...] = dkb[...] + dkw[QB:2 * QB]
            dkb[...] = dkw[2 * QB:3 * QB]
            dv_ref[...] = (dva[...] + dvw[0:QB]).astype(BF16)
            dva[...] = dvb[...] + dvw[QB:2 * QB]
            dvb[...] = dvw[2 * QB:3 * QB]

        @pl.when(i >= nq)
        def _():
            dk_ref[...] = dka[...].astype(BF16)
            dka[...] = dkb[...]
            dkb[...] = jnp.zeros_like(dkb)
            dv_ref[...] = dva[...].astype(BF16)
            dva[...] = dvb[...]
            dvb[...] = jnp.zeros_like(dvb)

    def kv(col, d):
        return pl.BlockSpec((QB, 128), lambda p, i: (jnp.clip(i - 2 + d, 0, nq - 1), col + p))

    cur = lambda p, i: (jnp.minimum(i, nq - 1), p)
    done = lambda p, i: (jnp.maximum(i - 2, 0), p)
    in_specs = [pl.BlockSpec((QB, 128), cur)]
    in_specs += [kv(4, d) for d in range(3)] + [kv(8, d) for d in range(3)]
    in_specs += [pl.BlockSpec((QB, 128), cur)] * 3 + [pl.BlockSpec((2, QB, KW), lambda p, i: (p, 0, 0))]
    o = jax.ShapeDtypeStruct((s, AW), BF16)
    return _call(body, name, (4, nq + 2), in_specs,
                 [pl.BlockSpec((QB, 128), cur), pl.BlockSpec((QB, 128), done), pl.BlockSpec((QB, 128), done),
                  pl.BlockSpec((2, QB, KW), lambda p, i: (p, 0, 0))],
                 [o, o, o, jax.ShapeDtypeStruct((HEADS, QB, KW), F32)],
                 [pltpu.VMEM((QB, 128), F32)] * 4, exchange=exchange)(
                     proj, proj, proj, proj, proj, proj, proj, datt, att, lse, tab)


def _rel_bias_grad(dtab, name):
    wdt = 640

    def body(x_ref, o_ref):
        x = x_ref[...]
        xc = x[0:64, 0:wdt]
        for qc in range(1, QB // 64):
            xc = xc + pltpu.roll(x[qc * 64:(qc + 1) * 64, :], KW - qc * 64, 1)[:, 0:wdt]
        r = lax.broadcasted_iota(jnp.int32, (64, 64), 0)
        c = lax.broadcasted_iota(jnp.int32, (64, 64), 1)
        flip = (r + c == 63).astype(F32)
        y = jnp.dot(flip, xc, preferred_element_type=F32, precision=lax.Precision.HIGHEST)
        z = pltpu.roll(y, 0, 1, stride=1, stride_axis=0)
        t = jnp.broadcast_to(jnp.sum(z, axis=0, keepdims=True), (8, wdt))
        e = lax.broadcasted_iota(jnp.int32, (wdt, wdt), 0)
        rr = lax.broadcasted_iota(jnp.int32, (wdt, wdt), 1)
        onehot = (jnp.clip(BAND - 1 - e, -MAXREL, MAXREL) + MAXREL == rr).astype(F32)
        o_ref[...] = jnp.dot(t, onehot, preferred_element_type=F32, precision=lax.Precision.HIGHEST)

    return _call(body, name, (HEADS,), [pl.BlockSpec((None, QB, KW), lambda h: (h, 0, 0))],
                 pl.BlockSpec((None, 8, wdt), lambda h: (h, 0, 0)),
                 jax.ShapeDtypeStruct((HEADS, 8, wdt), F32))(dtab)


def _pool_bwd(dmixed, pooled, wg, scale, name):
    s = dmixed.shape[0]
    tb = 512
    nt = s // tb
    e = tb + 16

    def body(dm_ref, dmn_ref, pl_ref, wg_ref, sc_ref, du_ref, dwg_ref, dsc_ref):
        i = pl.program_id(0)
        dm = jnp.concatenate([dm_ref[...], jnp.where(i == nt - 1, 0.0, dmn_ref[...])], axis=0)
        t = i * tb + lax.broadcasted_iota(jnp.int32, (e, 1), 0)

        @pl.when(i == 0)
        def _():
            dwg_ref[...] = jnp.zeros_like(dwg_ref)
            dsc_ref[...] = jnp.zeros_like(dsc_ref)

        for g, w in enumerate(POOL_WINDOWS):
            sl = slice(g * PG, (g + 1) * PG)
            wgb = wg_ref[g].astype(BF16)
            pb = pl_ref[:, sl]
            dsc_ref[:, sl] += jnp.sum(dm[0:tb, sl] * _dot(pb, wgb), axis=0, keepdims=True)
            dpre = (dm[:, sl] * sc_ref[:, sl]).astype(BF16)
            dwg_ref[g] += _dot_tn(pb, dpre[0:tb])
            dpool = _dot_nt(dpre, wgb)
            a = dpool / jnp.minimum(t + 1, w).astype(F32)
            sh = 1
            while sh < w:
                a = a + pltpu.roll(a, e - sh, 0)
                sh *= 2
            du_ref[:, sl] = (a[0:tb] - dpool[0:tb]).astype(BF16)

    nb16 = s // 16
    return _call(body, name, (nt,),
                 [pl.BlockSpec((tb, PW), _row),
                  pl.BlockSpec((16, PW), lambda i: (jnp.minimum((i + 1) * (tb // 16), nb16 - 1), 0)),
                  pl.BlockSpec((tb, PW), _row), pl.BlockSpec((4, PG, PG), lambda i: (0, 0, 0)),
                  pl.BlockSpec((1, PW), _fixed2)],
                 [pl.BlockSpec((tb, PW), _row), pl.BlockSpec((4, PG, PG), lambda i: (0, 0, 0)),
                  pl.BlockSpec((1, PW), _fixed2)],
                 [jax.ShapeDtypeStruct((s, PW), BF16), jax.ShapeDtypeStruct((4, PG, PG), F32),
                  jax.ShapeDtypeStruct((1, PW), F32)])(dmixed, dmixed, pooled, wg, scale)


def _dproj_specs(t):
    return [pl.BlockSpec((t, 512), _row)] * 4 + [pl.BlockSpec((t, 2 * D), _row)]


def _dproj_segment(refs, n):
    return refs[n][...] if n < 4 else refs[4][:, (n - 4) * 512:(n - 3) * 512]


def _mm_dh(segs, win, x, g1, dx1, below, name, exchange=None):
    s = x.shape[0]
    tm = 512

    def body(dq_ref, dk_ref, dv_ref, du_ref, dg_ref, w_ref, x_ref, g1_ref, dx1_ref, *rest):
        i = pl.program_id(0)
        outs = rest[2:] if below else rest

        @pl.when(i == 0)
        def _():
            for ref in outs[1::2]:
                ref[...] = jnp.zeros_like(ref)

        pieces = (dq_ref, dk_ref, dv_ref, du_ref, dg_ref)
        dh = _dot_nt(_dproj_segment(pieces, 0), w_ref[0])
        for n in range(1, NDEV):
            dh = dh + _dot_nt(_dproj_segment(pieces, n), w_ref[n])
        dn, dg1 = _rms_bwd(x_ref[...], g1_ref[...], dh)
        dx = dx1_ref[...] + dn
        outs[0][...] = dx
        outs[1][...] += dg1
        if below:
            df, dg4 = _rms_bwd(rest[0][...], rest[1][...], dx)
            outs[2][...] = df.astype(BF16)
            outs[3][...] += dg4

    rows, gain = pl.BlockSpec((tm, D), _row), pl.BlockSpec((1, D), _fixed2)
    vec = jax.ShapeDtypeStruct((1, D), F32)
    return _call(body, name, (s // tm,),
                 _dproj_specs(tm) + [_resident((NDEV, D, 512)), rows, gain, rows] + ([rows, gain] if below else []),
                 [rows, gain] + ([rows, gain] if below else []),
                 [jax.ShapeDtypeStruct((s, D), F32), vec] + ([jax.ShapeDtypeStruct((s, D), BF16), vec] if below else []),
                 exchange=exchange)(*segs, win, x, g1, dx1, *below)


def _dw_in(h, segs, name):
    s = h.shape[0]
    tk = 512
    nk = s // tk

    def body(h_ref, dq_ref, dk_ref, dv_ref, du_ref, dg_ref, o_ref, acc):
        k = pl.program_id(0)

        @pl.when(k == 0)
        def _():
            acc[...] = jnp.zeros_like(acc)

        ht = h_ref[...].T
        pieces = (dq_ref, dk_ref, dv_ref, du_ref, dg_ref)
        for n in range(NDEV):
            acc[n] += _dot(ht, _dproj_segment(pieces, n))

        @pl.when(k == nk - 1)
        def _():
            o_ref[...] = acc[...].astype(BF16)

    return _call(body, name, (nk,), [pl.BlockSpec((tk, D), _row)] + _dproj_specs(tk),
                 pl.BlockSpec((NDEV, D, 512), lambda k: (0, 0, 0)),
                 jax.ShapeDtypeStruct((NDEV, D, 512), BF16), [pltpu.VMEM((NDEV, D, 512), F32)])(h, *segs)


def _slot(px, py, pc):
    return 4 * px + 2 * py + pc


def _gather_exchange(shards, deliver):
    n = len(shards)

    def phases(ins, outs, sems):
        send_sems, recv_sems, local_sems = sems
        x, y, c = lax.axis_index("x"), lax.axis_index("y"), lax.axis_index("c")
        me, sibling = (x, y, c), (x, y, 1 - c)
        chips = [(1 - x, y), (x, 1 - y), (1 - x, 1 - y)]

        def copy(w, k, block, to, src=None):
            dst = outs[w].at[_slot(*block)]
            return pltpu.make_async_remote_copy(
                src_ref=dst if src is None else src, dst_ref=dst, send_sem=send_sems.at[7 * w + k],
                recv_sem=recv_sems.at[7 * w + k], device_id=to, device_id_type=MESH)

        def mine(w):
            return pltpu.make_async_copy(ins[w], outs[w].at[_slot(*me)], local_sems.at[w])

        def first(w):
            return [copy(w, 0, me, sibling, src=ins[w])] + [
                copy(w, 1 + j, me, (*chip, c), src=ins[w]) for j, chip in enumerate(chips)]

        def passed(w):
            return [copy(w, 4 + j, (*chip, c), sibling) for j, chip in enumerate(chips)]

        def send():
            for w in range(n):
                mine(w).start()
                for cp in first(w):
                    cp.start()

        def forward():
            for w in range(n):
                for j, chip in enumerate(chips):
                    copy(w, 1 + j, (*chip, c), me).wait_recv()
                    passed(w)[j].start()

        def finish():
            for w in range(n):
                copy(w, 0, sibling, me).wait_recv()
                for j, chip in enumerate(chips):
                    copy(w, 4 + j, (*chip, 1 - c), me).wait_recv()
            for w in range(n):
                for cp in first(w) + passed(w):
                    cp.wait_send()
                mine(w).wait()

        return [send, forward, finish]

    return _Exchange(shards, [jax.ShapeDtypeStruct((NDEV,) + a.shape, a.dtype) for a in shards],
                     [pltpu.SemaphoreType.DMA((7 * n,)), pltpu.SemaphoreType.DMA((7 * n,)),
                      pltpu.SemaphoreType.DMA((n,))], phases, deliver)


def _scatter_exchange(parts, deliver):
    n = len(parts)

    def phases(ins, outs, sems):
        send_sems, recv_sems, local_sems = sems
        x, y, c = lax.axis_index("x"), lax.axis_index("y"), lax.axis_index("c")
        me = _slot(x, y, c)
        peers = [((1 - x) if r & 4 else x, (1 - y) if r & 2 else y, (1 - c) if r & 1 else c) for r in range(1, NDEV)]

        def mine(w):
            return pltpu.make_async_copy(ins[w].at[me], outs[w].at[me], local_sems.at[w])

        def copy(w, r, block_here, block_there):
            return pltpu.make_async_remote_copy(
                src_ref=ins[w].at[block_here], dst_ref=outs[w].at[block_there], send_sem=send_sems.at[7 * w + r],
                recv_sem=recv_sems.at[7 * w + r], device_id=peers[r], device_id_type=MESH)

        def send():
            for w in range(n):
                mine(w).start()
                for r, peer in enumerate(peers):
                    copy(w, r, _slot(*peer), me).start()

        def finish():
            for w in range(n):
                for r, peer in enumerate(peers):
                    copy(w, r, me, _slot(*peer)).wait_recv()
            for w in range(n):
                for r, peer in enumerate(peers):
                    copy(w, r, _slot(*peer), me).wait_send()
                mine(w).wait()

        return [send, finish]

    return _Exchange(parts, [jax.ShapeDtypeStruct(a.shape, a.dtype) for a in parts],
                     [pltpu.SemaphoreType.DMA((7 * n,)), pltpu.SemaphoreType.DMA((7 * n,)),
                      pltpu.SemaphoreType.DMA((n,))], phases, deliver)


def _adamw(w, g, m, v):
    m = ADAM_B1 * m + (1.0 - ADAM_B1) * g
    v = ADAM_B2 * v + (1.0 - ADAM_B2) * (g * g)
    m_hat = m / (1.0 - ADAM_B1 ** ADAM_STEP)
    v_hat = v / (1.0 - ADAM_B2 ** ADAM_STEP)
    delta = -ADAM_LR * (m_hat / (jnp.sqrt(v_hat) + ADAM_EPS) + ADAM_WD * w)
    return delta, m, v


def _adamw_sharded(recvs, w, m, v, tr, name, exchange=None):
    _, r, cdim = w.shape
    nr = r // tr

    def body(r0_ref, r1_ref, w_ref, m_ref, v_ref, g_ref, d_ref, nm_ref, nv_ref):
        l = pl.program_id(0)

        def total(ref):
            acc = ref[0].astype(F32)
            for k in range(1, NDEV):
                acc = acc + ref[k].astype(F32)
            return acc

        g = jnp.where(l == 0, total(r0_ref), total(r1_ref))
        d, nm, nv = _adamw(w_ref[...], g, m_ref[...], v_ref[...])
        g_ref[...] = g
        d_ref[...] = d
        nm_ref[...] = nm
        nv_ref[...] = nv

    mine = pl.BlockSpec((None, tr, cdim), lambda l, i: (l, i, 0))
    out = jax.ShapeDtypeStruct(w.shape, F32)
    return _call(body, name, (DEPTH, nr),
                 [pl.BlockSpec((NDEV, tr, cdim), lambda l, i: (0, jnp.where(l == 0, i, nr - 1), 0)),
                  pl.BlockSpec((NDEV, tr, cdim), lambda l, i: (0, jnp.where(l == 1, i, 0), 0)),
                  mine, mine, mine],
                 [mine] * 4, [out] * 4, exchange=exchange)(recvs[0], recvs[1], w, m, v)


def _adamw_small(parts, w, m, v, name):
    r = w.shape[0]

    def body(p_ref, w_ref, m_ref, v_ref, g_ref, d_ref, nm_ref, nv_ref):
        g = p_ref[0]
        for k in range(1, NDEV):
            g = g + p_ref[k]
        d, nm, nv = _adamw(w_ref[...], g, m_ref[...], v_ref[...])
        g_ref[...] = g
        d_ref[...] = d
        nm_ref[...] = nm
        nv_ref[...] = nv

    whole = pl.BlockSpec((r, 1024), _fixed2)
    out = jax.ShapeDtypeStruct((r, 1024), F32)
    return _call(body, name, (1,), [pl.BlockSpec((NDEV, r, 1024), lambda i: (0, 0, 0)), whole, whole, whole],
                 [whole] * 4, [out] * 4)(parts, w, m, v)


SMALL = (("norm_mix_pre", (D,)), ("b_gate", (2 * D,)), ("rel_bias", (HEADS, NREL)), ("w_pool_group", (4, PG, PG)),
         ("pool_scale", (PW,)), ("norm_mix_post", (D,)), ("norm_ffn_pre", (D,)), ("conv_b", (NFF * 2 * FS,)),
         ("norm_ffn_post", (D,)))
SHARDED = (("w_in", 256), ("w_attn_out", 512), ("w_pool_out", 512), ("w_o", 128), ("w_up", 256), ("conv_w", 3),
           ("w_down", 176))
SMALL_ROWS = 168


def _pack_small(tree):
    flat = jnp.concatenate([tree[name].reshape(-1) for name, _ in SMALL])
    return jnp.pad(flat, (0, SMALL_ROWS * 1024 - flat.shape[0])).reshape(SMALL_ROWS, 1024)


def _unpack_small(packed):
    flat = packed.reshape(-1)
    out, at = {}, 0
    for name, shape in SMALL:
        size = DEPTH * int(np.prod(shape))
        out[name] = flat[at:at + size].reshape((DEPTH,) + shape)
        at += size
    return out


def _layer_fwd(l, x, h, p, gath, target, gnext, carry):
    n = f"l{l}"
    g = lambda name: p[name][l].reshape(1, -1)
    res = {"x": x, "h": h}
    res["tab"] = _bias_table(p["rel_bias"][l], f"bias_table_{n}")
    proj = _mm_in(h, gath["w_in"], f"mm_in_{n}", carry.get("mm_in"))
    att, res["lse"] = _attn_fwd(proj, res["tab"], f"attn_fwd_{n}", carry.get("attn_fwd"))
    pooled, mixed = _pool_fwd(proj, p["w_pool_group"][l], g("pool_scale"), f"pool_fwd_{n}")
    bgate = g("b_gate")
    z, ya, yb, mix, x1, h2 = _mix_fwd(att, mixed, gath["w_attn_out"], gath["w_pool_out"], proj, bgate, gath["w_o"],
                                      x, g("norm_mix_post"), g("norm_ffn_pre"), f"mix_fwd_{n}")
    cw = gath["conv_w"].reshape(2, NFF, 3, FS)
    cb = p["conv_b"][l].reshape(2, NFF, 1, FS)
    hu, a = _mm_up_conv_gate(h2, gath["w_up"], cw, cb, f"mm_up_conv_gate_{n}", carry.get("mm_up_conv_gate"))
    hu = hu.reshape(2, NFF, -1, FS)
    res.update(proj=proj, att=att, pooled=pooled, mixed=mixed, z=z, ya=ya, yb=yb, mix=mix, x1=x1, h2=h2, hu=hu,
               a=a, cw=cw, cb=cb, bgate=bgate)
    if target is None:
        f, x2, hn = _mm_down_fwd(a, gath["w_down"], x1, g("norm_ffn_post"), gnext, f"mm_down_fwd_{n}",
                                 carry.get("mm_down_fwd"))
        res["f"] = f
        return res, x2, hn
    dy, df, dg4, loss = _mm_down_loss(a, gath["w_down"], x1, g("norm_ffn_post"), target, f"mm_down_loss_{n}")
    return res, (dy, df, dg4), loss


def _layer_bwd(l, top, p, gath, res, carry, below):
    n = f"l{l}"
    g = lambda name: p[name][l].reshape(1, -1)
    big, small = {}, {}
    taken = lambda call: carry[call](big) if call in carry else None
    dx2, df, small["norm_ffn_post"] = top
    dhu, dcw, dcb, big["w_down"] = _conv_gate_bwd(res["hu"], df, gath["w_down"], res["cw"], res["cb"], res["a"],
                                                  f"conv_gate_bwd_{n}", taken("conv_gate_bwd"))
    big["conv_w"] = dcw.reshape(NDEV, 3, FS)
    small["conv_b"] = dcb
    dx1, dmix, small["norm_ffn_pre"], small["norm_mix_post"] = _mm_dh2(
        dhu, gath["w_up"], res["x1"], g("norm_ffn_pre"), dx2, res["mix"], g("norm_mix_post"), f"mm_dh2_{n}")
    big["w_up"] = _dw_up(res["h2"], dhu, f"dw_up_{n}")
    dgates, small["b_gate"], datt, dmixed, big["w_o"], big["w_attn_out"], big["w_pool_out"] = _mix_bwd(
        dmix, gath["w_o"], res["proj"], res["bgate"], res["ya"], res["yb"], gath["w_attn_out"], gath["w_pool_out"],
        res["z"], res["att"], res["mixed"], f"mix_bwd_{n}")
    dq, dk, dv, dtab = _attn_bwd(res["proj"], datt, res["att"], res["lse"], res["tab"], f"attn_bwd_{n}",
                                 taken("attn_bwd"))
    small["rel_bias"] = _rel_bias_grad(dtab, f"rel_bias_grad_{n}")[:, 0, :NREL]
    du, small["w_pool_group"], small["pool_scale"] = _pool_bwd(
        dmixed, res["pooled"], p["w_pool_group"][l], g("pool_scale"), f"pool_bwd_{n}")
    segs = (dq, dk, dv, du, dgates)
    big["w_in"] = _dw_in(res["h"], segs, f"dw_in_{n}")
    dx, small["norm_mix_pre"], *lower = _mm_dh(segs, gath["w_in"], res["x"], g("norm_mix_pre"), dx1, below,
                                               f"mm_dh_{n}", taken("mm_dh"))
    return (dx, *lower), big, small


def _gather_weights(gath, l, p, names):
    shards = [p[k][l] if k == "conv_w" else p[k][l].astype(BF16) for k in names]

    def deliver(results):
        for k, a in zip(names, results):
            if k in ("w_attn_out", "w_pool_out"):
                a = jnp.transpose(a, (1, 0, 2)).reshape(AW, D)
            elif k == "w_o":
                a = a.reshape(D, D)
            elif k == "w_down":
                a = a.reshape(NFF * FS, D)
            gath[k] = a

    return _gather_exchange(shards, deliver)


def _scatter_grads(recv, big, names):
    return _scatter_exchange([big[k] for k in names], lambda results: recv.update(zip(names, results)))


def kernel(x, norm_mix_pre, w_in, b_gate, rel_bias, w_attn_out, w_pool_group, pool_scale, w_pool_out, w_o, norm_mix_post, norm_ffn_pre, w_up, conv_w, conv_b, w_down, norm_ffn_post, loss_target, m_norm_mix_pre, m_w_in, m_b_gate, m_rel_bias, m_w_attn_out, m_w_pool_group, m_pool_scale, m_w_pool_out, m_w_o, m_norm_mix_post, m_norm_ffn_pre, m_w_up, m_conv_w, m_conv_b, m_w_down, m_norm_ffn_post, v_norm_mix_pre, v_w_in, v_b_gate, v_rel_bias, v_w_attn_out, v_w_pool_group, v_pool_scale, v_w_pool_out, v_w_o, v_norm_mix_post, v_norm_ffn_pre, v_w_up, v_conv_w, v_conv_b, v_w_down, v_norm_ffn_post):
    names = ("norm_mix_pre", "w_in", "b_gate", "rel_bias", "w_attn_out", "w_pool_group", "pool_scale", "w_pool_out",
             "w_o", "norm_mix_post", "norm_ffn_pre", "w_up", "conv_w", "conv_b", "w_down", "norm_ffn_post")
    p = dict(zip(names, (norm_mix_pre, w_in, b_gate, rel_bias, w_attn_out, w_pool_group, pool_scale, w_pool_out, w_o,
                         norm_mix_post, norm_ffn_pre, w_up, conv_w, conv_b, w_down, norm_ffn_post)))
    mom = dict(zip(names, (m_norm_mix_pre, m_w_in, m_b_gate, m_rel_bias, m_w_attn_out, m_w_pool_group, m_pool_scale,
                           m_w_pool_out, m_w_o, m_norm_mix_post, m_norm_ffn_pre, m_w_up, m_conv_w, m_conv_b, m_w_down,
                           m_norm_ffn_post)))
    var = dict(zip(names, (v_norm_mix_pre, v_w_in, v_b_gate, v_rel_bias, v_w_attn_out, v_w_pool_group, v_pool_scale,
                           v_w_pool_out, v_w_o, v_norm_mix_post, v_norm_ffn_pre, v_w_up, v_conv_w, v_conv_b, v_w_down,
                           v_norm_ffn_post)))
    s = x.shape[1]
    xs = x.reshape(s, D)
    target = loss_target.reshape(s, D)

    gath = [{}, {}]
    rest = ("w_attn_out", "w_pool_out", "w_o", "w_down", "conv_w")
    h0 = _rmsnorm_call(xs, p["norm_mix_pre"][0].reshape(1, D), "rmsnorm_l0",
                       _gather_weights(gath[0], 0, p, ("w_in",)))
    res0, x2, h1 = _layer_fwd(0, xs, h0, p, gath[0], None, p["norm_mix_pre"][1].reshape(1, D), {
        "mm_in": _gather_weights(gath[0], 0, p, rest),
        "attn_fwd": _gather_weights(gath[0], 0, p, ("w_up",)),
        "mm_up_conv_gate": _gather_weights(gath[1], 1, p, ("w_in",) + rest),
        "mm_down_fwd": _gather_weights(gath[1], 1, p, ("w_up",))})
    res1, top1, loss_part = _layer_fwd(1, x2, h1, p, gath[1], target, None, {})

    order = [k for k, _ in SHARDED]
    recv = [{}, {}]
    top0, big1, small1 = _layer_bwd(1, top1, p, gath[1], res1, {}, (res0["f"], p["norm_ffn_post"][0].reshape(1, D)))
    (grad_x,), big0, small0 = _layer_bwd(0, top0, p, gath[0], res0, {
        "conv_gate_bwd": lambda big: _scatter_grads(recv[1], big1, order),
        "attn_bwd": lambda big: _scatter_grads(recv[0], big, [k for k in order if k != "w_in"]),
        "mm_dh": lambda big: _scatter_grads(recv[0], big, ["w_in"])}, ())

    loss = lax.psum(loss_part[0, 0], ("x", "y", "c"))

    small = {k: jnp.stack([small0[k].reshape(shape), small1[k].reshape(shape)]) for k, shape in SMALL}
    parts = []
    carried = {"w_up": _gather_exchange([_pack_small(small)], parts.extend)}
    out = {}
    for k, tr in sorted(SHARDED, key=lambda kt: kt[0] not in carried):
        out[k] = _adamw_sharded((recv[0][k], recv[1][k]), p[k], mom[k], var[k], tr, f"adamw_{k}", carried.get(k))
    packed = _adamw_small(parts[0], _pack_small(p), _pack_small(mom), _pack_small(var), "adamw_small")
    unpacked = [_unpack_small(a) for a in packed]
    for k, _ in SMALL:
        out[k] = tuple(u[k] for u in unpacked)

    return (loss, grad_x.reshape(x.shape), *[out[k][0] for k in names], *[out[k][1] for k in names],
            *[out[k][2] for k in names], *[out[k][3] for k in names])
```
